```python
import math
import jax, jax.numpy as jnp
from jax import lax
import numpy as np

D_MODEL = 1024
BATCH = 8
SEQ = 2048
DEPTH = 2

RWKV_HEAD_SIZE = 64
RWKV_HEADS = D_MODEL // RWKV_HEAD_SIZE
DECAY_LORA = 64
AAA_LORA = 64
GATE_LORA = 128
RWKV_GN_EPS = 64e-5
N_LERP = 6
BRANCHES = ((128, 1), (512, 4), (2048, 16))
N_BRANCH = len(BRANCHES)
ATTN_HEAD_DIM = 64
HEADS_PER_BRANCH = 8
Q_DIM = N_BRANCH * HEADS_PER_BRANCH * ATTN_HEAD_DIM
ATTN_OUT_DIM = HEADS_PER_BRANCH * ATTN_HEAD_DIM
Q_BLOCK = 128
ALIBI_MAX = 8.0
FF_DENSE = 2816
N_EXPERTS = 8
TOP_K = 2
FF_EXPERT = 3584
RMS_EPS = 1e-5

kernel_name = "rwkv7_yoco_dilated_moe_block"


def _layer_counts():
    n_a = DEPTH - DEPTH // 2
    n_b = DEPTH // 2
    n_dense = (DEPTH + 1) // 2
    n_moe = DEPTH // 2
    return n_a, n_b, n_dense, n_moe


def rmsnorm(x, g):
    xf = x.astype(jnp.float32)
    y = xf * lax.rsqrt(jnp.mean(xf * xf, axis=-1, keepdims=True) + RMS_EPS)
    return (y * g.astype(jnp.float32)).astype(x.dtype)


def token_shift(h):
    return jnp.pad(h[:, :-1], ((0, 0), (1, 0), (0, 0)))


def wkv7_scan(r, decay, k, v, kk, a):
    b, _, h, n = r.shape

    def step(state, inp):
        r_t, w_t, k_t, v_t, kk_t, a_t = inp
        sa = jnp.einsum('bhvk,bhk->bhv', state, -kk_t)
        state = (state * w_t[:, :, None, :]
                 + sa[..., None] * (kk_t * a_t)[:, :, None, :]
                 + v_t[..., None] * k_t[:, :, None, :])
        y = jnp.einsum('bhvk,bhk->bhv', state, r_t)
        return state, y

    xs = tuple(jnp.moveaxis(t, 1, 0) for t in (r, decay, k, v, kk, a))
    state0 = jnp.zeros((b, h, n, n), jnp.float32)
    _, y = lax.scan(step, state0, xs)
    return jnp.moveaxis(y, 0, 1)


def rwkv7_time_mix(h, mu, wr, wk, wv, w0, w1, w2, a0, a1, a2, g1, g2,
                   k_k, k_a, r_k, lnx_w, lnx_b, wo):
    b, s, d = h.shape
    xx = token_shift(h) - h
    xr, xw, xk, xv, xa, xg = (h + xx * mu[i] for i in range(N_LERP))
    r = xr @ wr
    k = xk @ wk
    v = xv @ wv
    w_log = -jax.nn.softplus(-(w0 + jnp.tanh(xw @ w1) @ w2)) - 0.5
    a = jax.nn.sigmoid(a0 + (xa @ a1) @ a2)
    g = jax.nn.sigmoid(xg @ g1) @ g2

    def heads(t):
        return t.reshape(b, s, RWKV_HEADS, RWKV_HEAD_SIZE).astype(jnp.float32)

    kk = heads(k * k_k)
    kk = kk / jnp.maximum(jnp.sqrt(jnp.sum(kk * kk, axis=-1, keepdims=True)), 1e-12)
    k = k * (1.0 + (a - 1.0) * k_a)
    decay = jnp.exp(-jnp.exp(heads(w_log)))
    rh, kh, vh, ah = heads(r), heads(k), heads(v), heads(a)
    y = wkv7_scan(rh, decay, kh, vh, kk, ah)
    mean = jnp.mean(y, axis=-1, keepdims=True)
    var = jnp.mean(jnp.square(y - mean), axis=-1, keepdims=True)
    yn = ((y - mean) * lax.rsqrt(var + RWKV_GN_EPS)).reshape(b, s, d)
    yn = yn * lnx_w.astype(jnp.float32) + lnx_b.astype(jnp.float32)
    bonus = (jnp.sum(rh * kh * r_k.astype(jnp.float32), axis=-1, keepdims=True) * vh).reshape(b, s, d)
    return ((yn + bonus).astype(h.dtype) * g) @ wo


def dilated_branch(q, k, v, slopes, window, dilation):
    b, s, h, dh = q.shape
    n_keys = window // dilation + 1
    offs = jnp.arange(n_keys, dtype=jnp.int32) * dilation
    dist = offs.astype(jnp.float32)
    scale = 1.0 / math.sqrt(dh)
    slopes = slopes.astype(jnp.float32)

    def block(i):
        t = i * Q_BLOCK + jnp.arange(Q_BLOCK, dtype=jnp.int32)
        pos = t[:, None] - offs[None, :]
        valid = pos >= 0
        idx = jnp.maximum(pos, 0)
        qb = lax.dynamic_slice_in_dim(q, i * Q_BLOCK, Q_BLOCK, axis=1)
        kg = jnp.take(k, idx, axis=1)
        vg = jnp.take(v, idx, axis=1)
        sc = jnp.einsum('bqhd,bqkhd->bhqk', qb, kg).astype(jnp.float32) * scale
        sc = sc - slopes[None, :, None, None] * dist[None, None, None, :]
        sc = jnp.where(valid[None, None], sc, -jnp.inf)
        lse = jax.nn.logsumexp(sc, axis=-1)
        p = jnp.exp(sc - lse[..., None])
        o = jnp.einsum('bhqk,bqkhd->bqhd', p.astype(vg.dtype), vg)
        return o, jnp.transpose(lse, (0, 2, 1))

    o, lse = lax.map(block, jnp.arange(s // Q_BLOCK, dtype=jnp.int32))
    o = jnp.moveaxis(o, 0, 1).reshape(b, s, h, dh)
    lse = jnp.moveaxis(lse, 0, 1).reshape(b, s, h)
    return o, lse


def dilated_attention(h, wq, wo, k_sh, v_sh, slopes):
    b, s, _ = h.shape
    q = (h @ wq).reshape(b, s, N_BRANCH, HEADS_PER_BRANCH, ATTN_HEAD_DIM)
    outs, lses = [], []
    for gi, (window, dilation) in enumerate(BRANCHES):
        o, lse = dilated_branch(q[:, :, gi], k_sh[:, :, gi], v_sh[:, :, gi], slopes[gi], window, dilation)
        outs.append(o)
        lses.append(lse)
    wts = jax.nn.softmax(jnp.stack(lses), axis=0)
    o = jnp.sum(wts[..., None] * jnp.stack(outs).astype(jnp.float32), axis=0)
    return o.reshape(b, s, ATTN_OUT_DIM).astype(h.dtype) @ wo


def swiglu(h, w1, w3, w2):
    return (jax.nn.silu(h @ w1) * (h @ w3)) @ w2


def moe_swiglu(h, router, router_bias, w1, w3, w2):
    b, s, d = h.shape
    t = h.reshape(b * s, d)
    logits = (t @ router + router_bias).astype(jnp.float32)
    top_v, top_i = lax.top_k(logits, TOP_K)
    top_w = jax.nn.softmax(top_v, axis=-1)
    gates = jnp.sum(jax.nn.one_hot(top_i, N_EXPERTS, dtype=jnp.float32) * top_w[..., None], axis=1)
    out = jnp.zeros_like(t)
    for e in range(N_EXPERTS):
        out = out + gates[:, e:e + 1].astype(t.dtype) * swiglu(t, w1[e], w3[e], w2[e])
    return out.reshape(b, s, d)


def alibi_slopes():
    n = N_BRANCH * HEADS_PER_BRANCH
    sl = jnp.exp2(-ALIBI_MAX * (jnp.arange(n, dtype=jnp.float32) + 1.0) / n)
    return sl.reshape(N_BRANCH, HEADS_PER_BRANCH)


def setup_inputs(seed: int = 0) -> dict:
    key = jax.random.key(seed)
    ks = iter(jax.random.split(key, 48))
    n_a, n_b, n_dense, n_moe = _layer_counts()
    D = D_MODEL

    def nrm(shape, scale):
        return jax.random.normal(next(ks), shape, jnp.float32) * scale

    def unif(shape, lo, hi):
        return jax.random.uniform(next(ks), shape, jnp.float32, lo, hi)

    return {
        "x": nrm((BATCH, SEQ, D), 1.0),
        "norm_gain": 1.0 + nrm((DEPTH, 2, D), 0.05),
        "rwkv_mu": unif((n_a, N_LERP, D), 0.0, 1.0),
        "rwkv_wr": nrm((n_a, D, D), D ** -0.5),
        "rwkv_wk": nrm((n_a, D, D), D ** -0.5),
        "rwkv_wv": nrm((n_a, D, D), D ** -0.5),
        "rwkv_w0": unif((n_a, D), -6.0, -1.0),
        "rwkv_w1": nrm((n_a, D, DECAY_LORA), D ** -0.5),
        "rwkv_w2": nrm((n_a, DECAY_LORA, D), 0.5 * DECAY_LORA ** -0.5),
        "rwkv_a0": unif((n_a, D), -1.0, 1.0),
        "rwkv_a1": nrm((n_a, D, AAA_LORA), D ** -0.5),
        "rwkv_a2": nrm((n_a, AAA_LORA, D), 0.5 * AAA_LORA ** -0.5),
        "rwkv_g1": nrm((n_a, D, GATE_LORA), D ** -0.5),
        "rwkv_g2": nrm((n_a, GATE_LORA, D), GATE_LORA ** -0.5),
        "rwkv_k_k": 0.85 + nrm((n_a, D), 0.05),
        "rwkv_k_a": 1.0 + nrm((n_a, D), 0.05),
        "rwkv_r_k": nrm((n_a, RWKV_HEADS, RWKV_HEAD_SIZE), 0.1),
        "rwkv_lnx_w": 1.0 + nrm((n_a, D), 0.05),
        "rwkv_lnx_b": nrm((n_a, D), 0.02),
        "rwkv_wo": nrm((n_a, D, D), D ** -0.5),
        "kv_norm_gain": 1.0 + nrm((D,), 0.05),
        "w_kv": nrm((D, 2 * Q_DIM), D ** -0.5),
        "attn_wq": nrm((n_b, D, Q_DIM), D ** -0.5),
        "attn_wo": nrm((n_b, ATTN_OUT_DIM, D), ATTN_OUT_DIM ** -0.5),
        "ffn_w1": nrm((n_dense, D, FF_DENSE), D ** -0.5),
        "ffn_w3": nrm((n_dense, D, FF_DENSE), D ** -0.5),
        "ffn_w2": nrm((n_dense, FF_DENSE, D), FF_DENSE ** -0.5),
        "moe_router": nrm((n_moe, D, N_EXPERTS), D ** -0.5),
        "moe_router_bias": nrm((n_moe, N_EXPERTS), 0.01),
        "moe_w1": nrm((n_moe, N_EXPERTS, D, FF_EXPERT), D ** -0.5),
        "moe_w3": nrm((n_moe, N_EXPERTS, D, FF_EXPERT), D ** -0.5),
        "moe_w2": nrm((n_moe, N_EXPERTS, FF_EXPERT, D), FF_EXPERT ** -0.5),
        "final_norm_gain": 1.0 + nrm((D,), 0.05),
    }


def reference(x, norm_gain, rwkv_mu, rwkv_wr, rwkv_wk, rwkv_wv, rwkv_w0, rwkv_w1, rwkv_w2,
              rwkv_a0, rwkv_a1, rwkv_a2, rwkv_g1, rwkv_g2, rwkv_k_k, rwkv_k_a, rwkv_r_k,
              rwkv_lnx_w, rwkv_lnx_b, rwkv_wo, kv_norm_gain, w_kv, attn_wq, attn_wo,
              ffn_w1, ffn_w3, ffn_w2, moe_router, moe_router_bias, moe_w1, moe_w3, moe_w2,
              final_norm_gain):
    n_a, n_b, n_dense, n_moe = _layer_counts()
    b, s, _ = x.shape
    slopes = alibi_slopes()
    k_sh = v_sh = None
    for l in range(DEPTH):
        if l < n_a:
            h = rmsnorm(x, norm_gain[l, 0])
            x = x + rwkv7_time_mix(h, rwkv_mu[l], rwkv_wr[l], rwkv_wk[l], rwkv_wv[l],
                                   rwkv_w0[l], rwkv_w1[l], rwkv_w2[l],
                                   rwkv_a0[l], rwkv_a1[l], rwkv_a2[l],
                                   rwkv_g1[l], rwkv_g2[l], rwkv_k_k[l], rwkv_k_a[l], rwkv_r_k[l],
                                   rwkv_lnx_w[l], rwkv_lnx_b[l], rwkv_wo[l])
        else:
            if l == n_a:
                kv = (rmsnorm(x, kv_norm_gain) @ w_kv).reshape(
                    b, s, 2, N_BRANCH, HEADS_PER_BRANCH, ATTN_HEAD_DIM)
                k_sh, v_sh = kv[:, :, 0], kv[:, :, 1]
            j = l - n_a
            h = rmsnorm(x, norm_gain[l, 0])
            x = x + dilated_attention(h, attn_wq[j], attn_wo[j], k_sh, v_sh, slopes)
        h = rmsnorm(x, norm_gain[l, 1])
        if l % 2 == 0:
            i = l // 2
            x = x + swiglu(h, ffn_w1[i], ffn_w3[i], ffn_w2[i])
        else:
            i = l // 2
            x = x + moe_swiglu(h, moe_router[i], moe_router_bias[i], moe_w1[i], moe_w3[i], moe_w2[i])
    return rmsnorm(x, final_norm_gain)
```

```python
import functools

import jax
import jax.numpy as jnp
from jax import lax
from jax.experimental import pallas as pl
from jax.experimental.pallas import tpu as pltpu

F32 = jnp.float32
BF16 = jnp.bfloat16

RMS_EPS = 1e-5
GN_EPS = 64e-5
HEAD = 64
LANES = 128
BRANCHES = ((128, 1), (512, 4), (2048, 16))
HEADS_PER_BRANCH = 8
ALIBI_MAX = 8.0
N_EXPERTS = 8
NEG = -1e30
VMEM_LIMIT_BYTES = 56 * 1024 * 1024


def _params(*sem):
    return pltpu.CompilerParams(dimension_semantics=sem, vmem_limit_bytes=VMEM_LIMIT_BYTES)


def _dot(a, b):
    return jnp.dot(a, b, preferred_element_type=F32)


def _dot_nt(a, b):
    return lax.dot_general(a, b, (((1,), (1,)), ((), ())), preferred_element_type=F32)


def _bf(x):
    return x.astype(BF16)


def _split2(x):
    hi = x.astype(BF16)
    lo = (x - hi.astype(F32)).astype(BF16)
    return hi, lo


def _split3(x):
    hi = x.astype(BF16)
    r1 = x - hi.astype(F32)
    mid = r1.astype(BF16)
    lo = (r1 - mid.astype(F32)).astype(BF16)
    return hi, mid, lo


def _sigmoid(z):
    return 1.0 / (1.0 + jnp.exp(-z))


def _softplus(z):
    return jnp.maximum(z, 0.0) + jnp.log(1.0 + jnp.exp(-jnp.abs(z)))


def _rms(x, gain):
    return x * lax.rsqrt(jnp.mean(x * x, axis=-1, keepdims=True) + RMS_EPS) * gain


def _headsum(x, e, et):
    hi, lo = _split2(x)
    s = _dot(hi, e) + _dot(lo, e)
    shi, slo = _split2(s)
    return _dot(shi, et) + _dot(slo, et)


def _const_spec(shape):
    nd = len(shape)
    return pl.BlockSpec(shape, lambda *_: (0,) * nd)


def _rwkv_pre_kernel(x_ref, xp_ref, vec_ref, mu_ref, wr_ref, wk_ref, wv_ref, w1_ref, w2_ref,
                     a1_ref, a2_ref, g1_ref, g2_ref, e_ref, et_ref,
                     r_out, lw_out, k_out, v_out, kk_out, a_out, g_out):
    i = pl.program_id(1)
    vec = vec_ref[...]
    gain, w0, a0, k_k, k_a = (vec[j:j + 1] for j in range(5))
    x = x_ref[0]
    h = _rms(x, gain)
    hp = _rms(xp_ref[0][7:8, :], gain)
    hp = jnp.where(i > 0, hp, 0.0)
    rows = lax.broadcasted_iota(jnp.int32, h.shape, 0)
    hprev = jnp.where(rows == 0, hp, pltpu.roll(h, 1, 0))
    xx = hprev - h
    mu = mu_ref[...]
    xr, xw, xk, xv, xa, xg = (_bf(h + xx * mu[j:j + 1]) for j in range(6))
    r = _dot(xr, wr_ref[...])
    k = _dot(xk, wk_ref[...])
    v = _dot(xv, wv_ref[...])
    wl = w0 + _dot(_bf(jnp.tanh(_dot(xw, w1_ref[...]))), w2_ref[...])
    w_log = -_softplus(-wl) - 0.5
    a = _sigmoid(a0 + _dot(_bf(_dot(xa, a1_ref[...])), a2_ref[...]))
    g = _dot(_bf(_sigmoid(_dot(xg, g1_ref[...]))), g2_ref[...])
    kkr = k * k_k
    ss = _headsum(kkr * kkr, e_ref[...], et_ref[...])
    r_out[0] = r
    lw_out[0] = -jnp.exp(w_log)
    k_out[0] = k * (1.0 + (a - 1.0) * k_a)
    v_out[0] = v
    kk_out[0] = kkr / jnp.maximum(jnp.sqrt(ss), 1e-12)
    a_out[0] = a
    g_out[0] = g


def _rwkv_pre(x, vec, mu, wr, wk, wv, w1, w2, a1, a2, g1, g2, e, et, tm):
    b, s, d = x.shape
    tok = pl.BlockSpec((1, tm, d), lambda bi, i: (bi, i, 0))
    prev = pl.BlockSpec((1, 8, d), lambda bi, i: (bi, jnp.maximum(i * (tm // 8) - 1, 0), 0))
    consts = [vec, mu, wr, wk, wv, w1, w2, a1, a2, g1, g2, e, et]
    out = jax.ShapeDtypeStruct((b, s, d), F32)
    return pl.pallas_call(
        _rwkv_pre_kernel,
        grid=(b, s // tm),
        in_specs=[tok, prev] + [_const_spec(c.shape) for c in consts],
        out_specs=[tok] * 7,
        out_shape=[out] * 7,
        compiler_params=_params("parallel", "arbitrary"),
        name="rwkv_pre",
    )(x, x, *consts)


def _unit_lower_inverse(n, eye, steps):
    x = eye + n
    p = n
    for _ in range(steps):
        pb = _bf(p)
        p = _dot(pb, pb)
        x = x + _dot(_bf(x), _bf(p))
    return x


def _wkv_kernel(r_ref, lw_ref, k_ref, v_ref, kk_ref, a_ref, y_ref, *, chunk):
    c_len = chunk
    n_chunks = r_ref.shape[1] // c_len
    inv_steps = max(c_len.bit_length() - 2, 0)
    lane = lax.broadcasted_iota(jnp.int32, (c_len, LANES), 1)
    head0 = lane < HEAD
    rr = lax.broadcasted_iota(jnp.int32, (c_len, c_len), 0)
    cc = lax.broadcasted_iota(jnp.int32, (c_len, c_len), 1)
    lower = rr >= cc
    strict = rr > cc
    ones_lower = jnp.where(lower, 1.0, 0.0).astype(BF16)
    eye = jnp.where(rr == cc, 1.0, 0.0).astype(F32)
    r2 = lax.broadcasted_iota(jnp.int32, (LANES, LANES), 0)
    c2 = lax.broadcasted_iota(jnp.int32, (LANES, LANES), 1)
    same_head = (r2 < HEAD) == (c2 < HEAD)

    def per_head(fn):
        return jnp.where(head0, fn(0), fn(1))

    def body(c, state):
        sl = pl.ds(pl.multiple_of(c * c_len, c_len), c_len)
        lw = lw_ref[0, sl, :]
        hi, mid, lo = _split3(lw)
        cum = _dot(ones_lower, hi) + _dot(ones_lower, mid) + _dot(ones_lower, lo)
        cum_last = cum[c_len - 1:c_len, :]
        kk = kk_ref[0, sl, :]
        b = kk * a_ref[0, sl, :]
        k = k_ref[0, sl, :]
        v = v_ref[0, sl, :]
        e_neg = jnp.exp(-cum)
        e_end = jnp.exp(cum_last - cum)
        a_t = -kk * jnp.exp(cum - lw)
        r_t = r_ref[0, sl, :] * jnp.exp(cum)
        b_t = _bf(b * e_neg)
        k_t = _bf(k * e_neg)
        vb = _bf(v)
        at_b = _bf(a_t)
        rt_b = _bf(r_t)

        l_ak, m_rb, m_rk, t_inv = [], [], [], []
        for hd in range(2):
            keep = head0 if hd == 0 else jnp.logical_not(head0)
            at_h = _bf(jnp.where(keep, a_t, 0.0))
            rt_h = _bf(jnp.where(keep, r_t, 0.0))
            l_ab = jnp.where(strict, _dot_nt(at_h, b_t), 0.0)
            l_ak.append(_bf(jnp.where(strict, _dot_nt(at_h, k_t), 0.0)))
            m_rb.append(_bf(jnp.where(lower, _dot_nt(rt_h, b_t), 0.0)))
            m_rk.append(_bf(jnp.where(lower, _dot_nt(rt_h, k_t), 0.0)))
            t_inv.append(_bf(_unit_lower_inverse(l_ab, eye, inv_steps)))

        st_b = _bf(state)
        w = _dot_nt(at_b, st_b) + per_head(lambda hd: _dot(l_ak[hd], vb))
        wb = _bf(w)
        u = per_head(lambda hd: _dot(t_inv[hd], wb))
        ub = _bf(u)
        y = (_dot_nt(rt_b, st_b) + per_head(lambda hd: _dot(m_rb[hd], ub))
             + per_head(lambda hd: _dot(m_rk[hd], vb)))
        y_ref[0, sl, :] = y
        uv_t = _bf(jnp.transpose(jnp.concatenate([u, v], axis=0)))
        bk = _bf(jnp.concatenate([b * e_end, k * e_end], axis=0))
        return state * jnp.exp(cum_last) + jnp.where(same_head, _dot(uv_t, bk), 0.0)

    lax.fori_loop(0, n_chunks, body, jnp.zeros((LANES, LANES), F32))


def _wkv(r, lw, k, v, kk, a, chunk):
    b, s, d = r.shape
    spec = pl.BlockSpec((1, s, LANES), lambda bi, hi: (bi, 0, hi))
    return pl.pallas_call(
        functools.partial(_wkv_kernel, chunk=chunk),
        grid=(b, d // LANES),
        in_specs=[spec] * 6,
        out_specs=spec,
        out_shape=jax.ShapeDtypeStruct((b, s, d), F32),
        compiler_params=_params("parallel", "parallel"),
        name="wkv7",
    )(r, lw, k, v, kk, a)


def _rwkv_post_kernel(y_ref, r_ref, k_ref, v_ref, g_ref, x_ref, vec_ref, wo_ref, e_ref, et_ref, o_ref):
    vec = vec_ref[...]
    lnx_w, lnx_b, r_k = (vec[j:j + 1] for j in range(3))
    e = e_ref[...]
    et = et_ref[...]
    y = y_ref[...]
    v = v_ref[...]
    mean = _headsum(y, e, et) * (1.0 / HEAD)
    dy = y - mean
    var = _headsum(dy * dy, e, et) * (1.0 / HEAD)
    yn = dy * lax.rsqrt(var + GN_EPS) * lnx_w + lnx_b
    bonus = _headsum(r_ref[...] * k_ref[...] * r_k, e, et) * v
    o_ref[...] = x_ref[...] + _dot(_bf((yn + bonus) * g_ref[...]), wo_ref[...])


def _rwkv_post(y, r, k, v, g, x, vec, wo, e, et, tm):
    t, d = x.shape
    tok = pl.BlockSpec((tm, d), lambda i: (i, 0))
    consts = [vec, wo, e, et]
    return pl.pallas_call(
        _rwkv_post_kernel,
        grid=(t // tm,),
        in_specs=[tok] * 6 + [_const_spec(c.shape) for c in consts],
        out_specs=tok,
        out_shape=jax.ShapeDtypeStruct((t, d), F32),
        compiler_params=_params("parallel"),
        name="rwkv_post",
    )(y, r, k, v, g, x, *consts)


def _ffn_kernel(x_ref, gain_ref, w1_ref, w3_ref, w2_ref, o_ref, h_ref, acc_ref):
    f = pl.program_id(1)

    @pl.when(f == 0)
    def _():
        h_ref[...] = _bf(_rms(x_ref[...], gain_ref[...]))
        acc_ref[...] = jnp.zeros_like(acc_ref)

    h = h_ref[...]
    u = _dot(h, w1_ref[...])
    act = _bf(u * _sigmoid(u) * _dot(h, w3_ref[...]))
    acc_ref[...] += _dot(act, w2_ref[...])

    @pl.when(f == pl.num_programs(1) - 1)
    def _():
        o_ref[...] = x_ref[...] + acc_ref[...]


def _ffn(x, gain, w1, w3, w2, tm, tf):
    t, d = x.shape
    ff = w1.shape[1]
    tok = pl.BlockSpec((tm, d), lambda i, f: (i, 0))
    return pl.pallas_call(
        _ffn_kernel,
        grid=(t // tm, ff // tf),
        in_specs=[tok, _const_spec(gain.shape),
                  pl.BlockSpec((d, tf), lambda i, f: (0, f)),
                  pl.BlockSpec((d, tf), lambda i, f: (0, f)),
                  pl.BlockSpec((tf, d), lambda i, f: (f, 0))],
        out_specs=tok,
        out_shape=jax.ShapeDtypeStruct((t, d), F32),
        scratch_shapes=[pltpu.VMEM((tm, d), BF16), pltpu.VMEM((tm, d), F32)],
        compiler_params=_params("parallel", "arbitrary"),
        name="ffn_dense",
    )(x, gain, w1, w3, w2)


def _attn_proj_kernel(x_ref, gains_ref, wq_ref, wkv_ref, q_ref, k_ref, v_ref):
    x = x_ref[...]
    n = x * lax.rsqrt(jnp.mean(x * x, axis=-1, keepdims=True) + RMS_EPS)
    gains = gains_ref[...]
    qw = q_ref.shape[1]
    q = _dot(_bf(n * gains[0:1]), wq_ref[...])
    kv = _dot(_bf(n * gains[1:2]), wkv_ref[...])
    q_ref[...] = _bf(q * (1.0 / HEAD ** 0.5))
    k_ref[...] = _bf(kv[:, :qw])
    v_ref[...] = _bf(kv[:, qw:])


def _attn_proj(x, gains, wq, wkv, tm):
    t, d = x.shape
    qw = wq.shape[1]
    tok = pl.BlockSpec((tm, d), lambda i: (i, 0))
    out = pl.BlockSpec((tm, qw), lambda i: (i, 0))
    return pl.pallas_call(
        _attn_proj_kernel,
        grid=(t // tm,),
        in_specs=[tok, _const_spec(gains.shape), _const_spec(wq.shape), _const_spec(wkv.shape)],
        out_specs=[out] * 3,
        out_shape=[jax.ShapeDtypeStruct((t, qw), BF16)] * 3,
        compiler_params=_params("parallel"),
        name="attn_proj",
    )(x, gains, wq, wkv)


def _attn_kernel(slopes_ref, q_ref, k_ref, v_ref, o_ref, lse_ref, *, branch, dilation, qb):
    sub_len = q_ref.shape[1]
    n_blk = sub_len // qb
    head_base = branch * HEADS_PER_BRANCH + 2 * pl.program_id(2)
    rel = (lax.broadcasted_iota(jnp.int32, (qb, qb), 0)
           - lax.broadcasted_iota(jnp.int32, (qb, qb), 1)).astype(F32)
    lane = lax.broadcasted_iota(jnp.int32, (qb, LANES), 1)
    head0 = lane < HEAD
    lane_row = lax.broadcasted_iota(jnp.int32, (1, LANES), 1)
    head_mask = [jnp.where(lane_row < HEAD, 1.0, 0.0).astype(BF16),
                 jnp.where(lane_row < HEAD, 0.0, 1.0).astype(BF16)]
    bias_d, bias_p = [], []
    for j in range(2):
        step = slopes_ref[head_base + j] * float(dilation)
        bias_d.append(jnp.where(rel >= 0, -step * rel, NEG))
        bias_p.append(jnp.where(rel <= 0, -step * (rel + float(qb)), NEG))

    def block(i, carry):
        cur = pl.ds(pl.multiple_of(i * qb, qb), qb)
        q = q_ref[0, cur, :]
        kd = k_ref[0, cur, :]
        vd = v_ref[0, cur, :]
        if n_blk > 1:
            prv = pl.ds(pl.multiple_of(jnp.maximum(i - 1, 0) * qb, qb), qb)
            kp = k_ref[0, prv, :]
            vp = v_ref[0, prv, :]
        outs, lses = [], []
        for j in range(2):
            qh = q * head_mask[j]
            sd =_dot_nt(qh, kd) + bias_d[j]
            m = jnp.max(sd, axis=-1, keepdims=True)
            if n_blk > 1:
                sp = _dot_nt(qh, kp) + jnp.where(i > 0, bias_p[j], NEG)
                m = jnp.maximum(m, jnp.max(sp, axis=-1, keepdims=True))
            pd = jnp.exp(sd - m)
            den = jnp.sum(pd, axis=-1, keepdims=True)
            acc = _dot(_bf(pd), vd)
            if n_blk > 1:
                pp = jnp.exp(sp - m)
                den = den + jnp.sum(pp, axis=-1, keepdims=True)
                acc = acc + _dot(_bf(pp), vp)
            outs.append(acc / den)
            lses.append(m + jnp.log(den))
        o_ref[0, cur, :] = jnp.where(head0, outs[0], outs[1])
        lse_ref[0, cur, :] = jnp.where(head0, lses[0], lses[1])
        return carry

    lax.fori_loop(0, n_blk, block, 0)


def _attn_branch(slopes, q, k, v, branch, b, s):
    window, dilation = BRANCHES[branch]
    assert window % dilation == 0 and window // dilation == LANES
    width = HEADS_PER_BRANCH * HEAD
    tiles_q = q.shape[1] // LANES
    tiles_o = width // LANES
    sub_len = s // dilation
    qb = min(LANES, sub_len)
    assert sub_len % qb == 0
    view = lambda arr: arr.reshape(b, sub_len, dilation * arr.shape[1])
    in_spec = pl.BlockSpec(
        (1, sub_len, LANES), lambda bi, ri, hp: (bi, 0, ri * tiles_q + branch * tiles_o + hp))
    out_spec = pl.BlockSpec((1, sub_len, LANES), lambda bi, ri, hp: (bi, 0, ri * tiles_o + hp))
    out = jax.ShapeDtypeStruct((b, sub_len, dilation * width), F32)
    o, lse = pl.pallas_call(
        functools.partial(_attn_kernel, branch=branch, dilation=dilation, qb=qb),
        grid=(b, dilation, tiles_o),
        in_specs=[pl.BlockSpec(memory_space=pltpu.SMEM)] + [in_spec] * 3,
        out_specs=[out_spec] * 2,
        out_shape=[out, out],
        compiler_params=_params("parallel", "parallel", "parallel"),
        name=f"dilated_attn_{branch}",
    )(slopes, view(q), view(k), view(v))
    return o.reshape(b * s, width), lse.reshape(b * s, width)


def _attn_out_kernel(o0, o1, o2, l0, l1, l2, x_ref, gain_ref, wo_ref, rhi_ref, rlo_ref, rb_ref,
                     x_out, h_out, gate_out, sel_out, rank_out, cnt_out, cnt_ref):
    i = pl.program_id(0)

    @pl.when(i == 0)
    def _():
        cnt_ref[...] = jnp.zeros_like(cnt_ref)

    ls = [l0[...], l1[...], l2[...]]
    m = jnp.maximum(jnp.maximum(ls[0], ls[1]), ls[2])
    ws = [jnp.exp(l - m) for l in ls]
    o = (ws[0] * o0[...] + ws[1] * o1[...] + ws[2] * o2[...]) / (ws[0] + ws[1] + ws[2])
    x = x_ref[...] + _dot(_bf(o), wo_ref[...])
    x_out[...] = x
    h = _rms(x, gain_ref[...])
    h_out[...] = h
    hhi, hlo = _split2(h)
    logits = (_dot(hhi, rhi_ref[...]) + _dot(hhi, rlo_ref[...]) + _dot(hlo, rhi_ref[...])) + rb_ref[...]
    tm = logits.shape[0]
    lane = lax.broadcasted_iota(jnp.int32, logits.shape, 1)
    m1 = jnp.max(logits, axis=-1, keepdims=True)
    i1 = jnp.min(jnp.where(logits == m1, lane, LANES), axis=-1, keepdims=True)
    rest = jnp.where(lane == i1, -3e38, logits)
    m2 = jnp.max(rest, axis=-1, keepdims=True)
    i2 = jnp.min(jnp.where(rest == m2, lane, LANES), axis=-1, keepdims=True)
    ex = jnp.exp(m2 - m1)
    gate_out[...] = (jnp.where(lane == i1, 1.0 / (1.0 + ex), 0.0)
                     + jnp.where(lane == i2, ex / (1.0 + ex), 0.0))
    sel = jnp.where((lane == i1) | (lane == i2), 1.0, 0.0)
    sel_out[...] = sel
    rr = lax.broadcasted_iota(jnp.int32, (tm, tm), 0)
    cc = lax.broadcasted_iota(jnp.int32, (tm, tm), 1)
    before = jnp.where(rr > cc, 1.0, 0.0).astype(BF16)
    rank_out[...] = cnt_ref[...] + _dot(before, _bf(sel))
    cnt_ref[...] += jnp.sum(sel, axis=0, keepdims=True)
    cnt_out[...] = cnt_ref[...]


def _attn_out(os_, ls_, x, gain, wo, rhi, rlo, rb, tm):
    t, d = x.shape
    width = os_[0].shape[1]
    tok = pl.BlockSpec((tm, d), lambda i: (i, 0))
    att = pl.BlockSpec((tm, width), lambda i: (i, 0))
    small = pl.BlockSpec((tm, LANES), lambda i: (i, 0))
    consts = [gain, wo, rhi, rlo, rb]
    return pl.pallas_call(
        _attn_out_kernel,
        grid=(t // tm,),
        in_specs=[att] * 6 + [tok] + [_const_spec(c.shape) for c in consts],
        out_specs=[tok, tok, small, small, small, _const_spec((1, LANES))],
        out_shape=[jax.ShapeDtypeStruct((t, d), F32), jax.ShapeDtypeStruct((t, d), F32),
                   jax.ShapeDtypeStruct((t, LANES), F32), jax.ShapeDtypeStruct((t, LANES), F32),
                   jax.ShapeDtypeStruct((t, LANES), F32), jax.ShapeDtypeStruct((1, LANES), F32)],
        scratch_shapes=[pltpu.VMEM((1, LANES), F32)],
        compiler_params=_params("arbitrary"),
        name="attn_out_route",
    )(*os_, *ls_, x, *consts)


def _row_copy(src, src_row, dst, dst_row, sem):
    return pltpu.make_async_copy(src.at[pl.ds(src_row, 1)], dst.at[pl.ds(dst_row, 1)], sem)


def _moe_scatter_kernel(dest_ref, h_ref, buf_in, buf_out, sem):
    del buf_in
    n_tok = h_ref.shape[0]

    def issue(j, carry):
        for slot in range(2):
            _row_copy(h_ref, j, buf_out, dest_ref[2 * j + slot], sem).start()
        return carry

    def drain(j, carry):
        for slot in range(2):
            _row_copy(h_ref, j, buf_out, dest_ref[2 * j + slot], sem).wait()
        return carry

    lax.fori_loop(0, n_tok, issue, 0)
    lax.fori_loop(0, n_tok, drain, 0)


def _moe_scatter(dest_flat, h, rows, ts):
    t, d = h.shape
    return pl.pallas_call(
        _moe_scatter_kernel,
        grid=(t // ts,),
        in_specs=[pl.BlockSpec((2 * ts,), lambda i: (i,), memory_space=pltpu.SMEM),
                  pl.BlockSpec((ts, d), lambda i: (i, 0)),
                  pl.BlockSpec(memory_space=pl.ANY)],
        out_specs=pl.BlockSpec(memory_space=pl.ANY),
        out_shape=jax.ShapeDtypeStruct((rows, d), F32),
        scratch_shapes=[pltpu.SemaphoreType.DMA(())],
        input_output_aliases={2: 0},
        compiler_params=_params("arbitrary"),
        name="moe_scatter",
    )(dest_flat, h, jnp.zeros((rows, d), F32))


def _moe_ffn_kernel(src_ref, te_ref, nv_ref, x_ref, w1_ref, w3_ref, w2_ref, o_ref, acc_ref):
    del src_ref, te_ref
    i = pl.program_id(0)
    f = pl.program_id(1)

    @pl.when(i < nv_ref[0])
    def _():
        @pl.when(f == 0)
        def _():
            acc_ref[...] = jnp.zeros_like(acc_ref)

        h = _bf(x_ref[...])
        u = _dot(h, w1_ref[0])
        act = _bf(u * _sigmoid(u) * _dot(h, w3_ref[0]))
        acc_ref[...] += _dot(act, w2_ref[0])

        @pl.when(f == pl.num_programs(1) - 1)
        def _():
            o_ref[...] = acc_ref[...]

    @pl.when(i >= nv_ref[0])
    def _():
        o_ref[...] = jnp.zeros_like(o_ref)


def _moe_ffn(tile_src, tile_expert, n_valid, xs, w1, w3, w2, tm, tf):
    rows, d = xs.shape
    ff = w1.shape[2]
    tok = pl.BlockSpec((tm, d), lambda i, f, src, te, nv: (src[i], 0))
    return pl.pallas_call(
        _moe_ffn_kernel,
        grid_spec=pltpu.PrefetchScalarGridSpec(
            num_scalar_prefetch=3,
            grid=(rows // tm, ff // tf),
            in_specs=[tok,
                      pl.BlockSpec((1, d, tf), lambda i, f, src, te, nv: (te[i], 0, f)),
                      pl.BlockSpec((1, d, tf), lambda i, f, src, te, nv: (te[i], 0, f)),
                      pl.BlockSpec((1, tf, d), lambda i, f, src, te, nv: (te[i], f, 0))],
            out_specs=pl.BlockSpec((tm, d), lambda i, f, src, te, nv: (i, 0)),
            scratch_shapes=[pltpu.VMEM((tm, d), F32)],
        ),
        out_shape=jax.ShapeDtypeStruct((rows, d), F32),
        compiler_params=_params("arbitrary", "arbitrary"),
        name="moe_experts",
    )(tile_src, tile_expert, n_valid, xs, w1, w3, w2)


def _moe_combine_kernel(dest_ref, x_ref, gate_ref, gain_ref, y_hbm, o_ref, buf, sem):
    n_tok = x_ref.shape[0]

    def issue(j, carry):
        for slot in range(2):
            _row_copy(y_hbm, dest_ref[2 * j + slot], buf.at[slot], j, sem).start()
        return carry

    def drain(j, carry):
        for slot in range(2):
            _row_copy(y_hbm, dest_ref[2 * j + slot], buf.at[slot], j, sem).wait()
        return carry

    lax.fori_loop(0, n_tok, issue, 0)
    lax.fori_loop(0, n_tok, drain, 0)
    gate = gate_ref[...]
    x = x_ref[...] + gate[:, 0:1] * buf[0] + gate[:, 1:2] * buf[1]
    o_ref[...] = _rms(x, gain_ref[...])


def _moe_combine(dest_flat, x, gates, gain, ys, ts):
    t, d = x.shape
    tok = pl.BlockSpec((ts, d), lambda i: (i, 0))
    return pl.pallas_call(
        _moe_combine_kernel,
        grid=(t // ts,),
        in_specs=[pl.BlockSpec((2 * ts,), lambda i: (i,), memory_space=pltpu.SMEM),
                  tok, pl.BlockSpec((ts, 2), lambda i: (i, 0)), _const_spec(gain.shape),
                  pl.BlockSpec(memory_space=pl.ANY)],
        out_specs=tok,
        out_shape=jax.ShapeDtypeStruct((t, d), F32),
        scratch_shapes=[pltpu.VMEM((2, ts, d), F32), pltpu.SemaphoreType.DMA(())],
        compiler_params=_params("arbitrary"),
        name="moe_combine",
    )(dest_flat, x, gates, gain, ys)


def _tile(n, want):
    t = min(n, want)
    assert n % t == 0
    return t


def _pad_cols(w, n):
    return jnp.pad(w, ((0, 0), (0, n - w.shape[1])))


def _pad_rows(w, n):
    return jnp.pad(w, ((0, n - w.shape[0]), (0, 0)))


def kernel(x, norm_gain, rwkv_mu, rwkv_wr, rwkv_wk, rwkv_wv, rwkv_w0, rwkv_w1, rwkv_w2, rwkv_a0, rwkv_a1, rwkv_a2, rwkv_g1, rwkv_g2, rwkv_k_k, rwkv_k_a, rwkv_r_k, rwkv_lnx_w, rwkv_lnx_b, rwkv_wo, kv_norm_gain, w_kv, attn_wq, attn_wo, ffn_w1, ffn_w3, ffn_w2, moe_router, moe_router_bias, moe_w1, moe_w3, moe_w2, final_norm_gain):
    b, s, d = x.shape
    t = b * s
    n_heads = d // HEAD
    assert norm_gain.shape[0] == 2 and d % LANES == 0 and s % 8 == 0

    chan_head = jnp.arange(d) // HEAD
    e = (chan_head[:, None] == jnp.arange(LANES)[None, :]).astype(BF16)
    et = e.T

    zeros = jnp.zeros((d,), F32)
    vec_pre = jnp.stack([norm_gain[0, 0], rwkv_w0[0], rwkv_a0[0], rwkv_k_k[0], rwkv_k_a[0],
                         zeros, zeros, zeros])
    tm_pre = _tile(s, 256)
    r, lw, k, v, kk, a, g = _rwkv_pre(
        x, vec_pre, rwkv_mu[0], _bf(rwkv_wr[0]), _bf(rwkv_wk[0]), _bf(rwkv_wv[0]),
        _bf(_pad_cols(rwkv_w1[0], LANES)), _bf(_pad_rows(rwkv_w2[0], LANES)),
        _bf(_pad_cols(rwkv_a1[0], LANES)), _bf(_pad_rows(rwkv_a2[0], LANES)),
        _bf(rwkv_g1[0]), _bf(rwkv_g2[0]), e, et, tm_pre)
    y = _wkv(r, lw, k, v, kk, a, chunk=_tile(s, LANES))

    flat = lambda arr: arr.reshape(t, d)
    tm = _tile(t, 512)
    vec_post = jnp.stack([rwkv_lnx_w[0], rwkv_lnx_b[0], rwkv_r_k[0].reshape(d),
                          zeros, zeros, zeros, zeros, zeros])
    x1 = _rwkv_post(flat(y), flat(r), flat(k), flat(v), flat(g), flat(x), vec_post,
                    _bf(rwkv_wo[0]), e, et, tm)

    ff = ffn_w1.shape[2]
    tf = ff // 2 if (ff // 2) % LANES == 0 else ff
    x2 = _ffn(x1, norm_gain[0, 1][None, :], _bf(ffn_w1[0]), _bf(ffn_w3[0]), _bf(ffn_w2[0]), tm, tf)

    n_slopes = len(BRANCHES) * HEADS_PER_BRANCH
    slopes = jnp.exp2(-ALIBI_MAX * (jnp.arange(n_slopes, dtype=F32) + 1.0) / n_slopes)
    gains = jnp.stack([norm_gain[1, 0], kv_norm_gain] + [zeros] * 6)
    q, kx, vx = _attn_proj(x2, gains, _bf(attn_wq[0]), _bf(w_kv), tm)
    os_, ls_ = [], []
    for br in range(len(BRANCHES)):
        o_br, l_br = _attn_branch(slopes, q, kx, vx, br, b, s)
        os_.append(o_br)
        ls_.append(l_br)

    router = _pad_cols(moe_router[0], LANES)
    rhi, rlo = _split2(router)
    rbias = jnp.full((1, LANES), NEG, F32).at[0, :N_EXPERTS].set(moe_router_bias[0])
    x3, h4, gate, sel, rank, cnt = _attn_out(os_, ls_, x2, norm_gain[1, 1][None, :], _bf(attn_wo[0]),
                                             rhi, rlo, rbias, tm)

    tm_e = 512 if t >= 4096 else 128
    ts = _tile(t, 512)
    sel8 = sel[:, :N_EXPERTS] > 0.5
    counts = cnt[0, :N_EXPERTS].astype(jnp.int32)
    padded = ((counts + tm_e - 1) // tm_e) * tm_e
    ends = jnp.cumsum(padded)
    dest8 = (ends - padded)[None, :] + rank[:, :N_EXPERTS].astype(jnp.int32)
    e0 = jnp.argmax(sel8, axis=1)
    e1 = N_EXPERTS - 1 - jnp.argmax(sel8[:, ::-1], axis=1)
    pick = lambda arr: jnp.stack([jnp.take_along_axis(arr, e0[:, None], 1)[:, 0],
                                  jnp.take_along_axis(arr, e1[:, None], 1)[:, 0]], axis=1)
    dest = pick(dest8).reshape(2 * t)
    gates = pick(gate[:, :N_EXPERTS])
    rows = 2 * t + N_EXPERTS * tm_e
    n_tiles = rows // tm_e
    n_valid = (ends[-1] // tm_e).astype(jnp.int32)
    tile_src = jnp.minimum(jnp.arange(n_tiles, dtype=jnp.int32), n_valid - 1)
    tile_expert = jnp.minimum(
        jnp.sum(tile_src[:, None] * tm_e >= ends[None, :], axis=1), N_EXPERTS - 1).astype(jnp.int32)

    xs = _moe_scatter(dest, h4, rows, ts)
    ffe = moe_w1.shape[3]
    tfe = ffe // 2 if (ffe // 2) % LANES == 0 else ffe
    ys = _moe_ffn(tile_src, tile_expert, n_valid.reshape(1), xs,
                  _bf(moe_w1[0]), _bf(moe_w3[0]), _bf(moe_w2[0]), tm_e, tfe)
    out = _moe_combine(dest, x3, gates, final_norm_gain[None, :], ys, ts)
    return out.reshape(b, s, d)
```

```python
import functools

import jax
import jax.numpy as jnp
from jax import lax
from jax.experimental import pallas as pl
from jax.experimental.pallas import tpu as pltpu

F32 = jnp.float32
BF16 = jnp.bfloat16

RMS_EPS = 1e-5
GN_EPS = 64e-5
HEAD = 64
LANES = 128
BRANCHES = ((128, 1), (512, 4), (2048, 16))
HEADS_PER_BRANCH = 8
ALIBI_MAX = 8.0
N_EXPERTS = 8
NEG = -1e30
VMEM_LIMIT_BYTES = 56 * 1024 * 1024


def _params(*sem):
    return pltpu.CompilerParams(dimension_semantics=sem, vmem_limit_bytes=VMEM_LIMIT_BYTES)


def _dot(a, b):
    return jnp.dot(a, b, preferred_element_type=F32)


def _dot_nt(a, b):
    return lax.dot_general(a, b, (((1,), (1,)), ((), ())), preferred_element_type=F32)


def _bf(x):
    return x.astype(BF16)


def _split2(x):
    hi = x.astype(BF16)
    lo = (x - hi.astype(F32)).astype(BF16)
    return hi, lo


def _split3(x):
    hi = x.astype(BF16)
    r1 = x - hi.astype(F32)
    mid = r1.astype(BF16)
    lo = (r1 - mid.astype(F32)).astype(BF16)
    return hi, mid, lo


def _sigmoid(z):
    return 1.0 / (1.0 + jnp.exp(-z))


def _softplus(z):
    return jnp.maximum(z, 0.0) + jnp.log(1.0 + jnp.exp(-jnp.abs(z)))


def _rms(x, gain):
    return x * lax.rsqrt(jnp.mean(x * x, axis=-1, keepdims=True) + RMS_EPS) * gain


def _headsum(x, e, et):
    hi, lo = _split2(x)
    s = _dot(hi, e) + _dot(lo, e)
    shi, slo = _split2(s)
    return _dot(shi, et) + _dot(slo, et)


def _const_spec(shape):
    nd = len(shape)
    return pl.BlockSpec(shape, lambda *_: (0,) * nd)


def _rwkv_pre_kernel(x_ref, xp_ref, vec_ref, mu_ref, wr_ref, wk_ref, wv_ref, w1_ref, w2_ref,
                     a1_ref, a2_ref, g1_ref, g2_ref, e_ref, et_ref,
                     r_out, lw_out, k_out, v_out, kk_out, a_out, g_out):
    i = pl.program_id(1)
    vec = vec_ref[...]
    gain, w0, a0, k_k, k_a = (vec[j:j + 1] for j in range(5))
    x = x_ref[0]
    h = _rms(x, gain)
    hp = _rms(xp_ref[0][7:8, :], gain)
    hp = jnp.where(i > 0, hp, 0.0)
    rows = lax.broadcasted_iota(jnp.int32, h.shape, 0)
    hprev = jnp.where(rows == 0, hp, pltpu.roll(h, 1, 0))
    xx = hprev - h
    mu = mu_ref[...]
    xr, xw, xk, xv, xa, xg = (_bf(h + xx * mu[j:j + 1]) for j in range(6))
    r = _dot(xr, wr_ref[...])
    k = _dot(xk, wk_ref[...])
    v = _dot(xv, wv_ref[...])
    wl = w0 + _dot(_bf(jnp.tanh(_dot(xw, w1_ref[...]))), w2_ref[...])
    w_log = -_softplus(-wl) - 0.5
    a = _sigmoid(a0 + _dot(_bf(_dot(xa, a1_ref[...])), a2_ref[...]))
    g = _dot(_bf(_sigmoid(_dot(xg, g1_ref[...]))), g2_ref[...])
    kkr = k * k_k
    ss = _headsum(kkr * kkr, e_ref[...], et_ref[...])
    r_out[0] = r
    lw_out[0] = -jnp.exp(w_log)
    k_out[0] = k * (1.0 + (a - 1.0) * k_a)
    v_out[0] = v
    kk_out[0] = kkr / jnp.maximum(jnp.sqrt(ss), 1e-12)
    a_out[0] = a
    g_out[0] = g


def _rwkv_pre(x, vec, mu, wr, wk, wv, w1, w2, a1, a2, g1, g2, e, et, tm):
    b, s, d = x.shape
    tok = pl.BlockSpec((1, tm, d), lambda bi, i: (bi, i, 0))
    prev = pl.BlockSpec((1, 8, d), lambda bi, i: (bi, jnp.maximum(i * (tm // 8) - 1, 0), 0))
    consts = [vec, mu, wr, wk, wv, w1, w2, a1, a2, g1, g2, e, et]
    out = jax.ShapeDtypeStruct((b, s, d), F32)
    return pl.pallas_call(
        _rwkv_pre_kernel,
        grid=(b, s // tm),
        in_specs=[tok, prev] + [_const_spec(c.shape) for c in consts],
        out_specs=[tok] * 7,
        out_shape=[out] * 7,
        compiler_params=_params("parallel", "arbitrary"),
        name="rwkv_pre",
    )(x, x, *consts)


def _round_robin(generators):
    results = [None] * len(generators)
    live = list(enumerate(generators))
    while live:
        still = []
        for idx, gen in live:
            try:
                results[idx] = next(gen)
                still.append((idx, gen))
            except StopIteration:
                pass
        live = still
    return results


def _wkv_kernel(r_ref, lw_ref, k_ref, v_ref, kk_ref, a_ref, y_ref, st_ref, *, chunk):
    c_len = chunk
    n_chunks = r_ref.shape[1] // c_len
    n_pairs = r_ref.shape[2] // LANES
    inv_steps = max(c_len.bit_length() - 2, 0)
    lane = lax.broadcasted_iota(jnp.int32, (c_len, LANES), 1)
    head0 = lane < HEAD
    rr = lax.broadcasted_iota(jnp.int32, (c_len, c_len), 0)
    cc = lax.broadcasted_iota(jnp.int32, (c_len, c_len), 1)
    lower = rr >= cc
    strict = rr > cc
    ones_lower = jnp.where(lower, 1.0, 0.0).astype(BF16)
    eye = jnp.where(rr == cc, 1.0, 0.0).astype(F32)
    r2 = lax.broadcasted_iota(jnp.int32, (LANES, LANES), 0)
    c2 = lax.broadcasted_iota(jnp.int32, (LANES, LANES), 1)
    same_head = (r2 < HEAD) == (c2 < HEAD)

    def per_head(fn):
        return jnp.where(head0, fn(0), fn(1))

    @pl.when(pl.program_id(2) == 0)
    def _():
        st_ref[...] = jnp.zeros_like(st_ref)

    def chunk_pair(sl, lanes, state):
        lw = lw_ref[0, sl, lanes]
        hi, mid, lo = _split3(lw)
        cum = _dot(ones_lower, hi) + _dot(ones_lower, mid) + _dot(ones_lower, lo)
        yield
        cum_last = cum[c_len - 1:c_len, :]
        kk = kk_ref[0, sl, lanes]
        b = kk * a_ref[0, sl, lanes]
        k = k_ref[0, sl, lanes]
        v = v_ref[0, sl, lanes]
        e_neg = jnp.exp(-cum)
        e_end = jnp.exp(cum_last - cum)
        a_t = -kk * jnp.exp(cum - lw)
        r_t = r_ref[0, sl, lanes] * jnp.exp(cum)
        b_t = _bf(b * e_neg)
        k_t = _bf(k * e_neg)
        vb = _bf(v)
        lhs = _bf(jnp.concatenate([jnp.where(head0, a_t, 0.0), jnp.where(head0, 0.0, a_t),
                                   jnp.where(head0, r_t, 0.0), jnp.where(head0, 0.0, r_t)], axis=0))
        gram_b = _dot_nt(lhs, b_t)
        gram_k = _dot_nt(lhs, k_t)
        st_b = _bf(state)
        w0 = _dot_nt(_bf(a_t), st_b)
        y0 = _dot_nt(_bf(r_t), st_b)
        yield
        rows = lambda g, j: g[j * c_len:(j + 1) * c_len]
        l_ak = [_bf(jnp.where(strict, rows(gram_k, hd), 0.0)) for hd in range(2)]
        m_rb = [_bf(jnp.where(lower, rows(gram_b, 2 + hd), 0.0)) for hd in range(2)]
        m_rk = [_bf(jnp.where(lower, rows(gram_k, 2 + hd), 0.0)) for hd in range(2)]
        pw = [jnp.where(strict, rows(gram_b, hd), 0.0) for hd in range(2)]
        inv = [eye + pw[hd] for hd in range(2)]
        w = w0 + per_head(lambda hd: _dot(l_ak[hd], vb))
        y1 = y0 + per_head(lambda hd: _dot(m_rk[hd], vb))
        for _ in range(inv_steps):
            pb = [_bf(pw[hd]) for hd in range(2)]
            pw = [_dot(pb[hd], pb[hd]) for hd in range(2)]
            yield
            inv = [inv[hd] + _dot(_bf(inv[hd]), _bf(pw[hd])) for hd in range(2)]
        yield
        wb = _bf(w)
        u = per_head(lambda hd: _dot(_bf(inv[hd]), wb))
        yield
        ub = _bf(u)
        y = y1 + per_head(lambda hd: _dot(m_rb[hd], ub))
        uv_t = _bf(jnp.transpose(jnp.concatenate([u, v], axis=0)))
        bk = _bf(jnp.concatenate([b * e_end, k * e_end], axis=0))
        yield y, state * jnp.exp(cum_last) + jnp.where(same_head, _dot(uv_t, bk), 0.0)

    def body(c, carry):
        sl = pl.ds(pl.multiple_of(c * c_len, c_len), c_len)
        lanes = [slice(p * LANES, (p + 1) * LANES) for p in range(n_pairs)]
        results = _round_robin([chunk_pair(sl, lanes[p], st_ref[p]) for p in range(n_pairs)])
        for p, (y, new_state) in enumerate(results):
            y_ref[0, sl, lanes[p]] = y
            st_ref[p] = new_state
        return carry

    lax.fori_loop(0, n_chunks, body, 0)


def _wkv(r, lw, k, v, kk, a, chunk, seq_tile, n_pairs):
    b, s, d = r.shape
    width = n_pairs * LANES
    spec = pl.BlockSpec((1, seq_tile, width), lambda bi, hi, si: (bi, si, hi))
    return pl.pallas_call(
        functools.partial(_wkv_kernel, chunk=chunk),
        grid=(b, d // width, s // seq_tile),
        in_specs=[spec] * 6,
        out_specs=spec,
        out_shape=jax.ShapeDtypeStruct((b, s, d), F32),
        scratch_shapes=[pltpu.VMEM((n_pairs, LANES, LANES), F32)],
        compiler_params=_params("parallel", "parallel", "arbitrary"),
        name="wkv7",
    )(r, lw, k, v, kk, a)


def _rwkv_post_kernel(y_ref, r_ref, k_ref, v_ref, g_ref, x_ref, vec_ref, wo_ref, e_ref, et_ref, o_ref):
    vec = vec_ref[...]
    lnx_w, lnx_b, r_k = (vec[j:j + 1] for j in range(3))
    e = e_ref[...]
    et = et_ref[...]
    y = y_ref[...]
    v = v_ref[...]
    mean = _headsum(y, e, et) * (1.0 / HEAD)
    dy = y - mean
    var = _headsum(dy * dy, e, et) * (1.0 / HEAD)
    yn = dy * lax.rsqrt(var + GN_EPS) * lnx_w + lnx_b
    bonus = _headsum(r_ref[...] * k_ref[...] * r_k, e, et) * v
    o_ref[...] = x_ref[...] + _dot(_bf((yn + bonus) * g_ref[...]), wo_ref[...])


def _rwkv_post(y, r, k, v, g, x, vec, wo, e, et, tm):
    t, d = x.shape
    tok = pl.BlockSpec((tm, d), lambda i: (i, 0))
    consts = [vec, wo, e, et]
    return pl.pallas_call(
        _rwkv_post_kernel,
        grid=(t // tm,),
        in_specs=[tok] * 6 + [_const_spec(c.shape) for c in consts],
        out_specs=tok,
        out_shape=jax.ShapeDtypeStruct((t, d), F32),
        compiler_params=_params("parallel"),
        name="rwkv_post",
    )(y, r, k, v, g, x, *consts)


def _ffn_kernel(x_ref, gain_ref, w1_ref, w3_ref, w2_ref, o_ref, h_ref, acc_ref):
    f = pl.program_id(1)

    @pl.when(f == 0)
    def _():
        h_ref[...] = _bf(_rms(x_ref[...], gain_ref[...]))
        acc_ref[...] = jnp.zeros_like(acc_ref)

    h = h_ref[...]
    u = _dot(h, w1_ref[...])
    act = _bf(u * _sigmoid(u) * _dot(h, w3_ref[...]))
    acc_ref[...] += _dot(act, w2_ref[...])

    @pl.when(f == pl.num_programs(1) - 1)
    def _():
        o_ref[...] = x_ref[...] + acc_ref[...]


def _ffn(x, gain, w1, w3, w2, tm, tf):
    t, d = x.shape
    ff = w1.shape[1]
    tok = pl.BlockSpec((tm, d), lambda i, f: (i, 0))
    return pl.pallas_call(
        _ffn_kernel,
        grid=(t // tm, ff // tf),
        in_specs=[tok, _const_spec(gain.shape),
                  pl.BlockSpec((d, tf), lambda i, f: (0, f)),
                  pl.BlockSpec((d, tf), lambda i, f: (0, f)),
                  pl.BlockSpec((tf, d), lambda i, f: (f, 0))],
        out_specs=tok,
        out_shape=jax.ShapeDtypeStruct((t, d), F32),
        scratch_shapes=[pltpu.VMEM((tm, d), BF16), pltpu.VMEM((tm, d), F32)],
        compiler_params=_params("parallel", "arbitrary"),
        name="ffn_dense",
    )(x, gain, w1, w3, w2)


BRANCH_WIDTH = HEADS_PER_BRANCH * HEAD


def _attn_proj_kernel(x_ref, gains_ref, wq_ref, wkv_ref, *refs):
    outs, (q_scr, kv_scr) = refs[:-2], refs[-2:]
    x = x_ref[0]
    tm = x.shape[0]
    n = x * lax.rsqrt(jnp.mean(x * x, axis=-1, keepdims=True) + RMS_EPS)
    gains = gains_ref[...]
    q_tiles = q_scr.shape[0]
    q = _dot(_bf(n * gains[0:1]), wq_ref[...]) * (1.0 / HEAD ** 0.5)
    kv = _dot(_bf(n * gains[1:2]), wkv_ref[...])
    for ti in range(q_tiles):
        q_scr[ti] = q[:, ti * LANES:(ti + 1) * LANES]
    for ti in range(2 * q_tiles):
        kv_scr[ti] = kv[:, ti * LANES:(ti + 1) * LANES]
    tiles_br = BRANCH_WIDTH // LANES
    for br, (_, dil) in enumerate(BRANCHES):
        q_out, k_out, v_out = outs[3 * br:3 * br + 3]
        for res in range(dil):
            rows = pl.ds(res, tm // dil, stride=dil) if dil > 1 else slice(None)
            for ti in range(tiles_br):
                src = br * tiles_br + ti
                lanes = slice(ti * LANES, (ti + 1) * LANES)
                q_out[0, res, :, lanes] = _bf(q_scr[src, rows, :])
                k_out[0, res, :, lanes] = _bf(kv_scr[src, rows, :])
                v_out[0, res, :, lanes] = _bf(kv_scr[q_tiles + src, rows, :])


def _attn_proj(x, gains, wq, wkv, tm):
    b, s, d = x.shape
    qw = wq.shape[1]
    out_specs, out_shape = [], []
    for _, dil in BRANCHES:
        assert tm % (16 * dil) == 0
        out_specs += [pl.BlockSpec((1, dil, tm // dil, BRANCH_WIDTH), lambda bi, i: (bi, 0, i, 0))] * 3
        out_shape += [jax.ShapeDtypeStruct((b, dil, s // dil, BRANCH_WIDTH), BF16)] * 3
    return pl.pallas_call(
        _attn_proj_kernel,
        grid=(b, s // tm),
        in_specs=[pl.BlockSpec((1, tm, d), lambda bi, i: (bi, i, 0)),
                  _const_spec(gains.shape), _const_spec(wq.shape), _const_spec(wkv.shape)],
        out_specs=out_specs,
        out_shape=out_shape,
        scratch_shapes=[pltpu.VMEM((qw // LANES, tm, LANES), F32),
                        pltpu.VMEM((2 * qw // LANES, tm, LANES), F32)],
        compiler_params=_params("parallel", "parallel"),
        name="attn_proj",
    )(x, gains, wq, wkv)


def _attn_kernel(slopes_ref, q_ref, k_ref, v_ref, o_ref, lse_ref, *, branch, dilation, qb):
    sub_len = q_ref.shape[2]
    n_blk = sub_len // qb
    n_pairs = q_ref.shape[3] // LANES
    rel = (lax.broadcasted_iota(jnp.int32, (qb, qb), 0)
           - lax.broadcasted_iota(jnp.int32, (qb, qb), 1)).astype(F32)
    dist_d = jnp.where(rel >= 0, rel, -NEG)
    dist_p = jnp.where(rel <= 0, rel + float(qb), -NEG)
    lane = lax.broadcasted_iota(jnp.int32, (qb, LANES), 1)
    head0 = lane < HEAD
    lane_row = lax.broadcasted_iota(jnp.int32, (1, LANES), 1)
    head_mask = [jnp.where(lane_row < HEAD, 1.0, 0.0).astype(BF16),
                 jnp.where(lane_row < HEAD, 0.0, 1.0).astype(BF16)]

    def block(i, carry):
        cur = pl.ds(pl.multiple_of(i * qb, qb), qb)
        if n_blk > 1:
            prv = pl.ds(pl.multiple_of(jnp.maximum(i - 1, 0) * qb, qb), qb)
            has_prev = jnp.where(i > 0, 1.0, -NEG)

        def one_head(p, j):
            lanes = slice(p * LANES, (p + 1) * LANES)
            step = slopes_ref[branch * HEADS_PER_BRANCH + 2 * p + j] * float(dilation)
            qh = q_ref[0, 0, cur, lanes] * head_mask[j]
            sd = _dot_nt(qh, k_ref[0, 0, cur, lanes]) - step * dist_d
            if n_blk > 1:
                sp = _dot_nt(qh, k_ref[0, 0, prv, lanes]) - (step * has_prev) * dist_p
            yield
            m = jnp.max(sd, axis=-1, keepdims=True)
            if n_blk > 1:
                m = jnp.maximum(m, jnp.max(sp, axis=-1, keepdims=True))
            pd = jnp.exp(sd - m)
            den = jnp.sum(pd, axis=-1, keepdims=True)
            acc = _dot(_bf(pd), v_ref[0, 0, cur, lanes])
            if n_blk > 1:
                pp = jnp.exp(sp - m)
                den = den + jnp.sum(pp, axis=-1, keepdims=True)
                acc = acc + _dot(_bf(pp), v_ref[0, 0, prv, lanes])
            yield
            yield acc / den, m + jnp.log(den)

        res = _round_robin([one_head(p, j) for p in range(n_pairs) for j in range(2)])
        for p in range(n_pairs):
            lanes = slice(p * LANES, (p + 1) * LANES)
            (o_a, l_a), (o_b, l_b) = res[2 * p], res[2 * p + 1]
            o_ref[0, 0, cur, lanes] = jnp.where(head0, o_a, o_b)
            lse_ref[0, 0, cur, lanes] = jnp.where(head0, l_a, l_b)
        return carry

    lax.fori_loop(0, n_blk, block, 0)


def _attn_branch(slopes, q, k, v, branch):
    window, dilation = BRANCHES[branch]
    assert window % dilation == 0 and window // dilation == LANES
    b, _, sub_len, width = q.shape
    qb = min(LANES, sub_len)
    assert sub_len % qb == 0
    spec = pl.BlockSpec((1, 1, sub_len, width), lambda bi, ri: (bi, ri, 0, 0))
    out = jax.ShapeDtypeStruct(q.shape, F32)
    return pl.pallas_call(
        functools.partial(_attn_kernel, branch=branch, dilation=dilation, qb=qb),
        grid=(b, dilation),
        in_specs=[pl.BlockSpec(memory_space=pltpu.SMEM)] + [spec] * 3,
        out_specs=[spec] * 2,
        out_shape=[out, out],
        compiler_params=_params("parallel", "parallel"),
        name=f"dilated_attn_{branch}",
    )(slopes, q, k, v)


def _attn_out_kernel(o0, o1, o2, l0, l1, l2, x_ref, gain_ref, wo_ref, rhi_ref, rlo_ref, rb_ref,
                     x_out, h_out, gate_out, sel_out, rank_out, cnt_out, cnt_ref, o_scr, l_scr):
    @pl.when((pl.program_id(0) == 0) & (pl.program_id(1) == 0))
    def _():
        cnt_ref[...] = jnp.zeros_like(cnt_ref)

    tm = x_ref.shape[0]
    tiles_br = BRANCH_WIDTH // LANES
    for br, (o_ref, l_ref) in enumerate(((o0, l0), (o1, l1), (o2, l2))):
        dil = BRANCHES[br][1]
        for res in range(dil):
            rows = pl.ds(res, tm // dil, stride=dil) if dil > 1 else slice(None)
            for ti in range(tiles_br):
                lanes = slice(ti * LANES, (ti + 1) * LANES)
                o_scr[br, ti, rows, :] = o_ref[0, res, :, lanes]
                l_scr[br, ti, rows, :] = l_ref[0, res, :, lanes]
    merged = []
    for ti in range(tiles_br):
        ls = [l_scr[br, ti] for br in range(3)]
        m = jnp.maximum(jnp.maximum(ls[0], ls[1]), ls[2])
        ws = [jnp.exp(l - m) for l in ls]
        merged.append(_bf((ws[0] * o_scr[0, ti] + ws[1] * o_scr[1, ti] + ws[2] * o_scr[2, ti])
                          / (ws[0] + ws[1] + ws[2])))
    x = x_ref[...] + _dot(jnp.concatenate(merged, axis=1), wo_ref[...])
    x_out[...] = x
    h = _rms(x, gain_ref[...])
    h_out[...] = h
    hhi, hlo = _split2(h)
    logits = (_dot(hhi, rhi_ref[...]) + _dot(hhi, rlo_ref[...]) + _dot(hlo, rhi_ref[...])) + rb_ref[...]
    lane = lax.broadcasted_iota(jnp.int32, logits.shape, 1)
    m1 = jnp.max(logits, axis=-1, keepdims=True)
    i1 = jnp.min(jnp.where(logits == m1, lane, LANES), axis=-1, keepdims=True)
    rest = jnp.where(lane == i1, -3e38, logits)
    m2 = jnp.max(rest, axis=-1, keepdims=True)
    i2 = jnp.min(jnp.where(rest == m2, lane, LANES), axis=-1, keepdims=True)
    ex = jnp.exp(m2 - m1)
    gate_out[...] = (jnp.where(lane == i1, 1.0 / (1.0 + ex), 0.0)
                     + jnp.where(lane == i2, ex / (1.0 + ex), 0.0))
    sel = jnp.where((lane == i1) | (lane == i2), 1.0, 0.0)
    sel_out[...] = sel
    rr = lax.broadcasted_iota(jnp.int32, (tm, tm), 0)
    cc = lax.broadcasted_iota(jnp.int32, (tm, tm), 1)
    before = jnp.where(rr > cc, 1.0, 0.0).astype(BF16)
    rank_out[...] = cnt_ref[...] + _dot(before, _bf(sel))
    cnt_ref[...] += jnp.sum(sel, axis=0, keepdims=True)
    cnt_out[...] = cnt_ref[...]


def _attn_out(os_, ls_, x, gain, wo, rhi, rlo, rb, b, tm):
    t, d = x.shape
    tiles = t // (b * tm)
    tok = pl.BlockSpec((tm, d), lambda bi, i: (bi * tiles + i, 0))
    small = pl.BlockSpec((tm, LANES), lambda bi, i: (bi * tiles + i, 0))
    att = [pl.BlockSpec((1, dil, tm // dil, BRANCH_WIDTH), lambda bi, i: (bi, 0, i, 0))
           for _, dil in BRANCHES]
    consts = [gain, wo, rhi, rlo, rb]
    return pl.pallas_call(
        _attn_out_kernel,
        grid=(b, tiles),
        in_specs=att + att + [tok] + [_const_spec(c.shape) for c in consts],
        out_specs=[tok, tok, small, small, small, _const_spec((1, LANES))],
        out_shape=[jax.ShapeDtypeStruct((t, d), F32), jax.ShapeDtypeStruct((t, d), F32),
                   jax.ShapeDtypeStruct((t, LANES), F32), jax.ShapeDtypeStruct((t, LANES), F32),
                   jax.ShapeDtypeStruct((t, LANES), F32), jax.ShapeDtypeStruct((1, LANES), F32)],
        scratch_shapes=[pltpu.VMEM((1, LANES), F32),
                        pltpu.VMEM((len(BRANCHES), BRANCH_WIDTH // LANES, tm, LANES), F32),
                        pltpu.VMEM((len(BRANCHES), BRANCH_WIDTH // LANES, tm, LANES), F32)],
        compiler_params=_params("arbitrary", "arbitrary"),
        name="attn_out_route",
    )(*os_, *ls_, x, *consts)


DMA_UNROLL = 8


def _row_copy(src, src_row, dst, dst_row, sem):
    return pltpu.make_async_copy(src.at[pl.ds(src_row, 1)], dst.at[pl.ds(dst_row, 1)], sem)


def _moe_scatter_kernel(dest_ref, h_ref, buf_in, buf_out, sem):
    del buf_in
    n_tok = h_ref.shape[0]

    def issue(j, carry):
        for slot in range(2):
            _row_copy(h_ref, j, buf_out, dest_ref[2 * j + slot], sem).start(priority=slot)
        return carry

    def drain(j, carry):
        for slot in range(2):
            _row_copy(h_ref, j, buf_out, dest_ref[2 * j + slot], sem).wait()
        return carry

    lax.fori_loop(0, n_tok, issue, 0, unroll=DMA_UNROLL)
    lax.fori_loop(0, n_tok, drain, 0, unroll=DMA_UNROLL)


def _moe_scatter(dest_flat, h, rows, ts):
    t, d = h.shape
    return pl.pallas_call(
        _moe_scatter_kernel,
        grid=(t // ts,),
        in_specs=[pl.BlockSpec((2 * ts,), lambda i: (i,), memory_space=pltpu.SMEM),
                  pl.BlockSpec((ts, d), lambda i: (i, 0)),
                  pl.BlockSpec(memory_space=pl.ANY)],
        out_specs=pl.BlockSpec(memory_space=pl.ANY),
        out_shape=jax.ShapeDtypeStruct((rows, d), F32),
        scratch_shapes=[pltpu.SemaphoreType.DMA(())],
        input_output_aliases={2: 0},
        compiler_params=_params("arbitrary"),
        name="moe_scatter",
    )(dest_flat, h, jnp.zeros((rows, d), F32))


def _moe_ffn_kernel(src_ref, te_ref, nv_ref, x_ref, w1_ref, w3_ref, w2_ref, o_ref, acc_ref):
    del src_ref, te_ref
    i = pl.program_id(0)
    f = pl.program_id(1)

    @pl.when(i < nv_ref[0])
    def _():
        @pl.when(f == 0)
        def _():
            acc_ref[...] = jnp.zeros_like(acc_ref)

        h = _bf(x_ref[...])
        u = _dot(h, w1_ref[0])
        act = _bf(u * _sigmoid(u) * _dot(h, w3_ref[0]))
        acc_ref[...] += _dot(act, w2_ref[0])

        @pl.when(f == pl.num_programs(1) - 1)
        def _():
            o_ref[...] = acc_ref[...]

    @pl.when(i >= nv_ref[0])
    def _():
        o_ref[...] = jnp.zeros_like(o_ref)


def _moe_ffn(tile_src, tile_expert, n_valid, xs, w1, w3, w2, tm, tf):
    rows, d = xs.shape
    ff = w1.shape[2]
    tok = pl.BlockSpec((tm, d), lambda i, f, src, te, nv: (src[i], 0))
    return pl.pallas_call(
        _moe_ffn_kernel,
        grid_spec=pltpu.PrefetchScalarGridSpec(
            num_scalar_prefetch=3,
            grid=(rows // tm, ff // tf),
            in_specs=[tok,
                      pl.BlockSpec((1, d, tf), lambda i, f, src, te, nv: (te[i], 0, f)),
                      pl.BlockSpec((1, d, tf), lambda i, f, src, te, nv: (te[i], 0, f)),
                      pl.BlockSpec((1, tf, d), lambda i, f, src, te, nv: (te[i], f, 0))],
            out_specs=pl.BlockSpec((tm, d), lambda i, f, src, te, nv: (i, 0)),
            scratch_shapes=[pltpu.VMEM((tm, d), F32)],
        ),
        out_shape=jax.ShapeDtypeStruct((rows, d), F32),
        compiler_params=_params("arbitrary", "arbitrary"),
        name="moe_experts",
    )(tile_src, tile_expert, n_valid, xs, w1, w3, w2)


def _moe_combine_kernel(dest_ref, x_ref, gate_ref, gain_ref, y_hbm, o_ref, buf, sem):
    n_tok = x_ref.shape[0]

    def issue(j, carry):
        for slot in range(2):
            _row_copy(y_hbm, dest_ref[2 * j + slot], buf.at[slot], j, sem).start(priority=slot)
        return carry

    def drain(j, carry):
        for slot in range(2):
            _row_copy(y_hbm, dest_ref[2 * j + slot], buf.at[slot], j, sem).wait()
        return carry

    lax.fori_loop(0, n_tok, issue, 0, unroll=DMA_UNROLL)
    lax.fori_loop(0, n_tok, drain, 0, unroll=DMA_UNROLL)
    gate = gate_ref[...]
    x = x_ref[...] + gate[:, 0:1] * buf[0] + gate[:, 1:2] * buf[1]
    o_ref[...] = _rms(x, gain_ref[...])


def _moe_combine(dest_flat, x, gates, gain, ys, ts):
    t, d = x.shape
    tok = pl.BlockSpec((ts, d), lambda i: (i, 0))
    return pl.pallas_call(
        _moe_combine_kernel,
        grid=(t // ts,),
        in_specs=[pl.BlockSpec((2 * ts,), lambda i: (i,), memory_space=pltpu.SMEM),
                  tok, pl.BlockSpec((ts, 2), lambda i: (i, 0)), _const_spec(gain.shape),
                  pl.BlockSpec(memory_space=pl.ANY)],
        out_specs=tok,
        out_shape=jax.ShapeDtypeStruct((t, d), F32),
        scratch_shapes=[pltpu.VMEM((2, ts, d), F32), pltpu.SemaphoreType.DMA(())],
        compiler_params=_params("arbitrary"),
        name="moe_combine",
    )(dest_flat, x, gates, gain, ys)


def _tile(n, want):
    t = min(n, want)
    assert n % t == 0
    return t


def _pad_cols(w, n):
    return jnp.pad(w, ((0, 0), (0, n - w.shape[1])))


def _pad_rows(w, n):
    return jnp.pad(w, ((0, n - w.shape[0]), (0, 0)))


def kernel(x, norm_gain, rwkv_mu, rwkv_wr, rwkv_wk, rwkv_wv, rwkv_w0, rwkv_w1, rwkv_w2, rwkv_a0, rwkv_a1, rwkv_a2, rwkv_g1, rwkv_g2, rwkv_k_k, rwkv_k_a, rwkv_r_k, rwkv_lnx_w, rwkv_lnx_b, rwkv_wo, kv_norm_gain, w_kv, attn_wq, attn_wo, ffn_w1, ffn_w3, ffn_w2, moe_router, moe_router_bias, moe_w1, moe_w3, moe_w2, final_norm_gain):
    b, s, d = x.shape
    t = b * s
    n_heads = d // HEAD
    assert norm_gain.shape[0] == 2 and d % LANES == 0 and s % 8 == 0

    chan_head = jnp.arange(d) // HEAD
    e = (chan_head[:, None] == jnp.arange(LANES)[None, :]).astype(BF16)
    et = e.T

    zeros = jnp.zeros((d,), F32)
    vec_pre = jnp.stack([norm_gain[0, 0], rwkv_w0[0], rwkv_a0[0], rwkv_k_k[0], rwkv_k_a[0],
                         zeros, zeros, zeros])
    tm_pre = _tile(s, 256)
    r, lw, k, v, kk, a, g = _rwkv_pre(
        x, vec_pre, rwkv_mu[0], _bf(rwkv_wr[0]), _bf(rwkv_wk[0]), _bf(rwkv_wv[0]),
        _bf(_pad_cols(rwkv_w1[0], LANES)), _bf(_pad_rows(rwkv_w2[0], LANES)),
        _bf(_pad_cols(rwkv_a1[0], LANES)), _bf(_pad_rows(rwkv_a2[0], LANES)),
        _bf(rwkv_g1[0]), _bf(rwkv_g2[0]), e, et, tm_pre)
    y = _wkv(r, lw, k, v, kk, a, chunk=_tile(s, LANES), seq_tile=_tile(s, 512), n_pairs=4)

    flat = lambda arr: arr.reshape(t, d)
    tm = _tile(t, 512)
    vec_post = jnp.stack([rwkv_lnx_w[0], rwkv_lnx_b[0], rwkv_r_k[0].reshape(d),
                          zeros, zeros, zeros, zeros, zeros])
    x1 = _rwkv_post(flat(y), flat(r), flat(k), flat(v), flat(g), flat(x), vec_post,
                    _bf(rwkv_wo[0]), e, et, tm)

    ff = ffn_w1.shape[2]
    tf = ff // 2 if (ff // 2) % LANES == 0 else ff
    x2 = _ffn(x1, norm_gain[0, 1][None, :], _bf(ffn_w1[0]), _bf(ffn_w3[0]), _bf(ffn_w2[0]), tm, tf)

    n_slopes = len(BRANCHES) * HEADS_PER_BRANCH
    slopes = jnp.exp2(-ALIBI_MAX * (jnp.arange(n_slopes, dtype=F32) + 1.0) / n_slopes)
    gains = jnp.stack([norm_gain[1, 0], kv_norm_gain] + [zeros] * 6)
    tm_a = _tile(s, 512)
    qkv = _attn_proj(x2.reshape(b, s, d), gains, _bf(attn_wq[0]), _bf(w_kv), tm_a)
    os_, ls_ = [], []
    for br in range(len(BRANCHES)):
        o_br, l_br = _attn_branch(slopes, *qkv[3 * br:3 * br + 3], br)
        os_.append(o_br)
        ls_.append(l_br)

    router = _pad_cols(moe_router[0], LANES)
    rhi, rlo = _split2(router)
    rbias = jnp.full((1, LANES), NEG, F32).at[0, :N_EXPERTS].set(moe_router_bias[0])
    x3, h4, gate, sel, rank, cnt = _attn_out(os_, ls_, x2, norm_gain[1, 1][None, :], _bf(attn_wo[0]),
                                             rhi, rlo, rbias, b, tm_a)

    tm_e = 512 if t >= 4096 else 128
    ts = _tile(t, 512)
    sel8 = sel[:, :N_EXPERTS] > 0.5
    counts = cnt[0, :N_EXPERTS].astype(jnp.int32)
    padded = ((counts + tm_e - 1) // tm_e) * tm_e
    ends = jnp.cumsum(padded)
    dest8 = (ends - padded)[None, :] + rank[:, :N_EXPERTS].astype(jnp.int32)
    e0 = jnp.argmax(sel8, axis=1)
    e1 = N_EXPERTS - 1 - jnp.argmax(sel8[:, ::-1], axis=1)
    pick = lambda arr: jnp.stack([jnp.take_along_axis(arr, e0[:, None], 1)[:, 0],
                                  jnp.take_along_axis(arr, e1[:, None], 1)[:, 0]], axis=1)
    dest = pick(dest8).reshape(2 * t)
    gates = pick(gate[:, :N_EXPERTS])
    rows = 2 * t + N_EXPERTS * tm_e
    n_tiles = rows // tm_e
    n_valid = (ends[-1] // tm_e).astype(jnp.int32)
    tile_src = jnp.minimum(jnp.arange(n_tiles, dtype=jnp.int32), n_valid - 1)
    tile_expert = jnp.minimum(
        jnp.sum(tile_src[:, None] * tm_e >= ends[None, :], axis=1), N_EXPERTS - 1).astype(jnp.int32)

    xs = _moe_scatter(dest, h4, rows, ts)
    ffe = moe_w1.shape[3]
    tfe = ffe // 2 if (ffe // 2) % LANES == 0 else ffe
    ys = _moe_ffn(tile_src, tile_expert, n_valid.reshape(1), xs,
                  _bf(moe_w1[0]), _bf(moe_w3[0]), _bf(moe_w2[0]), tm_e, tfe)
    out = _moe_combine(dest, x3, gates, final_norm_gain[None, :], ys, ts)
    return out.reshape(b, s, d)
```

```python
import functools

import jax
import jax.numpy as jnp
from jax import lax
from jax.experimental import pallas as pl
from jax.experimental.pallas import tpu as pltpu

F32 = jnp.float32
BF16 = jnp.bfloat16

RMS_EPS = 1e-5
GN_EPS = 64e-5
HEAD = 64
LANES = 128
BRANCHES = ((128, 1), (512, 4), (2048, 16))
HEADS_PER_BRANCH = 8
ALIBI_MAX = 8.0
N_EXPERTS = 8
NEG = -1e30
VMEM_LIMIT_BYTES = 56 * 1024 * 1024


def _params(*sem):
    return pltpu.CompilerParams(dimension_semantics=sem, vmem_limit_bytes=VMEM_LIMIT_BYTES)


def _dot(a, b):
    return jnp.dot(a, b, preferred_element_type=F32)


def _dot_nt(a, b):
    return lax.dot_general(a, b, (((1,), (1,)), ((), ())), preferred_element_type=F32)


def _bf(x):
    return x.astype(BF16)


def _split2(x):
    hi = x.astype(BF16)
    lo = (x - hi.astype(F32)).astype(BF16)
    return hi, lo


def _split3(x):
    hi = x.astype(BF16)
    r1 = x - hi.astype(F32)
    mid = r1.astype(BF16)
    lo = (r1 - mid.astype(F32)).astype(BF16)
    return hi, mid, lo


def _sigmoid(z):
    return 1.0 / (1.0 + jnp.exp(-z))


def _softplus(z):
    return jnp.maximum(z, 0.0) + jnp.log(1.0 + jnp.exp(-jnp.abs(z)))


def _rms(x, gain):
    return x * lax.rsqrt(jnp.mean(x * x, axis=-1, keepdims=True) + RMS_EPS) * gain


def _headsum(x, e, et):
    hi, lo = _split2(x)
    s = _dot(hi, e) + _dot(lo, e)
    shi, slo = _split2(s)
    return _dot(shi, et) + _dot(slo, et)


def _const_spec(shape):
    nd = len(shape)
    return pl.BlockSpec(shape, lambda *_: (0,) * nd)


def _rwkv_pre_kernel(x_ref, xp_ref, vec_ref, mu_ref, wr_ref, wk_ref, wv_ref, w1_ref, w2_ref,
                     a1_ref, a2_ref, g1_ref, g2_ref, e_ref, et_ref,
                     r_out, cum_out, k_out, v_out, kk_out, b_out, g_out, *, chunk):
    i = pl.program_id(1)
    vec = vec_ref[...]
    gain, w0, a0, k_k, k_a = (vec[j:j + 1] for j in range(5))
    x = x_ref[0]
    h = _rms(x, gain)
    hp = _rms(xp_ref[0][7:8, :], gain)
    hp = jnp.where(i > 0, hp, 0.0)
    rows = lax.broadcasted_iota(jnp.int32, h.shape, 0)
    hprev = jnp.where(rows == 0, hp, pltpu.roll(h, 1, 0))
    xx = hprev - h
    mu = mu_ref[...]
    xr, xw, xk, xv, xa, xg = (_bf(h + xx * mu[j:j + 1]) for j in range(6))
    r = _dot(xr, wr_ref[...])
    k = _dot(xk, wk_ref[...])
    v = _dot(xv, wv_ref[...])
    wl = w0 + _dot(_bf(jnp.tanh(_dot(xw, w1_ref[...]))), w2_ref[...])
    w_log = -_softplus(-wl) - 0.5
    a = _sigmoid(a0 + _dot(_bf(_dot(xa, a1_ref[...])), a2_ref[...]))
    g = _dot(_bf(_sigmoid(_dot(xg, g1_ref[...]))), g2_ref[...])
    kkr = k * k_k
    ss = _headsum(kkr * kkr, e_ref[...], et_ref[...])
    kk = kkr / jnp.maximum(jnp.sqrt(ss), 1e-12)
    lw = -jnp.exp(w_log)
    tm = lw.shape[0]
    rr = lax.broadcasted_iota(jnp.int32, (tm, tm), 0)
    cc = lax.broadcasted_iota(jnp.int32, (tm, tm), 1)
    same_chunk = jnp.bitwise_xor(rr, cc) < chunk
    in_chunk = jnp.where(rr >= cc, jnp.where(same_chunk, 1.0, 0.0), 0.0).astype(BF16)
    hi, mid, lo = _split3(lw)
    cum_out[0] = _dot(in_chunk, hi) + _dot(in_chunk, mid) + _dot(in_chunk, lo)
    r_out[0] = _bf(r)
    k_out[0] = _bf(k * (1.0 + (a - 1.0) * k_a))
    v_out[0] = _bf(v)
    kk_out[0] = _bf(kk)
    b_out[0] = _bf(kk * a)
    g_out[0] = _bf(g)


def _rwkv_pre(x, vec, mu, wr, wk, wv, w1, w2, a1, a2, g1, g2, e, et, tm, chunk):
    b, s, d = x.shape
    assert tm % chunk == 0
    tok = pl.BlockSpec((1, tm, d), lambda bi, i: (bi, i, 0))
    prev = pl.BlockSpec((1, 8, d), lambda bi, i: (bi, jnp.maximum(i * (tm // 8) - 1, 0), 0))
    consts = [vec, mu, wr, wk, wv, w1, w2, a1, a2, g1, g2, e, et]
    out = lambda dt: jax.ShapeDtypeStruct((b, s, d), dt)
    return pl.pallas_call(
        functools.partial(_rwkv_pre_kernel, chunk=chunk),
        grid=(b, s // tm),
        in_specs=[tok, prev] + [_const_spec(c.shape) for c in consts],
        out_specs=[tok] * 7,
        out_shape=[out(BF16), out(F32)] + [out(BF16)] * 5,
        compiler_params=_params("parallel", "arbitrary"),
        name="rwkv_pre",
    )(x, x, *consts)


def _round_robin(generators):
    results = [None] * len(generators)
    live = list(enumerate(generators))
    while live:
        still = []
        for idx, gen in live:
            try:
                results[idx] = next(gen)
                still.append((idx, gen))
            except StopIteration:
                pass
        live = still
    return results


def _wkv_kernel(r_ref, cum_ref, k_ref, v_ref, kk_ref, b_ref, y_ref, st_ref, *, chunk):
    c_len = chunk
    n_chunks = r_ref.shape[1] // c_len
    n_pairs = r_ref.shape[2] // LANES
    inv_steps = max(c_len.bit_length() - 2, 0)
    lane = lax.broadcasted_iota(jnp.int32, (c_len, LANES), 1)
    head0 = lane < HEAD
    rr = lax.broadcasted_iota(jnp.int32, (c_len, c_len), 0)
    cc = lax.broadcasted_iota(jnp.int32, (c_len, c_len), 1)
    lower = rr >= cc
    strict = rr > cc
    eye = jnp.where(rr == cc, 1.0, 0.0).astype(F32)
    r2 = lax.broadcasted_iota(jnp.int32, (LANES, LANES), 0)
    c2 = lax.broadcasted_iota(jnp.int32, (LANES, LANES), 1)
    same_head = (r2 < HEAD) == (c2 < HEAD)

    @pl.when(pl.program_id(2) == 0)
    def _():
        st_ref[...] = jnp.zeros_like(st_ref)

    first_row = lax.broadcasted_iota(jnp.int32, (c_len, LANES), 0) == 0
    rows = lambda mat, j: mat[j * c_len:(j + 1) * c_len]
    own_head = lambda mat: jnp.where(head0, rows(mat, 0), rows(mat, 1))

    def chunk_pair(sl, lanes, state):
        cum = cum_ref[0, sl, lanes]
        cum_prev = jnp.where(first_row, 0.0, pltpu.roll(cum, 1, 0))
        cum_last = cum[c_len - 1:c_len, :]
        kk = kk_ref[0, sl, lanes].astype(F32)
        b = b_ref[0, sl, lanes].astype(F32)
        k = k_ref[0, sl, lanes].astype(F32)
        vb = v_ref[0, sl, lanes]
        e_neg = jnp.exp(-cum)
        e_end = jnp.exp(cum_last - cum)
        a_t = -kk * jnp.exp(cum_prev)
        r_t = r_ref[0, sl, lanes].astype(F32) * jnp.exp(cum)
        lhs = _bf(jnp.concatenate([jnp.where(head0, a_t, 0.0), jnp.where(head0, 0.0, a_t),
                                   jnp.where(head0, r_t, 0.0), jnp.where(head0, 0.0, r_t)], axis=0))
        gram = _dot_nt(lhs, _bf(jnp.concatenate([b * e_neg, k * e_neg], axis=0)))
        from_state = _dot_nt(_bf(jnp.concatenate([a_t, r_t], axis=0)), _bf(state))
        yield
        gram_b, gram_k = gram[:, :c_len], gram[:, c_len:]
        from_v = _dot(_bf(jnp.concatenate(
            [jnp.where(strict, rows(gram_k, 0), 0.0), jnp.where(strict, rows(gram_k, 1), 0.0),
             jnp.where(lower, rows(gram_k, 2), 0.0), jnp.where(lower, rows(gram_k, 3), 0.0)], axis=0)), vb)
        m_rb = _bf(jnp.concatenate([jnp.where(lower, rows(gram_b, 2), 0.0),
                                    jnp.where(lower, rows(gram_b, 3), 0.0)], axis=0))
        w = rows(from_state, 0) + own_head(from_v[:2 * c_len])
        y1 = rows(from_state, 1) + own_head(from_v[2 * c_len:])
        pw = [jnp.where(strict, rows(gram_b, hd), 0.0) for hd in range(2)]
        inv = [eye + pw[hd] for hd in range(2)]
        if inv_steps:
            pw = [_dot(_bf(pw[hd]), _bf(pw[hd])) for hd in range(2)]
        for step in range(inv_steps):
            yield
            if step < inv_steps - 1:
                both = [_dot(_bf(jnp.concatenate([inv[hd], pw[hd]], axis=0)), _bf(pw[hd])) for hd in range(2)]
                inv = [inv[hd] + rows(both[hd], 0) for hd in range(2)]
                pw = [rows(both[hd], 1) for hd in range(2)]
            else:
                inv = [inv[hd] + _dot(_bf(inv[hd]), _bf(pw[hd])) for hd in range(2)]
        yield
        u = own_head(_dot(_bf(jnp.concatenate(inv, axis=0)), _bf(w)))
        yield
        y = y1 + own_head(_dot(m_rb, _bf(u)))
        uv_t = _bf(jnp.transpose(jnp.concatenate([u, vb.astype(F32)], axis=0)))
        bk = _bf(jnp.concatenate([b * e_end, k * e_end], axis=0))
        yield y, state * jnp.exp(cum_last) + jnp.where(same_head, _dot(uv_t, bk), 0.0)

    def body(c, carry):
        sl = pl.ds(pl.multiple_of(c * c_len, c_len), c_len)
        lanes = [slice(p * LANES, (p + 1) * LANES) for p in range(n_pairs)]
        results = _round_robin([chunk_pair(sl, lanes[p], st_ref[p]) for p in range(n_pairs)])
        for p, (y, new_state) in enumerate(results):
            y_ref[0, sl, lanes[p]] = _bf(y)
            st_ref[p] = new_state
        return carry

    lax.fori_loop(0, n_chunks, body, 0)


def _wkv(r, cum, k, v, kk, bb, chunk, seq_tile, n_pairs):
    b, s, d = r.shape
    width = n_pairs * LANES
    spec = pl.BlockSpec((1, seq_tile, width), lambda bi, hi, si: (bi, si, hi))
    return pl.pallas_call(
        functools.partial(_wkv_kernel, chunk=chunk),
        grid=(b, d // width, s // seq_tile),
        in_specs=[spec] * 6,
        out_specs=spec,
        out_shape=jax.ShapeDtypeStruct((b, s, d), BF16),
        scratch_shapes=[pltpu.VMEM((n_pairs, LANES, LANES), F32)],
        compiler_params=_params("parallel", "parallel", "arbitrary"),
        name="wkv7",
    )(r, cum, k, v, kk, bb)


def _rwkv_post_kernel(y_ref, r_ref, k_ref, v_ref, g_ref, x_ref, vec_ref, wo_ref, e_ref, et_ref, o_ref):
    vec = vec_ref[...]
    lnx_w, lnx_b, r_k = (vec[j:j + 1] for j in range(3))
    e = e_ref[...]
    et = et_ref[...]
    y = y_ref[...].astype(F32)
    v = v_ref[...].astype(F32)
    mean = _headsum(y, e, et) * (1.0 / HEAD)
    dy = y - mean
    var = _headsum(dy * dy, e, et) * (1.0 / HEAD)
    yn = dy * lax.rsqrt(var + GN_EPS) * lnx_w + lnx_b
    bonus = _headsum(r_ref[...].astype(F32) * k_ref[...].astype(F32) * r_k, e, et) * v
    o_ref[...] = x_ref[...] + _dot(_bf((yn + bonus) * g_ref[...].astype(F32)), wo_ref[...])


def _rwkv_post(y, r, k, v, g, x, vec, wo, e, et, tm):
    t, d = x.shape
    tok = pl.BlockSpec((tm, d), lambda i: (i, 0))
    consts = [vec, wo, e, et]
    return pl.pallas_call(
        _rwkv_post_kernel,
        grid=(t // tm,),
        in_specs=[tok] * 6 + [_const_spec(c.shape) for c in consts],
        out_specs=tok,
        out_shape=jax.ShapeDtypeStruct((t, d), F32),
        compiler_params=_params("parallel"),
        name="rwkv_post",
    )(y, r, k, v, g, x, *consts)


def _ffn_kernel(x_ref, gain_ref, w1_ref, w3_ref, w2_ref, o_ref, h_ref, acc_ref):
    f = pl.program_id(1)

    @pl.when(f == 0)
    def _():
        h_ref[...] = _bf(_rms(x_ref[...], gain_ref[...]))
        acc_ref[...] = jnp.zeros_like(acc_ref)

    h = h_ref[...]
    u = _dot(h, w1_ref[...])
    act = _bf(u * _sigmoid(u) * _dot(h, w3_ref[...]))
    acc_ref[...] += _dot(act, w2_ref[...])

    @pl.when(f == pl.num_programs(1) - 1)
    def _():
        o_ref[...] = x_ref[...] + acc_ref[...]


def _ffn(x, gain, w1, w3, w2, tm, tf):
    t, d = x.shape
    ff = w1.shape[1]
    tok = pl.BlockSpec((tm, d), lambda i, f: (i, 0))
    return pl.pallas_call(
        _ffn_kernel,
        grid=(t // tm, ff // tf),
        in_specs=[tok, _const_spec(gain.shape),
                  pl.BlockSpec((d, tf), lambda i, f: (0, f)),
                  pl.BlockSpec((d, tf), lambda i, f: (0, f)),
                  pl.BlockSpec((tf, d), lambda i, f: (f, 0))],
        out_specs=tok,
        out_shape=jax.ShapeDtypeStruct((t, d), F32),
        scratch_shapes=[pltpu.VMEM((tm, d), BF16), pltpu.VMEM((tm, d), F32)],
        compiler_params=_params("parallel", "arbitrary"),
        name="ffn_dense",
    )(x, gain, w1, w3, w2)


BRANCH_WIDTH = HEADS_PER_BRANCH * HEAD


def _attn_proj_kernel(x_ref, gains_ref, wq_ref, wkv_ref, *refs):
    outs, (q_scr, kv_scr) = refs[:-2], refs[-2:]
    x = x_ref[0]
    tm = x.shape[0]
    n = x * lax.rsqrt(jnp.mean(x * x, axis=-1, keepdims=True) + RMS_EPS)
    gains = gains_ref[...]
    q_tiles = q_scr.shape[0]
    q = _dot(_bf(n * gains[0:1]), wq_ref[...]) * (1.0 / HEAD ** 0.5)
    kv = _dot(_bf(n * gains[1:2]), wkv_ref[...])
    for ti in range(q_tiles):
        q_scr[ti] = q[:, ti * LANES:(ti + 1) * LANES]
    for ti in range(2 * q_tiles):
        kv_scr[ti] = kv[:, ti * LANES:(ti + 1) * LANES]
    tiles_br = BRANCH_WIDTH // LANES
    for br, (_, dil) in enumerate(BRANCHES):
        q_out, k_out, v_out = outs[3 * br:3 * br + 3]
        for res in range(dil):
            rows = pl.ds(res, tm // dil, stride=dil) if dil > 1 else slice(None)
            for ti in range(tiles_br):
                src = br * tiles_br + ti
                lanes = slice(ti * LANES, (ti + 1) * LANES)
                q_out[0, res, :, lanes] = _bf(q_scr[src, rows, :])
                k_out[0, res, :, lanes] = _bf(kv_scr[src, rows, :])
                v_out[0, res, :, lanes] = _bf(kv_scr[q_tiles + src, rows, :])


def _attn_proj(x, gains, wq, wkv, tm):
    b, s, d = x.shape
    qw = wq.shape[1]
    out_specs, out_shape = [], []
    for _, dil in BRANCHES:
        assert tm % (16 * dil) == 0
        out_specs += [pl.BlockSpec((1, dil, tm // dil, BRANCH_WIDTH), lambda bi, i: (bi, 0, i, 0))] * 3
        out_shape += [jax.ShapeDtypeStruct((b, dil, s // dil, BRANCH_WIDTH), BF16)] * 3
    return pl.pallas_call(
        _attn_proj_kernel,
        grid=(b, s // tm),
        in_specs=[pl.BlockSpec((1, tm, d), lambda bi, i: (bi, i, 0)),
                  _const_spec(gains.shape), _const_spec(wq.shape), _const_spec(wkv.shape)],
        out_specs=out_specs,
        out_shape=out_shape,
        scratch_shapes=[pltpu.VMEM((qw // LANES, tm, LANES), F32),
                        pltpu.VMEM((2 * qw // LANES, tm, LANES), F32)],
        compiler_params=_params("parallel", "parallel"),
        name="attn_proj",
    )(x, gains, wq, wkv)


def _attn_kernel(slopes_ref, q_ref, k_ref, v_ref, o_ref, lse_ref, *, branch, dilation, qb):
    sub_len = q_ref.shape[2]
    n_blk = sub_len // qb
    n_pairs = q_ref.shape[3] // LANES
    rel = (lax.broadcasted_iota(jnp.int32, (qb, qb), 0)
           - lax.broadcasted_iota(jnp.int32, (qb, qb), 1)).astype(F32)
    dist_d = jnp.where(rel >= 0, rel, -NEG)
    dist_p = jnp.where(rel <= 0, rel + float(qb), -NEG)
    lane = lax.broadcasted_iota(jnp.int32, (qb, LANES), 1)
    head0 = lane < HEAD
    lane_row = lax.broadcasted_iota(jnp.int32, (1, LANES), 1)
    head_mask = [jnp.where(lane_row < HEAD, 1.0, 0.0).astype(BF16),
                 jnp.where(lane_row < HEAD, 0.0, 1.0).astype(BF16)]

    def block(i, carry):
        cur = pl.ds(pl.multiple_of(i * qb, qb), qb)
        if n_blk > 1:
            prv = pl.ds(pl.multiple_of(jnp.maximum(i - 1, 0) * qb, qb), qb)
            has_prev = jnp.where(i > 0, 1.0, -NEG)

        def one_head(p, j):
            lanes = slice(p * LANES, (p + 1) * LANES)
            step = slopes_ref[branch * HEADS_PER_BRANCH + 2 * p + j] * float(dilation)
            qh = q_ref[0, 0, cur, lanes] * head_mask[j]
            sd = _dot_nt(qh, k_ref[0, 0, cur, lanes]) - step * dist_d
            if n_blk > 1:
                sp = _dot_nt(qh, k_ref[0, 0, prv, lanes]) - (step * has_prev) * dist_p
            yield
            m = jnp.max(sd, axis=-1, keepdims=True)
            if n_blk > 1:
                m = jnp.maximum(m, jnp.max(sp, axis=-1, keepdims=True))
            pd = jnp.exp(sd - m)
            den = jnp.sum(pd, axis=-1, keepdims=True)
            acc = _dot(_bf(pd), v_ref[0, 0, cur, lanes])
            if n_blk > 1:
                pp = jnp.exp(sp - m)
                den = den + jnp.sum(pp, axis=-1, keepdims=True)
                acc = acc + _dot(_bf(pp), v_ref[0, 0, prv, lanes])
            yield
            yield acc / den, m + jnp.log(den)

        res = _round_robin([one_head(p, j) for p in range(n_pairs) for j in range(2)])
        for p in range(n_pairs):
            lanes = slice(p * LANES, (p + 1) * LANES)
            (o_a, l_a), (o_b, l_b) = res[2 * p], res[2 * p + 1]
            o_ref[0, 0, cur, lanes] = _bf(jnp.where(head0, o_a, o_b))
            lse_ref[0, 0, cur, lanes] = jnp.where(head0, l_a, l_b)
        return carry

    lax.fori_loop(0, n_blk, block, 0)


def _attn_branch(slopes, q, k, v, branch):
    window, dilation = BRANCHES[branch]
    assert window % dilation == 0 and window // dilation == LANES
    b, _, sub_len, width = q.shape
    qb = min(LANES, sub_len)
    assert sub_len % qb == 0
    spec = pl.BlockSpec((1, 1, sub_len, width), lambda bi, ri: (bi, ri, 0, 0))
    return pl.pallas_call(
        functools.partial(_attn_kernel, branch=branch, dilation=dilation, qb=qb),
        grid=(b, dilation),
        in_specs=[pl.BlockSpec(memory_space=pltpu.SMEM)] + [spec] * 3,
        out_specs=[spec] * 2,
        out_shape=[jax.ShapeDtypeStruct(q.shape, BF16), jax.ShapeDtypeStruct(q.shape, F32)],
        compiler_params=_params("parallel", "parallel"),
        name=f"dilated_attn_{branch}",
    )(slopes, q, k, v)


def _attn_out_kernel(o0, o1, o2, l0, l1, l2, x_ref, gain_ref, wo_ref, rhi_ref, rlo_ref, rb_ref,
                     x_out, h_out, gate_out, sel_out, rank_out, cnt_out, cnt_ref, o_scr, l_scr):
    @pl.when((pl.program_id(0) == 0) & (pl.program_id(1) == 0))
    def _():
        cnt_ref[...] = jnp.zeros_like(cnt_ref)

    tm = x_ref.shape[0]
    tiles_br = BRANCH_WIDTH // LANES
    for br, (o_ref, l_ref) in enumerate(((o0, l0), (o1, l1), (o2, l2))):
        dil = BRANCHES[br][1]
        for res in range(dil):
            rows = pl.ds(res, tm // dil, stride=dil) if dil > 1 else slice(None)
            for ti in range(tiles_br):
                lanes = slice(ti * LANES, (ti + 1) * LANES)
                o_scr[br, ti, rows, :] = o_ref[0, res, :, lanes].astype(F32)
                l_scr[br, ti, rows, :] = l_ref[0, res, :, lanes]
    merged = []
    for ti in range(tiles_br):
        ls = [l_scr[br, ti] for br in range(3)]
        m = jnp.maximum(jnp.maximum(ls[0], ls[1]), ls[2])
        ws = [jnp.exp(l - m) for l in ls]
        merged.append(_bf((ws[0] * o_scr[0, ti] + ws[1] * o_scr[1, ti] + ws[2] * o_scr[2, ti])
                          / (ws[0] + ws[1] + ws[2])))
    x = x_ref[...] + _dot(jnp.concatenate(merged, axis=1), wo_ref[...])
    x_out[...] = x
    h = _rms(x, gain_ref[...])
    h_out[...] = h
    hhi, hlo = _split2(h)
    logits = (_dot(hhi, rhi_ref[...]) + _dot(hhi, rlo_ref[...]) + _dot(hlo, rhi_ref[...])) + rb_ref[...]
    lane = lax.broadcasted_iota(jnp.int32, logits.shape, 1)
    m1 = jnp.max(logits, axis=-1, keepdims=True)
    i1 = jnp.min(jnp.where(logits == m1, lane, LANES), axis=-1, keepdims=True)
    rest = jnp.where(lane == i1, -3e38, logits)
    m2 = jnp.max(rest, axis=-1, keepdims=True)
    i2 = jnp.min(jnp.where(rest == m2, lane, LANES), axis=-1, keepdims=True)
    ex = jnp.exp(m2 - m1)
    gate_out[...] = (jnp.where(lane == i1, 1.0 / (1.0 + ex), 0.0)
                     + jnp.where(lane == i2, ex / (1.0 + ex), 0.0))
    sel = jnp.where((lane == i1) | (lane == i2), 1.0, 0.0)
    sel_out[...] = sel
    rr = lax.broadcasted_iota(jnp.int32, (tm, tm), 0)
    cc = lax.broadcasted_iota(jnp.int32, (tm, tm), 1)
    before = jnp.where(rr > cc, 1.0, 0.0).astype(BF16)
    rank_out[...] = cnt_ref[...] + _dot(before, _bf(sel))
    cnt_ref[...] += jnp.sum(sel, axis=0, keepdims=True)
    cnt_out[...] = cnt_ref[...]


def _attn_out(os_, ls_, x, gain, wo, rhi, rlo, rb, b, tm):
    t, d = x.shape
    tiles = t // (b * tm)
    tok = pl.BlockSpec((tm, d), lambda bi, i: (bi * tiles + i, 0))
    small = pl.BlockSpec((tm, LANES), lambda bi, i: (bi * tiles + i, 0))
    att = [pl.BlockSpec((1, dil, tm // dil, BRANCH_WIDTH), lambda bi, i: (bi, 0, i, 0))
           for _, dil in BRANCHES]
    consts = [gain, wo, rhi, rlo, rb]
    return pl.pallas_call(
        _attn_out_kernel,
        grid=(b, tiles),
        in_specs=att + att + [tok] + [_const_spec(c.shape) for c in consts],
        out_specs=[tok, tok, small, small, small, _const_spec((1, LANES))],
        out_shape=[jax.ShapeDtypeStruct((t, d), F32), jax.ShapeDtypeStruct((t, d), F32),
                   jax.ShapeDtypeStruct((t, LANES), F32), jax.ShapeDtypeStruct((t, LANES), F32),
                   jax.ShapeDtypeStruct((t, LANES), F32), jax.ShapeDtypeStruct((1, LANES), F32)],
        scratch_shapes=[pltpu.VMEM((1, LANES), F32),
                        pltpu.VMEM((len(BRANCHES), BRANCH_WIDTH // LANES, tm, LANES), F32),
                        pltpu.VMEM((len(BRANCHES), BRANCH_WIDTH // LANES, tm, LANES), F32)],
        compiler_params=_params("arbitrary", "arbitrary"),
        name="attn_out_route",
    )(*os_, *ls_, x, *consts)


DMA_UNROLL = 8


def _row_copy(src, src_row, dst, dst_row, sem):
    return pltpu.make_async_copy(src.at[pl.ds(src_row, 1)], dst.at[pl.ds(dst_row, 1)], sem)


def _moe_scatter_kernel(dest_ref, h_ref, buf_in, buf_out, sem):
    del buf_in
    n_tok = h_ref.shape[0]

    def issue(j, carry):
        for slot in range(2):
            _row_copy(h_ref, j, buf_out, dest_ref[2 * j + slot], sem).start(priority=slot)
        return carry

    def drain(j, carry):
        for slot in range(2):
            _row_copy(h_ref, j, buf_out, dest_ref[2 * j + slot], sem).wait()
        return carry

    lax.fori_loop(0, n_tok, issue, 0, unroll=DMA_UNROLL)
    lax.fori_loop(0, n_tok, drain, 0, unroll=DMA_UNROLL)


def _moe_scatter(dest_flat, h, rows, ts):
    t, d = h.shape
    return pl.pallas_call(
        _moe_scatter_kernel,
        grid=(t // ts,),
        in_specs=[pl.BlockSpec((2 * ts,), lambda i: (i,), memory_space=pltpu.SMEM),
                  pl.BlockSpec((ts, d), lambda i: (i, 0)),
                  pl.BlockSpec(memory_space=pl.ANY)],
        out_specs=pl.BlockSpec(memory_space=pl.ANY),
        out_shape=jax.ShapeDtypeStruct((rows, d), F32),
        scratch_shapes=[pltpu.SemaphoreType.DMA(())],
        input_output_aliases={2: 0},
        compiler_params=_params("arbitrary"),
        name="moe_scatter",
    )(dest_flat, h, jnp.zeros((rows, d), F32))


def _moe_ffn_kernel(src_ref, te_ref, nv_ref, x_ref, w1_ref, w3_ref, w2_ref, o_ref, acc_ref):
    del src_ref, te_ref
    i = pl.program_id(0)
    f = pl.program_id(1)

    @pl.when(i < nv_ref[0])
    def _():
        @pl.when(f == 0)
        def _():
            acc_ref[...] = jnp.zeros_like(acc_ref)

        h = _bf(x_ref[...])
        u = _dot(h, w1_ref[0])
        act = _bf(u * _sigmoid(u) * _dot(h, w3_ref[0]))
        acc_ref[...] += _dot(act, w2_ref[0])

        @pl.when(f == pl.num_programs(1) - 1)
        def _():
            o_ref[...] = acc_ref[...]

    @pl.when(i >= nv_ref[0])
    def _():
        o_ref[...] = jnp.zeros_like(o_ref)


def _moe_ffn(tile_src, tile_expert, n_valid, xs, w1, w3, w2, tm, tf):
    rows, d = xs.shape
    ff = w1.shape[2]
    tok = pl.BlockSpec((tm, d), lambda i, f, src, te, nv: (src[i], 0))
    return pl.pallas_call(
        _moe_ffn_kernel,
        grid_spec=pltpu.PrefetchScalarGridSpec(
            num_scalar_prefetch=3,
            grid=(rows // tm, ff // tf),
            in_specs=[tok,
                      pl.BlockSpec((1, d, tf), lambda i, f, src, te, nv: (te[i], 0, f)),
                      pl.BlockSpec((1, d, tf), lambda i, f, src, te, nv: (te[i], 0, f)),
                      pl.BlockSpec((1, tf, d), lambda i, f, src, te, nv: (te[i], f, 0))],
            out_specs=pl.BlockSpec((tm, d), lambda i, f, src, te, nv: (i, 0)),
            scratch_shapes=[pltpu.VMEM((tm, d), F32)],
        ),
        out_shape=jax.ShapeDtypeStruct((rows, d), F32),
        compiler_params=_params("arbitrary", "arbitrary"),
        name="moe_experts",
    )(tile_src, tile_expert, n_valid, xs, w1, w3, w2)


def _moe_combine_kernel(dest_ref, x_ref, gate_ref, gain_ref, y_hbm, o_ref, buf, sem):
    n_tok = x_ref.shape[0]

    def issue(j, carry):
        for slot in range(2):
            _row_copy(y_hbm, dest_ref[2 * j + slot], buf.at[slot], j, sem).start(priority=slot)
        return carry

    def drain(j, carry):
        for slot in range(2):
            _row_copy(y_hbm, dest_ref[2 * j + slot], buf.at[slot], j, sem).wait()
        return carry

    lax.fori_loop(0, n_tok, issue, 0, unroll=DMA_UNROLL)
    lax.fori_loop(0, n_tok, drain, 0, unroll=DMA_UNROLL)
    gate = gate_ref[...]
    x = x_ref[...] + gate[:, 0:1] * buf[0] + gate[:, 1:2] * buf[1]
    o_ref[...] = _rms(x, gain_ref[...])


def _moe_combine(dest_flat, x, gates, gain, ys, ts):
    t, d = x.shape
    tok = pl.BlockSpec((ts, d), lambda i: (i, 0))
    return pl.pallas_call(
        _moe_combine_kernel,
        grid=(t // ts,),
        in_specs=[pl.BlockSpec((2 * ts,), lambda i: (i,), memory_space=pltpu.SMEM),
                  tok, pl.BlockSpec((ts, 2), lambda i: (i, 0)), _const_spec(gain.shape),
                  pl.BlockSpec(memory_space=pl.ANY)],
        out_specs=tok,
        out_shape=jax.ShapeDtypeStruct((t, d), F32),
        scratch_shapes=[pltpu.VMEM((2, ts, d), F32), pltpu.SemaphoreType.DMA(())],
        compiler_params=_params("arbitrary"),
        name="moe_combine",
    )(dest_flat, x, gates, gain, ys)


def _tile(n, want):
    t = min(n, want)
    assert n % t == 0
    return t


def _pad_cols(w, n):
    return jnp.pad(w, ((0, 0), (0, n - w.shape[1])))


def _pad_rows(w, n):
    return jnp.pad(w, ((0, n - w.shape[0]), (0, 0)))


def kernel(x, norm_gain, rwkv_mu, rwkv_wr, rwkv_wk, rwkv_wv, rwkv_w0, rwkv_w1, rwkv_w2, rwkv_a0, rwkv_a1, rwkv_a2, rwkv_g1, rwkv_g2, rwkv_k_k, rwkv_k_a, rwkv_r_k, rwkv_lnx_w, rwkv_lnx_b, rwkv_wo, kv_norm_gain, w_kv, attn_wq, attn_wo, ffn_w1, ffn_w3, ffn_w2, moe_router, moe_router_bias, moe_w1, moe_w3, moe_w2, final_norm_gain):
    b, s, d = x.shape
    t = b * s
    n_heads = d // HEAD
    assert norm_gain.shape[0] == 2 and d % LANES == 0 and s % 8 == 0

    chan_head = jnp.arange(d) // HEAD
    e = (chan_head[:, None] == jnp.arange(LANES)[None, :]).astype(BF16)
    et = e.T

    zeros = jnp.zeros((d,), F32)
    vec_pre = jnp.stack([norm_gain[0, 0], rwkv_w0[0], rwkv_a0[0], rwkv_k_k[0], rwkv_k_a[0],
                         zeros, zeros, zeros])
    tm_pre = _tile(s, 256)
    chunk = _tile(s, LANES)
    r, cum, k, v, kk, kka, g = _rwkv_pre(
        x, vec_pre, rwkv_mu[0], _bf(rwkv_wr[0]), _bf(rwkv_wk[0]), _bf(rwkv_wv[0]),
        _bf(_pad_cols(rwkv_w1[0], LANES)), _bf(_pad_rows(rwkv_w2[0], LANES)),
        _bf(_pad_cols(rwkv_a1[0], LANES)), _bf(_pad_rows(rwkv_a2[0], LANES)),
        _bf(rwkv_g1[0]), _bf(rwkv_g2[0]), e, et, tm_pre, chunk)
    y = _wkv(r, cum, k, v, kk, kka, chunk=chunk, seq_tile=_tile(s, 512), n_pairs=4)

    flat = lambda arr: arr.reshape(t, d)
    tm = _tile(t, 512)
    vec_post = jnp.stack([rwkv_lnx_w[0], rwkv_lnx_b[0], rwkv_r_k[0].reshape(d),
                          zeros, zeros, zeros, zeros, zeros])
    x1 = _rwkv_post(flat(y), flat(r), flat(k), flat(v), flat(g), flat(x), vec_post,
                    _bf(rwkv_wo[0]), e, et, tm)

    ff = ffn_w1.shape[2]
    tf = ff // 2 if (ff // 2) % LANES == 0 else ff
    x2 = _ffn(x1, norm_gain[0, 1][None, :], _bf(ffn_w1[0]), _bf(ffn_w3[0]), _bf(ffn_w2[0]), tm, tf)

    n_slopes = len(BRANCHES) * HEADS_PER_BRANCH
    slopes = jnp.exp2(-ALIBI_MAX * (jnp.arange(n_slopes, dtype=F32) + 1.0) / n_slopes)
    gains = jnp.stack([norm_gain[1, 0], kv_norm_gain] + [zeros] * 6)
    tm_a = _tile(s, 512)
    qkv = _attn_proj(x2.reshape(b, s, d), gains, _bf(attn_wq[0]), _bf(w_kv), tm_a)
    os_, ls_ = [], []
    for br in range(len(BRANCHES)):
        o_br, l_br = _attn_branch(slopes, *qkv[3 * br:3 * br + 3], br)
        os_.append(o_br)
        ls_.append(l_br)

    router = _pad_cols(moe_router[0], LANES)
    rhi, rlo = _split2(router)
    rbias = jnp.full((1, LANES), NEG, F32).at[0, :N_EXPERTS].set(moe_router_bias[0])
    x3, h4, gate, sel, rank, cnt = _attn_out(os_, ls_, x2, norm_gain[1, 1][None, :], _bf(attn_wo[0]),
                                             rhi, rlo, rbias, b, tm_a)

    tm_e = 512 if t >= 4096 else 128
    ts = _tile(t, 512)
    sel8 = sel[:, :N_EXPERTS] > 0.5
    counts = cnt[0, :N_EXPERTS].astype(jnp.int32)
    padded = ((counts + tm_e - 1) // tm_e) * tm_e
    ends = jnp.cumsum(padded)
    dest8 = (ends - padded)[None, :] + rank[:, :N_EXPERTS].astype(jnp.int32)
    e0 = jnp.argmax(sel8, axis=1)
    e1 = N_EXPERTS - 1 - jnp.argmax(sel8[:, ::-1], axis=1)
    pick = lambda arr: jnp.stack([jnp.take_along_axis(arr, e0[:, None], 1)[:, 0],
                                  jnp.take_along_axis(arr, e1[:, None], 1)[:, 0]], axis=1)
    dest = pick(dest8).reshape(2 * t)
    gates = pick(gate[:, :N_EXPERTS])
    rows = 2 * t + N_EXPERTS * tm_e
    n_tiles = rows // tm_e
    n_valid = (ends[-1] // tm_e).astype(jnp.int32)
    tile_src = jnp.minimum(jnp.arange(n_tiles, dtype=jnp.int32), n_valid - 1)
    tile_expert = jnp.minimum(
        jnp.sum(tile_src[:, None] * tm_e >= ends[None, :], axis=1), N_EXPERTS - 1).astype(jnp.int32)

    xs = _moe_scatter(dest, h4, rows, ts)
    ffe = moe_w1.shape[3]
    tfe = ffe // 2 if (ffe // 2) % LANES == 0 else ffe
    ys = _moe_ffn(tile_src, tile_expert, n_valid.reshape(1), xs,
                  _bf(moe_w1[0]), _bf(moe_w3[0]), _bf(moe_w2[0]), tm_e, tfe)
    out = _moe_combine(dest, x3, gates, final_norm_gain[None, :], ys, ts)
    return out.reshape(b, s, d)
```

```python
import functools
import math

import jax
import jax.numpy as jnp
from jax import lax
from jax.experimental import pallas as pl
from jax.experimental.pallas import tpu as pltpu

F32 = jnp.float32
BF16 = jnp.bfloat16

RMS_EPS = 1e-5
GN_EPS = 64e-5
HEAD = 64
LANES = 128
BRANCHES = ((128, 1), (512, 4), (2048, 16))
HEADS_PER_BRANCH = 8
ALIBI_MAX = 8.0
N_EXPERTS = 8
NEG = -1e30
VMEM_LIMIT_BYTES = 56 * 1024 * 1024


def _params(*sem):
    return pltpu.CompilerParams(dimension_semantics=sem, vmem_limit_bytes=VMEM_LIMIT_BYTES)


def _dot(a, b):
    return jnp.dot(a, b, preferred_element_type=F32)


def _dot_nt(a, b):
    return lax.dot_general(a, b, (((1,), (1,)), ((), ())), preferred_element_type=F32)


def _bf(x):
    return x.astype(BF16)


def _split2(x):
    hi = x.astype(BF16)
    lo = (x - hi.astype(F32)).astype(BF16)
    return hi, lo


def _sigmoid(z):
    return 1.0 / (1.0 + jnp.exp(-z))


def _rms(x, gain):
    return x * lax.rsqrt(jnp.mean(x * x, axis=-1, keepdims=True) + RMS_EPS) * gain


def _headsum(x, e, et):
    return _dot(_bf(_dot(_bf(x), e)), et)


def _const_spec(shape):
    nd = len(shape)
    return pl.BlockSpec(shape, lambda *_: (0,) * nd)


def _rwkv_pre_kernel(x_ref, xp_ref, vec_ref, mu_ref, wr_ref, wk_ref, wv_ref, w1_ref, w2_ref,
                     a1_ref, a2_ref, g1_ref, g2_ref, e_ref, et_ref,
                     r_out, cum_out, k_out, v_out, kk_out, b_out, g_out, *, chunk):
    i = pl.program_id(1)
    vec = vec_ref[...]
    gain, w0, a0, k_k, k_a = (vec[j:j + 1] for j in range(5))
    x = x_ref[0]
    h = _rms(x, gain)
    hp = _rms(xp_ref[0][7:8, :], gain)
    hp = jnp.where(i > 0, hp, 0.0)
    rows = lax.broadcasted_iota(jnp.int32, h.shape, 0)
    hprev = jnp.where(rows == 0, hp, pltpu.roll(h, 1, 0))
    xx = hprev - h
    mu = mu_ref[...]
    xr, xw, xk, xv, xa, xg = (_bf(h + xx * mu[j:j + 1]) for j in range(6))
    r = _dot(xr, wr_ref[...])
    k = _dot(xk, wk_ref[...])
    v = _dot(xv, wv_ref[...])
    wl = w0 + _dot(_bf(jnp.tanh(_dot(xw, w1_ref[...]))), w2_ref[...])
    a = _sigmoid(a0 + _dot(_bf(_dot(xa, a1_ref[...])), a2_ref[...]))
    g = _dot(_bf(_sigmoid(_dot(xg, g1_ref[...]))), g2_ref[...])
    kkr = k * k_k
    ss = _headsum(kkr * kkr, e_ref[...], et_ref[...])
    kk = kkr / jnp.maximum(jnp.sqrt(ss), 1e-12)
    lw = _sigmoid(wl) * (-math.exp(-0.5))
    tm = lw.shape[0]
    rr = lax.broadcasted_iota(jnp.int32, (tm, tm), 0)
    cc = lax.broadcasted_iota(jnp.int32, (tm, tm), 1)
    same_chunk = jnp.bitwise_xor(rr, cc) < chunk
    in_chunk = jnp.where(rr >= cc, jnp.where(same_chunk, 1.0, 0.0), 0.0).astype(BF16)
    hi, lo = _split2(lw)
    cum_out[0] = _dot(in_chunk, hi) + _dot(in_chunk, lo)
    r_out[0] = _bf(r)
    k_out[0] = _bf(k * (1.0 + (a - 1.0) * k_a))
    v_out[0] = _bf(v)
    kk_out[0] = _bf(kk)
    b_out[0] = _bf(kk * a)
    g_out[0] = _bf(g)


def _rwkv_pre(x, vec, mu, wr, wk, wv, w1, w2, a1, a2, g1, g2, e, et, tm, chunk):
    b, s, d = x.shape
    assert tm % chunk == 0
    tok = pl.BlockSpec((1, tm, d), lambda bi, i: (bi, i, 0))
    prev = pl.BlockSpec((1, 8, d), lambda bi, i: (bi, jnp.maximum(i * (tm // 8) - 1, 0), 0))
    consts = [vec, mu, wr, wk, wv, w1, w2, a1, a2, g1, g2, e, et]
    out = lambda dt: jax.ShapeDtypeStruct((b, s, d), dt)
    return pl.pallas_call(
        functools.partial(_rwkv_pre_kernel, chunk=chunk),
        grid=(b, s // tm),
        in_specs=[tok, prev] + [_const_spec(c.shape) for c in consts],
        out_specs=[tok] * 7,
        out_shape=[out(BF16), out(F32)] + [out(BF16)] * 5,
        compiler_params=_params("parallel", "arbitrary"),
        name="rwkv_pre",
    )(x, x, *consts)


def _round_robin(generators):
    results = [None] * len(generators)
    live = list(enumerate(generators))
    while live:
        still = []
        for idx, gen in live:
            try:
                results[idx] = next(gen)
                still.append((idx, gen))
            except StopIteration:
                pass
        live = still
    return results


def _wkv_kernel(r_ref, cum_ref, k_ref, v_ref, kk_ref, b_ref, y_ref, st_ref, *, chunk):
    c_len = chunk
    n_chunks = r_ref.shape[1] // c_len
    n_pairs = r_ref.shape[2] // LANES
    inv_steps = max(c_len.bit_length() - 2, 0)
    lane = lax.broadcasted_iota(jnp.int32, (c_len, LANES), 1)
    head0 = lane < HEAD
    rr = lax.broadcasted_iota(jnp.int32, (c_len, c_len), 0)
    cc = lax.broadcasted_iota(jnp.int32, (c_len, c_len), 1)
    lower = rr >= cc
    strict = rr > cc
    eye = jnp.where(rr == cc, 1.0, 0.0).astype(F32)
    r2 = lax.broadcasted_iota(jnp.int32, (LANES, LANES), 0)
    c2 = lax.broadcasted_iota(jnp.int32, (LANES, LANES), 1)
    same_head = (r2 < HEAD) == (c2 < HEAD)

    @pl.when(pl.program_id(2) == 0)
    def _():
        st_ref[...] = jnp.zeros_like(st_ref)

    first_row = lax.broadcasted_iota(jnp.int32, (c_len, LANES), 0) == 0
    rows = lambda mat, j: mat[j * c_len:(j + 1) * c_len]
    own_head = lambda mat: jnp.where(head0, rows(mat, 0), rows(mat, 1))

    def chunk_pair(sl, lanes, state):
        cum = cum_ref[0, sl, lanes]
        cum_prev = jnp.where(first_row, 0.0, pltpu.roll(cum, 1, 0))
        cum_last = cum[c_len - 1:c_len, :]
        kk = kk_ref[0, sl, lanes].astype(F32)
        b = b_ref[0, sl, lanes].astype(F32)
        k = k_ref[0, sl, lanes].astype(F32)
        vb = v_ref[0, sl, lanes]
        e_neg = jnp.exp(-cum)
        e_end = jnp.exp(cum_last - cum)
        a_t = -kk * jnp.exp(cum_prev)
        r_t = r_ref[0, sl, lanes].astype(F32) * jnp.exp(cum)
        lhs = _bf(jnp.concatenate([jnp.where(head0, a_t, 0.0), jnp.where(head0, 0.0, a_t),
                                   jnp.where(head0, r_t, 0.0), jnp.where(head0, 0.0, r_t)], axis=0))
        gram = _dot_nt(lhs, _bf(jnp.concatenate([b * e_neg, k * e_neg], axis=0)))
        from_state = _dot_nt(_bf(jnp.concatenate([a_t, r_t], axis=0)), _bf(state))
        yield
        gram_b, gram_k = gram[:, :c_len], gram[:, c_len:]
        from_v = _dot(_bf(jnp.concatenate(
            [jnp.where(strict, rows(gram_k, 0), 0.0), jnp.where(strict, rows(gram_k, 1), 0.0),
             jnp.where(lower, rows(gram_k, 2), 0.0), jnp.where(lower, rows(gram_k, 3), 0.0)], axis=0)), vb)
        m_rb = _bf(jnp.concatenate([jnp.where(lower, rows(gram_b, 2), 0.0),
                                    jnp.where(lower, rows(gram_b, 3), 0.0)], axis=0))
        w = rows(from_state, 0) + own_head(from_v[:2 * c_len])
        y1 = rows(from_state, 1) + own_head(from_v[2 * c_len:])
        pw = [jnp.where(strict, rows(gram_b, hd), 0.0) for hd in range(2)]
        inv = [eye + pw[hd] for hd in range(2)]
        if inv_steps:
            pw = [_dot(_bf(pw[hd]), _bf(pw[hd])) for hd in range(2)]
        for step in range(inv_steps):
            yield
            if step < inv_steps - 1:
                both = [_dot(_bf(jnp.concatenate([inv[hd], pw[hd]], axis=0)), _bf(pw[hd])) for hd in range(2)]
                inv = [inv[hd] + rows(both[hd], 0) for hd in range(2)]
                pw = [rows(both[hd], 1) for hd in range(2)]
            else:
                inv = [inv[hd] + _dot(_bf(inv[hd]), _bf(pw[hd])) for hd in range(2)]
        yield
        u = own_head(_dot(_bf(jnp.concatenate(inv, axis=0)), _bf(w)))
        yield
        y = y1 + own_head(_dot(m_rb, _bf(u)))
        uv_t = _bf(jnp.transpose(jnp.concatenate([u, vb.astype(F32)], axis=0)))
        bk = _bf(jnp.concatenate([b * e_end, k * e_end], axis=0))
        yield y, state * jnp.exp(cum_last) + jnp.where(same_head, _dot(uv_t, bk), 0.0)

    def body(c, carry):
        sl = pl.ds(pl.multiple_of(c * c_len, c_len), c_len)
        lanes = [slice(p * LANES, (p + 1) * LANES) for p in range(n_pairs)]
        results = _round_robin([chunk_pair(sl, lanes[p], st_ref[p]) for p in range(n_pairs)])
        for p, (y, new_state) in enumerate(results):
            y_ref[0, sl, lanes[p]] = _bf(y)
            st_ref[p] = new_state
        return carry

    lax.fori_loop(0, n_chunks, body, 0)


def _wkv(r, cum, k, v, kk, bb, chunk, seq_tile, n_pairs):
    b, s, d = r.shape
    width = n_pairs * LANES
    spec = pl.BlockSpec((1, seq_tile, width), lambda bi, hi, si: (bi, si, hi))
    return pl.pallas_call(
        functools.partial(_wkv_kernel, chunk=chunk),
        grid=(b, d // width, s // seq_tile),
        in_specs=[spec] * 6,
        out_specs=spec,
        out_shape=jax.ShapeDtypeStruct((b, s, d), BF16),
        scratch_shapes=[pltpu.VMEM((n_pairs, LANES, LANES), F32)],
        compiler_params=_params("parallel", "parallel", "arbitrary"),
        name="wkv7",
    )(r, cum, k, v, kk, bb)


def _rwkv_post_kernel(y_ref, r_ref, k_ref, v_ref, g_ref, x_ref, vec_ref, wo_ref, e_ref, et_ref, o_ref):
    vec = vec_ref[...]
    lnx_w, lnx_b, r_k = (vec[j:j + 1] for j in range(3))
    e = e_ref[...]
    et = et_ref[...]
    y = y_ref[...].astype(F32)
    v = v_ref[...].astype(F32)
    mean = _headsum(y, e, et) * (1.0 / HEAD)
    dy = y - mean
    var = _headsum(dy * dy, e, et) * (1.0 / HEAD)
    yn = dy * lax.rsqrt(var + GN_EPS) * lnx_w + lnx_b
    bonus = _headsum(r_ref[...].astype(F32) * k_ref[...].astype(F32) * r_k, e, et) * v
    o_ref[...] = x_ref[...] + _dot(_bf((yn + bonus) * g_ref[...].astype(F32)), wo_ref[...])


def _rwkv_post(y, r, k, v, g, x, vec, wo, e, et, tm):
    t, d = x.shape
    tok = pl.BlockSpec((tm, d), lambda i: (i, 0))
    consts = [vec, wo, e, et]
    return pl.pallas_call(
        _rwkv_post_kernel,
        grid=(t // tm,),
        in_specs=[tok] * 6 + [_const_spec(c.shape) for c in consts],
        out_specs=tok,
        out_shape=jax.ShapeDtypeStruct((t, d), F32),
        compiler_params=_params("parallel"),
        name="rwkv_post",
    )(y, r, k, v, g, x, *consts)


def _ffn_kernel(x_ref, gain_ref, w1_ref, w3_ref, w2_ref, o_ref, h_ref, acc_ref):
    f = pl.program_id(1)

    @pl.when(f == 0)
    def _():
        h_ref[...] = _bf(_rms(x_ref[...], gain_ref[...]))
        acc_ref[...] = jnp.zeros_like(acc_ref)

    h = h_ref[...]
    u = _dot(h, w1_ref[...])
    act = _bf(u * _sigmoid(u) * _dot(h, w3_ref[...]))
    acc_ref[...] += _dot(act, w2_ref[...])

    @pl.when(f == pl.num_programs(1) - 1)
    def _():
        o_ref[...] = x_ref[...] + acc_ref[...]


def _ffn(x, gain, w1, w3, w2, tm, tf):
    t, d = x.shape
    ff = w1.shape[1]
    tok = pl.BlockSpec((tm, d), lambda i, f: (i, 0))
    return pl.pallas_call(
        _ffn_kernel,
        grid=(t // tm, ff // tf),
        in_specs=[tok, _const_spec(gain.shape),
                  pl.BlockSpec((d, tf), lambda i, f: (0, f)),
                  pl.BlockSpec((d, tf), lambda i, f: (0, f)),
                  pl.BlockSpec((tf, d), lambda i, f: (f, 0))],
        out_specs=tok,
        out_shape=jax.ShapeDtypeStruct((t, d), F32),
        scratch_shapes=[pltpu.VMEM((tm, d), BF16), pltpu.VMEM((tm, d), F32)],
        compiler_params=_params("parallel", "arbitrary"),
        name="ffn_dense",
    )(x, gain, w1, w3, w2)


BRANCH_WIDTH = HEADS_PER_BRANCH * HEAD


def _attn_proj_kernel(x_ref, gains_ref, wq_ref, wkv_ref, *refs):
    outs, (q_scr, kv_scr) = refs[:-2], refs[-2:]
    x = x_ref[0]
    tm = x.shape[0]
    n = x * lax.rsqrt(jnp.mean(x * x, axis=-1, keepdims=True) + RMS_EPS)
    gains = gains_ref[...]
    q_tiles = q_scr.shape[0]
    q = _dot(_bf(n * gains[0:1]), wq_ref[...]) * (1.0 / HEAD ** 0.5)
    kv = _dot(_bf(n * gains[1:2]), wkv_ref[...])
    for ti in range(q_tiles):
        q_scr[ti] = q[:, ti * LANES:(ti + 1) * LANES]
    for ti in range(2 * q_tiles):
        kv_scr[ti] = kv[:, ti * LANES:(ti + 1) * LANES]
    tiles_br = BRANCH_WIDTH // LANES
    for br, (_, dil) in enumerate(BRANCHES):
        q_out, k_out, v_out = outs[3 * br:3 * br + 3]
        for res in range(dil):
            rows = pl.ds(res, tm // dil, stride=dil) if dil > 1 else slice(None)
            for ti in range(tiles_br):
                src = br * tiles_br + ti
                lanes = slice(ti * LANES, (ti + 1) * LANES)
                q_out[0, res, :, lanes] = _bf(q_scr[src, rows, :])
                k_out[0, res, :, lanes] = _bf(kv_scr[src, rows, :])
                v_out[0, res, :, lanes] = _bf(kv_scr[q_tiles + src, rows, :])


def _attn_proj(x, gains, wq, wkv, tm):
    b, s, d = x.shape
    qw = wq.shape[1]
    out_specs, out_shape = [], []
    for _, dil in BRANCHES:
        assert tm % (16 * dil) == 0
        out_specs += [pl.BlockSpec((1, dil, tm // dil, BRANCH_WIDTH), lambda bi, i: (bi, 0, i, 0))] * 3
        out_shape += [jax.ShapeDtypeStruct((b, dil, s // dil, BRANCH_WIDTH), BF16)] * 3
    return pl.pallas_call(
        _attn_proj_kernel,
        grid=(b, s // tm),
        in_specs=[pl.BlockSpec((1, tm, d), lambda bi, i: (bi, i, 0)),
                  _const_spec(gains.shape), _const_spec(wq.shape), _const_spec(wkv.shape)],
        out_specs=out_specs,
        out_shape=out_shape,
        scratch_shapes=[pltpu.VMEM((qw // LANES, tm, LANES), F32),
                        pltpu.VMEM((2 * qw // LANES, tm, LANES), F32)],
        compiler_params=_params("parallel", "parallel"),
        name="attn_proj",
    )(x, gains, wq, wkv)


def _attn_kernel(slopes_ref, q_ref, k_ref, v_ref, o_ref, lse_ref, *, branch, dilation, qb):
    sub_len = q_ref.shape[2]
    n_blk = sub_len // qb
    n_pairs = q_ref.shape[3] // LANES
    rel = (lax.broadcasted_iota(jnp.int32, (qb, qb), 0)
           - lax.broadcasted_iota(jnp.int32, (qb, qb), 1)).astype(F32)
    dist_d = jnp.where(rel >= 0, rel, -NEG)
    dist_p = jnp.where(rel <= 0, rel + float(qb), -NEG)
    lane = lax.broadcasted_iota(jnp.int32, (qb, LANES), 1)
    head0 = lane < HEAD
    lane_row = lax.broadcasted_iota(jnp.int32, (1, LANES), 1)
    head_mask = [jnp.where(lane_row < HEAD, 1.0, 0.0).astype(BF16),
                 jnp.where(lane_row < HEAD, 0.0, 1.0).astype(BF16)]

    def block(i, carry):
        cur = pl.ds(pl.multiple_of(i * qb, qb), qb)
        if n_blk > 1:
            prv = pl.ds(pl.multiple_of(jnp.maximum(i - 1, 0) * qb, qb), qb)
            has_prev = jnp.where(i > 0, 1.0, -NEG)

        def one_head(p, j):
            lanes = slice(p * LANES, (p + 1) * LANES)
            step = slopes_ref[branch * HEADS_PER_BRANCH + 2 * p + j] * float(dilation)
            qh = q_ref[0, 0, cur, lanes] * head_mask[j]
            sd = _dot_nt(qh, k_ref[0, 0, cur, lanes]) - step * dist_d
            if n_blk > 1:
                sp = _dot_nt(qh, k_ref[0, 0, prv, lanes]) - (step * has_prev) * dist_p
            yield
            m = jnp.max(sd, axis=-1, keepdims=True)
            if n_blk > 1:
                m = jnp.maximum(m, jnp.max(sp, axis=-1, keepdims=True))
            pd = jnp.exp(sd - m)
            den = jnp.sum(pd, axis=-1, keepdims=True)
            acc = _dot(_bf(pd), v_ref[0, 0, cur, lanes])
            if n_blk > 1:
                pp = jnp.exp(sp - m)
                den = den + jnp.sum(pp, axis=-1, keepdims=True)
                acc = acc + _dot(_bf(pp), v_ref[0, 0, prv, lanes])
            yield
            yield acc / den, m + jnp.log(den)

        res = _round_robin([one_head(p, j) for p in range(n_pairs) for j in range(2)])
        for p in range(n_pairs):
            lanes = slice(p * LANES, (p + 1) * LANES)
            (o_a, l_a), (o_b, l_b) = res[2 * p], res[2 * p + 1]
            o_ref[0, 0, cur, lanes] = _bf(jnp.where(head0, o_a, o_b))
            lse_ref[0, 0, cur, lanes] = jnp.where(head0, l_a, l_b)
        return carry

    lax.fori_loop(0, n_blk, block, 0, unroll=2 if n_blk % 2 == 0 else 1)


def _attn_branch(slopes, q, k, v, branch):
    window, dilation = BRANCHES[branch]
    assert window % dilation == 0 and window // dilation == LANES
    b, _, sub_len, width = q.shape
    qb = min(LANES, sub_len)
    assert sub_len % qb == 0
    spec = pl.BlockSpec((1, 1, sub_len, width), lambda bi, ri: (bi, ri, 0, 0))
    return pl.pallas_call(
        functools.partial(_attn_kernel, branch=branch, dilation=dilation, qb=qb),
        grid=(b, dilation),
        in_specs=[pl.BlockSpec(memory_space=pltpu.SMEM)] + [spec] * 3,
        out_specs=[spec] * 2,
        out_shape=[jax.ShapeDtypeStruct(q.shape, BF16), jax.ShapeDtypeStruct(q.shape, F32)],
        compiler_params=_params("parallel", "parallel"),
        name=f"dilated_attn_{branch}",
    )(slopes, q, k, v)


def _attn_out_kernel(o0, o1, o2, l0, l1, l2, x_ref, gain_ref, wo_ref, rhi_ref, rlo_ref, rb_ref,
                     x_out, h_out, gate_out, sel_out, rank_out, cnt_out, cnt_ref, o_scr, l_scr):
    @pl.when((pl.program_id(0) == 0) & (pl.program_id(1) == 0))
    def _():
        cnt_ref[...] = jnp.zeros_like(cnt_ref)

    tm = x_ref.shape[0]
    tiles_br = BRANCH_WIDTH // LANES
    for br, (o_ref, l_ref) in enumerate(((o0, l0), (o1, l1), (o2, l2))):
        dil = BRANCHES[br][1]
        for res in range(dil):
            rows = pl.ds(res, tm // dil, stride=dil) if dil > 1 else slice(None)
            for ti in range(tiles_br):
                lanes = slice(ti * LANES, (ti + 1) * LANES)
                o_scr[br, ti, rows, :] = o_ref[0, res, :, lanes].astype(F32)
                l_scr[br, ti, rows, :] = l_ref[0, res, :, lanes]
    merged = []
    for ti in range(tiles_br):
        ls = [l_scr[br, ti] for br in range(3)]
        m = jnp.maximum(jnp.maximum(ls[0], ls[1]), ls[2])
        ws = [jnp.exp(l - m) for l in ls]
        merged.append(_bf((ws[0] * o_scr[0, ti] + ws[1] * o_scr[1, ti] + ws[2] * o_scr[2, ti])
                          / (ws[0] + ws[1] + ws[2])))
    x = x_ref[...] + _dot(jnp.concatenate(merged, axis=1), wo_ref[...])
    x_out[...] = x
    h = _rms(x, gain_ref[...])
    h_out[...] = h
    hhi, hlo = _split2(h)
    logits = (_dot(hhi, rhi_ref[...]) + _dot(hhi, rlo_ref[...]) + _dot(hlo, rhi_ref[...])) + rb_ref[...]
    lane = lax.broadcasted_iota(jnp.int32, logits.shape, 1)
    m1 = jnp.max(logits, axis=-1, keepdims=True)
    i1 = jnp.min(jnp.where(logits == m1, lane, LANES), axis=-1, keepdims=True)
    rest = jnp.where(lane == i1, -3e38, logits)
    m2 = jnp.max(rest, axis=-1, keepdims=True)
    i2 = jnp.min(jnp.where(rest == m2, lane, LANES), axis=-1, keepdims=True)
    ex = jnp.exp(m2 - m1)
    gate_out[...] = (jnp.where(lane == i1, 1.0 / (1.0 + ex), 0.0)
                     + jnp.where(lane == i2, ex / (1.0 + ex), 0.0))
    sel = jnp.where((lane == i1) | (lane == i2), 1.0, 0.0)
    sel_out[...] = sel
    rr = lax.broadcasted_iota(jnp.int32, (tm, tm), 0)
    cc = lax.broadcasted_iota(jnp.int32, (tm, tm), 1)
    before = jnp.where(rr > cc, 1.0, 0.0).astype(BF16)
    rank_out[...] = cnt_ref[...] + _dot(before, _bf(sel))
    cnt_ref[...] += jnp.sum(sel, axis=0, keepdims=True)
    cnt_out[...] = cnt_ref[...]


def _attn_out(os_, ls_, x, gain, wo, rhi, rlo, rb, b, tm):
    t, d = x.shape
    tiles = t // (b * tm)
    tok = pl.BlockSpec((tm, d), lambda bi, i: (bi * tiles + i, 0))
    small = pl.BlockSpec((tm, LANES), lambda bi, i: (bi * tiles + i, 0))
    att = [pl.BlockSpec((1, dil, tm // dil, BRANCH_WIDTH), lambda bi, i: (bi, 0, i, 0))
           for _, dil in BRANCHES]
    consts = [gain, wo, rhi, rlo, rb]
    return pl.pallas_call(
        _attn_out_kernel,
        grid=(b, tiles),
        in_specs=att + att + [tok] + [_const_spec(c.shape) for c in consts],
        out_specs=[tok, tok, small, small, small, _const_spec((1, LANES))],
        out_shape=[jax.ShapeDtypeStruct((t, d), F32), jax.ShapeDtypeStruct((t, d), F32),
                   jax.ShapeDtypeStruct((t, LANES), F32), jax.ShapeDtypeStruct((t, LANES), F32),
                   jax.ShapeDtypeStruct((t, LANES), F32), jax.ShapeDtypeStruct((1, LANES), F32)],
        scratch_shapes=[pltpu.VMEM((1, LANES), F32),
                        pltpu.VMEM((len(BRANCHES), BRANCH_WIDTH // LANES, tm, LANES), F32),
                        pltpu.VMEM((len(BRANCHES), BRANCH_WIDTH // LANES, tm, LANES), F32)],
        compiler_params=_params("arbitrary", "arbitrary"),
        name="attn_out_route",
    )(*os_, *ls_, x, *consts)


DMA_UNROLL = 8


def _row_copy(src, src_row, dst, dst_row, sem):
    return pltpu.make_async_copy(src.at[pl.ds(src_row, 1)], dst.at[pl.ds(dst_row, 1)], sem)


def _moe_scatter_kernel(ends_ref, dest_ref, h_ref, out_ref, stage, zeros, sems, zero_sem):
    i = pl.program_id(0)
    last = pl.num_programs(0) - 1
    n_tok = h_ref.shape[0]
    tile = zeros.shape[0]
    slot = lax.rem(i, 2)

    def zero_fill(e):
        if e < N_EXPERTS:
            end = ends_ref[e]
            begin = ends_ref[e - 1] if e else 0
            start, used = jnp.maximum(end - tile, 0), end > begin
        else:
            start = ends_ref[N_EXPERTS - 1] + (e - N_EXPERTS) * tile
            used = start < out_ref.shape[0]
            start = jnp.minimum(start, out_ref.shape[0] - tile)
        dst = out_ref.at[pl.ds(pl.multiple_of(start, tile), tile)]
        return used, pltpu.make_async_copy(zeros, dst, zero_sem)

    @pl.when(i == 0)
    def _():
        zeros[...] = jnp.zeros_like(zeros)
        for e in range(2 * N_EXPERTS):
            used, copy = zero_fill(e)
            pl.when(used)(copy.start)
        for e in range(2 * N_EXPERTS):
            used, copy = zero_fill(e)
            pl.when(used)(copy.wait)

    stage[slot] = h_ref[...]

    def issue(j, carry):
        for k in range(2):
            _row_copy(stage.at[slot], j, out_ref, dest_ref[2 * j + k], sems.at[slot]).start(priority=k)
        return carry

    def drain(which):
        def body(j, carry):
            for _ in range(2):
                _row_copy(stage.at[which], 0, out_ref, 0, sems.at[which]).wait()
            return carry
        lax.fori_loop(0, n_tok, body, 0, unroll=DMA_UNROLL)

    lax.fori_loop(0, n_tok, issue, 0, unroll=DMA_UNROLL)
    pl.when(i > 0)(lambda: drain(1 - slot))
    pl.when(i == last)(lambda: drain(slot))


def _moe_scatter(ends, dest_flat, h, rows, ts, tile):
    t, d = h.shape
    return pl.pallas_call(
        _moe_scatter_kernel,
        grid=(t // ts,),
        in_specs=[pl.BlockSpec(memory_space=pltpu.SMEM),
                  pl.BlockSpec((2 * ts,), lambda i: (i,), memory_space=pltpu.SMEM),
                  pl.BlockSpec((ts, d), lambda i: (i, 0))],
        out_specs=pl.BlockSpec(memory_space=pl.ANY),
        out_shape=jax.ShapeDtypeStruct((rows, d), F32),
        scratch_shapes=[pltpu.VMEM((2, ts, d), F32), pltpu.VMEM((tile, d), F32),
                        pltpu.SemaphoreType.DMA((2,)), pltpu.SemaphoreType.DMA(())],
        compiler_params=_params("arbitrary"),
        name="moe_scatter",
    )(ends, dest_flat, h)


def _moe_ffn_kernel(src_ref, te_ref, nv_ref, x_ref, w1_ref, w3_ref, w2_ref, o_ref, acc_ref):
    del src_ref, te_ref
    i = pl.program_id(0)
    f = pl.program_id(1)

    @pl.when(i < nv_ref[0])
    def _():
        @pl.when(f == 0)
        def _():
            acc_ref[...] = jnp.zeros_like(acc_ref)

        h = _bf(x_ref[...])
        u = _dot(h, w1_ref[0])
        act = _bf(u * _sigmoid(u) * _dot(h, w3_ref[0]))
        acc_ref[...] += _dot(act, w2_ref[0])

        @pl.when(f == pl.num_programs(1) - 1)
        def _():
            o_ref[...] = acc_ref[...]

    @pl.when(i >= nv_ref[0])
    def _():
        o_ref[...] = jnp.zeros_like(o_ref)


def _moe_ffn(tile_src, tile_expert, n_valid, xs, w1, w3, w2, tm, tf):
    rows, d = xs.shape
    ff = w1.shape[2]
    tok = pl.BlockSpec((tm, d), lambda i, f, src, te, nv: (src[i], 0))
    return pl.pallas_call(
        _moe_ffn_kernel,
        grid_spec=pltpu.PrefetchScalarGridSpec(
            num_scalar_prefetch=3,
            grid=(rows // tm, ff // tf),
            in_specs=[tok,
                      pl.BlockSpec((1, d, tf), lambda i, f, src, te, nv: (te[i], 0, f)),
                      pl.BlockSpec((1, d, tf), lambda i, f, src, te, nv: (te[i], 0, f)),
                      pl.BlockSpec((1, tf, d), lambda i, f, src, te, nv: (te[i], f, 0))],
            out_specs=pl.BlockSpec((tm, d), lambda i, f, src, te, nv: (i, 0)),
            scratch_shapes=[pltpu.VMEM((tm, d), F32)],
        ),
        out_shape=jax.ShapeDtypeStruct((rows, d), F32),
        compiler_params=_params("arbitrary", "arbitrary"),
        name="moe_experts",
    )(tile_src, tile_expert, n_valid, xs, w1, w3, w2)


def _moe_combine_kernel(dest_ref, next_ref, x_ref, gate_ref, gain_ref, y_hbm, o_ref, buf, sems):
    i = pl.program_id(0)
    last = pl.num_programs(0) - 1
    n_tok = x_ref.shape[0]
    slot = lax.rem(i, 2)

    def gather(idx_ref, into):
        def body(j, carry):
            for k in range(2):
                _row_copy(y_hbm, idx_ref[2 * j + k], buf.at[into, k], j, sems.at[into]).start(priority=k)
            return carry
        lax.fori_loop(0, n_tok, body, 0, unroll=DMA_UNROLL)

    def drain(j, carry):
        for k in range(2):
            _row_copy(y_hbm, 0, buf.at[slot, k], 0, sems.at[slot]).wait()
        return carry

    pl.when(i == 0)(lambda: gather(dest_ref, slot))
    pl.when(i < last)(lambda: gather(next_ref, 1 - slot))
    lax.fori_loop(0, n_tok, drain, 0, unroll=DMA_UNROLL)
    gate = gate_ref[...]
    x = x_ref[...] + gate[:, 0:1] * buf[slot, 0] + gate[:, 1:2] * buf[slot, 1]
    o_ref[...] = _rms(x, gain_ref[...])


def _moe_combine(dest_flat, x, gates, gain, ys, ts):
    t, d = x.shape
    n = t // ts
    tok = pl.BlockSpec((ts, d), lambda i: (i, 0))
    return pl.pallas_call(
        _moe_combine_kernel,
        grid=(n,),
        in_specs=[pl.BlockSpec((2 * ts,), lambda i: (i,), memory_space=pltpu.SMEM),
                  pl.BlockSpec((2 * ts,), lambda i: (jnp.minimum(i + 1, n - 1),), memory_space=pltpu.SMEM),
                  tok, pl.BlockSpec((ts, 2), lambda i: (i, 0)), _const_spec(gain.shape),
                  pl.BlockSpec(memory_space=pl.ANY)],
        out_specs=tok,
        out_shape=jax.ShapeDtypeStruct((t, d), F32),
        scratch_shapes=[pltpu.VMEM((2, 2, ts, d), F32), pltpu.SemaphoreType.DMA((2,))],
        compiler_params=_params("arbitrary"),
        name="moe_combine",
    )(dest_flat, dest_flat, x, gates, gain, ys)


def _tile(n, want):
    t = min(n, want)
    assert n % t == 0
    return t


def _pad_cols(w, n):
    return jnp.pad(w, ((0, 0), (0, n - w.shape[1])))


def _pad_rows(w, n):
    return jnp.pad(w, ((0, n - w.shape[0]), (0, 0)))


def kernel(x, norm_gain, rwkv_mu, rwkv_wr, rwkv_wk, rwkv_wv, rwkv_w0, rwkv_w1, rwkv_w2, rwkv_a0, rwkv_a1, rwkv_a2, rwkv_g1, rwkv_g2, rwkv_k_k, rwkv_k_a, rwkv_r_k, rwkv_lnx_w, rwkv_lnx_b, rwkv_wo, kv_norm_gain, w_kv, attn_wq, attn_wo, ffn_w1, ffn_w3, ffn_w2, moe_router, moe_router_bias, moe_w1, moe_w3, moe_w2, final_norm_gain):
    b, s, d = x.shape
    t = b * s
    n_heads = d // HEAD
    assert norm_gain.shape[0] == 2 and d % LANES == 0 and s % 8 == 0

    chan_head = jnp.arange(d) // HEAD
    e = (chan_head[:, None] == jnp.arange(LANES)[None, :]).astype(BF16)
    et = e.T

    zeros = jnp.zeros((d,), F32)
    vec_pre = jnp.stack([norm_gain[0, 0], rwkv_w0[0], rwkv_a0[0], rwkv_k_k[0], rwkv_k_a[0],
                         zeros, zeros, zeros])
    tm_pre = _tile(s, 256)
    chunk = _tile(s, LANES)
    r, cum, k, v, kk, kka, g = _rwkv_pre(
        x, vec_pre, rwkv_mu[0], _bf(rwkv_wr[0]), _bf(rwkv_wk[0]), _bf(rwkv_wv[0]),
        _bf(_pad_cols(rwkv_w1[0], LANES)), _bf(_pad_rows(rwkv_w2[0], LANES)),
        _bf(_pad_cols(rwkv_a1[0], LANES)), _bf(_pad_rows(rwkv_a2[0], LANES)),
        _bf(rwkv_g1[0]), _bf(rwkv_g2[0]), e, et, tm_pre, chunk)
    y = _wkv(r, cum, k, v, kk, kka, chunk=chunk, seq_tile=_tile(s, 512), n_pairs=4)

    flat = lambda arr: arr.reshape(t, d)
    tm = _tile(t, 512)
    vec_post = jnp.stack([rwkv_lnx_w[0], rwkv_lnx_b[0], rwkv_r_k[0].reshape(d),
                          zeros, zeros, zeros, zeros, zeros])
    x1 = _rwkv_post(flat(y), flat(r), flat(k), flat(v), flat(g), flat(x), vec_post,
                    _bf(rwkv_wo[0]), e, et, tm)

    ff = ffn_w1.shape[2]
    tf = ff // 2 if (ff // 2) % LANES == 0 else ff
    x2 = _ffn(x1, norm_gain[0, 1][None, :], _bf(ffn_w1[0]), _bf(ffn_w3[0]), _bf(ffn_w2[0]), tm, tf)

    n_slopes = len(BRANCHES) * HEADS_PER_BRANCH
    slopes = jnp.exp2(-ALIBI_MAX * (jnp.arange(n_slopes, dtype=F32) + 1.0) / n_slopes)
    gains = jnp.stack([norm_gain[1, 0], kv_norm_gain] + [zeros] * 6)
    tm_a = _tile(s, 512)
    qkv = _attn_proj(x2.reshape(b, s, d), gains, _bf(attn_wq[0]), _bf(w_kv), tm_a)
    os_, ls_ = [], []
    for br in range(len(BRANCHES)):
        o_br, l_br = _attn_branch(slopes, *qkv[3 * br:3 * br + 3], br)
        os_.append(o_br)
        ls_.append(l_br)

    router = _pad_cols(moe_router[0], LANES)
    rhi, rlo = _split2(router)
    rbias = jnp.full((1, LANES), NEG, F32).at[0, :N_EXPERTS].set(moe_router_bias[0])
    x3, h4, gate, sel, rank, cnt = _attn_out(os_, ls_, x2, norm_gain[1, 1][None, :], _bf(attn_wo[0]),
                                             rhi, rlo, rbias, b, tm_a)

    tm_e = 512 if t >= 4096 else 128
    ts = _tile(t, 512)
    sel8 = sel[:, :N_EXPERTS] > 0.5
    counts = cnt[0, :N_EXPERTS].astype(jnp.int32)
    padded = ((counts + tm_e - 1) // tm_e) * tm_e
    ends = jnp.cumsum(padded)
    dest8 = (ends - padded)[None, :] + rank[:, :N_EXPERTS].astype(jnp.int32)
    e0 = jnp.argmax(sel8, axis=1)
    e1 = N_EXPERTS - 1 - jnp.argmax(sel8[:, ::-1], axis=1)
    pick = lambda arr: jnp.stack([jnp.take_along_axis(arr, e0[:, None], 1)[:, 0],
                                  jnp.take_along_axis(arr, e1[:, None], 1)[:, 0]], axis=1)
    dest = pick(dest8).reshape(2 * t)
    gates = pick(gate[:, :N_EXPERTS])
    rows = 2 * t + N_EXPERTS * tm_e
    n_tiles = rows // tm_e
    n_valid = (ends[-1] // tm_e).astype(jnp.int32)
    tile_src = jnp.minimum(jnp.arange(n_tiles, dtype=jnp.int32), n_valid - 1)
    tile_expert = jnp.minimum(
        jnp.sum(tile_src[:, None] * tm_e >= ends[None, :], axis=1), N_EXPERTS - 1).astype(jnp.int32)

    xs = _moe_scatter(ends.astype(jnp.int32), dest, h4, rows, ts, tm_e)
    ffe = moe_w1.shape[3]
    tfe = ffe // 2 if (ffe // 2) % LANES == 0 else ffe
    ys = _moe_ffn(tile_src, tile_expert, n_valid.reshape(1), xs,
                  _bf(moe_w1[0]), _bf(moe_w3[0]), _bf(moe_w2[0]), tm_e, tfe)
    out = _moe_combine(dest, x3, gates, final_norm_gain[None, :], ys, ts)
    return out.reshape(b, s, d)
```

```python
import functools
import math

import jax
import jax.numpy as jnp
from jax import lax
from jax.experimental import pallas as pl
from jax.experimental.pallas import tpu as pltpu

F32 = jnp.float32
BF16 = jnp.bfloat16

RMS_EPS = 1e-5
GN_EPS = 64e-5
HEAD = 64
LANES = 128
BRANCHES = ((128, 1), (512, 4), (2048, 16))
HEADS_PER_BRANCH = 8
ALIBI_MAX = 8.0
N_EXPERTS = 8
NEG = -1e30
VMEM_LIMIT_BYTES = 56 * 1024 * 1024


def _params(*sem):
    return pltpu.CompilerParams(dimension_semantics=sem, vmem_limit_bytes=VMEM_LIMIT_BYTES)


def _dot(a, b):
    return jnp.dot(a, b, preferred_element_type=F32)


def _dot_nt(a, b):
    return lax.dot_general(a, b, (((1,), (1,)), ((), ())), preferred_element_type=F32)


def _bf(x):
    return x.astype(BF16)


def _split2(x):
    hi = x.astype(BF16)
    lo = (x - hi.astype(F32)).astype(BF16)
    return hi, lo


def _sigmoid(z):
    return 1.0 / (1.0 + jnp.exp(-z))


def _rms(x, gain):
    return x * lax.rsqrt(jnp.mean(x * x, axis=-1, keepdims=True) + RMS_EPS) * gain


def _headsum(x, e, et):
    return _dot(_bf(_dot(_bf(x), e)), et)


def _const_spec(shape):
    nd = len(shape)
    return pl.BlockSpec(shape, lambda *_: (0,) * nd)


def _rwkv_pre_kernel(x_ref, xp_ref, vec_ref, mu_ref, wr_ref, wk_ref, wv_ref, w1_ref, w2_ref,
                     a1_ref, a2_ref, g1_ref, g2_ref, e_ref, et_ref,
                     r_out, cum_out, k_out, v_out, kk_out, b_out, g_out, *, chunk):
    i = pl.program_id(1)
    vec = vec_ref[...]
    gain, w0, a0, k_k, k_a = (vec[j:j + 1] for j in range(5))
    x = x_ref[0]
    h = _rms(x, gain)
    hp = _rms(xp_ref[0][7:8, :], gain)
    hp = jnp.where(i > 0, hp, 0.0)
    rows = lax.broadcasted_iota(jnp.int32, h.shape, 0)
    hprev = jnp.where(rows == 0, hp, pltpu.roll(h, 1, 0))
    xx = hprev - h
    mu = mu_ref[...]
    xr, xw, xk, xv, xa, xg = (_bf(h + xx * mu[j:j + 1]) for j in range(6))
    r = _dot(xr, wr_ref[...])
    k = _dot(xk, wk_ref[...])
    v = _dot(xv, wv_ref[...])
    wl = w0 + _dot(_bf(jnp.tanh(_dot(xw, w1_ref[...]))), w2_ref[...])
    a = _sigmoid(a0 + _dot(_bf(_dot(xa, a1_ref[...])), a2_ref[...]))
    g = _dot(_bf(_sigmoid(_dot(xg, g1_ref[...]))), g2_ref[...])
    kkr = k * k_k
    ss = _headsum(kkr * kkr, e_ref[...], et_ref[...])
    kk = kkr / jnp.maximum(jnp.sqrt(ss), 1e-12)
    lw = _sigmoid(wl) * (-math.exp(-0.5))
    tm = lw.shape[0]
    rr = lax.broadcasted_iota(jnp.int32, (tm, tm), 0)
    cc = lax.broadcasted_iota(jnp.int32, (tm, tm), 1)
    same_chunk = jnp.bitwise_xor(rr, cc) < chunk
    in_chunk = jnp.where(rr >= cc, jnp.where(same_chunk, 1.0, 0.0), 0.0).astype(BF16)
    hi, lo = _split2(lw)
    cum_out[0] = _dot(in_chunk, hi) + _dot(in_chunk, lo)
    r_out[0] = _bf(r)
    k_out[0] = _bf(k * (1.0 + (a - 1.0) * k_a))
    v_out[0] = _bf(v)
    kk_out[0] = _bf(kk)
    b_out[0] = _bf(kk * a)
    g_out[0] = _bf(g)


def _rwkv_pre(x, vec, mu, wr, wk, wv, w1, w2, a1, a2, g1, g2, e, et, tm, chunk):
    b, s, d = x.shape
    assert tm % chunk == 0
    tok = pl.BlockSpec((1, tm, d), lambda bi, i: (bi, i, 0))
    prev = pl.BlockSpec((1, 8, d), lambda bi, i: (bi, jnp.maximum(i * (tm // 8) - 1, 0), 0))
    consts = [vec, mu, wr, wk, wv, w1, w2, a1, a2, g1, g2, e, et]
    out = lambda dt: jax.ShapeDtypeStruct((b, s, d), dt)
    return pl.pallas_call(
        functools.partial(_rwkv_pre_kernel, chunk=chunk),
        grid=(b, s // tm),
        in_specs=[tok, prev] + [_const_spec(c.shape) for c in consts],
        out_specs=[tok] * 7,
        out_shape=[out(BF16), out(F32)] + [out(BF16)] * 5,
        compiler_params=_params("parallel", "arbitrary"),
        name="rwkv_pre",
    )(x, x, *consts)


def _round_robin(generators):
    results = [None] * len(generators)
    live = list(enumerate(generators))
    while live:
        still = []
        for idx, gen in live:
            try:
                results[idx] = next(gen)
                still.append((idx, gen))
            except StopIteration:
                pass
        live = still
    return results


def _wkv_kernel(r_ref, cum_ref, k_ref, v_ref, kk_ref, b_ref, y_ref, st_ref, *, chunk):
    c_len = chunk
    n_chunks = r_ref.shape[1] // c_len
    n_pairs = r_ref.shape[2] // LANES
    inv_steps = max(c_len.bit_length() - 2, 0)
    lane = lax.broadcasted_iota(jnp.int32, (c_len, LANES), 1)
    head0 = lane < HEAD
    rr = lax.broadcasted_iota(jnp.int32, (c_len, c_len), 0)
    cc = lax.broadcasted_iota(jnp.int32, (c_len, c_len), 1)
    lower = rr >= cc
    strict = rr > cc
    eye = jnp.where(rr == cc, 1.0, 0.0).astype(F32)
    r2 = lax.broadcasted_iota(jnp.int32, (LANES, LANES), 0)
    c2 = lax.broadcasted_iota(jnp.int32, (LANES, LANES), 1)
    same_head = (r2 < HEAD) == (c2 < HEAD)

    @pl.when(pl.program_id(2) == 0)
    def _():
        st_ref[...] = jnp.zeros_like(st_ref)

    first_row = lax.broadcasted_iota(jnp.int32, (c_len, LANES), 0) == 0
    rows = lambda mat, j: mat[j * c_len:(j + 1) * c_len]
    own_head = lambda mat: jnp.where(head0, rows(mat, 0), rows(mat, 1))

    def chunk_pair(sl, lanes, state):
        cum = cum_ref[0, sl, lanes]
        cum_prev = jnp.where(first_row, 0.0, pltpu.roll(cum, 1, 0))
        cum_last = cum[c_len - 1:c_len, :]
        kk = kk_ref[0, sl, lanes].astype(F32)
        b = b_ref[0, sl, lanes].astype(F32)
        k = k_ref[0, sl, lanes].astype(F32)
        vb = v_ref[0, sl, lanes]
        e_neg = jnp.exp(-cum)
        e_end = jnp.exp(cum_last - cum)
        a_t = -kk * jnp.exp(cum_prev)
        r_t = r_ref[0, sl, lanes].astype(F32) * jnp.exp(cum)
        lhs = _bf(jnp.concatenate([jnp.where(head0, a_t, 0.0), jnp.where(head0, 0.0, a_t),
                                   jnp.where(head0, r_t, 0.0), jnp.where(head0, 0.0, r_t)], axis=0))
        gram = _dot_nt(lhs, _bf(jnp.concatenate([b * e_neg, k * e_neg], axis=0)))
        from_state = _dot_nt(_bf(jnp.concatenate([a_t, r_t], axis=0)), _bf(state))
        yield
        gram_b, gram_k = gram[:, :c_len], gram[:, c_len:]
        from_v = _dot(_bf(jnp.concatenate(
            [jnp.where(strict, rows(gram_k, 0), 0.0), jnp.where(strict, rows(gram_k, 1), 0.0),
             jnp.where(lower, rows(gram_k, 2), 0.0), jnp.where(lower, rows(gram_k, 3), 0.0)], axis=0)), vb)
        m_rb = _bf(jnp.concatenate([jnp.where(lower, rows(gram_b, 2), 0.0),
                                    jnp.where(lower, rows(gram_b, 3), 0.0)], axis=0))
        w = rows(from_state, 0) + own_head(from_v[:2 * c_len])
        y1 = rows(from_state, 1) + own_head(from_v[2 * c_len:])
        pw = [jnp.where(strict, rows(gram_b, hd), 0.0) for hd in range(2)]
        inv = [eye + pw[hd] for hd in range(2)]
        if inv_steps:
            pw = [_dot(_bf(pw[hd]), _bf(pw[hd])) for hd in range(2)]
        for step in range(inv_steps):
            yield
            if step < inv_steps - 1:
                both = [_dot(_bf(jnp.concatenate([inv[hd], pw[hd]], axis=0)), _bf(pw[hd])) for hd in range(2)]
                inv = [inv[hd] + rows(both[hd], 0) for hd in range(2)]
                pw = [rows(both[hd], 1) for hd in range(2)]
            else:
                inv = [inv[hd] + _dot(_bf(inv[hd]), _bf(pw[hd])) for hd in range(2)]
        yield
        u = own_head(_dot(_bf(jnp.concatenate(inv, axis=0)), _bf(w)))
        yield
        y = y1 + own_head(_dot(m_rb, _bf(u)))
        uv_t = _bf(jnp.transpose(jnp.concatenate([u, vb.astype(F32)], axis=0)))
        bk = _bf(jnp.concatenate([b * e_end, k * e_end], axis=0))
        yield y, state * jnp.exp(cum_last) + jnp.where(same_head, _dot(uv_t, bk), 0.0)

    def body(c, carry):
        sl = pl.ds(pl.multiple_of(c * c_len, c_len), c_len)
        lanes = [slice(p * LANES, (p + 1) * LANES) for p in range(n_pairs)]
        results = _round_robin([chunk_pair(sl, lanes[p], st_ref[p]) for p in range(n_pairs)])
        for p, (y, new_state) in enumerate(results):
            y_ref[0, sl, lanes[p]] = _bf(y)
            st_ref[p] = new_state
        return carry

    lax.fori_loop(0, n_chunks, body, 0)


def _wkv(r, cum, k, v, kk, bb, chunk, seq_tile, n_pairs):
    b, s, d = r.shape
    width = n_pairs * LANES
    spec = pl.BlockSpec((1, seq_tile, width), lambda bi, hi, si: (bi, si, hi))
    return pl.pallas_call(
        functools.partial(_wkv_kernel, chunk=chunk),
        grid=(b, d // width, s // seq_tile),
        in_specs=[spec] * 6,
        out_specs=spec,
        out_shape=jax.ShapeDtypeStruct((b, s, d), BF16),
        scratch_shapes=[pltpu.VMEM((n_pairs, LANES, LANES), F32)],
        compiler_params=_params("parallel", "parallel", "arbitrary"),
        name="wkv7",
    )(r, cum, k, v, kk, bb)


def _rwkv_post_kernel(y_ref, r_ref, k_ref, v_ref, g_ref, x_ref, vec_ref, wo_ref, e_ref, et_ref, o_ref):
    vec = vec_ref[...]
    lnx_w, lnx_b, r_k = (vec[j:j + 1] for j in range(3))
    e = e_ref[...]
    et = et_ref[...]
    y = y_ref[...].astype(F32)
    v = v_ref[...].astype(F32)
    mean = _headsum(y, e, et) * (1.0 / HEAD)
    dy = y - mean
    var = _headsum(dy * dy, e, et) * (1.0 / HEAD)
    yn = dy * lax.rsqrt(var + GN_EPS) * lnx_w + lnx_b
    bonus = _headsum(r_ref[...].astype(F32) * k_ref[...].astype(F32) * r_k, e, et) * v
    o_ref[...] = x_ref[...] + _dot(_bf((yn + bonus) * g_ref[...].astype(F32)), wo_ref[...])


def _rwkv_post(y, r, k, v, g, x, vec, wo, e, et, tm):
    t, d = x.shape
    tok = pl.BlockSpec((tm, d), lambda i: (i, 0))
    consts = [vec, wo, e, et]
    return pl.pallas_call(
        _rwkv_post_kernel,
        grid=(t // tm,),
        in_specs=[tok] * 6 + [_const_spec(c.shape) for c in consts],
        out_specs=tok,
        out_shape=jax.ShapeDtypeStruct((t, d), F32),
        compiler_params=_params("parallel"),
        name="rwkv_post",
    )(y, r, k, v, g, x, *consts)


def _ffn_kernel(x_ref, gain_ref, w1_ref, w3_ref, w2_ref, o_ref, h_ref, acc_ref):
    f = pl.program_id(1)

    @pl.when(f == 0)
    def _():
        h_ref[...] = _bf(_rms(x_ref[...], gain_ref[...]))
        acc_ref[...] = jnp.zeros_like(acc_ref)

    h = h_ref[...]
    u = _dot(h, w1_ref[...])
    act = _bf(u * _sigmoid(u) * _dot(h, w3_ref[...]))
    acc_ref[...] += _dot(act, w2_ref[...])

    @pl.when(f == pl.num_programs(1) - 1)
    def _():
        o_ref[...] = x_ref[...] + acc_ref[...]


def _ffn(x, gain, w1, w3, w2, tm, tf):
    t, d = x.shape
    ff = w1.shape[1]
    tok = pl.BlockSpec((tm, d), lambda i, f: (i, 0))
    return pl.pallas_call(
        _ffn_kernel,
        grid=(t // tm, ff // tf),
        in_specs=[tok, _const_spec(gain.shape),
                  pl.BlockSpec((d, tf), lambda i, f: (0, f)),
                  pl.BlockSpec((d, tf), lambda i, f: (0, f)),
                  pl.BlockSpec((tf, d), lambda i, f: (f, 0))],
        out_specs=tok,
        out_shape=jax.ShapeDtypeStruct((t, d), F32),
        scratch_shapes=[pltpu.VMEM((tm, d), BF16), pltpu.VMEM((tm, d), F32)],
        compiler_params=_params("parallel", "arbitrary"),
        name="ffn_dense",
    )(x, gain, w1, w3, w2)


BRANCH_WIDTH = HEADS_PER_BRANCH * HEAD


def _attn_proj_kernel(x_ref, gains_ref, wq_ref, wkv_ref, *refs):
    outs, (q_scr, kv_scr) = refs[:-2], refs[-2:]
    x = x_ref[0]
    tm = x.shape[0]
    n = x * lax.rsqrt(jnp.mean(x * x, axis=-1, keepdims=True) + RMS_EPS)
    gains = gains_ref[...]
    q_tiles = q_scr.shape[0]
    q = _dot(_bf(n * gains[0:1]), wq_ref[...]) * (1.0 / HEAD ** 0.5)
    kv = _dot(_bf(n * gains[1:2]), wkv_ref[...])
    for ti in range(q_tiles):
        q_scr[ti] = q[:, ti * LANES:(ti + 1) * LANES]
    for ti in range(2 * q_tiles):
        kv_scr[ti] = kv[:, ti * LANES:(ti + 1) * LANES]
    tiles_br = BRANCH_WIDTH // LANES
    for br, (_, dil) in enumerate(BRANCHES):
        q_out, k_out, v_out = outs[3 * br:3 * br + 3]
        for res in range(dil):
            rows = pl.ds(res, tm // dil, stride=dil) if dil > 1 else slice(None)
            for ti in range(tiles_br):
                src = br * tiles_br + ti
                lanes = slice(ti * LANES, (ti + 1) * LANES)
                q_out[0, res, :, lanes] = _bf(q_scr[src, rows, :])
                k_out[0, res, :, lanes] = _bf(kv_scr[src, rows, :])
                v_out[0, res, :, lanes] = _bf(kv_scr[q_tiles + src, rows, :])


def _attn_proj(x, gains, wq, wkv, tm):
    b, s, d = x.shape
    qw = wq.shape[1]
    out_specs, out_shape = [], []
    for _, dil in BRANCHES:
        assert tm % (16 * dil) == 0
        out_specs += [pl.BlockSpec((1, dil, tm // dil, BRANCH_WIDTH), lambda bi, i: (bi, 0, i, 0))] * 3
        out_shape += [jax.ShapeDtypeStruct((b, dil, s // dil, BRANCH_WIDTH), BF16)] * 3
    return pl.pallas_call(
        _attn_proj_kernel,
        grid=(b, s // tm),
        in_specs=[pl.BlockSpec((1, tm, d), lambda bi, i: (bi, i, 0)),
                  _const_spec(gains.shape), _const_spec(wq.shape), _const_spec(wkv.shape)],
        out_specs=out_specs,
        out_shape=out_shape,
        scratch_shapes=[pltpu.VMEM((qw // LANES, tm, LANES), F32),
                        pltpu.VMEM((2 * qw // LANES, tm, LANES), F32)],
        compiler_params=_params("parallel", "parallel"),
        name="attn_proj",
    )(x, gains, wq, wkv)


def _attn_kernel(slopes_ref, q_ref, k_ref, v_ref, o_ref, lse_ref, *, branch, dilation, qb):
    sub_len = q_ref.shape[2]
    n_blk = sub_len // qb
    n_pairs = q_ref.shape[3] // LANES
    rel = (lax.broadcasted_iota(jnp.int32, (qb, qb), 0)
           - lax.broadcasted_iota(jnp.int32, (qb, qb), 1)).astype(F32)
    dist_d = jnp.where(rel >= 0, rel, -NEG)
    dist_p = jnp.where(rel <= 0, rel + float(qb), -NEG)
    lane = lax.broadcasted_iota(jnp.int32, (qb, LANES), 1)
    head0 = lane < HEAD
    lane_row = lax.broadcasted_iota(jnp.int32, (1, LANES), 1)
    head_mask = [jnp.where(lane_row < HEAD, 1.0, 0.0).astype(BF16),
                 jnp.where(lane_row < HEAD, 0.0, 1.0).astype(BF16)]

    def block(i, carry):
        cur = pl.ds(pl.multiple_of(i * qb, qb), qb)
        if n_blk > 1:
            prv = pl.ds(pl.multiple_of(jnp.maximum(i - 1, 0) * qb, qb), qb)
            has_prev = jnp.where(i > 0, 1.0, -NEG)

        def one_head(p, j):
            lanes = slice(p * LANES, (p + 1) * LANES)
            step = slopes_ref[branch * HEADS_PER_BRANCH + 2 * p + j] * float(dilation)
            qh = q_ref[0, 0, cur, lanes] * head_mask[j]
            sd = _dot_nt(qh, k_ref[0, 0, cur, lanes]) - step * dist_d
            if n_blk > 1:
                sp = _dot_nt(qh, k_ref[0, 0, prv, lanes]) - (step * has_prev) * dist_p
            yield
            m = jnp.max(sd, axis=-1, keepdims=True)
            if n_blk > 1:
                m = jnp.maximum(m, jnp.max(sp, axis=-1, keepdims=True))
            pd = jnp.exp(sd - m)
            den = jnp.sum(pd, axis=-1, keepdims=True)
            acc = _dot(_bf(pd), v_ref[0, 0, cur, lanes])
            if n_blk > 1:
                pp = jnp.exp(sp - m)
                den = den + jnp.sum(pp, axis=-1, keepdims=True)
                acc = acc + _dot(_bf(pp), v_ref[0, 0, prv, lanes])
            yield
            yield acc / den, m + jnp.log(den)

        res = _round_robin([one_head(p, j) for p in range(n_pairs) for j in range(2)])
        lse = jnp.zeros((qb, LANES), F32)
        for p in range(n_pairs):
            lanes = slice(p * LANES, (p + 1) * LANES)
            (o_a, l_a), (o_b, l_b) = res[2 * p], res[2 * p + 1]
            o_ref[0, 0, cur, lanes] = _bf(jnp.where(head0, o_a, o_b))
            lse = jnp.where(lane == 2 * p, l_a, jnp.where(lane == 2 * p + 1, l_b, lse))
        lse_ref[0, 0, cur, :] = lse
        return carry

    lax.fori_loop(0, n_blk, block, 0, unroll=2 if n_blk % 2 == 0 else 1)


def _attn_branch(slopes, q, k, v, branch):
    window, dilation = BRANCHES[branch]
    assert window % dilation == 0 and window // dilation == LANES
    b, _, sub_len, width = q.shape
    qb = min(LANES, sub_len)
    assert sub_len % qb == 0
    spec = pl.BlockSpec((1, 1, sub_len, width), lambda bi, ri: (bi, ri, 0, 0))
    return pl.pallas_call(
        functools.partial(_attn_kernel, branch=branch, dilation=dilation, qb=qb),
        grid=(b, dilation),
        in_specs=[pl.BlockSpec(memory_space=pltpu.SMEM)] + [spec] * 3,
        out_specs=[spec, pl.BlockSpec((1, 1, sub_len, LANES), lambda bi, ri: (bi, ri, 0, 0))],
        out_shape=[jax.ShapeDtypeStruct(q.shape, BF16),
                   jax.ShapeDtypeStruct(q.shape[:3] + (LANES,), F32)],
        compiler_params=_params("parallel", "parallel"),
        name=f"dilated_attn_{branch}",
    )(slopes, q, k, v)


def _attn_out_kernel(o0, o1, o2, l0, l1, l2, x_ref, gain_ref, wo_ref, rhi_ref, rlo_ref, rb_ref, expand_ref,
                     x_out, h_out, route_out, cnt_out, cnt_ref, o_scr, l_scr):
    @pl.when((pl.program_id(0) == 0) & (pl.program_id(1) == 0))
    def _():
        cnt_ref[...] = jnp.zeros_like(cnt_ref)

    tm = x_ref.shape[0]
    tiles_br = BRANCH_WIDTH // LANES
    for br, (o_ref, l_ref) in enumerate(((o0, l0), (o1, l1), (o2, l2))):
        dil = BRANCHES[br][1]
        for res in range(dil):
            rows = pl.ds(res, tm // dil, stride=dil) if dil > 1 else slice(None)
            l_scr[br, rows, :] = l_ref[0, res]
            for ti in range(tiles_br):
                lanes = slice(ti * LANES, (ti + 1) * LANES)
                o_scr[br, ti, rows, :] = o_ref[0, res, :, lanes].astype(F32)
    ls = [l_scr[br] for br in range(3)]
    m = jnp.maximum(jnp.maximum(ls[0], ls[1]), ls[2])
    ws = [jnp.exp(l - m) for l in ls]
    total = ws[0] + ws[1] + ws[2]
    spread = []
    for br in range(3):
        hi, lo = _split2(ws[br] / total)
        spread.append(_dot(hi, expand_ref[...]) + _dot(lo, expand_ref[...]))
    merged = []
    for ti in range(tiles_br):
        lanes = slice(ti * LANES, (ti + 1) * LANES)
        merged.append(_bf(sum(spread[br][:, lanes] * o_scr[br, ti] for br in range(3))))
    x = x_ref[...] + _dot(jnp.concatenate(merged, axis=1), wo_ref[...])
    x_out[...] = x
    h = _rms(x, gain_ref[...])
    h_out[...] = h
    hhi, hlo = _split2(h)
    logits = (_dot(hhi, rhi_ref[...]) + _dot(hhi, rlo_ref[...]) + _dot(hlo, rhi_ref[...])) + rb_ref[...]
    lane = lax.broadcasted_iota(jnp.int32, logits.shape, 1)
    m1 = jnp.max(logits, axis=-1, keepdims=True)
    i1 = jnp.min(jnp.where(logits == m1, lane, LANES), axis=-1, keepdims=True)
    rest = jnp.where(lane == i1, -3e38, logits)
    m2 = jnp.max(rest, axis=-1, keepdims=True)
    i2 = jnp.min(jnp.where(rest == m2, lane, LANES), axis=-1, keepdims=True)
    ex = jnp.exp(m2 - m1)
    sel = jnp.where((lane == i1) | (lane == i2), 1.0, 0.0)
    rr = lax.broadcasted_iota(jnp.int32, (tm, tm), 0)
    cc = lax.broadcasted_iota(jnp.int32, (tm, tm), 1)
    before = jnp.where(rr > cc, 1.0, 0.0).astype(BF16)
    rank = cnt_ref[...] + _dot(before, _bf(sel))
    cnt_ref[...] += jnp.sum(sel, axis=0, keepdims=True)
    cnt_out[...] = cnt_ref[...]
    columns = [i1.astype(F32), i2.astype(F32),
               jnp.sum(jnp.where(lane == i1, rank, 0.0), axis=-1, keepdims=True),
               jnp.sum(jnp.where(lane == i2, rank, 0.0), axis=-1, keepdims=True),
               1.0 / (1.0 + ex), ex / (1.0 + ex)]
    route = jnp.zeros(logits.shape, F32)
    for j, col in enumerate(columns):
        route = jnp.where(lane == j, col, route)
    route_out[...] = route


def _attn_out(os_, ls_, x, gain, wo, rhi, rlo, rb, b, tm):
    t, d = x.shape
    tiles = t // (b * tm)
    tok = pl.BlockSpec((tm, d), lambda bi, i: (bi * tiles + i, 0))
    small = pl.BlockSpec((tm, LANES), lambda bi, i: (bi * tiles + i, 0))
    att = [pl.BlockSpec((1, dil, tm // dil, BRANCH_WIDTH), lambda bi, i: (bi, 0, i, 0))
           for _, dil in BRANCHES]
    lse = [pl.BlockSpec((1, dil, tm // dil, LANES), lambda bi, i: (bi, 0, i, 0)) for _, dil in BRANCHES]
    expand = (jnp.arange(LANES)[:, None] == jnp.arange(BRANCH_WIDTH)[None, :] // HEAD).astype(BF16)
    consts = [gain, wo, rhi, rlo, rb, expand]
    return pl.pallas_call(
        _attn_out_kernel,
        grid=(b, tiles),
        in_specs=att + lse + [tok] + [_const_spec(c.shape) for c in consts],
        out_specs=[tok, tok, small, _const_spec((1, LANES))],
        out_shape=[jax.ShapeDtypeStruct((t, d), F32), jax.ShapeDtypeStruct((t, d), F32),
                   jax.ShapeDtypeStruct((t, LANES), F32), jax.ShapeDtypeStruct((1, LANES), F32)],
        scratch_shapes=[pltpu.VMEM((1, LANES), F32),
                        pltpu.VMEM((len(BRANCHES), BRANCH_WIDTH // LANES, tm, LANES), F32),
                        pltpu.VMEM((len(BRANCHES), tm, LANES), F32)],
        compiler_params=_params("arbitrary", "arbitrary"),
        name="attn_out_route",
    )(*os_, *ls_, x, *consts)


DMA_UNROLL = 8


def _row_copy(src, src_row, dst, dst_row, sem):
    return pltpu.make_async_copy(src.at[pl.ds(src_row, 1)], dst.at[pl.ds(dst_row, 1)], sem)


def _sorted_row(meta_ref, route_ref, j, k):
    return meta_ref[route_ref[4 * j + k]] + route_ref[4 * j + 2 + k]


def _moe_scatter_kernel(meta_ref, route_ref, h_ref, out_ref, zeros, sem, zero_sem):
    n_tok = h_ref.shape[0]
    tile = zeros.shape[0]
    ends = lambda e: meta_ref[N_EXPERTS + e]

    def zero_fill(e):
        if e < N_EXPERTS:
            start, used = jnp.maximum(ends(e) - tile, 0), ends(e) > meta_ref[e]
        else:
            start = ends(N_EXPERTS - 1) + (e - N_EXPERTS) * tile
            used = start < out_ref.shape[0]
            start = jnp.minimum(start, out_ref.shape[0] - tile)
        dst = out_ref.at[pl.ds(pl.multiple_of(start, tile), tile)]
        return used, pltpu.make_async_copy(zeros, dst, zero_sem)

    @pl.when(pl.program_id(0) == 0)
    def _():
        zeros[...] = jnp.zeros_like(zeros)
        for e in range(2 * N_EXPERTS):
            used, copy = zero_fill(e)
            pl.when(used)(copy.start)
        for e in range(2 * N_EXPERTS):
            used, copy = zero_fill(e)
            pl.when(used)(copy.wait)

    def issue(j, carry):
        for k in range(2):
            _row_copy(h_ref, j, out_ref, _sorted_row(meta_ref, route_ref, j, k), sem).start(priority=k)
        return carry

    def drain(j, carry):
        for _ in range(2):
            _row_copy(h_ref, 0, out_ref, 0, sem).wait()
        return carry

    lax.fori_loop(0, n_tok, issue, 0, unroll=DMA_UNROLL)
    lax.fori_loop(0, n_tok, drain, 0, unroll=DMA_UNROLL)


def _moe_scatter(meta, route_flat, h, rows, ts, tile):
    t, d = h.shape
    return pl.pallas_call(
        _moe_scatter_kernel,
        grid=(t // ts,),
        in_specs=[pl.BlockSpec(memory_space=pltpu.SMEM),
                  pl.BlockSpec((4 * ts,), lambda i: (i,), memory_space=pltpu.SMEM),
                  pl.BlockSpec((ts, d), lambda i: (i, 0))],
        out_specs=pl.BlockSpec(memory_space=pl.ANY),
        out_shape=jax.ShapeDtypeStruct((rows, d), F32),
        scratch_shapes=[pltpu.VMEM((tile, d), F32), pltpu.SemaphoreType.DMA(()),
                        pltpu.SemaphoreType.DMA(())],
        compiler_params=_params("arbitrary"),
        name="moe_scatter",
    )(meta, route_flat, h)


def _moe_ffn_kernel(src_ref, te_ref, nv_ref, x_ref, w1_ref, w3_ref, w2_ref, o_ref, acc_ref):
    del src_ref, te_ref
    i = pl.program_id(0)
    f = pl.program_id(1)

    @pl.when(i < nv_ref[0])
    def _():
        @pl.when(f == 0)
        def _():
            acc_ref[...] = jnp.zeros_like(acc_ref)

        h = _bf(x_ref[...])
        u = _dot(h, w1_ref[0])
        act = _bf(u * _sigmoid(u) * _dot(h, w3_ref[0]))
        acc_ref[...] += _dot(act, w2_ref[0])

        @pl.when(f == pl.num_programs(1) - 1)
        def _():
            o_ref[...] = acc_ref[...]

    @pl.when(i >= nv_ref[0])
    def _():
        o_ref[...] = jnp.zeros_like(o_ref)


def _moe_ffn(tile_src, tile_expert, n_valid, xs, w1, w3, w2, tm, tf):
    rows, d = xs.shape
    ff = w1.shape[2]
    tok = pl.BlockSpec((tm, d), lambda i, f, src, te, nv: (src[i], 0))
    return pl.pallas_call(
        _moe_ffn_kernel,
        grid_spec=pltpu.PrefetchScalarGridSpec(
            num_scalar_prefetch=3,
            grid=(rows // tm, ff // tf),
            in_specs=[tok,
                      pl.BlockSpec((1, d, tf), lambda i, f, src, te, nv: (te[i], 0, f)),
                      pl.BlockSpec((1, d, tf), lambda i, f, src, te, nv: (te[i], 0, f)),
                      pl.BlockSpec((1, tf, d), lambda i, f, src, te, nv: (te[i], f, 0))],
            out_specs=pl.BlockSpec((tm, d), lambda i, f, src, te, nv: (i, 0)),
            scratch_shapes=[pltpu.VMEM((tm, d), F32)],
        ),
        out_shape=jax.ShapeDtypeStruct((rows, d), F32),
        compiler_params=_params("arbitrary", "arbitrary"),
        name="moe_experts",
    )(tile_src, tile_expert, n_valid, xs, w1, w3, w2)


def _moe_combine_kernel(meta_ref, route_ref, x_ref, gates_ref, gain_ref, y_hbm, o_ref, buf, sem):
    n_tok = x_ref.shape[0]

    def issue(j, carry):
        for k in range(2):
            _row_copy(y_hbm, _sorted_row(meta_ref, route_ref, j, k), buf.at[k], j, sem).start(priority=k)
        return carry

    def drain(j, carry):
        for k in range(2):
            _row_copy(y_hbm, 0, buf.at[k], 0, sem).wait()
        return carry

    lax.fori_loop(0, n_tok, issue, 0, unroll=DMA_UNROLL)
    lax.fori_loop(0, n_tok, drain, 0, unroll=DMA_UNROLL)
    gates = gates_ref[...]
    x = x_ref[...] + gates[:, 4:5] * buf[0] + gates[:, 5:6] * buf[1]
    o_ref[...] = _rms(x, gain_ref[...])


def _moe_combine(meta, route_flat, x, route, gain, ys, ts):
    t, d = x.shape
    tok = pl.BlockSpec((ts, d), lambda i: (i, 0))
    return pl.pallas_call(
        _moe_combine_kernel,
        grid=(t // ts,),
        in_specs=[pl.BlockSpec(memory_space=pltpu.SMEM),
                  pl.BlockSpec((4 * ts,), lambda i: (i,), memory_space=pltpu.SMEM),
                  tok, pl.BlockSpec((ts, LANES), lambda i: (i, 0)), _const_spec(gain.shape),
                  pl.BlockSpec(memory_space=pl.ANY)],
        out_specs=tok,
        out_shape=jax.ShapeDtypeStruct((t, d), F32),
        scratch_shapes=[pltpu.VMEM((2, ts, d), F32), pltpu.SemaphoreType.DMA(())],
        compiler_params=_params("arbitrary"),
        name="moe_combine",
    )(meta, route_flat, x, route, gain, ys)


def _tile(n, want):
    t = min(n, want)
    assert n % t == 0
    return t


def _pad_cols(w, n):
    return jnp.pad(w, ((0, 0), (0, n - w.shape[1])))


def _pad_rows(w, n):
    return jnp.pad(w, ((0, n - w.shape[0]), (0, 0)))


def kernel(x, norm_gain, rwkv_mu, rwkv_wr, rwkv_wk, rwkv_wv, rwkv_w0, rwkv_w1, rwkv_w2, rwkv_a0, rwkv_a1, rwkv_a2, rwkv_g1, rwkv_g2, rwkv_k_k, rwkv_k_a, rwkv_r_k, rwkv_lnx_w, rwkv_lnx_b, rwkv_wo, kv_norm_gain, w_kv, attn_wq, attn_wo, ffn_w1, ffn_w3, ffn_w2, moe_router, moe_router_bias, moe_w1, moe_w3, moe_w2, final_norm_gain):
    b, s, d = x.shape
    t = b * s
    n_heads = d // HEAD
    assert norm_gain.shape[0] == 2 and d % LANES == 0 and s % 8 == 0

    chan_head = jnp.arange(d) // HEAD
    e = (chan_head[:, None] == jnp.arange(LANES)[None, :]).astype(BF16)
    et = e.T

    zeros = jnp.zeros((d,), F32)
    vec_pre = jnp.stack([norm_gain[0, 0], rwkv_w0[0], rwkv_a0[0], rwkv_k_k[0], rwkv_k_a[0],
                         zeros, zeros, zeros])
    tm_pre = _tile(s, 256)
    chunk = _tile(s, LANES)
    r, cum, k, v, kk, kka, g = _rwkv_pre(
        x, vec_pre, rwkv_mu[0], _bf(rwkv_wr[0]), _bf(rwkv_wk[0]), _bf(rwkv_wv[0]),
        _bf(_pad_cols(rwkv_w1[0], LANES)), _bf(_pad_rows(rwkv_w2[0], LANES)),
        _bf(_pad_cols(rwkv_a1[0], LANES)), _bf(_pad_rows(rwkv_a2[0], LANES)),
        _bf(rwkv_g1[0]), _bf(rwkv_g2[0]), e, et, tm_pre, chunk)
    y = _wkv(r, cum, k, v, kk, kka, chunk=chunk, seq_tile=_tile(s, 512), n_pairs=4)

    flat = lambda arr: arr.reshape(t, d)
    tm = _tile(t, 512)
    vec_post = jnp.stack([rwkv_lnx_w[0], rwkv_lnx_b[0], rwkv_r_k[0].reshape(d),
                          zeros, zeros, zeros, zeros, zeros])
    x1 = _rwkv_post(flat(y), flat(r), flat(k), flat(v), flat(g), flat(x), vec_post,
                    _bf(rwkv_wo[0]), e, et, tm)

    ff = ffn_w1.shape[2]
    tf = ff // 2 if (ff // 2) % LANES == 0 else ff
    x2 = _ffn(x1, norm_gain[0, 1][None, :], _bf(ffn_w1[0]), _bf(ffn_w3[0]), _bf(ffn_w2[0]), tm, tf)

    n_slopes = len(BRANCHES) * HEADS_PER_BRANCH
    slopes = jnp.exp2(-ALIBI_MAX * (jnp.arange(n_slopes, dtype=F32) + 1.0) / n_slopes)
    gains = jnp.stack([norm_gain[1, 0], kv_norm_gain] + [zeros] * 6)
    tm_a = _tile(s, 512)
    qkv = _attn_proj(x2.reshape(b, s, d), gains, _bf(attn_wq[0]), _bf(w_kv), tm_a)
    os_, ls_ = [], []
    for br in range(len(BRANCHES)):
        o_br, l_br = _attn_branch(slopes, *qkv[3 * br:3 * br + 3], br)
        os_.append(o_br)
        ls_.append(l_br)

    router = _pad_cols(moe_router[0], LANES)
    rhi, rlo = _split2(router)
    rbias = jnp.full((1, LANES), NEG, F32).at[0, :N_EXPERTS].set(moe_router_bias[0])
    x3, h4, route, cnt = _attn_out(os_, ls_, x2, norm_gain[1, 1][None, :], _bf(attn_wo[0]),
                                   rhi, rlo, rbias, b, tm_a)

    tm_e = 512 if t >= 4096 else 128
    ts = _tile(t, 512)
    counts = cnt[0, :N_EXPERTS].astype(jnp.int32)
    padded = ((counts + tm_e - 1) // tm_e) * tm_e
    ends = jnp.cumsum(padded)
    meta = jnp.concatenate([ends - padded, ends]).astype(jnp.int32)
    route_flat = route[:, :4].astype(jnp.int32).reshape(4 * t)
    rows = 2 * t + N_EXPERTS * tm_e
    n_tiles = rows // tm_e
    n_valid = (ends[-1] // tm_e).astype(jnp.int32)
    tile_src = jnp.minimum(jnp.arange(n_tiles, dtype=jnp.int32), n_valid - 1)
    tile_expert = jnp.minimum(
        jnp.sum(tile_src[:, None] * tm_e >= ends[None, :], axis=1), N_EXPERTS - 1).astype(jnp.int32)

    xs = _moe_scatter(meta, route_flat, h4, rows, ts, tm_e)
    ffe = moe_w1.shape[3]
    tfe = ffe // 2 if (ffe // 2) % LANES == 0 else ffe
    ys = _moe_ffn(tile_src, tile_expert, n_valid.reshape(1), xs,
                  _bf(moe_w1[0]), _bf(moe_w3[0]), _bf(moe_w2[0]), tm_e, tfe)
    out = _moe_combine(meta, route_flat, x3, route, final_norm_gain[None, :], ys, ts)
    return out.reshape(b, s, d)
```

```python
import functools
import math

import jax
import jax.numpy as jnp
from jax import lax
from jax.experimental import pallas as pl
from jax.experimental.pallas import tpu as pltpu

F32 = jnp.float32
BF16 = jnp.bfloat16

RMS_EPS = 1e-5
GN_EPS = 64e-5
HEAD = 64
LANES = 128
BRANCHES = ((128, 1), (512, 4), (2048, 16))
HEADS_PER_BRANCH = 8
ALIBI_MAX = 8.0
N_EXPERTS = 8
NEG = -1e30
VMEM_LIMIT_BYTES = 56 * 1024 * 1024


def _params(*sem):
    return pltpu.CompilerParams(dimension_semantics=sem, vmem_limit_bytes=VMEM_LIMIT_BYTES)


def _dot(a, b):
    return jnp.dot(a, b, preferred_element_type=F32)


def _dot_nt(a, b):
    return lax.dot_general(a, b, (((1,), (1,)), ((), ())), preferred_element_type=F32)


def _bf(x):
    return x.astype(BF16)


def _split2(x):
    hi = x.astype(BF16)
    lo = (x - hi.astype(F32)).astype(BF16)
    return hi, lo


def _sigmoid(z):
    return 1.0 / (1.0 + jnp.exp(-z))


def _rms(x, gain):
    return x * lax.rsqrt(jnp.mean(x * x, axis=-1, keepdims=True) + RMS_EPS) * gain


def _headsum(x, e, et):
    return _dot(_bf(_dot(_bf(x), e)), et)


def _const_spec(shape):
    nd = len(shape)
    return pl.BlockSpec(shape, lambda *_: (0,) * nd)


def _rwkv_pre_kernel(x_ref, xp_ref, vec_ref, mu_ref, wr_ref, wk_ref, wv_ref, w1_ref, w2_ref,
                     a1_ref, a2_ref, g1_ref, g2_ref, e_ref, et_ref,
                     r_out, cum_out, k_out, v_out, kk_out, b_out, g_out, *, chunk):
    i = pl.program_id(1)
    vec = vec_ref[...]
    gain, w0, a0, k_k, k_a = (vec[j:j + 1] for j in range(5))
    x = x_ref[0]
    h = _rms(x, gain)
    hp = _rms(xp_ref[0][7:8, :], gain)
    hp = jnp.where(i > 0, hp, 0.0)
    rows = lax.broadcasted_iota(jnp.int32, h.shape, 0)
    hprev = jnp.where(rows == 0, hp, pltpu.roll(h, 1, 0))
    xx = hprev - h
    mu = mu_ref[...]
    xr, xw, xk, xv, xa, xg = (_bf(h + xx * mu[j:j + 1]) for j in range(6))
    r = _dot(xr, wr_ref[...])
    k = _dot(xk, wk_ref[...])
    v = _dot(xv, wv_ref[...])
    wl = w0 + _dot(_bf(jnp.tanh(_dot(xw, w1_ref[...]))), w2_ref[...])
    a = _sigmoid(a0 + _dot(_bf(_dot(xa, a1_ref[...])), a2_ref[...]))
    g = _dot(_bf(_sigmoid(_dot(xg, g1_ref[...]))), g2_ref[...])
    kkr = k * k_k
    ss = _headsum(kkr * kkr, e_ref[...], et_ref[...])
    kk = kkr / jnp.maximum(jnp.sqrt(ss), 1e-12)
    lw = _sigmoid(wl) * (-math.exp(-0.5))
    tm = lw.shape[0]
    rr = lax.broadcasted_iota(jnp.int32, (tm, tm), 0)
    cc = lax.broadcasted_iota(jnp.int32, (tm, tm), 1)
    same_chunk = jnp.bitwise_xor(rr, cc) < chunk
    in_chunk = jnp.where(rr >= cc, jnp.where(same_chunk, 1.0, 0.0), 0.0).astype(BF16)
    hi, lo = _split2(lw)
    cum_out[0] = _dot(in_chunk, hi) + _dot(in_chunk, lo)
    r_out[0] = _bf(r)
    k_out[0] = _bf(k * (1.0 + (a - 1.0) * k_a))
    v_out[0] = _bf(v)
    kk_out[0] = _bf(kk)
    b_out[0] = _bf(kk * a)
    g_out[0] = _bf(g)


def _rwkv_pre(x, vec, mu, wr, wk, wv, w1, w2, a1, a2, g1, g2, e, et, tm, chunk):
    b, s, d = x.shape
    assert tm % chunk == 0
    tok = pl.BlockSpec((1, tm, d), lambda bi, i: (bi, i, 0))
    prev = pl.BlockSpec((1, 8, d), lambda bi, i: (bi, jnp.maximum(i * (tm // 8) - 1, 0), 0))
    consts = [vec, mu, wr, wk, wv, w1, w2, a1, a2, g1, g2, e, et]
    out = lambda dt: jax.ShapeDtypeStruct((b, s, d), dt)
    return pl.pallas_call(
        functools.partial(_rwkv_pre_kernel, chunk=chunk),
        grid=(b, s // tm),
        in_specs=[tok, prev] + [_const_spec(c.shape) for c in consts],
        out_specs=[tok] * 7,
        out_shape=[out(BF16), out(F32)] + [out(BF16)] * 5,
        compiler_params=_params("parallel", "arbitrary"),
        name="rwkv_pre",
    )(x, x, *consts)


def _round_robin(generators):
    results = [None] * len(generators)
    live = list(enumerate(generators))
    while live:
        still = []
        for idx, gen in live:
            try:
                results[idx] = next(gen)
                still.append((idx, gen))
            except StopIteration:
                pass
        live = still
    return results


def _wkv_kernel(r_ref, cum_ref, k_ref, v_ref, kk_ref, b_ref, y_ref, st_ref, *, chunk):
    c_len = chunk
    n_chunks = r_ref.shape[1] // c_len
    n_pairs = r_ref.shape[2] // LANES
    inv_steps = max(c_len.bit_length() - 2, 0)
    lane = lax.broadcasted_iota(jnp.int32, (c_len, LANES), 1)
    head0 = lane < HEAD
    rr = lax.broadcasted_iota(jnp.int32, (c_len, c_len), 0)
    cc = lax.broadcasted_iota(jnp.int32, (c_len, c_len), 1)
    lower = rr >= cc
    strict = rr > cc
    eye = jnp.where(rr == cc, 1.0, 0.0).astype(F32)
    r2 = lax.broadcasted_iota(jnp.int32, (LANES, LANES), 0)
    c2 = lax.broadcasted_iota(jnp.int32, (LANES, LANES), 1)
    same_head = (r2 < HEAD) == (c2 < HEAD)

    @pl.when(pl.program_id(2) == 0)
    def _():
        st_ref[...] = jnp.zeros_like(st_ref)

    first_row = lax.broadcasted_iota(jnp.int32, (c_len, LANES), 0) == 0
    rows = lambda mat, j: mat[j * c_len:(j + 1) * c_len]
    own_head = lambda mat: jnp.where(head0, rows(mat, 0), rows(mat, 1))

    def chunk_pair(sl, lanes, state):
        cum = cum_ref[0, sl, lanes]
        cum_prev = jnp.where(first_row, 0.0, pltpu.roll(cum, 1, 0))
        cum_last = cum[c_len - 1:c_len, :]
        kk = kk_ref[0, sl, lanes].astype(F32)
        b = b_ref[0, sl, lanes].astype(F32)
        k = k_ref[0, sl, lanes].astype(F32)
        vb = v_ref[0, sl, lanes]
        e_neg = jnp.exp(-cum)
        e_end = jnp.exp(cum_last - cum)
        a_t = -kk * jnp.exp(cum_prev)
        r_t = r_ref[0, sl, lanes].astype(F32) * jnp.exp(cum)
        lhs = _bf(jnp.concatenate([jnp.where(head0, a_t, 0.0), jnp.where(head0, 0.0, a_t),
                                   jnp.where(head0, r_t, 0.0), jnp.where(head0, 0.0, r_t)], axis=0))
        gram = _dot_nt(lhs, _bf(jnp.concatenate([b * e_neg, k * e_neg], axis=0)))
        from_state = _dot_nt(_bf(jnp.concatenate([a_t, r_t], axis=0)), _bf(state))
        yield
        gram_b, gram_k = gram[:, :c_len], gram[:, c_len:]
        from_v = _dot(_bf(jnp.concatenate(
            [jnp.where(strict, rows(gram_k, 0), 0.0), jnp.where(strict, rows(gram_k, 1), 0.0),
             jnp.where(lower, rows(gram_k, 2), 0.0), jnp.where(lower, rows(gram_k, 3), 0.0)], axis=0)), vb)
        m_rb = _bf(jnp.concatenate([jnp.where(lower, rows(gram_b, 2), 0.0),
                                    jnp.where(lower, rows(gram_b, 3), 0.0)], axis=0))
        w = rows(from_state, 0) + own_head(from_v[:2 * c_len])
        y1 = rows(from_state, 1) + own_head(from_v[2 * c_len:])
        pw = [jnp.where(strict, rows(gram_b, hd), 0.0) for hd in range(2)]
        inv = [eye + pw[hd] for hd in range(2)]
        if inv_steps:
            pw = [_dot(_bf(pw[hd]), _bf(pw[hd])) for hd in range(2)]
        for step in range(inv_steps):
            yield
            if step < inv_steps - 1:
                both = [_dot(_bf(jnp.concatenate([inv[hd], pw[hd]], axis=0)), _bf(pw[hd])) for hd in range(2)]
                inv = [inv[hd] + rows(both[hd], 0) for hd in range(2)]
                pw = [rows(both[hd], 1) for hd in range(2)]
            else:
                inv = [inv[hd] + _dot(_bf(inv[hd]), _bf(pw[hd])) for hd in range(2)]
        yield
        u = own_head(_dot(_bf(jnp.concatenate(inv, axis=0)), _bf(w)))
        yield
        y = y1 + own_head(_dot(m_rb, _bf(u)))
        uv_t = _bf(jnp.transpose(jnp.concatenate([u, vb.astype(F32)], axis=0)))
        bk = _bf(jnp.concatenate([b * e_end, k * e_end], axis=0))
        yield y, state * jnp.exp(cum_last) + jnp.where(same_head, _dot(uv_t, bk), 0.0)

    def body(c, carry):
        sl = pl.ds(pl.multiple_of(c * c_len, c_len), c_len)
        lanes = [slice(p * LANES, (p + 1) * LANES) for p in range(n_pairs)]
        results = _round_robin([chunk_pair(sl, lanes[p], st_ref[p]) for p in range(n_pairs)])
        for p, (y, new_state) in enumerate(results):
            y_ref[0, sl, lanes[p]] = _bf(y)
            st_ref[p] = new_state
        return carry

    lax.fori_loop(0, n_chunks, body, 0)


def _wkv(r, cum, k, v, kk, bb, chunk, seq_tile, n_pairs):
    b, s, d = r.shape
    width = n_pairs * LANES
    spec = pl.BlockSpec((1, seq_tile, width), lambda bi, hi, si: (bi, si, hi))
    return pl.pallas_call(
        functools.partial(_wkv_kernel, chunk=chunk),
        grid=(b, d // width, s // seq_tile),
        in_specs=[spec] * 6,
        out_specs=spec,
        out_shape=jax.ShapeDtypeStruct((b, s, d), BF16),
        scratch_shapes=[pltpu.VMEM((n_pairs, LANES, LANES), F32)],
        compiler_params=_params("parallel", "parallel", "arbitrary"),
        name="wkv7",
    )(r, cum, k, v, kk, bb)


def _rwkv_post_kernel(y_ref, r_ref, k_ref, v_ref, g_ref, x_ref, vec_ref, wo_ref, e_ref, et_ref, o_ref):
    vec = vec_ref[...]
    lnx_w, lnx_b, r_k = (vec[j:j + 1] for j in range(3))
    e = e_ref[...]
    et = et_ref[...]
    y = y_ref[...].astype(F32)
    v = v_ref[...].astype(F32)
    mean = _headsum(y, e, et) * (1.0 / HEAD)
    dy = y - mean
    var = _headsum(dy * dy, e, et) * (1.0 / HEAD)
    yn = dy * lax.rsqrt(var + GN_EPS) * lnx_w + lnx_b
    bonus = _headsum(r_ref[...].astype(F32) * k_ref[...].astype(F32) * r_k, e, et) * v
    o_ref[...] = x_ref[...] + _dot(_bf((yn + bonus) * g_ref[...].astype(F32)), wo_ref[...])


def _rwkv_post(y, r, k, v, g, x, vec, wo, e, et, tm):
    t, d = x.shape
    tok = pl.BlockSpec((tm, d), lambda i: (i, 0))
    consts = [vec, wo, e, et]
    return pl.pallas_call(
        _rwkv_post_kernel,
        grid=(t // tm,),
        in_specs=[tok] * 6 + [_const_spec(c.shape) for c in consts],
        out_specs=tok,
        out_shape=jax.ShapeDtypeStruct((t, d), F32),
        compiler_params=_params("parallel"),
        name="rwkv_post",
    )(y, r, k, v, g, x, *consts)


def _ffn_kernel(x_ref, gain_ref, w1_ref, w3_ref, w2_ref, o_ref, h_ref, acc_ref):
    f = pl.program_id(1)

    @pl.when(f == 0)
    def _():
        h_ref[...] = _bf(_rms(x_ref[...], gain_ref[...]))
        acc_ref[...] = jnp.zeros_like(acc_ref)

    h = h_ref[...]
    u = _dot(h, w1_ref[...])
    act = _bf(u * _sigmoid(u) * _dot(h, w3_ref[...]))
    acc_ref[...] += _dot(act, w2_ref[...])

    @pl.when(f == pl.num_programs(1) - 1)
    def _():
        o_ref[...] = x_ref[...] + acc_ref[...]


def _ffn(x, gain, w1, w3, w2, tm, tf):
    t, d = x.shape
    ff = w1.shape[1]
    tok = pl.BlockSpec((tm, d), lambda i, f: (i, 0))
    return pl.pallas_call(
        _ffn_kernel,
        grid=(t // tm, ff // tf),
        in_specs=[tok, _const_spec(gain.shape),
                  pl.BlockSpec((d, tf), lambda i, f: (0, f)),
                  pl.BlockSpec((d, tf), lambda i, f: (0, f)),
                  pl.BlockSpec((tf, d), lambda i, f: (f, 0))],
        out_specs=tok,
        out_shape=jax.ShapeDtypeStruct((t, d), F32),
        scratch_shapes=[pltpu.VMEM((tm, d), BF16), pltpu.VMEM((tm, d), F32)],
        compiler_params=_params("parallel", "arbitrary"),
        name="ffn_dense",
    )(x, gain, w1, w3, w2)


BRANCH_WIDTH = HEADS_PER_BRANCH * HEAD


def _attn_proj_kernel(x_ref, gains_ref, wq_ref, wkv_ref, *refs):
    outs, (q_scr, kv_scr) = refs[:-2], refs[-2:]
    x = x_ref[0]
    tm = x.shape[0]
    n = x * lax.rsqrt(jnp.mean(x * x, axis=-1, keepdims=True) + RMS_EPS)
    gains = gains_ref[...]
    q_tiles = q_scr.shape[0]
    q = _dot(_bf(n * gains[0:1]), wq_ref[...]) * (1.0 / HEAD ** 0.5)
    kv = _dot(_bf(n * gains[1:2]), wkv_ref[...])
    for ti in range(q_tiles):
        q_scr[ti] = q[:, ti * LANES:(ti + 1) * LANES]
    for ti in range(2 * q_tiles):
        kv_scr[ti] = kv[:, ti * LANES:(ti + 1) * LANES]
    tiles_br = BRANCH_WIDTH // LANES
    for br, (_, dil) in enumerate(BRANCHES):
        q_out, k_out, v_out = outs[3 * br:3 * br + 3]
        for res in range(dil):
            rows = pl.ds(res, tm // dil, stride=dil) if dil > 1 else slice(None)
            for ti in range(tiles_br):
                src = br * tiles_br + ti
                lanes = slice(ti * LANES, (ti + 1) * LANES)
                q_out[0, res, :, lanes] = _bf(q_scr[src, rows, :])
                k_out[0, res, :, lanes] = _bf(kv_scr[src, rows, :])
                v_out[0, res, :, lanes] = _bf(kv_scr[q_tiles + src, rows, :])


def _attn_proj(x, gains, wq, wkv, tm):
    b, s, d = x.shape
    qw = wq.shape[1]
    out_specs, out_shape = [], []
    for _, dil in BRANCHES:
        assert tm % (16 * dil) == 0
        out_specs += [pl.BlockSpec((1, dil, tm // dil, BRANCH_WIDTH), lambda bi, i: (bi, 0, i, 0))] * 3
        out_shape += [jax.ShapeDtypeStruct((b, dil, s // dil, BRANCH_WIDTH), BF16)] * 3
    return pl.pallas_call(
        _attn_proj_kernel,
        grid=(b, s // tm),
        in_specs=[pl.BlockSpec((1, tm, d), lambda bi, i: (bi, i, 0)),
                  _const_spec(gains.shape), _const_spec(wq.shape), _const_spec(wkv.shape)],
        out_specs=out_specs,
        out_shape=out_shape,
        scratch_shapes=[pltpu.VMEM((qw // LANES, tm, LANES), F32),
                        pltpu.VMEM((2 * qw // LANES, tm, LANES), F32)],
        compiler_params=_params("parallel", "parallel"),
        name="attn_proj",
    )(x, gains, wq, wkv)


def _attn_kernel(slopes_ref, q_ref, k_ref, v_ref, o_ref, lse_ref, *, branch, dilation, qb):
    sub_len = q_ref.shape[2]
    n_blk = sub_len // qb
    n_pairs = q_ref.shape[3] // LANES
    rel = (lax.broadcasted_iota(jnp.int32, (qb, qb), 0)
           - lax.broadcasted_iota(jnp.int32, (qb, qb), 1)).astype(F32)
    dist_d = jnp.where(rel >= 0, rel, -NEG)
    dist_p = jnp.where(rel <= 0, rel + float(qb), -NEG)
    lane = lax.broadcasted_iota(jnp.int32, (qb, LANES), 1)
    head0 = lane < HEAD
    lane_row = lax.broadcasted_iota(jnp.int32, (1, LANES), 1)
    head_mask = [jnp.where(lane_row < HEAD, 1.0, 0.0).astype(BF16),
                 jnp.where(lane_row < HEAD, 0.0, 1.0).astype(BF16)]

    def block(i, carry):
        cur = pl.ds(pl.multiple_of(i * qb, qb), qb)
        if n_blk > 1:
            prv = pl.ds(pl.multiple_of(jnp.maximum(i - 1, 0) * qb, qb), qb)
            has_prev = jnp.where(i > 0, 1.0, -NEG)

        def one_head(p, j):
            lanes = slice(p * LANES, (p + 1) * LANES)
            step = slopes_ref[branch * HEADS_PER_BRANCH + 2 * p + j] * float(dilation)
            qh = q_ref[0, 0, cur, lanes] * head_mask[j]
            sd = _dot_nt(qh, k_ref[0, 0, cur, lanes]) - step * dist_d
            if n_blk > 1:
                sp = _dot_nt(qh, k_ref[0, 0, prv, lanes]) - (step * has_prev) * dist_p
            yield
            m = jnp.max(sd, axis=-1, keepdims=True)
            if n_blk > 1:
                m = jnp.maximum(m, jnp.max(sp, axis=-1, keepdims=True))
            pd = jnp.exp(sd - m)
            den = jnp.sum(pd, axis=-1, keepdims=True)
            acc = _dot(_bf(pd), v_ref[0, 0, cur, lanes])
            if n_blk > 1:
                pp = jnp.exp(sp - m)
                den = den + jnp.sum(pp, axis=-1, keepdims=True)
                acc = acc + _dot(_bf(pp), v_ref[0, 0, prv, lanes])
            yield
            yield acc / den, m + jnp.log(den)

        res = _round_robin([one_head(p, j) for p in range(n_pairs) for j in range(2)])
        lse = jnp.zeros((qb, LANES), F32)
        for p in range(n_pairs):
            lanes = slice(p * LANES, (p + 1) * LANES)
            (o_a, l_a), (o_b, l_b) = res[2 * p], res[2 * p + 1]
            o_ref[0, 0, cur, lanes] = _bf(jnp.where(head0, o_a, o_b))
            lse = jnp.where(lane == 2 * p, l_a, jnp.where(lane == 2 * p + 1, l_b, lse))
        lse_ref[0, 0, cur, :] = lse
        return carry

    lax.fori_loop(0, n_blk, block, 0, unroll=2 if n_blk % 2 == 0 else 1)


def _attn_branch(slopes, q, k, v, branch):
    window, dilation = BRANCHES[branch]
    assert window % dilation == 0 and window // dilation == LANES
    b, _, sub_len, width = q.shape
    qb = min(LANES, sub_len)
    assert sub_len % qb == 0
    spec = pl.BlockSpec((1, 1, sub_len, width), lambda bi, ri: (bi, ri, 0, 0))
    return pl.pallas_call(
        functools.partial(_attn_kernel, branch=branch, dilation=dilation, qb=qb),
        grid=(b, dilation),
        in_specs=[pl.BlockSpec(memory_space=pltpu.SMEM)] + [spec] * 3,
        out_specs=[spec, pl.BlockSpec((1, 1, sub_len, LANES), lambda bi, ri: (bi, ri, 0, 0))],
        out_shape=[jax.ShapeDtypeStruct(q.shape, BF16),
                   jax.ShapeDtypeStruct(q.shape[:3] + (LANES,), F32)],
        compiler_params=_params("parallel", "parallel"),
        name=f"dilated_attn_{branch}",
    )(slopes, q, k, v)


def _attn_out_kernel(o0, o1, o2, l0, l1, l2, x_ref, gain_ref, wo_ref, router_ref, rb_ref, expand_ref,
                     x_out, h_out, route_out, cnt_out, cnt_ref, o_scr, l_scr):
    @pl.when((pl.program_id(0) == 0) & (pl.program_id(1) == 0))
    def _():
        cnt_ref[...] = jnp.zeros_like(cnt_ref)

    tm = x_ref.shape[0]
    tiles_br = BRANCH_WIDTH // LANES
    for br, (o_ref, l_ref) in enumerate(((o0, l0), (o1, l1), (o2, l2))):
        dil = BRANCHES[br][1]
        for res in range(dil):
            rows = pl.ds(res, tm // dil, stride=dil) if dil > 1 else slice(None)
            l_scr[br, rows, :] = l_ref[0, res]
            for ti in range(tiles_br):
                lanes = slice(ti * LANES, (ti + 1) * LANES)
                o_scr[br, ti, rows, :] = o_ref[0, res, :, lanes].astype(F32)
    ls = [l_scr[br] for br in range(3)]
    m = jnp.maximum(jnp.maximum(ls[0], ls[1]), ls[2])
    ws = [jnp.exp(l - m) for l in ls]
    total = ws[0] + ws[1] + ws[2]
    spread = []
    for br in range(3):
        hi, lo = _split2(ws[br] / total)
        spread.append(_dot(hi, expand_ref[...]) + _dot(lo, expand_ref[...]))
    merged = []
    for ti in range(tiles_br):
        lanes = slice(ti * LANES, (ti + 1) * LANES)
        merged.append(_bf(sum(spread[br][:, lanes] * o_scr[br, ti] for br in range(3))))
    x = x_ref[...] + _dot(jnp.concatenate(merged, axis=1), wo_ref[...])
    x_out[...] = x
    h = _rms(x, gain_ref[...])
    h_out[...] = h
    hhi, hlo = _split2(h)
    both = _dot(hhi, router_ref[...])
    logits = both[:, :LANES] + both[:, LANES:] + _dot(hlo, router_ref[:, :LANES]) + rb_ref[...]
    lane = lax.broadcasted_iota(jnp.int32, logits.shape, 1)
    m1 = jnp.max(logits, axis=-1, keepdims=True)
    i1 = jnp.min(jnp.where(logits == m1, lane, LANES), axis=-1, keepdims=True)
    rest = jnp.where(lane == i1, -3e38, logits)
    m2 = jnp.max(rest, axis=-1, keepdims=True)
    i2 = jnp.min(jnp.where(rest == m2, lane, LANES), axis=-1, keepdims=True)
    ex = jnp.exp(m2 - m1)
    sel = jnp.where((lane == i1) | (lane == i2), 1.0, 0.0)
    rr = lax.broadcasted_iota(jnp.int32, (tm, tm), 0)
    cc = lax.broadcasted_iota(jnp.int32, (tm, tm), 1)
    before = jnp.where(rr > cc, 1.0, 0.0).astype(BF16)
    rank = cnt_ref[...] + _dot(before, _bf(sel))
    cnt_ref[...] += jnp.sum(sel, axis=0, keepdims=True)
    cnt_out[...] = cnt_ref[...]
    columns = [i1.astype(F32), i2.astype(F32),
               jnp.sum(jnp.where(lane == i1, rank, 0.0), axis=-1, keepdims=True),
               jnp.sum(jnp.where(lane == i2, rank, 0.0), axis=-1, keepdims=True),
               1.0 / (1.0 + ex), ex / (1.0 + ex)]
    route = jnp.zeros(logits.shape, F32)
    for j, col in enumerate(columns):
        route = jnp.where(lane == j, col, route)
    route_out[...] = route


def _attn_out(os_, ls_, x, gain, wo, router, rb, b, tm):
    t, d = x.shape
    tiles = t // (b * tm)
    tok = pl.BlockSpec((tm, d), lambda bi, i: (bi * tiles + i, 0))
    small = pl.BlockSpec((tm, LANES), lambda bi, i: (bi * tiles + i, 0))
    att = [pl.BlockSpec((1, dil, tm // dil, BRANCH_WIDTH), lambda bi, i: (bi, 0, i, 0))
           for _, dil in BRANCHES]
    lse = [pl.BlockSpec((1, dil, tm // dil, LANES), lambda bi, i: (bi, 0, i, 0)) for _, dil in BRANCHES]
    expand = (jnp.arange(LANES)[:, None] == jnp.arange(BRANCH_WIDTH)[None, :] // HEAD).astype(BF16)
    consts = [gain, wo, router, rb, expand]
    return pl.pallas_call(
        _attn_out_kernel,
        grid=(b, tiles),
        in_specs=att + lse + [tok] + [_const_spec(c.shape) for c in consts],
        out_specs=[tok, tok, small, _const_spec((1, LANES))],
        out_shape=[jax.ShapeDtypeStruct((t, d), F32), jax.ShapeDtypeStruct((t, d), F32),
                   jax.ShapeDtypeStruct((t, LANES), F32), jax.ShapeDtypeStruct((1, LANES), F32)],
        scratch_shapes=[pltpu.VMEM((1, LANES), F32),
                        pltpu.VMEM((len(BRANCHES), BRANCH_WIDTH // LANES, tm, LANES), F32),
                        pltpu.VMEM((len(BRANCHES), tm, LANES), F32)],
        compiler_params=_params("arbitrary", "arbitrary"),
        name="attn_out_route",
    )(*os_, *ls_, x, *consts)


DMA_UNROLL = 8


def _row_copy(src, src_row, dst, dst_row, sem):
    return pltpu.make_async_copy(src.at[pl.ds(src_row, 1)], dst.at[pl.ds(dst_row, 1)], sem)


def _moe_scatter_kernel(meta_ref, dest_ref, h_ref, out_ref, zeros, sem, zero_sem):
    n_tok = h_ref.shape[0]
    tile = zeros.shape[0]
    ends = lambda e: meta_ref[N_EXPERTS + e]

    def zero_fill(e):
        if e < N_EXPERTS:
            start, used = jnp.maximum(ends(e) - tile, 0), ends(e) > meta_ref[e]
        else:
            start = ends(N_EXPERTS - 1) + (e - N_EXPERTS) * tile
            used = start < out_ref.shape[0]
            start = jnp.minimum(start, out_ref.shape[0] - tile)
        dst = out_ref.at[pl.ds(pl.multiple_of(start, tile), tile)]
        return used, pltpu.make_async_copy(zeros, dst, zero_sem)

    @pl.when(pl.program_id(0) == 0)
    def _():
        zeros[...] = jnp.zeros_like(zeros)
        for e in range(2 * N_EXPERTS):
            used, copy = zero_fill(e)
            pl.when(used)(copy.start)
        for e in range(2 * N_EXPERTS):
            used, copy = zero_fill(e)
            pl.when(used)(copy.wait)

    def issue(j, carry):
        for k in range(2):
            _row_copy(h_ref, j, out_ref, dest_ref[2 * j + k], sem).start(priority=k)
        return carry

    def drain(j, carry):
        for _ in range(2):
            _row_copy(h_ref, 0, out_ref, 0, sem).wait()
        return carry

    lax.fori_loop(0, n_tok, issue, 0, unroll=DMA_UNROLL)
    lax.fori_loop(0, n_tok, drain, 0, unroll=DMA_UNROLL)


def _moe_scatter(meta, dest_flat, h, rows, ts, tile):
    t, d = h.shape
    return pl.pallas_call(
        _moe_scatter_kernel,
        grid=(t // ts,),
        in_specs=[pl.BlockSpec(memory_space=pltpu.SMEM),
                  pl.BlockSpec((2 * ts,), lambda i: (i,), memory_space=pltpu.SMEM),
                  pl.BlockSpec((ts, d), lambda i: (i, 0))],
        out_specs=pl.BlockSpec(memory_space=pl.ANY),
        out_shape=jax.ShapeDtypeStruct((rows, d), F32),
        scratch_shapes=[pltpu.VMEM((tile, d), F32), pltpu.SemaphoreType.DMA(()),
                        pltpu.SemaphoreType.DMA(())],
        compiler_params=_params("arbitrary"),
        name="moe_scatter",
    )(meta, dest_flat, h)


def _moe_ffn_kernel(src_ref, te_ref, nv_ref, x_ref, w1_ref, w3_ref, w2_ref, o_ref, acc_ref):
    del src_ref, te_ref
    i = pl.program_id(0)
    f = pl.program_id(1)

    @pl.when(i < nv_ref[0])
    def _():
        @pl.when(f == 0)
        def _():
            acc_ref[...] = jnp.zeros_like(acc_ref)

        h = _bf(x_ref[...])
        u = _dot(h, w1_ref[0])
        act = _bf(u * _sigmoid(u) * _dot(h, w3_ref[0]))
        acc_ref[...] += _dot(act, w2_ref[0])

        @pl.when(f == pl.num_programs(1) - 1)
        def _():
            o_ref[...] = acc_ref[...]

    @pl.when(i >= nv_ref[0])
    def _():
        o_ref[...] = jnp.zeros_like(o_ref)


def _moe_ffn(tile_src, tile_expert, n_valid, xs, w1, w3, w2, tm, tf):
    rows, d = xs.shape
    ff = w1.shape[2]
    tok = pl.BlockSpec((tm, d), lambda i, f, src, te, nv: (src[i], 0))
    return pl.pallas_call(
        _moe_ffn_kernel,
        grid_spec=pltpu.PrefetchScalarGridSpec(
            num_scalar_prefetch=3,
            grid=(rows // tm, ff // tf),
            in_specs=[tok,
                      pl.BlockSpec((1, d, tf), lambda i, f, src, te, nv: (te[i], 0, f)),
                      pl.BlockSpec((1, d, tf), lambda i, f, src, te, nv: (te[i], 0, f)),
                      pl.BlockSpec((1, tf, d), lambda i, f, src, te, nv: (te[i], f, 0))],
            out_specs=pl.BlockSpec((tm, d), lambda i, f, src, te, nv: (i, 0)),
            scratch_shapes=[pltpu.VMEM((tm, d), F32)],
        ),
        out_shape=jax.ShapeDtypeStruct((rows, d), F32),
        compiler_params=_params("arbitrary", "arbitrary"),
        name="moe_experts",
    )(tile_src, tile_expert, n_valid, xs, w1, w3, w2)


def _moe_combine_kernel(dest_ref, x_ref, gates_ref, gain_ref, y_hbm, o_ref, buf, sem):
    n_tok = x_ref.shape[0]

    def issue(j, carry):
        for k in range(2):
            _row_copy(y_hbm, dest_ref[2 * j + k], buf.at[k], j, sem).start(priority=k)
        return carry

    def drain(j, carry):
        for k in range(2):
            _row_copy(y_hbm, 0, buf.at[k], 0, sem).wait()
        return carry

    lax.fori_loop(0, n_tok, issue, 0, unroll=DMA_UNROLL)
    lax.fori_loop(0, n_tok, drain, 0, unroll=DMA_UNROLL)
    gates = gates_ref[...]
    x = x_ref[...] + gates[:, 4:5] * buf[0] + gates[:, 5:6] * buf[1]
    o_ref[...] = _rms(x, gain_ref[...])


def _moe_combine(dest_flat, x, route, gain, ys, ts):
    t, d = x.shape
    tok = pl.BlockSpec((ts, d), lambda i: (i, 0))
    return pl.pallas_call(
        _moe_combine_kernel,
        grid=(t // ts,),
        in_specs=[pl.BlockSpec((2 * ts,), lambda i: (i,), memory_space=pltpu.SMEM),
                  tok, pl.BlockSpec((ts, LANES), lambda i: (i, 0)), _const_spec(gain.shape),
                  pl.BlockSpec(memory_space=pl.ANY)],
        out_specs=tok,
        out_shape=jax.ShapeDtypeStruct((t, d), F32),
        scratch_shapes=[pltpu.VMEM((2, ts, d), F32), pltpu.SemaphoreType.DMA(())],
        compiler_params=_params("arbitrary"),
        name="moe_combine",
    )(dest_flat, x, route, gain, ys)


def _tile(n, want):
    t = min(n, want)
    assert n % t == 0
    return t


def _pad_cols(w, n):
    return jnp.pad(w, ((0, 0), (0, n - w.shape[1])))


def _pad_rows(w, n):
    return jnp.pad(w, ((0, n - w.shape[0]), (0, 0)))


def kernel(x, norm_gain, rwkv_mu, rwkv_wr, rwkv_wk, rwkv_wv, rwkv_w0, rwkv_w1, rwkv_w2, rwkv_a0, rwkv_a1, rwkv_a2, rwkv_g1, rwkv_g2, rwkv_k_k, rwkv_k_a, rwkv_r_k, rwkv_lnx_w, rwkv_lnx_b, rwkv_wo, kv_norm_gain, w_kv, attn_wq, attn_wo, ffn_w1, ffn_w3, ffn_w2, moe_router, moe_router_bias, moe_w1, moe_w3, moe_w2, final_norm_gain):
    b, s, d = x.shape
    t = b * s
    n_heads = d // HEAD
    assert norm_gain.shape[0] == 2 and d % LANES == 0 and s % 8 == 0

    chan_head = jnp.arange(d) // HEAD
    e = (chan_head[:, None] == jnp.arange(LANES)[None, :]).astype(BF16)
    et = e.T

    zeros = jnp.zeros((d,), F32)
    vec_pre = jnp.stack([norm_gain[0, 0], rwkv_w0[0], rwkv_a0[0], rwkv_k_k[0], rwkv_k_a[0],
                         zeros, zeros, zeros])
    tm_pre = _tile(s, 256)
    chunk = _tile(s, LANES)
    r, cum, k, v, kk, kka, g = _rwkv_pre(
        x, vec_pre, rwkv_mu[0], _bf(rwkv_wr[0]), _bf(rwkv_wk[0]), _bf(rwkv_wv[0]),
        _bf(_pad_cols(rwkv_w1[0], LANES)), _bf(_pad_rows(rwkv_w2[0], LANES)),
        _bf(_pad_cols(rwkv_a1[0], LANES)), _bf(_pad_rows(rwkv_a2[0], LANES)),
        _bf(rwkv_g1[0]), _bf(rwkv_g2[0]), e, et, tm_pre, chunk)
    y = _wkv(r, cum, k, v, kk, kka, chunk=chunk, seq_tile=_tile(s, 512), n_pairs=4)

    flat = lambda arr: arr.reshape(t, d)
    tm = _tile(t, 512)
    vec_post = jnp.stack([rwkv_lnx_w[0], rwkv_lnx_b[0], rwkv_r_k[0].reshape(d),
                          zeros, zeros, zeros, zeros, zeros])
    x1 = _rwkv_post(flat(y), flat(r), flat(k), flat(v), flat(g), flat(x), vec_post,
                    _bf(rwkv_wo[0]), e, et, tm)

    ff = ffn_w1.shape[2]
    tf = ff // 2 if (ff // 2) % LANES == 0 else ff
    x2 = _ffn(x1, norm_gain[0, 1][None, :], _bf(ffn_w1[0]), _bf(ffn_w3[0]), _bf(ffn_w2[0]), tm, tf)

    n_slopes = len(BRANCHES) * HEADS_PER_BRANCH
    slopes = jnp.exp2(-ALIBI_MAX * (jnp.arange(n_slopes, dtype=F32) + 1.0) / n_slopes)
    gains = jnp.stack([norm_gain[1, 0], kv_norm_gain] + [zeros] * 6)
    tm_a = _tile(s, 512)
    qkv = _attn_proj(x2.reshape(b, s, d), gains, _bf(attn_wq[0]), _bf(w_kv), tm_a)
    os_, ls_ = [], []
    for br in range(len(BRANCHES)):
        o_br, l_br = _attn_branch(slopes, *qkv[3 * br:3 * br + 3], br)
        os_.append(o_br)
        ls_.append(l_br)

    router = jnp.concatenate(_split2(_pad_cols(moe_router[0], LANES)), axis=1)
    rbias = jnp.full((1, LANES), NEG, F32).at[0, :N_EXPERTS].set(moe_router_bias[0])
    x3, h4, route, cnt = _attn_out(os_, ls_, x2, norm_gain[1, 1][None, :], _bf(attn_wo[0]),
                                   router, rbias, b, tm_a)

    tm_e = 512 if t >= 4096 else 128
    ts = _tile(t, 512)
    counts = cnt[0, :N_EXPERTS].astype(jnp.int32)
    padded = ((counts + tm_e - 1) // tm_e) * tm_e
    ends = jnp.cumsum(padded)
    meta = jnp.concatenate([ends - padded, ends]).astype(jnp.int32)
    route_i = route[:, :4].astype(jnp.int32)
    group_start = jnp.sum(jnp.where(route_i[:, :2, None] == jnp.arange(N_EXPERTS), ends - padded, 0), axis=-1)
    dest = (group_start + route_i[:, 2:4]).astype(jnp.int32).reshape(2 * t)
    rows = 2 * t + N_EXPERTS * tm_e
    n_tiles = rows // tm_e
    n_valid = (ends[-1] // tm_e).astype(jnp.int32)
    tile_src = jnp.minimum(jnp.arange(n_tiles, dtype=jnp.int32), n_valid - 1)
    tile_expert = jnp.minimum(
        jnp.sum(tile_src[:, None] * tm_e >= ends[None, :], axis=1), N_EXPERTS - 1).astype(jnp.int32)

    xs = _moe_scatter(meta, dest, h4, rows, ts, tm_e)
    ffe = moe_w1.shape[3]
    tfe = ffe // 2 if (ffe // 2) % LANES == 0 else ffe
    ys = _moe_ffn(tile_src, tile_expert, n_valid.reshape(1), xs,
                  _bf(moe_w1[0]), _bf(moe_w3[0]), _bf(moe_w2[0]), tm_e, tfe)
    out = _moe_combine(dest, x3, route, final_norm_gain[None, :], ys, ts)
    return out.reshape(b, s, d)
```

```python
import functools
import math

import jax
import jax.numpy as jnp
from jax import lax
from jax.experimental import pallas as pl
from jax.experimental.pallas import tpu as pltpu

F32 = jnp.float32
BF16 = jnp.bfloat16

RMS_EPS = 1e-5
GN_EPS = 64e-5
HEAD = 64
LANES = 128
BRANCHES = ((128, 1), (512, 4), (2048, 16))
HEADS_PER_BRANCH = 8
ALIBI_MAX = 8.0
N_EXPERTS = 8
NEG = -1e30
VMEM_LIMIT_BYTES = 56 * 1024 * 1024


def _params(*sem):
    return pltpu.CompilerParams(dimension_semantics=sem, vmem_limit_bytes=VMEM_LIMIT_BYTES)


def _dot(a, b):
    return jnp.dot(a, b, preferred_element_type=F32)


def _dot_nt(a, b):
    return lax.dot_general(a, b, (((1,), (1,)), ((), ())), preferred_element_type=F32)


def _bf(x):
    return x.astype(BF16)


def _split2(x):
    hi = x.astype(BF16)
    lo = (x - hi.astype(F32)).astype(BF16)
    return hi, lo


def _sigmoid(z):
    return 1.0 / (1.0 + jnp.exp(-z))


def _rms(x, gain):
    return x * lax.rsqrt(jnp.mean(x * x, axis=-1, keepdims=True) + RMS_EPS) * gain


def _headsum(x, e, et):
    return _dot(_bf(_dot(_bf(x), e)), et)


def _const_spec(shape):
    nd = len(shape)
    return pl.BlockSpec(shape, lambda *_: (0,) * nd)


def _rwkv_pre_kernel(x_ref, xp_ref, vec_ref, mu_ref, wr_ref, wk_ref, wv_ref, w1_ref, w2_ref,
                     a1_ref, a2_ref, g1_ref, g2_ref, e_ref, et_ref,
                     r_out, cum_out, k_out, v_out, kk_out, b_out, g_out, *, chunk):
    i = pl.program_id(1)
    vec = vec_ref[...]
    gain, w0, a0, k_k, k_a = (vec[j:j + 1] for j in range(5))
    x = x_ref[0]
    h = _rms(x, gain)
    hp = _rms(xp_ref[0][7:8, :], gain)
    hp = jnp.where(i > 0, hp, 0.0)
    rows = lax.broadcasted_iota(jnp.int32, h.shape, 0)
    hprev = jnp.where(rows == 0, hp, pltpu.roll(h, 1, 0))
    xx = hprev - h
    mu = mu_ref[...]
    xr, xw, xk, xv, xa, xg = (_bf(h + xx * mu[j:j + 1]) for j in range(6))
    r = _dot(xr, wr_ref[...])
    k = _dot(xk, wk_ref[...])
    v = _dot(xv, wv_ref[...])
    wl = w0 + _dot(_bf(jnp.tanh(_dot(xw, w1_ref[...]))), w2_ref[...])
    a = _sigmoid(a0 + _dot(_bf(_dot(xa, a1_ref[...])), a2_ref[...]))
    g = _dot(_bf(_sigmoid(_dot(xg, g1_ref[...]))), g2_ref[...])
    kkr = k * k_k
    ss = _headsum(kkr * kkr, e_ref[...], et_ref[...])
    kk = kkr / jnp.maximum(jnp.sqrt(ss), 1e-12)
    lw = _sigmoid(wl) * (-math.exp(-0.5))
    tm = lw.shape[0]
    rr = lax.broadcasted_iota(jnp.int32, (tm, tm), 0)
    cc = lax.broadcasted_iota(jnp.int32, (tm, tm), 1)
    same_chunk = jnp.bitwise_xor(rr, cc) < chunk
    in_chunk = jnp.where(rr >= cc, jnp.where(same_chunk, 1.0, 0.0), 0.0).astype(BF16)
    hi, lo = _split2(lw)
    cum_out[0] = _dot(in_chunk, hi) + _dot(in_chunk, lo)
    r_out[0] = _bf(r)
    k_out[0] = _bf(k * (1.0 + (a - 1.0) * k_a))
    v_out[0] = _bf(v)
    kk_out[0] = _bf(kk)
    b_out[0] = _bf(kk * a)
    g_out[0] = _bf(g)


def _rwkv_pre(x, vec, mu, wr, wk, wv, w1, w2, a1, a2, g1, g2, e, et, tm, chunk):
    b, s, d = x.shape
    assert tm % chunk == 0
    tok = pl.BlockSpec((1, tm, d), lambda bi, i: (bi, i, 0))
    prev = pl.BlockSpec((1, 8, d), lambda bi, i: (bi, jnp.maximum(i * (tm // 8) - 1, 0), 0))
    consts = [vec, mu, wr, wk, wv, w1, w2, a1, a2, g1, g2, e, et]
    out = lambda dt: jax.ShapeDtypeStruct((b, s, d), dt)
    return pl.pallas_call(
        functools.partial(_rwkv_pre_kernel, chunk=chunk),
        grid=(b, s // tm),
        in_specs=[tok, prev] + [_const_spec(c.shape) for c in consts],
        out_specs=[tok] * 7,
        out_shape=[out(BF16), out(F32)] + [out(BF16)] * 5,
        compiler_params=_params("parallel", "arbitrary"),
        name="rwkv_pre",
    )(x, x, *consts)


def _round_robin(generators):
    results = [None] * len(generators)
    live = list(enumerate(generators))
    while live:
        still = []
        for idx, gen in live:
            try:
                results[idx] = next(gen)
                still.append((idx, gen))
            except StopIteration:
                pass
        live = still
    return results


def _wkv_kernel(r_ref, cum_ref, k_ref, v_ref, kk_ref, b_ref, y_ref, st_ref, *, chunk):
    c_len = chunk
    n_chunks = r_ref.shape[1] // c_len
    n_pairs = r_ref.shape[2] // LANES
    inv_steps = max(c_len.bit_length() - 2, 0)
    lane = lax.broadcasted_iota(jnp.int32, (c_len, LANES), 1)
    head0 = lane < HEAD
    rr = lax.broadcasted_iota(jnp.int32, (c_len, c_len), 0)
    cc = lax.broadcasted_iota(jnp.int32, (c_len, c_len), 1)
    lower = rr >= cc
    strict = rr > cc
    eye = jnp.where(rr == cc, 1.0, 0.0).astype(F32)
    r2 = lax.broadcasted_iota(jnp.int32, (LANES, LANES), 0)
    c2 = lax.broadcasted_iota(jnp.int32, (LANES, LANES), 1)
    same_head = (r2 < HEAD) == (c2 < HEAD)

    @pl.when(pl.program_id(2) == 0)
    def _():
        st_ref[...] = jnp.zeros_like(st_ref)

    first_row = lax.broadcasted_iota(jnp.int32, (c_len, LANES), 0) == 0
    rows = lambda mat, j: mat[j * c_len:(j + 1) * c_len]
    own_head = lambda mat: jnp.where(head0, rows(mat, 0), rows(mat, 1))

    def chunk_pair(sl, lanes, state):
        cum = cum_ref[0, sl, lanes]
        cum_prev = jnp.where(first_row, 0.0, pltpu.roll(cum, 1, 0))
        cum_last = cum[c_len - 1:c_len, :]
        kk = kk_ref[0, sl, lanes].astype(F32)
        b = b_ref[0, sl, lanes].astype(F32)
        k = k_ref[0, sl, lanes].astype(F32)
        vb = v_ref[0, sl, lanes]
        e_neg = jnp.exp(-cum)
        e_end = jnp.exp(cum_last - cum)
        a_t = -kk * jnp.exp(cum_prev)
        r_t = r_ref[0, sl, lanes].astype(F32) * jnp.exp(cum)
        lhs = _bf(jnp.concatenate([jnp.where(head0, a_t, 0.0), jnp.where(head0, 0.0, a_t),
                                   jnp.where(head0, r_t, 0.0), jnp.where(head0, 0.0, r_t)], axis=0))
        gram = _dot_nt(lhs, _bf(jnp.concatenate([b * e_neg, k * e_neg], axis=0)))
        from_state = _dot_nt(_bf(jnp.concatenate([a_t, r_t], axis=0)), _bf(state))
        yield
        gram_b, gram_k = gram[:, :c_len], gram[:, c_len:]
        from_v = _dot(_bf(jnp.concatenate(
            [jnp.where(strict, rows(gram_k, 0), 0.0), jnp.where(strict, rows(gram_k, 1), 0.0),
             jnp.where(lower, rows(gram_k, 2), 0.0), jnp.where(lower, rows(gram_k, 3), 0.0)], axis=0)), vb)
        m_rb = _bf(jnp.concatenate([jnp.where(lower, rows(gram_b, 2), 0.0),
                                    jnp.where(lower, rows(gram_b, 3), 0.0)], axis=0))
        w = rows(from_state, 0) + own_head(from_v[:2 * c_len])
        y1 = rows(from_state, 1) + own_head(from_v[2 * c_len:])
        pw = [jnp.where(strict, rows(gram_b, hd), 0.0) for hd in range(2)]
        inv = [eye + pw[hd] for hd in range(2)]
        if inv_steps:
            pw = [_dot(_bf(pw[hd]), _bf(pw[hd])) for hd in range(2)]
        for step in range(inv_steps):
            yield
            if step < inv_steps - 1:
                both = [_dot(_bf(jnp.concatenate([inv[hd], pw[hd]], axis=0)), _bf(pw[hd])) for hd in range(2)]
                inv = [inv[hd] + rows(both[hd], 0) for hd in range(2)]
                pw = [rows(both[hd], 1) for hd in range(2)]
            else:
                inv = [inv[hd] + _dot(_bf(inv[hd]), _bf(pw[hd])) for hd in range(2)]
        yield
        u = own_head(_dot(_bf(jnp.concatenate(inv, axis=0)), _bf(w)))
        yield
        y = y1 + own_head(_dot(m_rb, _bf(u)))
        uv_t = _bf(jnp.transpose(jnp.concatenate([u, vb.astype(F32)], axis=0)))
        bk = _bf(jnp.concatenate([b * e_end, k * e_end], axis=0))
        yield y, state * jnp.exp(cum_last) + jnp.where(same_head, _dot(uv_t, bk), 0.0)

    def body(c, carry):
        sl = pl.ds(pl.multiple_of(c * c_len, c_len), c_len)
        lanes = [slice(p * LANES, (p + 1) * LANES) for p in range(n_pairs)]
        results = _round_robin([chunk_pair(sl, lanes[p], st_ref[p]) for p in range(n_pairs)])
        for p, (y, new_state) in enumerate(results):
            y_ref[0, sl, lanes[p]] = _bf(y)
            st_ref[p] = new_state
        return carry

    lax.fori_loop(0, n_chunks, body, 0)


def _wkv(r, cum, k, v, kk, bb, chunk, seq_tile, n_pairs):
    b, s, d = r.shape
    width = n_pairs * LANES
    spec = pl.BlockSpec((1, seq_tile, width), lambda bi, hi, si: (bi, si, hi))
    return pl.pallas_call(
        functools.partial(_wkv_kernel, chunk=chunk),
        grid=(b, d // width, s // seq_tile),
        in_specs=[spec] * 6,
        out_specs=spec,
        out_shape=jax.ShapeDtypeStruct((b, s, d), BF16),
        scratch_shapes=[pltpu.VMEM((n_pairs, LANES, LANES), F32)],
        compiler_params=_params("parallel", "parallel", "arbitrary"),
        name="wkv7",
    )(r, cum, k, v, kk, bb)


def _rwkv_post_kernel(y_ref, r_ref, k_ref, v_ref, g_ref, x_ref, vec_ref, wo_ref, e_ref, et_ref, o_ref):
    vec = vec_ref[...]
    lnx_w, lnx_b, r_k = (vec[j:j + 1] for j in range(3))
    e = e_ref[...]
    et = et_ref[...]
    y = y_ref[...].astype(F32)
    v = v_ref[...].astype(F32)
    mean = _headsum(y, e, et) * (1.0 / HEAD)
    dy = y - mean
    var = _headsum(dy * dy, e, et) * (1.0 / HEAD)
    yn = dy * lax.rsqrt(var + GN_EPS) * lnx_w + lnx_b
    bonus = _headsum(r_ref[...].astype(F32) * k_ref[...].astype(F32) * r_k, e, et) * v
    o_ref[...] = x_ref[...] + _dot(_bf((yn + bonus) * g_ref[...].astype(F32)), wo_ref[...])


def _rwkv_post(y, r, k, v, g, x, vec, wo, e, et, tm):
    t, d = x.shape
    tok = pl.BlockSpec((tm, d), lambda i: (i, 0))
    consts = [vec, wo, e, et]
    return pl.pallas_call(
        _rwkv_post_kernel,
        grid=(t // tm,),
        in_specs=[tok] * 6 + [_const_spec(c.shape) for c in consts],
        out_specs=tok,
        out_shape=jax.ShapeDtypeStruct((t, d), F32),
        compiler_params=_params("parallel"),
        name="rwkv_post",
    )(y, r, k, v, g, x, *consts)


FFN_SUB_ROWS = 512


def _ffn_kernel(x_ref, gain_ref, w1_ref, w3_ref, w2_ref, o_ref, h_ref, acc_ref):
    f = pl.program_id(1)

    @pl.when(f == 0)
    def _():
        h_ref[...] = _bf(_rms(x_ref[...], gain_ref[...]))
        acc_ref[...] = jnp.zeros_like(acc_ref)

    for start in range(0, h_ref.shape[0], FFN_SUB_ROWS):
        rows = slice(start, min(start + FFN_SUB_ROWS, h_ref.shape[0]))
        h = h_ref[rows, :]
        u = _dot(h, w1_ref[...])
        act = _bf(u * _sigmoid(u) * _dot(h, w3_ref[...]))
        acc_ref[rows, :] += _dot(act, w2_ref[...])

    @pl.when(f == pl.num_programs(1) - 1)
    def _():
        o_ref[...] = x_ref[...] + acc_ref[...]


def _ffn(x, gain, w1, w3, w2, tm, tf):
    t, d = x.shape
    ff = w1.shape[1]
    tok = pl.BlockSpec((tm, d), lambda i, f: (i, 0))
    return pl.pallas_call(
        _ffn_kernel,
        grid=(t // tm, ff // tf),
        in_specs=[tok, _const_spec(gain.shape),
                  pl.BlockSpec((d, tf), lambda i, f: (0, f)),
                  pl.BlockSpec((d, tf), lambda i, f: (0, f)),
                  pl.BlockSpec((tf, d), lambda i, f: (f, 0))],
        out_specs=tok,
        out_shape=jax.ShapeDtypeStruct((t, d), F32),
        scratch_shapes=[pltpu.VMEM((tm, d), BF16), pltpu.VMEM((tm, d), F32)],
        compiler_params=_params("parallel", "arbitrary"),
        name="ffn_dense",
    )(x, gain, w1, w3, w2)


BRANCH_WIDTH = HEADS_PER_BRANCH * HEAD


def _attn_proj_kernel(x_ref, gains_ref, wq_ref, wkv_ref, *refs):
    outs, (q_scr, kv_scr) = refs[:-2], refs[-2:]
    x = x_ref[0]
    tm = x.shape[0]
    n = x * lax.rsqrt(jnp.mean(x * x, axis=-1, keepdims=True) + RMS_EPS)
    gains = gains_ref[...]
    q_tiles = q_scr.shape[0]
    q = _dot(_bf(n * gains[0:1]), wq_ref[...]) * (1.0 / HEAD ** 0.5)
    kv = _dot(_bf(n * gains[1:2]), wkv_ref[...])
    for ti in range(q_tiles):
        q_scr[ti] = q[:, ti * LANES:(ti + 1) * LANES]
    for ti in range(2 * q_tiles):
        kv_scr[ti] = kv[:, ti * LANES:(ti + 1) * LANES]
    tiles_br = BRANCH_WIDTH // LANES
    for br, (_, dil) in enumerate(BRANCHES):
        q_out, k_out, v_out = outs[3 * br:3 * br + 3]
        for res in range(dil):
            rows = pl.ds(res, tm // dil, stride=dil) if dil > 1 else slice(None)
            for ti in range(tiles_br):
                src = br * tiles_br + ti
                lanes = slice(ti * LANES, (ti + 1) * LANES)
                q_out[0, res, :, lanes] = _bf(q_scr[src, rows, :])
                k_out[0, res, :, lanes] = _bf(kv_scr[src, rows, :])
                v_out[0, res, :, lanes] = _bf(kv_scr[q_tiles + src, rows, :])


def _attn_proj(x, gains, wq, wkv, tm):
    b, s, d = x.shape
    qw = wq.shape[1]
    out_specs, out_shape = [], []
    for _, dil in BRANCHES:
        assert tm % (16 * dil) == 0
        out_specs += [pl.BlockSpec((1, dil, tm // dil, BRANCH_WIDTH), lambda bi, i: (bi, 0, i, 0))] * 3
        out_shape += [jax.ShapeDtypeStruct((b, dil, s // dil, BRANCH_WIDTH), BF16)] * 3
    return pl.pallas_call(
        _attn_proj_kernel,
        grid=(b, s // tm),
        in_specs=[pl.BlockSpec((1, tm, d), lambda bi, i: (bi, i, 0)),
                  _const_spec(gains.shape), _const_spec(wq.shape), _const_spec(wkv.shape)],
        out_specs=out_specs,
        out_shape=out_shape,
        scratch_shapes=[pltpu.VMEM((qw // LANES, tm, LANES), F32),
                        pltpu.VMEM((2 * qw // LANES, tm, LANES), F32)],
        compiler_params=_params("parallel", "parallel"),
        name="attn_proj",
    )(x, gains, wq, wkv)


def _attn_kernel(slopes_ref, q_ref, k_ref, v_ref, o_ref, lse_ref, *, branch, dilation, qb):
    sub_len = q_ref.shape[2]
    n_blk = sub_len // qb
    n_pairs = q_ref.shape[3] // LANES
    col2 = lax.broadcasted_iota(jnp.int32, (qb, 2 * qb), 1)
    rel = (lax.broadcasted_iota(jnp.int32, (qb, 2 * qb), 0) - jnp.where(col2 < qb, col2, col2 - qb)).astype(F32)
    dist_d = jnp.where(rel >= 0, rel, -NEG)
    dist_p = jnp.where(rel <= 0, rel + float(qb), -NEG)
    first_head = lax.broadcasted_iota(jnp.int32, (1, 2 * qb), 1) < qb
    lane = lax.broadcasted_iota(jnp.int32, (qb, LANES), 1)
    head0 = lane < HEAD
    lane_row = lax.broadcasted_iota(jnp.int32, (1, LANES), 1)
    head_mask = [jnp.where(lane_row < HEAD, 1.0, 0.0).astype(BF16),
                 jnp.where(lane_row < HEAD, 0.0, 1.0).astype(BF16)]
    head_ones = [jnp.where(head0, 1.0, 0.0).astype(BF16), jnp.where(head0, 0.0, 1.0).astype(BF16)]

    def block(residues, i):
        cur = pl.ds(pl.multiple_of(i * qb, qb), qb)
        if n_blk > 1:
            prv = pl.ds(pl.multiple_of(jnp.maximum(i - 1, 0) * qb, qb), qb)
            has_prev = jnp.where(i > 0, 1.0, -NEG)

        def one_pair(r, p):
            lanes = slice(p * LANES, (p + 1) * LANES)
            slope = [slopes_ref[branch * HEADS_PER_BRANCH + 2 * p + j] * float(dilation) for j in range(2)]
            step = jnp.where(first_head, slope[0], slope[1])

            def keys(rows):
                k = k_ref[0, r, rows, lanes]
                return jnp.concatenate([k * head_mask[0], k * head_mask[1]], axis=0)

            def values(rows):
                v = v_ref[0, r, rows, lanes]
                return jnp.concatenate(
                    [jnp.concatenate([v * head_mask[0], head_ones[0]], axis=1),
                     jnp.concatenate([v * head_mask[1], head_ones[1]], axis=1)], axis=0)

            q = q_ref[0, r, cur, lanes]
            sd = _dot_nt(q, keys(cur)) - step * dist_d
            if n_blk > 1:
                sp = _dot_nt(q, keys(prv)) - (step * has_prev) * dist_p
            yield
            both = jnp.maximum(sd, sp) if n_blk > 1 else sd
            m = [jnp.max(both[:, :qb], axis=-1, keepdims=True), jnp.max(both[:, qb:], axis=-1, keepdims=True)]
            probs = lambda s: _bf(jnp.concatenate([jnp.exp(s[:, :qb] - m[0]), jnp.exp(s[:, qb:] - m[1])], axis=1))
            acc = _dot(probs(sd), values(cur))
            if n_blk > 1:
                acc = acc + _dot(probs(sp), values(prv))
            yield
            den = acc[:, LANES:]
            lse = jnp.where(head0, m[0], m[1]) + jnp.log(den)
            yield acc[:, :LANES] / den, lse[:, 0:1], lse[:, HEAD:HEAD + 1]

        res = _round_robin([one_pair(r, p) for r in residues for p in range(n_pairs)])
        for n, r in enumerate(residues):
            lse = jnp.zeros((qb, LANES), F32)
            for p in range(n_pairs):
                out, l_a, l_b = res[n * n_pairs + p]
                o_ref[0, r, cur, p * LANES:(p + 1) * LANES] = _bf(out)
                lse = jnp.where(lane == 2 * p, l_a, jnp.where(lane == 2 * p + 1, l_b, lse))
            lse_ref[0, r, cur, :] = lse

    n_res = q_ref.shape[1]
    if n_blk == 1:
        block(range(n_res), 0)
    else:
        for r in range(n_res):
            lax.fori_loop(0, n_blk, lambda i, c, r=r: (block([r], i), c)[1], 0,
                          unroll=2 if n_blk % 2 == 0 else 1)


def _attn_branch(slopes, q, k, v, branch):
    window, dilation = BRANCHES[branch]
    assert window % dilation == 0 and window // dilation == LANES
    b, _, sub_len, width = q.shape
    qb = min(LANES, sub_len)
    assert sub_len % qb == 0
    n_res = max(1, min(dilation, 512 // sub_len))
    assert dilation % n_res == 0
    spec = pl.BlockSpec((1, n_res, sub_len, width), lambda bi, ri: (bi, ri, 0, 0))
    return pl.pallas_call(
        functools.partial(_attn_kernel, branch=branch, dilation=dilation, qb=qb),
        grid=(b, dilation // n_res),
        in_specs=[pl.BlockSpec(memory_space=pltpu.SMEM)] + [spec] * 3,
        out_specs=[spec, pl.BlockSpec((1, n_res, sub_len, LANES), lambda bi, ri: (bi, ri, 0, 0))],
        out_shape=[jax.ShapeDtypeStruct(q.shape, BF16),
                   jax.ShapeDtypeStruct(q.shape[:3] + (LANES,), F32)],
        compiler_params=_params("parallel", "parallel"),
        name=f"dilated_attn_{branch}",
    )(slopes, q, k, v)


def _attn_out_kernel(o0, o1, o2, l0, l1, l2, x_ref, gain_ref, wo_ref, router_ref, rb_ref, expand_ref,
                     x_out, h_out, route_out, cnt_out, cnt_ref, o_scr, l_scr):
    @pl.when((pl.program_id(0) == 0) & (pl.program_id(1) == 0))
    def _():
        cnt_ref[...] = jnp.zeros_like(cnt_ref)

    tm = x_ref.shape[0]
    tiles_br = BRANCH_WIDTH // LANES
    for br, (o_ref, l_ref) in enumerate(((o0, l0), (o1, l1), (o2, l2))):
        dil = BRANCHES[br][1]
        for res in range(dil):
            rows = pl.ds(res, tm // dil, stride=dil) if dil > 1 else slice(None)
            l_scr[br, rows, :] = l_ref[0, res]
            for ti in range(tiles_br):
                lanes = slice(ti * LANES, (ti + 1) * LANES)
                o_scr[br, ti, rows, :] = o_ref[0, res, :, lanes].astype(F32)
    ls = [l_scr[br] for br in range(3)]
    m = jnp.maximum(jnp.maximum(ls[0], ls[1]), ls[2])
    ws = [jnp.exp(l - m) for l in ls]
    total = ws[0] + ws[1] + ws[2]
    spread = []
    for br in range(3):
        hi, lo = _split2(ws[br] / total)
        spread.append(_dot(hi, expand_ref[...]) + _dot(lo, expand_ref[...]))
    merged = []
    for ti in range(tiles_br):
        lanes = slice(ti * LANES, (ti + 1) * LANES)
        merged.append(_bf(sum(spread[br][:, lanes] * o_scr[br, ti] for br in range(3))))
    x = x_ref[...] + _dot(jnp.concatenate(merged, axis=1), wo_ref[...])
    x_out[...] = x
    h = _rms(x, gain_ref[...])
    h_out[...] = h
    hhi, hlo = _split2(h)
    both = _dot(hhi, router_ref[...])
    logits = both[:, :LANES] + both[:, LANES:] + _dot(hlo, router_ref[:, :LANES]) + rb_ref[...]
    lane = lax.broadcasted_iota(jnp.int32, logits.shape, 1)
    m1 = jnp.max(logits, axis=-1, keepdims=True)
    i1 = jnp.min(jnp.where(logits == m1, lane, LANES), axis=-1, keepdims=True)
    rest = jnp.where(lane == i1, -3e38, logits)
    m2 = jnp.max(rest, axis=-1, keepdims=True)
    i2 = jnp.min(jnp.where(rest == m2, lane, LANES), axis=-1, keepdims=True)
    ex = jnp.exp(m2 - m1)
    sel = jnp.where((lane == i1) | (lane == i2), 1.0, 0.0)
    rr = lax.broadcasted_iota(jnp.int32, (tm, tm), 0)
    cc = lax.broadcasted_iota(jnp.int32, (tm, tm), 1)
    before = jnp.where(rr > cc, 1.0, 0.0).astype(BF16)
    rank = cnt_ref[...] + _dot(before, _bf(sel))
    cnt_ref[...] += jnp.sum(sel, axis=0, keepdims=True)
    cnt_out[...] = cnt_ref[...]
    columns = [i1.astype(F32), i2.astype(F32),
               jnp.sum(jnp.where(lane == i1, rank, 0.0), axis=-1, keepdims=True),
               jnp.sum(jnp.where(lane == i2, rank, 0.0), axis=-1, keepdims=True),
               1.0 / (1.0 + ex), ex / (1.0 + ex)]
    route = jnp.zeros(logits.shape, F32)
    for j, col in enumerate(columns):
        route = jnp.where(lane == j, col, route)
    route_out[...] = route


def _attn_out(os_, ls_, x, gain, wo, router, rb, b, tm):
    t, d = x.shape
    tiles = t // (b * tm)
    tok = pl.BlockSpec((tm, d), lambda bi, i: (bi * tiles + i, 0))
    small = pl.BlockSpec((tm, LANES), lambda bi, i: (bi * tiles + i, 0))
    att = [pl.BlockSpec((1, dil, tm // dil, BRANCH_WIDTH), lambda bi, i: (bi, 0, i, 0))
           for _, dil in BRANCHES]
    lse = [pl.BlockSpec((1, dil, tm // dil, LANES), lambda bi, i: (bi, 0, i, 0)) for _, dil in BRANCHES]
    expand = (jnp.arange(LANES)[:, None] == jnp.arange(BRANCH_WIDTH)[None, :] // HEAD).astype(BF16)
    consts = [gain, wo, router, rb, expand]
    return pl.pallas_call(
        _attn_out_kernel,
        grid=(b, tiles),
        in_specs=att + lse + [tok] + [_const_spec(c.shape) for c in consts],
        out_specs=[tok, tok, small, _const_spec((1, LANES))],
        out_shape=[jax.ShapeDtypeStruct((t, d), F32), jax.ShapeDtypeStruct((t, d), F32),
                   jax.ShapeDtypeStruct((t, LANES), F32), jax.ShapeDtypeStruct((1, LANES), F32)],
        scratch_shapes=[pltpu.VMEM((1, LANES), F32),
                        pltpu.VMEM((len(BRANCHES), BRANCH_WIDTH // LANES, tm, LANES), F32),
                        pltpu.VMEM((len(BRANCHES), tm, LANES), F32)],
        compiler_params=_params("arbitrary", "arbitrary"),
        name="attn_out_route",
    )(*os_, *ls_, x, *consts)


DMA_UNROLL = 8


def _row_copy(src, src_row, dst, dst_row, sem):
    return pltpu.make_async_copy(src.at[pl.ds(src_row, 1)], dst.at[pl.ds(dst_row, 1)], sem)


def _moe_scatter_kernel(meta_ref, dest_ref, h_ref, out_ref, zeros, sem, zero_sem):
    n_tok = h_ref.shape[0]
    tile = zeros.shape[0]
    ends = lambda e: meta_ref[N_EXPERTS + e]

    def zero_fill(e):
        if e < N_EXPERTS:
            start, used = jnp.maximum(ends(e) - tile, 0), ends(e) > meta_ref[e]
        else:
            start = ends(N_EXPERTS - 1) + (e - N_EXPERTS) * tile
            used = start < out_ref.shape[0]
            start = jnp.minimum(start, out_ref.shape[0] - tile)
        dst = out_ref.at[pl.ds(pl.multiple_of(start, tile), tile)]
        return used, pltpu.make_async_copy(zeros, dst, zero_sem)

    @pl.when(pl.program_id(0) == 0)
    def _():
        zeros[...] = jnp.zeros_like(zeros)
        for e in range(2 * N_EXPERTS):
            used, copy = zero_fill(e)
            pl.when(used)(copy.start)
        for e in range(2 * N_EXPERTS):
            used, copy = zero_fill(e)
            pl.when(used)(copy.wait)

    def issue(j, carry):
        for k in range(2):
            _row_copy(h_ref, j, out_ref, dest_ref[2 * j + k], sem).start(priority=k)
        return carry

    def drain(j, carry):
        for _ in range(2):
            _row_copy(h_ref, 0, out_ref, 0, sem).wait()
        return carry

    lax.fori_loop(0, n_tok, issue, 0, unroll=DMA_UNROLL)
    lax.fori_loop(0, n_tok, drain, 0, unroll=DMA_UNROLL)


def _moe_scatter(meta, dest_flat, h, rows, ts, tile):
    t, d = h.shape
    return pl.pallas_call(
        _moe_scatter_kernel,
        grid=(t // ts,),
        in_specs=[pl.BlockSpec(memory_space=pltpu.SMEM),
                  pl.BlockSpec((2 * ts,), lambda i: (i,), memory_space=pltpu.SMEM),
                  pl.BlockSpec((ts, d), lambda i: (i, 0))],
        out_specs=pl.BlockSpec(memory_space=pl.ANY),
        out_shape=jax.ShapeDtypeStruct((rows, d), F32),
        scratch_shapes=[pltpu.VMEM((tile, d), F32), pltpu.SemaphoreType.DMA(()),
                        pltpu.SemaphoreType.DMA(())],
        compiler_params=_params("arbitrary"),
        name="moe_scatter",
    )(meta, dest_flat, h)


def _moe_ffn_kernel(src_ref, te_ref, nv_ref, x_ref, w1_ref, w3_ref, w2_ref, o_ref, acc_ref):
    del src_ref, te_ref
    i = pl.program_id(0)
    f = pl.program_id(1)

    @pl.when(i < nv_ref[0])
    def _():
        @pl.when(f == 0)
        def _():
            acc_ref[...] = jnp.zeros_like(acc_ref)

        h = _bf(x_ref[...])
        u = _dot(h, w1_ref[0])
        act = _bf(u * _sigmoid(u) * _dot(h, w3_ref[0]))
        acc_ref[...] += _dot(act, w2_ref[0])

        @pl.when(f == pl.num_programs(1) - 1)
        def _():
            o_ref[...] = acc_ref[...]

    @pl.when(i >= nv_ref[0])
    def _():
        o_ref[...] = jnp.zeros_like(o_ref)


def _moe_ffn(tile_src, tile_expert, n_valid, xs, w1, w3, w2, tm, tf):
    rows, d = xs.shape
    ff = w1.shape[2]
    tok = pl.BlockSpec((tm, d), lambda i, f, src, te, nv: (src[i], 0))
    return pl.pallas_call(
        _moe_ffn_kernel,
        grid_spec=pltpu.PrefetchScalarGridSpec(
            num_scalar_prefetch=3,
            grid=(rows // tm, ff // tf),
            in_specs=[tok,
                      pl.BlockSpec((1, d, tf), lambda i, f, src, te, nv: (te[i], 0, f)),
                      pl.BlockSpec((1, d, tf), lambda i, f, src, te, nv: (te[i], 0, f)),
                      pl.BlockSpec((1, tf, d), lambda i, f, src, te, nv: (te[i], f, 0))],
            out_specs=pl.BlockSpec((tm, d), lambda i, f, src, te, nv: (i, 0)),
            scratch_shapes=[pltpu.VMEM((tm, d), F32)],
        ),
        out_shape=jax.ShapeDtypeStruct((rows, d), F32),
        compiler_params=_params("arbitrary", "arbitrary"),
        name="moe_experts",
    )(tile_src, tile_expert, n_valid, xs, w1, w3, w2)


def _moe_combine_kernel(dest_ref, x_ref, gates_ref, gain_ref, y_hbm, o_ref, buf, sem):
    n_tok = x_ref.shape[0]

    def issue(j, carry):
        for k in range(2):
            _row_copy(y_hbm, dest_ref[2 * j + k], buf.at[k], j, sem).start(priority=k)
        return carry

    def drain(j, carry):
        for k in range(2):
            _row_copy(y_hbm, 0, buf.at[k], 0, sem).wait()
        return carry

    lax.fori_loop(0, n_tok, issue, 0, unroll=DMA_UNROLL)
    lax.fori_loop(0, n_tok, drain, 0, unroll=DMA_UNROLL)
    gates = gates_ref[...]
    x = x_ref[...] + gates[:, 4:5] * buf[0] + gates[:, 5:6] * buf[1]
    o_ref[...] = _rms(x, gain_ref[...])


def _moe_combine(dest_flat, x, route, gain, ys, ts):
    t, d = x.shape
    tok = pl.BlockSpec((ts, d), lambda i: (i, 0))
    return pl.pallas_call(
        _moe_combine_kernel,
        grid=(t // ts,),
        in_specs=[pl.BlockSpec((2 * ts,), lambda i: (i,), memory_space=pltpu.SMEM),
                  tok, pl.BlockSpec((ts, LANES), lambda i: (i, 0)), _const_spec(gain.shape),
                  pl.BlockSpec(memory_space=pl.ANY)],
        out_specs=tok,
        out_shape=jax.ShapeDtypeStruct((t, d), F32),
        scratch_shapes=[pltpu.VMEM((2, ts, d), F32), pltpu.SemaphoreType.DMA(())],
        compiler_params=_params("arbitrary"),
        name="moe_combine",
    )(dest_flat, x, route, gain, ys)


def _tile(n, want):
    t = min(n, want)
    assert n % t == 0
    return t


def _pad_cols(w, n):
    return jnp.pad(w, ((0, 0), (0, n - w.shape[1])))


def _pad_rows(w, n):
    return jnp.pad(w, ((0, n - w.shape[0]), (0, 0)))


def kernel(x, norm_gain, rwkv_mu, rwkv_wr, rwkv_wk, rwkv_wv, rwkv_w0, rwkv_w1, rwkv_w2, rwkv_a0, rwkv_a1, rwkv_a2, rwkv_g1, rwkv_g2, rwkv_k_k, rwkv_k_a, rwkv_r_k, rwkv_lnx_w, rwkv_lnx_b, rwkv_wo, kv_norm_gain, w_kv, attn_wq, attn_wo, ffn_w1, ffn_w3, ffn_w2, moe_router, moe_router_bias, moe_w1, moe_w3, moe_w2, final_norm_gain):
    b, s, d = x.shape
    t = b * s
    n_heads = d // HEAD
    assert norm_gain.shape[0] == 2 and d % LANES == 0 and s % 8 == 0

    chan_head = jnp.arange(d) // HEAD
    e = (chan_head[:, None] == jnp.arange(LANES)[None, :]).astype(BF16)
    et = e.T

    zeros = jnp.zeros((d,), F32)
    vec_pre = jnp.stack([norm_gain[0, 0], rwkv_w0[0], rwkv_a0[0], rwkv_k_k[0], rwkv_k_a[0],
                         zeros, zeros, zeros])
    tm_pre = _tile(s, 256)
    chunk = _tile(s, LANES)
    r, cum, k, v, kk, kka, g = _rwkv_pre(
        x, vec_pre, rwkv_mu[0], _bf(rwkv_wr[0]), _bf(rwkv_wk[0]), _bf(rwkv_wv[0]),
        _bf(_pad_cols(rwkv_w1[0], LANES)), _bf(_pad_rows(rwkv_w2[0], LANES)),
        _bf(_pad_cols(rwkv_a1[0], LANES)), _bf(_pad_rows(rwkv_a2[0], LANES)),
        _bf(rwkv_g1[0]), _bf(rwkv_g2[0]), e, et, tm_pre, chunk)
    y = _wkv(r, cum, k, v, kk, kka, chunk=chunk, seq_tile=_tile(s, 512), n_pairs=4)

    flat = lambda arr: arr.reshape(t, d)
    tm = _tile(t, 512)
    vec_post = jnp.stack([rwkv_lnx_w[0], rwkv_lnx_b[0], rwkv_r_k[0].reshape(d),
                          zeros, zeros, zeros, zeros, zeros])
    x1 = _rwkv_post(flat(y), flat(r), flat(k), flat(v), flat(g), flat(x), vec_post,
                    _bf(rwkv_wo[0]), e, et, tm)

    ff = ffn_w1.shape[2]
    tf = ff // 2 if (ff // 2) % LANES == 0 else ff
    x2 = _ffn(x1, norm_gain[0, 1][None, :], _bf(ffn_w1[0]), _bf(ffn_w3[0]), _bf(ffn_w2[0]),
              _tile(t, 2 * FFN_SUB_ROWS), tf)

    n_slopes = len(BRANCHES) * HEADS_PER_BRANCH
    slopes = jnp.exp2(-ALIBI_MAX * (jnp.arange(n_slopes, dtype=F32) + 1.0) / n_slopes)
    gains = jnp.stack([norm_gain[1, 0], kv_norm_gain] + [zeros] * 6)
    tm_a = _tile(s, 512)
    qkv = _attn_proj(x2.reshape(b, s, d), gains, _bf(attn_wq[0]), _bf(w_kv), tm_a)
    os_, ls_ = [], []
    for br in range(len(BRANCHES)):
        o_br, l_br = _attn_branch(slopes, *qkv[3 * br:3 * br + 3], br)
        os_.append(o_br)
        ls_.append(l_br)

    router = jnp.concatenate(_split2(_pad_cols(moe_router[0], LANES)), axis=1)
    rbias = jnp.full((1, LANES), NEG, F32).at[0, :N_EXPERTS].set(moe_router_bias[0])
    x3, h4, route, cnt = _attn_out(os_, ls_, x2, norm_gain[1, 1][None, :], _bf(attn_wo[0]),
                                   router, rbias, b, tm_a)

    tm_e = 512 if t >= 4096 else 128
    ts = _tile(t, 512)
    counts = cnt[0, :N_EXPERTS].astype(jnp.int32)
    padded = ((counts + tm_e - 1) // tm_e) * tm_e
    ends = jnp.cumsum(padded)
    meta = jnp.concatenate([ends - padded, ends]).astype(jnp.int32)
    route_i = route[:, :4].astype(jnp.int32)
    group_start = jnp.sum(jnp.where(route_i[:, :2, None] == jnp.arange(N_EXPERTS), ends - padded, 0), axis=-1)
    dest = (group_start + route_i[:, 2:4]).astype(jnp.int32).reshape(2 * t)
    rows = 2 * t + N_EXPERTS * tm_e
    n_tiles = rows // tm_e
    n_valid = (ends[-1] // tm_e).astype(jnp.int32)
    tile_src = jnp.minimum(jnp.arange(n_tiles, dtype=jnp.int32), n_valid - 1)
    tile_expert = jnp.minimum(
        jnp.sum(tile_src[:, None] * tm_e >= ends[None, :], axis=1), N_EXPERTS - 1).astype(jnp.int32)

    xs = _moe_scatter(meta, dest, h4, rows, ts, tm_e)
    ffe = moe_w1.shape[3]
    tfe = ffe // 2 if (ffe // 2) % LANES == 0 else ffe
    ys = _moe_ffn(tile_src, tile_expert, n_valid.reshape(1), xs,
                  _bf(moe_w1[0]), _bf(moe_w3[0]), _bf(moe_w2[0]), tm_e, tfe)
    out = _moe_combine(dest, x3, route, final_norm_gain[None, :], ys, ts)
    return out.reshape(b, s, d)
```

```python
import functools
import math

import jax
import jax.numpy as jnp
from jax import lax
from jax.experimental import pallas as pl
from jax.experimental.pallas import tpu as pltpu

F32 = jnp.float32
BF16 = jnp.bfloat16

RMS_EPS = 1e-5
GN_EPS = 64e-5
HEAD = 64
LANES = 128
BRANCHES = ((128, 1), (512, 4), (2048, 16))
HEADS_PER_BRANCH = 8
ALIBI_MAX = 8.0
N_EXPERTS = 8
NEG = -1e30
VMEM_LIMIT_BYTES = 56 * 1024 * 1024


def _params(*sem):
    return pltpu.CompilerParams(dimension_semantics=sem, vmem_limit_bytes=VMEM_LIMIT_BYTES)


def _dot(a, b):
    return jnp.dot(a, b, preferred_element_type=F32)


def _dot_nt(a, b):
    return lax.dot_general(a, b, (((1,), (1,)), ((), ())), preferred_element_type=F32)


def _bf(x):
    return x.astype(BF16)


def _split2(x):
    hi = x.astype(BF16)
    lo = (x - hi.astype(F32)).astype(BF16)
    return hi, lo


def _sigmoid(z):
    return 1.0 / (1.0 + jnp.exp(-z))


def _rms(x, gain):
    return x * lax.rsqrt(jnp.mean(x * x, axis=-1, keepdims=True) + RMS_EPS) * gain


def _headsum(x, e, et):
    return _dot(_bf(_dot(_bf(x), e)), et)


def _const_spec(shape):
    nd = len(shape)
    return pl.BlockSpec(shape, lambda *_: (0,) * nd)


def _rwkv_pre_kernel(x_ref, xp_ref, vec_ref, mu_ref, wr_ref, wk_ref, wv_ref, w1_ref, w2_ref,
                     a1_ref, a2_ref, g1_ref, g2_ref, e_ref, et_ref,
                     r_out, cum_out, k_out, v_out, kk_out, b_out, g_out, *, chunk):
    i = pl.program_id(1)
    vec = vec_ref[...]
    gain, w0, a0, k_k, k_a = (vec[j:j + 1] for j in range(5))
    x = x_ref[0]
    h = _rms(x, gain)
    hp = _rms(xp_ref[0][7:8, :], gain)
    hp = jnp.where(i > 0, hp, 0.0)
    rows = lax.broadcasted_iota(jnp.int32, h.shape, 0)
    hprev = jnp.where(rows == 0, hp, pltpu.roll(h, 1, 0))
    xx = hprev - h
    mu = mu_ref[...]
    xr, xw, xk, xv, xa, xg = (_bf(h + xx * mu[j:j + 1]) for j in range(6))
    r = _dot(xr, wr_ref[...])
    k = _dot(xk, wk_ref[...])
    v = _dot(xv, wv_ref[...])
    wl = w0 + _dot(_bf(jnp.tanh(_dot(xw, w1_ref[...]))), w2_ref[...])
    a = _sigmoid(a0 + _dot(_bf(_dot(xa, a1_ref[...])), a2_ref[...]))
    g = _dot(_bf(_sigmoid(_dot(xg, g1_ref[...]))), g2_ref[...])
    kkr = k * k_k
    ss = _headsum(kkr * kkr, e_ref[...], et_ref[...])
    kk = kkr / jnp.maximum(jnp.sqrt(ss), 1e-12)
    lw = _sigmoid(wl) * (-math.exp(-0.5))
    tm = lw.shape[0]
    rr = lax.broadcasted_iota(jnp.int32, (tm, tm), 0)
    cc = lax.broadcasted_iota(jnp.int32, (tm, tm), 1)
    same_chunk = jnp.bitwise_xor(rr, cc) < chunk
    in_chunk = jnp.where(rr >= cc, jnp.where(same_chunk, 1.0, 0.0), 0.0).astype(BF16)
    hi, lo = _split2(lw)
    cum_out[0] = _dot(in_chunk, hi) + _dot(in_chunk, lo)
    r_out[0] = _bf(r)
    k_out[0] = _bf(k * (1.0 + (a - 1.0) * k_a))
    v_out[0] = _bf(v)
    kk_out[0] = _bf(kk)
    b_out[0] = _bf(kk * a)
    g_out[0] = _bf(g)


def _rwkv_pre(x, vec, mu, wr, wk, wv, w1, w2, a1, a2, g1, g2, e, et, tm, chunk):
    b, s, d = x.shape
    assert tm % chunk == 0
    tok = pl.BlockSpec((1, tm, d), lambda bi, i: (bi, i, 0))
    prev = pl.BlockSpec((1, 8, d), lambda bi, i: (bi, jnp.maximum(i * (tm // 8) - 1, 0), 0))
    consts = [vec, mu, wr, wk, wv, w1, w2, a1, a2, g1, g2, e, et]
    out = lambda dt: jax.ShapeDtypeStruct((b, s, d), dt)
    return pl.pallas_call(
        functools.partial(_rwkv_pre_kernel, chunk=chunk),
        grid=(b, s // tm),
        in_specs=[tok, prev] + [_const_spec(c.shape) for c in consts],
        out_specs=[tok] * 7,
        out_shape=[out(BF16), out(F32)] + [out(BF16)] * 5,
        compiler_params=_params("parallel", "arbitrary"),
        name="rwkv_pre",
    )(x, x, *consts)


def _round_robin(generators):
    results = [None] * len(generators)
    live = list(enumerate(generators))
    while live:
        still = []
        for idx, gen in live:
            try:
                results[idx] = next(gen)
                still.append((idx, gen))
            except StopIteration:
                pass
        live = still
    return results


def _wkv_kernel(r_ref, cum_ref, k_ref, v_ref, kk_ref, b_ref, vec_ref, y_ref, st_ref, raw_ref, *, chunk):
    c_len = chunk
    n_chunks = r_ref.shape[1] // c_len
    n_pairs = r_ref.shape[2] // LANES
    inv_steps = max(c_len.bit_length() - 2, 0)
    lane = lax.broadcasted_iota(jnp.int32, (c_len, LANES), 1)
    head0 = lane < HEAD
    rr = lax.broadcasted_iota(jnp.int32, (c_len, c_len), 0)
    cc = lax.broadcasted_iota(jnp.int32, (c_len, c_len), 1)
    lower = rr >= cc
    strict = rr > cc
    eye = jnp.where(rr == cc, 1.0, 0.0).astype(F32)
    r2 = lax.broadcasted_iota(jnp.int32, (LANES, LANES), 0)
    c2 = lax.broadcasted_iota(jnp.int32, (LANES, LANES), 1)
    same_head = (r2 < HEAD) == (c2 < HEAD)

    @pl.when(pl.program_id(2) == 0)
    def _():
        st_ref[...] = jnp.zeros_like(st_ref)
        raw_ref[...] = jnp.zeros_like(raw_ref)

    first_row = lax.broadcasted_iota(jnp.int32, (c_len, LANES), 0) == 0
    rows = lambda mat, j: mat[j * c_len:(j + 1) * c_len]
    own_head = lambda mat: jnp.where(head0, rows(mat, 0), rows(mat, 1))

    def head_sum(x):
        s0 = jnp.sum(jnp.where(head0, x, 0.0), axis=-1, keepdims=True)
        s1 = jnp.sum(jnp.where(head0, 0.0, x), axis=-1, keepdims=True)
        return jnp.where(head0, s0, s1)

    def chunk_pair(sl, lanes, state):
        cum = cum_ref[0, sl, lanes]
        cum_prev = jnp.where(first_row, 0.0, pltpu.roll(cum, 1, 0))
        cum_last = cum[c_len - 1:c_len, :]
        kk = kk_ref[0, sl, lanes].astype(F32)
        b = b_ref[0, sl, lanes].astype(F32)
        k = k_ref[0, sl, lanes].astype(F32)
        vb = v_ref[0, sl, lanes]
        e_neg = jnp.exp(-cum)
        e_end = jnp.exp(cum_last - cum)
        a_t = -kk * jnp.exp(cum_prev)
        r = r_ref[0, sl, lanes].astype(F32)
        r_t = r * jnp.exp(cum)
        lhs = _bf(jnp.concatenate([jnp.where(head0, a_t, 0.0), jnp.where(head0, 0.0, a_t),
                                   jnp.where(head0, r_t, 0.0), jnp.where(head0, 0.0, r_t)], axis=0))
        gram = _dot_nt(lhs, _bf(jnp.concatenate([b * e_neg, k * e_neg], axis=0)))
        from_state = _dot_nt(_bf(jnp.concatenate([a_t, r_t], axis=0)), _bf(state))
        yield
        gram_b, gram_k = gram[:, :c_len], gram[:, c_len:]
        from_v = _dot(_bf(jnp.concatenate(
            [jnp.where(strict, rows(gram_k, 0), 0.0), jnp.where(strict, rows(gram_k, 1), 0.0),
             jnp.where(lower, rows(gram_k, 2), 0.0), jnp.where(lower, rows(gram_k, 3), 0.0)], axis=0)), vb)
        m_rb = _bf(jnp.concatenate([jnp.where(lower, rows(gram_b, 2), 0.0),
                                    jnp.where(lower, rows(gram_b, 3), 0.0)], axis=0))
        w = rows(from_state, 0) + own_head(from_v[:2 * c_len])
        y1 = rows(from_state, 1) + own_head(from_v[2 * c_len:])
        pw = [jnp.where(strict, rows(gram_b, hd), 0.0) for hd in range(2)]
        inv = [eye + pw[hd] for hd in range(2)]
        if inv_steps:
            pw = [_dot(_bf(pw[hd]), _bf(pw[hd])) for hd in range(2)]
        for step in range(inv_steps):
            yield
            if step < inv_steps - 1:
                both = [_dot(_bf(jnp.concatenate([inv[hd], pw[hd]], axis=0)), _bf(pw[hd])) for hd in range(2)]
                inv = [inv[hd] + rows(both[hd], 0) for hd in range(2)]
                pw = [rows(both[hd], 1) for hd in range(2)]
            else:
                inv = [inv[hd] + _dot(_bf(inv[hd]), _bf(pw[hd])) for hd in range(2)]
        yield
        u = own_head(_dot(_bf(jnp.concatenate(inv, axis=0)), _bf(w)))
        yield
        y = y1 + own_head(_dot(m_rb, _bf(u)))
        v = vb.astype(F32)
        uv_t = _bf(jnp.transpose(jnp.concatenate([u, v], axis=0)))
        bk = _bf(jnp.concatenate([b * e_end, k * e_end], axis=0))
        yield y, state * jnp.exp(cum_last) + jnp.where(same_head, _dot(uv_t, bk), 0.0)

    def finish(sl, lanes, y):
        lnx_w, lnx_b, r_k = (vec_ref[j:j + 1, lanes] for j in range(3))
        r, k, v = (ref[0, sl, lanes].astype(F32) for ref in (r_ref, k_ref, v_ref))
        dy = y - head_sum(y) * (1.0 / HEAD)
        yn = dy * lax.rsqrt(head_sum(dy * dy) * (1.0 / HEAD) + GN_EPS) * lnx_w + lnx_b
        y_ref[0, sl, lanes] = _bf(yn + head_sum(r * k * r_k) * v)

    pair_lanes = [slice(p * LANES, (p + 1) * LANES) for p in range(n_pairs)]
    chunk_rows = lambda c: pl.ds(pl.multiple_of(c * c_len, c_len), c_len)

    def body(c, carry):
        before = chunk_rows(jnp.maximum(c - 1, 0))
        for p in range(n_pairs):
            finish(before, pair_lanes[p], raw_ref[p])
        sl = chunk_rows(c)
        results = _round_robin([chunk_pair(sl, pair_lanes[p], st_ref[p]) for p in range(n_pairs)])
        for p, (y, new_state) in enumerate(results):
            raw_ref[p] = y
            st_ref[p] = new_state
        return carry

    lax.fori_loop(0, n_chunks, body, 0)
    for p in range(n_pairs):
        finish(chunk_rows(n_chunks - 1), pair_lanes[p], raw_ref[p])


def _wkv(r, cum, k, v, kk, bb, vec, chunk, seq_tile, n_pairs):
    b, s, d = r.shape
    width = n_pairs * LANES
    spec = pl.BlockSpec((1, seq_tile, width), lambda bi, hi, si: (bi, si, hi))
    return pl.pallas_call(
        functools.partial(_wkv_kernel, chunk=chunk),
        grid=(b, d // width, s // seq_tile),
        in_specs=[spec] * 6 + [pl.BlockSpec((vec.shape[0], width), lambda bi, hi, si: (0, hi))],
        out_specs=spec,
        out_shape=jax.ShapeDtypeStruct((b, s, d), BF16),
        scratch_shapes=[pltpu.VMEM((n_pairs, LANES, LANES), F32),
                        pltpu.VMEM((n_pairs, chunk, LANES), F32)],
        compiler_params=_params("parallel", "parallel", "arbitrary"),
        name="wkv7",
    )(r, cum, k, v, kk, bb, vec)


def _rwkv_post_kernel(z_ref, g_ref, x_ref, wo_ref, o_ref):
    o_ref[...] = x_ref[...] + _dot(z_ref[...] * g_ref[...], wo_ref[...])


def _rwkv_post(z, g, x, wo, tm):
    t, d = x.shape
    tok = pl.BlockSpec((tm, d), lambda i: (i, 0))
    return pl.pallas_call(
        _rwkv_post_kernel,
        grid=(t // tm,),
        in_specs=[tok] * 3 + [_const_spec(wo.shape)],
        out_specs=tok,
        out_shape=jax.ShapeDtypeStruct((t, d), F32),
        compiler_params=_params("parallel"),
        name="rwkv_post",
    )(z, g, x, wo)


FFN_SUB_ROWS = 512


def _ffn_kernel(x_ref, gain_ref, w1_ref, w3_ref, w2_ref, o_ref, h_ref, acc_ref):
    f = pl.program_id(1)

    @pl.when(f == 0)
    def _():
        h_ref[...] = _bf(_rms(x_ref[...], gain_ref[...]))
        acc_ref[...] = jnp.zeros_like(acc_ref)

    for start in range(0, h_ref.shape[0], FFN_SUB_ROWS):
        rows = slice(start, min(start + FFN_SUB_ROWS, h_ref.shape[0]))
        h = h_ref[rows, :]
        u = _dot(h, w1_ref[...])
        act = _bf(u * _sigmoid(u) * _dot(h, w3_ref[...]))
        acc_ref[rows, :] += _dot(act, w2_ref[...])

    @pl.when(f == pl.num_programs(1) - 1)
    def _():
        o_ref[...] = x_ref[...] + acc_ref[...]


def _ffn(x, gain, w1, w3, w2, tm, tf):
    t, d = x.shape
    ff = w1.shape[1]
    tok = pl.BlockSpec((tm, d), lambda i, f: (i, 0))
    return pl.pallas_call(
        _ffn_kernel,
        grid=(t // tm, ff // tf),
        in_specs=[tok, _const_spec(gain.shape),
                  pl.BlockSpec((d, tf), lambda i, f: (0, f)),
                  pl.BlockSpec((d, tf), lambda i, f: (0, f)),
                  pl.BlockSpec((tf, d), lambda i, f: (f, 0))],
        out_specs=tok,
        out_shape=jax.ShapeDtypeStruct((t, d), F32),
        scratch_shapes=[pltpu.VMEM((tm, d), BF16), pltpu.VMEM((tm, d), F32)],
        compiler_params=_params("parallel", "arbitrary"),
        name="ffn_dense",
    )(x, gain, w1, w3, w2)


BRANCH_WIDTH = HEADS_PER_BRANCH * HEAD


def _attn_proj_kernel(x_ref, gains_ref, wq_ref, wkv_ref, *refs):
    outs, (q_scr, kv_scr) = refs[:-2], refs[-2:]
    x = x_ref[0]
    tm = x.shape[0]
    n = x * lax.rsqrt(jnp.mean(x * x, axis=-1, keepdims=True) + RMS_EPS)
    gains = gains_ref[...]
    q_tiles = q_scr.shape[0]
    qw = q_tiles * LANES
    tiles_br = BRANCH_WIDTH // LANES
    hq = _bf(n * gains[0:1])
    hkv = _bf(n * gains[1:2])

    def project(h, w, scr, first_tile, scale=None):
        val = _dot(h, w)
        for ti in range(q_tiles):
            tile = val[:, ti * LANES:(ti + 1) * LANES]
            scr[first_tile + ti] = tile if scale is None else tile * scale

    def permute(which, scr, first_tile):
        for br, (_, dil) in enumerate(BRANCHES):
            out = outs[3 * br + which]
            for res in range(dil):
                rows = pl.ds(res, tm // dil, stride=dil) if dil > 1 else slice(None)
                for ti in range(tiles_br):
                    out[0, res, :, ti * LANES:(ti + 1) * LANES] = _bf(
                        scr[first_tile + br * tiles_br + ti, rows, :])

    project(hq, wq_ref[...], q_scr, 0, 1.0 / HEAD ** 0.5)
    project(hkv, wkv_ref[:, :qw], kv_scr, 0)
    permute(0, q_scr, 0)
    project(hkv, wkv_ref[:, qw:], kv_scr, q_tiles)
    permute(1, kv_scr, 0)
    permute(2, kv_scr, q_tiles)


def _attn_proj(x, gains, wq, wkv, tm):
    b, s, d = x.shape
    qw = wq.shape[1]
    out_specs, out_shape = [], []
    for _, dil in BRANCHES:
        assert tm % (16 * dil) == 0
        out_specs += [pl.BlockSpec((1, dil, tm // dil, BRANCH_WIDTH), lambda bi, i: (bi, 0, i, 0))] * 3
        out_shape += [jax.ShapeDtypeStruct((b, dil, s // dil, BRANCH_WIDTH), BF16)] * 3
    return pl.pallas_call(
        _attn_proj_kernel,
        grid=(b, s // tm),
        in_specs=[pl.BlockSpec((1, tm, d), lambda bi, i: (bi, i, 0)),
                  _const_spec(gains.shape), _const_spec(wq.shape), _const_spec(wkv.shape)],
        out_specs=out_specs,
        out_shape=out_shape,
        scratch_shapes=[pltpu.VMEM((qw // LANES, tm, LANES), F32),
                        pltpu.VMEM((2 * qw // LANES, tm, LANES), F32)],
        compiler_params=_params("parallel", "parallel"),
        name="attn_proj",
    )(x, gains, wq, wkv)


def _attn_kernel(slopes_ref, q_ref, k_ref, v_ref, o_ref, lse_ref, *, branch, dilation, qb):
    sub_len = q_ref.shape[2]
    n_blk = sub_len // qb
    n_pairs = q_ref.shape[3] // LANES
    col2 = lax.broadcasted_iota(jnp.int32, (qb, 2 * qb), 1)
    rel = (lax.broadcasted_iota(jnp.int32, (qb, 2 * qb), 0) - jnp.where(col2 < qb, col2, col2 - qb)).astype(F32)
    dist_d = jnp.where(rel >= 0, rel, -NEG)
    dist_p = jnp.where(rel <= 0, rel + float(qb), -NEG)
    first_head = lax.broadcasted_iota(jnp.int32, (1, 2 * qb), 1) < qb
    lane = lax.broadcasted_iota(jnp.int32, (qb, LANES), 1)
    head0 = lane < HEAD
    lane_row = lax.broadcasted_iota(jnp.int32, (1, LANES), 1)
    head_mask = [jnp.where(lane_row < HEAD, 1.0, 0.0).astype(BF16),
                 jnp.where(lane_row < HEAD, 0.0, 1.0).astype(BF16)]
    head_ones = [jnp.where(head0, 1.0, 0.0).astype(BF16), jnp.where(head0, 0.0, 1.0).astype(BF16)]

    def block(residues, i):
        cur = pl.ds(pl.multiple_of(i * qb, qb), qb)
        if n_blk > 1:
            prv = pl.ds(pl.multiple_of(jnp.maximum(i - 1, 0) * qb, qb), qb)
            has_prev = jnp.where(i > 0, 1.0, -NEG)

        def one_pair(r, p):
            lanes = slice(p * LANES, (p + 1) * LANES)
            slope = [slopes_ref[branch * HEADS_PER_BRANCH + 2 * p + j] * float(dilation) for j in range(2)]
            step = jnp.where(first_head, slope[0], slope[1])

            def keys(rows):
                k = k_ref[0, r, rows, lanes]
                return jnp.concatenate([k * head_mask[0], k * head_mask[1]], axis=0)

            def values(rows):
                v = v_ref[0, r, rows, lanes]
                return jnp.concatenate(
                    [jnp.concatenate([v * head_mask[0], head_ones[0]], axis=1),
                     jnp.concatenate([v * head_mask[1], head_ones[1]], axis=1)], axis=0)

            q = q_ref[0, r, cur, lanes]
            sd = _dot_nt(q, keys(cur)) - step * dist_d
            if n_blk > 1:
                sp = _dot_nt(q, keys(prv)) - (step * has_prev) * dist_p
            yield
            both = jnp.maximum(sd, sp) if n_blk > 1 else sd
            m = [jnp.max(both[:, :qb], axis=-1, keepdims=True), jnp.max(both[:, qb:], axis=-1, keepdims=True)]
            probs = lambda s: _bf(jnp.concatenate([jnp.exp(s[:, :qb] - m[0]), jnp.exp(s[:, qb:] - m[1])], axis=1))
            acc = _dot(probs(sd), values(cur))
            if n_blk > 1:
                acc = acc + _dot(probs(sp), values(prv))
            yield
            den = acc[:, LANES:]
            lse = jnp.where(head0, m[0], m[1]) + jnp.log(den)
            yield acc[:, :LANES] / den, lse[:, 0:1], lse[:, HEAD:HEAD + 1]

        res = _round_robin([one_pair(r, p) for r in residues for p in range(n_pairs)])
        for n, r in enumerate(residues):
            lse = jnp.zeros((qb, LANES), F32)
            for p in range(n_pairs):
                out, l_a, l_b = res[n * n_pairs + p]
                o_ref[0, r, cur, p * LANES:(p + 1) * LANES] = _bf(out)
                lse = jnp.where(lane == 2 * p, l_a, jnp.where(lane == 2 * p + 1, l_b, lse))
            lse_ref[0, r, cur, :] = lse

    n_res = q_ref.shape[1]
    if n_blk == 1:
        block(range(n_res), 0)
    else:
        for r in range(n_res):
            lax.fori_loop(0, n_blk, lambda i, c, r=r: (block([r], i), c)[1], 0,
                          unroll=2 if n_blk % 2 == 0 else 1)


def _attn_branch(slopes, q, k, v, branch):
    window, dilation = BRANCHES[branch]
    assert window % dilation == 0 and window // dilation == LANES
    b, _, sub_len, width = q.shape
    qb = min(LANES, sub_len)
    assert sub_len % qb == 0
    n_res = max(1, min(dilation, 512 // sub_len))
    assert dilation % n_res == 0
    spec = pl.BlockSpec((1, n_res, sub_len, width), lambda bi, ri: (bi, ri, 0, 0))
    return pl.pallas_call(
        functools.partial(_attn_kernel, branch=branch, dilation=dilation, qb=qb),
        grid=(b, dilation // n_res),
        in_specs=[pl.BlockSpec(memory_space=pltpu.SMEM)] + [spec] * 3,
        out_specs=[spec, pl.BlockSpec((1, n_res, sub_len, LANES), lambda bi, ri: (bi, ri, 0, 0))],
        out_shape=[jax.ShapeDtypeStruct(q.shape, BF16),
                   jax.ShapeDtypeStruct(q.shape[:3] + (LANES,), F32)],
        compiler_params=_params("parallel", "parallel"),
        name=f"dilated_attn_{branch}",
    )(slopes, q, k, v)


def _attn_out_kernel(o0, o1, o2, l0, l1, l2, x_ref, gain_ref, wo_ref, router_ref, rb_ref, expand_ref,
                     x_out, h_out, route_out, cnt_out, cnt_ref, o_scr, l_scr):
    @pl.when((pl.program_id(0) == 0) & (pl.program_id(1) == 0))
    def _():
        cnt_ref[...] = jnp.zeros_like(cnt_ref)

    tm = x_ref.shape[0]
    tiles_br = BRANCH_WIDTH // LANES
    for br, (o_ref, l_ref) in enumerate(((o0, l0), (o1, l1), (o2, l2))):
        dil = BRANCHES[br][1]
        for res in range(dil):
            rows = pl.ds(res, tm // dil, stride=dil) if dil > 1 else slice(None)
            l_scr[br, rows, :] = l_ref[0, res]
            for ti in range(tiles_br):
                lanes = slice(ti * LANES, (ti + 1) * LANES)
                o_scr[br, ti, rows, :] = o_ref[0, res, :, lanes].astype(F32)
    ls = [l_scr[br] for br in range(3)]
    m = jnp.maximum(jnp.maximum(ls[0], ls[1]), ls[2])
    ws = [jnp.exp(l - m) for l in ls]
    total = ws[0] + ws[1] + ws[2]
    spread = [_dot(_bf(ws[br] / total), expand_ref[...]) for br in range(3)]
    merged = []
    for ti in range(tiles_br):
        lanes = slice(ti * LANES, (ti + 1) * LANES)
        merged.append(_bf(sum(spread[br][:, lanes] * o_scr[br, ti] for br in range(3))))
    x = x_ref[...] + _dot(jnp.concatenate(merged, axis=1), wo_ref[...])
    x_out[...] = x
    h = _rms(x, gain_ref[...])
    h_out[...] = h
    hhi, hlo = _split2(h)
    both = _dot(hhi, router_ref[...])
    logits = both[:, :LANES] + both[:, LANES:] + _dot(hlo, router_ref[:, :LANES]) + rb_ref[...]
    lane = lax.broadcasted_iota(jnp.int32, logits.shape, 1)
    m1 = jnp.max(logits, axis=-1, keepdims=True)
    i1 = jnp.min(jnp.where(logits == m1, lane, LANES), axis=-1, keepdims=True)
    rest = jnp.where(lane == i1, -3e38, logits)
    m2 = jnp.max(rest, axis=-1, keepdims=True)
    i2 = jnp.min(jnp.where(rest == m2, lane, LANES), axis=-1, keepdims=True)
    ex = jnp.exp(m2 - m1)
    sel = jnp.where((lane == i1) | (lane == i2), 1.0, 0.0)
    rr = lax.broadcasted_iota(jnp.int32, (tm, tm), 0)
    cc = lax.broadcasted_iota(jnp.int32, (tm, tm), 1)
    before = jnp.where(rr > cc, 1.0, 0.0).astype(BF16)
    rank = cnt_ref[...] + _dot(before, _bf(sel))
    cnt_ref[...] += jnp.sum(sel, axis=0, keepdims=True)
    cnt_out[...] = cnt_ref[...]
    columns = [i1.astype(F32), i2.astype(F32),
               jnp.sum(jnp.where(lane == i1, rank, 0.0), axis=-1, keepdims=True),
               jnp.sum(jnp.where(lane == i2, rank, 0.0), axis=-1, keepdims=True),
               1.0 / (1.0 + ex), ex / (1.0 + ex)]
    route = jnp.zeros(logits.shape, F32)
    for j, col in enumerate(columns):
        route = jnp.where(lane == j, col, route)
    route_out[...] = route


def _attn_out(os_, ls_, x, gain, wo, router, rb, b, tm):
    t, d = x.shape
    tiles = t // (b * tm)
    tok = pl.BlockSpec((tm, d), lambda bi, i: (bi * tiles + i, 0))
    small = pl.BlockSpec((tm, LANES), lambda bi, i: (bi * tiles + i, 0))
    att = [pl.BlockSpec((1, dil, tm // dil, BRANCH_WIDTH), lambda bi, i: (bi, 0, i, 0))
           for _, dil in BRANCHES]
    lse = [pl.BlockSpec((1, dil, tm // dil, LANES), lambda bi, i: (bi, 0, i, 0)) for _, dil in BRANCHES]
    expand = (jnp.arange(LANES)[:, None] == jnp.arange(BRANCH_WIDTH)[None, :] // HEAD).astype(BF16)
    consts = [gain, wo, router, rb, expand]
    return pl.pallas_call(
        _attn_out_kernel,
        grid=(b, tiles),
        in_specs=att + lse + [tok] + [_const_spec(c.shape) for c in consts],
        out_specs=[tok, tok, small, _const_spec((1, LANES))],
        out_shape=[jax.ShapeDtypeStruct((t, d), F32), jax.ShapeDtypeStruct((t, d), F32),
                   jax.ShapeDtypeStruct((t, LANES), F32), jax.ShapeDtypeStruct((1, LANES), F32)],
        scratch_shapes=[pltpu.VMEM((1, LANES), F32),
                        pltpu.VMEM((len(BRANCHES), BRANCH_WIDTH // LANES, tm, LANES), F32),
                        pltpu.VMEM((len(BRANCHES), tm, LANES), F32)],
        compiler_params=_params("arbitrary", "arbitrary"),
        name="attn_out_route",
    )(*os_, *ls_, x, *consts)


DMA_UNROLL = 8


def _row_copy(src, src_row, dst, dst_row, sem):
    return pltpu.make_async_copy(src.at[pl.ds(src_row, 1)], dst.at[pl.ds(dst_row, 1)], sem)


def _moe_scatter_kernel(meta_ref, dest_ref, h_ref, out_ref, zeros, sem, zero_sem):
    n_tok = h_ref.shape[0]
    tile = zeros.shape[0]
    ends = lambda e: meta_ref[N_EXPERTS + e]

    def zero_fill(e):
        if e < N_EXPERTS:
            start, used = jnp.maximum(ends(e) - tile, 0), ends(e) > meta_ref[e]
        else:
            start = ends(N_EXPERTS - 1) + (e - N_EXPERTS) * tile
            used = start < out_ref.shape[0]
            start = jnp.minimum(start, out_ref.shape[0] - tile)
        dst = out_ref.at[pl.ds(pl.multiple_of(start, tile), tile)]
        return used, pltpu.make_async_copy(zeros, dst, zero_sem)

    @pl.when(pl.program_id(0) == 0)
    def _():
        zeros[...] = jnp.zeros_like(zeros)
        for e in range(2 * N_EXPERTS):
            used, copy = zero_fill(e)
            pl.when(used)(copy.start)
        for e in range(2 * N_EXPERTS):
            used, copy = zero_fill(e)
            pl.when(used)(copy.wait)

    def issue(j, carry):
        for k in range(2):
            _row_copy(h_ref, j, out_ref, dest_ref[2 * j + k], sem).start(priority=k)
        return carry

    def drain(j, carry):
        for _ in range(2):
            _row_copy(h_ref, 0, out_ref, 0, sem).wait()
        return carry

    lax.fori_loop(0, n_tok, issue, 0, unroll=DMA_UNROLL)
    lax.fori_loop(0, n_tok, drain, 0, unroll=DMA_UNROLL)


def _moe_scatter(meta, dest_flat, h, rows, ts, tile):
    t, d = h.shape
    return pl.pallas_call(
        _moe_scatter_kernel,
        grid=(t // ts,),
        in_specs=[pl.BlockSpec(memory_space=pltpu.SMEM),
                  pl.BlockSpec((2 * ts,), lambda i: (i,), memory_space=pltpu.SMEM),
                  pl.BlockSpec((ts, d), lambda i: (i, 0))],
        out_specs=pl.BlockSpec(memory_space=pl.ANY),
        out_shape=jax.ShapeDtypeStruct((rows, d), F32),
        scratch_shapes=[pltpu.VMEM((tile, d), F32), pltpu.SemaphoreType.DMA(()),
                        pltpu.SemaphoreType.DMA(())],
        compiler_params=_params("arbitrary"),
        name="moe_scatter",
    )(meta, dest_flat, h)


def _moe_ffn_kernel(src_ref, te_ref, nv_ref, x_ref, w1_ref, w3_ref, w2_ref, o_ref, acc_ref):
    del src_ref, te_ref
    i = pl.program_id(0)
    f = pl.program_id(1)

    @pl.when(i < nv_ref[0])
    def _():
        @pl.when(f == 0)
        def _():
            acc_ref[...] = jnp.zeros_like(acc_ref)

        h = _bf(x_ref[...])
        u = _dot(h, w1_ref[0])
        act = _bf(u * _sigmoid(u) * _dot(h, w3_ref[0]))
        acc_ref[...] += _dot(act, w2_ref[0])

        @pl.when(f == pl.num_programs(1) - 1)
        def _():
            o_ref[...] = acc_ref[...]

    @pl.when(i >= nv_ref[0])
    def _():
        o_ref[...] = jnp.zeros_like(o_ref)


def _moe_ffn(tile_src, tile_expert, n_valid, xs, w1, w3, w2, tm, tf):
    rows, d = xs.shape
    ff = w1.shape[2]
    tok = pl.BlockSpec((tm, d), lambda i, f, src, te, nv: (src[i], 0))
    return pl.pallas_call(
        _moe_ffn_kernel,
        grid_spec=pltpu.PrefetchScalarGridSpec(
            num_scalar_prefetch=3,
            grid=(rows // tm, ff // tf),
            in_specs=[tok,
                      pl.BlockSpec((1, d, tf), lambda i, f, src, te, nv: (te[i], 0, f)),
                      pl.BlockSpec((1, d, tf), lambda i, f, src, te, nv: (te[i], 0, f)),
                      pl.BlockSpec((1, tf, d), lambda i, f, src, te, nv: (te[i], f, 0))],
            out_specs=pl.BlockSpec((tm, d), lambda i, f, src, te, nv: (i, 0)),
            scratch_shapes=[pltpu.VMEM((tm, d), F32)],
        ),
        out_shape=jax.ShapeDtypeStruct((rows, d), F32),
        compiler_params=_params("arbitrary", "arbitrary"),
        name="moe_experts",
    )(tile_src, tile_expert, n_valid, xs, w1, w3, w2)


def _moe_combine_kernel(dest_ref, x_ref, gates_ref, gain_ref, y_hbm, o_ref, buf, sem):
    n_tok = x_ref.shape[0]

    def issue(j, carry):
        for k in range(2):
            _row_copy(y_hbm, dest_ref[2 * j + k], buf.at[k], j, sem).start(priority=k)
        return carry

    def drain(j, carry):
        for k in range(2):
            _row_copy(y_hbm, 0, buf.at[k], 0, sem).wait()
        return carry

    lax.fori_loop(0, n_tok, issue, 0, unroll=DMA_UNROLL)
    lax.fori_loop(0, n_tok, drain, 0, unroll=DMA_UNROLL)
    gates = gates_ref[...]
    x = x_ref[...] + gates[:, 4:5] * buf[0] + gates[:, 5:6] * buf[1]
    o_ref[...] = _rms(x, gain_ref[...])


def _moe_combine(dest_flat, x, route, gain, ys, ts):
    t, d = x.shape
    tok = pl.BlockSpec((ts, d), lambda i: (i, 0))
    return pl.pallas_call(
        _moe_combine_kernel,
        grid=(t // ts,),
        in_specs=[pl.BlockSpec((2 * ts,), lambda i: (i,), memory_space=pltpu.SMEM),
                  tok, pl.BlockSpec((ts, LANES), lambda i: (i, 0)), _const_spec(gain.shape),
                  pl.BlockSpec(memory_space=pl.ANY)],
        out_specs=tok,
        out_shape=jax.ShapeDtypeStruct((t, d), F32),
        scratch_shapes=[pltpu.VMEM((2, ts, d), F32), pltpu.SemaphoreType.DMA(())],
        compiler_params=_params("arbitrary"),
        name="moe_combine",
    )(dest_flat, x, route, gain, ys)


def _tile(n, want):
    t = min(n, want)
    assert n % t == 0
    return t


def _pad_cols(w, n):
    return jnp.pad(w, ((0, 0), (0, n - w.shape[1])))


def _pad_rows(w, n):
    return jnp.pad(w, ((0, n - w.shape[0]), (0, 0)))


def kernel(x, norm_gain, rwkv_mu, rwkv_wr, rwkv_wk, rwkv_wv, rwkv_w0, rwkv_w1, rwkv_w2, rwkv_a0, rwkv_a1, rwkv_a2, rwkv_g1, rwkv_g2, rwkv_k_k, rwkv_k_a, rwkv_r_k, rwkv_lnx_w, rwkv_lnx_b, rwkv_wo, kv_norm_gain, w_kv, attn_wq, attn_wo, ffn_w1, ffn_w3, ffn_w2, moe_router, moe_router_bias, moe_w1, moe_w3, moe_w2, final_norm_gain):
    b, s, d = x.shape
    t = b * s
    n_heads = d // HEAD
    assert norm_gain.shape[0] == 2 and d % LANES == 0 and s % 8 == 0

    chan_head = jnp.arange(d) // HEAD
    e = (chan_head[:, None] == jnp.arange(LANES)[None, :]).astype(BF16)
    et = e.T

    zeros = jnp.zeros((d,), F32)
    vec_pre = jnp.stack([norm_gain[0, 0], rwkv_w0[0], rwkv_a0[0], rwkv_k_k[0], rwkv_k_a[0],
                         zeros, zeros, zeros])
    tm_pre = _tile(s, 256)
    chunk = _tile(s, LANES)
    r, cum, k, v, kk, kka, g = _rwkv_pre(
        x, vec_pre, rwkv_mu[0], _bf(rwkv_wr[0]), _bf(rwkv_wk[0]), _bf(rwkv_wv[0]),
        _bf(_pad_cols(rwkv_w1[0], LANES)), _bf(_pad_rows(rwkv_w2[0], LANES)),
        _bf(_pad_cols(rwkv_a1[0], LANES)), _bf(_pad_rows(rwkv_a2[0], LANES)),
        _bf(rwkv_g1[0]), _bf(rwkv_g2[0]), e, et, tm_pre, chunk)
    vec_post = jnp.stack([rwkv_lnx_w[0], rwkv_lnx_b[0], rwkv_r_k[0].reshape(d),
                          zeros, zeros, zeros, zeros, zeros])
    z = _wkv(r, cum, k, v, kk, kka, vec_post, chunk=chunk, seq_tile=_tile(s, 2048), n_pairs=4)

    flat = lambda arr: arr.reshape(t, d)
    tm = _tile(t, 512)
    x1 = _rwkv_post(flat(z), flat(g), flat(x), _bf(rwkv_wo[0]), tm)

    ff = ffn_w1.shape[2]
    tf = ff // 2 if (ff // 2) % LANES == 0 else ff
    x2 = _ffn(x1, norm_gain[0, 1][None, :], _bf(ffn_w1[0]), _bf(ffn_w3[0]), _bf(ffn_w2[0]),
              _tile(t, 2 * FFN_SUB_ROWS), tf)

    n_slopes = len(BRANCHES) * HEADS_PER_BRANCH
    slopes = jnp.exp2(-ALIBI_MAX * (jnp.arange(n_slopes, dtype=F32) + 1.0) / n_slopes)
    gains = jnp.stack([norm_gain[1, 0], kv_norm_gain] + [zeros] * 6)
    tm_a = _tile(s, 512)
    qkv = _attn_proj(x2.reshape(b, s, d), gains, _bf(attn_wq[0]), _bf(w_kv), tm_a)
    os_, ls_ = [], []
    for br in range(len(BRANCHES)):
        o_br, l_br = _attn_branch(slopes, *qkv[3 * br:3 * br + 3], br)
        os_.append(o_br)
        ls_.append(l_br)

    router = jnp.concatenate(_split2(_pad_cols(moe_router[0], LANES)), axis=1)
    rbias = jnp.full((1, LANES), NEG, F32).at[0, :N_EXPERTS].set(moe_router_bias[0])
    x3, h4, route, cnt = _attn_out(os_, ls_, x2, norm_gain[1, 1][None, :], _bf(attn_wo[0]),
                                   router, rbias, b, tm_a)

    tm_e = 512 if t >= 4096 else 128
    ts = _tile(t, 512)
    counts = cnt[0, :N_EXPERTS].astype(jnp.int32)
    padded = ((counts + tm_e - 1) // tm_e) * tm_e
    ends = jnp.cumsum(padded)
    meta = jnp.concatenate([ends - padded, ends]).astype(jnp.int32)
    route_i = route[:, :4].astype(jnp.int32)
    group_start = jnp.sum(jnp.where(route_i[:, :2, None] == jnp.arange(N_EXPERTS), ends - padded, 0), axis=-1)
    dest = (group_start + route_i[:, 2:4]).astype(jnp.int32).reshape(2 * t)
    rows = 2 * t + N_EXPERTS * tm_e
    n_tiles = rows // tm_e
    n_valid = (ends[-1] // tm_e).astype(jnp.int32)
    tile_src = jnp.minimum(jnp.arange(n_tiles, dtype=jnp.int32), n_valid - 1)
    tile_expert = jnp.minimum(
        jnp.sum(tile_src[:, None] * tm_e >= ends[None, :], axis=1), N_EXPERTS - 1).astype(jnp.int32)

    xs = _moe_scatter(meta, dest, h4, rows, ts, tm_e)
    ffe = moe_w1.shape[3]
    tfe = ffe // 2 if (ffe // 2) % LANES == 0 else ffe
    ys = _moe_ffn(tile_src, tile_expert, n_valid.reshape(1), xs,
                  _bf(moe_w1[0]), _bf(moe_w3[0]), _bf(moe_w2[0]), tm_e, tfe)
    out = _moe_combine(dest, x3, route, final_norm_gain[None, :], ys, ts)
    return out.reshape(b, s, d)
```

```python
import functools
import math

import jax
import jax.numpy as jnp
from jax import lax
from jax.experimental import pallas as pl
from jax.experimental.pallas import tpu as pltpu

F32 = jnp.float32
BF16 = jnp.bfloat16

RMS_EPS = 1e-5
GN_EPS = 64e-5
HEAD = 64
LANES = 128
BRANCHES = ((128, 1), (512, 4), (2048, 16))
HEADS_PER_BRANCH = 8
ALIBI_MAX = 8.0
N_EXPERTS = 8
NEG = -1e30
VMEM_LIMIT_BYTES = 56 * 1024 * 1024


def _params(*sem):
    return pltpu.CompilerParams(dimension_semantics=sem, vmem_limit_bytes=VMEM_LIMIT_BYTES)


def _dot(a, b):
    return jnp.dot(a, b, preferred_element_type=F32)


def _dot_nt(a, b):
    return lax.dot_general(a, b, (((1,), (1,)), ((), ())), preferred_element_type=F32)


def _bf(x):
    return x.astype(BF16)


def _split2(x):
    hi = x.astype(BF16)
    lo = (x - hi.astype(F32)).astype(BF16)
    return hi, lo


def _sigmoid(z):
    return 1.0 / (1.0 + jnp.exp(-z))


def _rms(x, gain):
    return x * lax.rsqrt(jnp.mean(x * x, axis=-1, keepdims=True) + RMS_EPS) * gain


def _headsum(x, e, et):
    return _dot(_bf(_dot(_bf(x), e)), et)


def _const_spec(shape):
    nd = len(shape)
    return pl.BlockSpec(shape, lambda *_: (0,) * nd)


def _rwkv_pre_kernel(x_ref, xp_ref, vec_ref, mu_ref, wr_ref, wk_ref, wv_ref, w1_ref, w2_ref,
                     a1_ref, a2_ref, g1_ref, g2_ref,
                     r_out, lw_out, k_out, v_out, kk_out, a_out, g_out):
    i = pl.program_id(1)
    vec = vec_ref[...]
    gain, w0, a0, k_k, k_a = (vec[j:j + 1] for j in range(5))
    x = x_ref[0]
    h = _rms(x, gain)
    hp = _rms(xp_ref[0][7:8, :], gain)
    hp = jnp.where(i > 0, hp, 0.0)
    rows = lax.broadcasted_iota(jnp.int32, h.shape, 0)
    hprev = jnp.where(rows == 0, hp, pltpu.roll(h, 1, 0))
    xx = hprev - h
    mu = mu_ref[...]
    xr, xw, xk, xv, xa, xg = (_bf(h + xx * mu[j:j + 1]) for j in range(6))
    r = _dot(xr, wr_ref[...])
    k = _dot(xk, wk_ref[...])
    v = _dot(xv, wv_ref[...])
    wl = w0 + _dot(_bf(jnp.tanh(_dot(xw, w1_ref[...]))), w2_ref[...])
    a = _sigmoid(a0 + _dot(_bf(_dot(xa, a1_ref[...])), a2_ref[...]))
    g = _dot(_bf(_sigmoid(_dot(xg, g1_ref[...]))), g2_ref[...])
    lw_out[0] = _sigmoid(wl) * (-math.exp(-0.5))
    r_out[0] = _bf(r)
    k_out[0] = _bf(k * (1.0 + (a - 1.0) * k_a))
    v_out[0] = _bf(v)
    kk_out[0] = _bf(k * k_k)
    a_out[0] = _bf(a)
    g_out[0] = _bf(g)


def _rwkv_pre(x, vec, mu, wr, wk, wv, w1, w2, a1, a2, g1, g2, tm):
    b, s, d = x.shape
    tok = pl.BlockSpec((1, tm, d), lambda bi, i: (bi, i, 0))
    prev = pl.BlockSpec((1, 8, d), lambda bi, i: (bi, jnp.maximum(i * (tm // 8) - 1, 0), 0))
    consts = [vec, mu, wr, wk, wv, w1, w2, a1, a2, g1, g2]
    out = lambda dt: jax.ShapeDtypeStruct((b, s, d), dt)
    return pl.pallas_call(
        _rwkv_pre_kernel,
        grid=(b, s // tm),
        in_specs=[tok, prev] + [_const_spec(c.shape) for c in consts],
        out_specs=[tok] * 7,
        out_shape=[out(BF16), out(F32)] + [out(BF16)] * 5,
        compiler_params=_params("parallel", "arbitrary"),
        name="rwkv_pre",
    )(x, x, *consts)


def _round_robin(generators):
    results = [None] * len(generators)
    live = list(enumerate(generators))
    while live:
        still = []
        for idx, gen in live:
            try:
                results[idx] = next(gen)
                still.append((idx, gen))
            except StopIteration:
                pass
        live = still
    return results


def _wkv_kernel(r_ref, lw_ref, k_ref, v_ref, kk_ref, a_ref, vec_ref, y_ref, st_ref, raw_ref, *, chunk):
    c_len = chunk
    n_chunks = r_ref.shape[1] // c_len
    n_pairs = r_ref.shape[2] // LANES
    inv_steps = max(c_len.bit_length() - 2, 0)
    lane = lax.broadcasted_iota(jnp.int32, (c_len, LANES), 1)
    head0 = lane < HEAD
    rr = lax.broadcasted_iota(jnp.int32, (c_len, c_len), 0)
    cc = lax.broadcasted_iota(jnp.int32, (c_len, c_len), 1)
    lower = rr >= cc
    strict = rr > cc
    eye = jnp.where(rr == cc, 1.0, 0.0).astype(F32)
    r2 = lax.broadcasted_iota(jnp.int32, (LANES, LANES), 0)
    c2 = lax.broadcasted_iota(jnp.int32, (LANES, LANES), 1)
    same_head = (r2 < HEAD) == (c2 < HEAD)

    @pl.when(pl.program_id(2) == 0)
    def _():
        st_ref[...] = jnp.zeros_like(st_ref)
        raw_ref[...] = jnp.zeros_like(raw_ref)

    row_index = lax.broadcasted_iota(jnp.int32, (c_len, LANES), 0)
    scan_shifts = [1 << j for j in range(c_len.bit_length() - 1)]
    rows = lambda mat, j: mat[j * c_len:(j + 1) * c_len]
    own_head = lambda mat: jnp.where(head0, rows(mat, 0), rows(mat, 1))

    def head_sum(x):
        s0 = jnp.sum(jnp.where(head0, x, 0.0), axis=-1, keepdims=True)
        s1 = jnp.sum(jnp.where(head0, 0.0, x), axis=-1, keepdims=True)
        return jnp.where(head0, s0, s1)

    def chunk_pair(sl, lanes, state):
        cum = lw_ref[0, sl, lanes]
        for shift in scan_shifts:
            cum = cum + jnp.where(row_index >= shift, pltpu.roll(cum, shift, 0), 0.0)
        cum_prev = jnp.where(row_index == 0, 0.0, pltpu.roll(cum, 1, 0))
        cum_last = cum[c_len - 1:c_len, :]
        kk = kk_ref[0, sl, lanes].astype(F32)
        kk = kk / jnp.maximum(jnp.sqrt(head_sum(kk * kk)), 1e-12)
        b = kk * a_ref[0, sl, lanes].astype(F32)
        k = k_ref[0, sl, lanes].astype(F32)
        vb = v_ref[0, sl, lanes]
        e_neg = jnp.exp(-cum)
        e_end = jnp.exp(cum_last - cum)
        a_t = -kk * jnp.exp(cum_prev)
        r = r_ref[0, sl, lanes].astype(F32)
        r_t = r * jnp.exp(cum)
        lhs = _bf(jnp.concatenate([jnp.where(head0, a_t, 0.0), jnp.where(head0, 0.0, a_t),
                                   jnp.where(head0, r_t, 0.0), jnp.where(head0, 0.0, r_t)], axis=0))
        gram = _dot_nt(lhs, _bf(jnp.concatenate([b * e_neg, k * e_neg], axis=0)))
        from_state = _dot_nt(_bf(jnp.concatenate([a_t, r_t], axis=0)), _bf(state))
        yield
        gram_b, gram_k = gram[:, :c_len], gram[:, c_len:]
        from_v = _dot(_bf(jnp.concatenate(
            [jnp.where(strict, rows(gram_k, 0), 0.0), jnp.where(strict, rows(gram_k, 1), 0.0),
             jnp.where(lower, rows(gram_k, 2), 0.0), jnp.where(lower, rows(gram_k, 3), 0.0)], axis=0)), vb)
        m_rb = _bf(jnp.concatenate([jnp.where(lower, rows(gram_b, 2), 0.0),
                                    jnp.where(lower, rows(gram_b, 3), 0.0)], axis=0))
        w = rows(from_state, 0) + own_head(from_v[:2 * c_len])
        y1 = rows(from_state, 1) + own_head(from_v[2 * c_len:])
        pw = [jnp.where(strict, rows(gram_b, hd), 0.0) for hd in range(2)]
        inv = [eye + pw[hd] for hd in range(2)]
        if inv_steps:
            pw = [_dot(_bf(pw[hd]), _bf(pw[hd])) for hd in range(2)]
        for step in range(inv_steps):
            yield
            if step < inv_steps - 1:
                both = [_dot(_bf(jnp.concatenate([inv[hd], pw[hd]], axis=0)), _bf(pw[hd])) for hd in range(2)]
                inv = [inv[hd] + rows(both[hd], 0) for hd in range(2)]
                pw = [rows(both[hd], 1) for hd in range(2)]
            else:
                inv = [inv[hd] + _dot(_bf(inv[hd]), _bf(pw[hd])) for hd in range(2)]
        yield
        u = own_head(_dot(_bf(jnp.concatenate(inv, axis=0)), _bf(w)))
        yield
        y = y1 + own_head(_dot(m_rb, _bf(u)))
        v = vb.astype(F32)
        uv_t = _bf(jnp.transpose(jnp.concatenate([u, v], axis=0)))
        bk = _bf(jnp.concatenate([b * e_end, k * e_end], axis=0))
        yield y, state * jnp.exp(cum_last) + jnp.where(same_head, _dot(uv_t, bk), 0.0)

    def finish(sl, lanes, y):
        lnx_w, lnx_b, r_k = (vec_ref[j:j + 1, lanes] for j in range(3))
        r, k, v = (ref[0, sl, lanes].astype(F32) for ref in (r_ref, k_ref, v_ref))
        dy = y - head_sum(y) * (1.0 / HEAD)
        yn = dy * lax.rsqrt(head_sum(dy * dy) * (1.0 / HEAD) + GN_EPS) * lnx_w + lnx_b
        y_ref[0, sl, lanes] = _bf(yn + head_sum(r * k * r_k) * v)

    pair_lanes = [slice(p * LANES, (p + 1) * LANES) for p in range(n_pairs)]
    chunk_rows = lambda c: pl.ds(pl.multiple_of(c * c_len, c_len), c_len)

    def body(c, carry):
        before = chunk_rows(jnp.maximum(c - 1, 0))
        for p in range(n_pairs):
            finish(before, pair_lanes[p], raw_ref[p])
        sl = chunk_rows(c)
        results = _round_robin([chunk_pair(sl, pair_lanes[p], st_ref[p]) for p in range(n_pairs)])
        for p, (y, new_state) in enumerate(results):
            raw_ref[p] = y
            st_ref[p] = new_state
        return carry

    lax.fori_loop(0, n_chunks, body, 0, unroll=2 if n_chunks % 2 == 0 else 1)
    for p in range(n_pairs):
        finish(chunk_rows(n_chunks - 1), pair_lanes[p], raw_ref[p])


def _wkv(r, lw, k, v, kk, a, vec, chunk, seq_tile, n_pairs):
    b, s, d = r.shape
    width = n_pairs * LANES
    spec = pl.BlockSpec((1, seq_tile, width), lambda bi, hi, si: (bi, si, hi))
    return pl.pallas_call(
        functools.partial(_wkv_kernel, chunk=chunk),
        grid=(b, d // width, s // seq_tile),
        in_specs=[spec] * 6 + [pl.BlockSpec((vec.shape[0], width), lambda bi, hi, si: (0, hi))],
        out_specs=spec,
        out_shape=jax.ShapeDtypeStruct((b, s, d), BF16),
        scratch_shapes=[pltpu.VMEM((n_pairs, LANES, LANES), F32),
                        pltpu.VMEM((n_pairs, chunk, LANES), F32)],
        compiler_params=_params("parallel", "parallel", "arbitrary"),
        name="wkv7",
    )(r, lw, k, v, kk, a, vec)


def _rwkv_post_kernel(z_ref, g_ref, x_ref, wo_ref, o_ref):
    o_ref[...] = x_ref[...] + _dot(z_ref[...] * g_ref[...], wo_ref[...])


def _rwkv_post(z, g, x, wo, tm):
    t, d = x.shape
    tok = pl.BlockSpec((tm, d), lambda i: (i, 0))
    return pl.pallas_call(
        _rwkv_post_kernel,
        grid=(t // tm,),
        in_specs=[tok] * 3 + [_const_spec(wo.shape)],
        out_specs=tok,
        out_shape=jax.ShapeDtypeStruct((t, d), F32),
        compiler_params=_params("parallel"),
        name="rwkv_post",
    )(z, g, x, wo)


FFN_SUB_ROWS = 512


def _ffn_kernel(x_ref, gain_ref, w1_ref, w3_ref, w2_ref, o_ref, h_ref, acc_ref):
    f = pl.program_id(1)

    @pl.when(f == 0)
    def _():
        h_ref[...] = _bf(_rms(x_ref[...], gain_ref[...]))
        acc_ref[...] = jnp.zeros_like(acc_ref)

    for start in range(0, h_ref.shape[0], FFN_SUB_ROWS):
        rows = slice(start, min(start + FFN_SUB_ROWS, h_ref.shape[0]))
        h = h_ref[rows, :]
        u = _dot(h, w1_ref[...])
        act = _bf(u * _sigmoid(u) * _dot(h, w3_ref[...]))
        acc_ref[rows, :] += _dot(act, w2_ref[...])

    @pl.when(f == pl.num_programs(1) - 1)
    def _():
        o_ref[...] = x_ref[...] + acc_ref[...]


def _ffn(x, gain, w1, w3, w2, tm, tf):
    t, d = x.shape
    ff = w1.shape[1]
    tok = pl.BlockSpec((tm, d), lambda i, f: (i, 0))
    return pl.pallas_call(
        _ffn_kernel,
        grid=(t // tm, ff // tf),
        in_specs=[tok, _const_spec(gain.shape),
                  pl.BlockSpec((d, tf), lambda i, f: (0, f)),
                  pl.BlockSpec((d, tf), lambda i, f: (0, f)),
                  pl.BlockSpec((tf, d), lambda i, f: (f, 0))],
        out_specs=tok,
        out_shape=jax.ShapeDtypeStruct((t, d), F32),
        scratch_shapes=[pltpu.VMEM((tm, d), BF16), pltpu.VMEM((tm, d), F32)],
        compiler_params=_params("parallel", "arbitrary"),
        name="ffn_dense",
    )(x, gain, w1, w3, w2)


BRANCH_WIDTH = HEADS_PER_BRANCH * HEAD


def _attn_proj_kernel(x_ref, gains_ref, wq_ref, wkv_ref, *refs):
    outs, (q_scr, kv_scr) = refs[:-2], refs[-2:]
    x = x_ref[0]
    tm = x.shape[0]
    n = x * lax.rsqrt(jnp.mean(x * x, axis=-1, keepdims=True) + RMS_EPS)
    gains = gains_ref[...]
    q_tiles = q_scr.shape[0]
    qw = q_tiles * LANES
    tiles_br = BRANCH_WIDTH // LANES
    hq = _bf(n * gains[0:1])
    hkv = _bf(n * gains[1:2])

    def project(h, w, scr, first_tile, scale=None):
        val = _dot(h, w)
        for ti in range(q_tiles):
            tile = val[:, ti * LANES:(ti + 1) * LANES]
            scr[first_tile + ti] = tile if scale is None else tile * scale

    def permute(which, scr, first_tile):
        for br, (_, dil) in enumerate(BRANCHES):
            out = outs[3 * br + which]
            for res in range(dil):
                rows = pl.ds(res, tm // dil, stride=dil) if dil > 1 else slice(None)
                for ti in range(tiles_br):
                    out[0, res, :, ti * LANES:(ti + 1) * LANES] = _bf(
                        scr[first_tile + br * tiles_br + ti, rows, :])

    project(hq, wq_ref[...], q_scr, 0, 1.0 / HEAD ** 0.5)
    project(hkv, wkv_ref[:, :qw], kv_scr, 0)
    permute(0, q_scr, 0)
    project(hkv, wkv_ref[:, qw:], kv_scr, q_tiles)
    permute(1, kv_scr, 0)
    permute(2, kv_scr, q_tiles)


def _attn_proj(x, gains, wq, wkv, tm):
    b, s, d = x.shape
    qw = wq.shape[1]
    out_specs, out_shape = [], []
    for _, dil in BRANCHES:
        assert tm % (16 * dil) == 0
        out_specs += [pl.BlockSpec((1, dil, tm // dil, BRANCH_WIDTH), lambda bi, i: (bi, 0, i, 0))] * 3
        out_shape += [jax.ShapeDtypeStruct((b, dil, s // dil, BRANCH_WIDTH), BF16)] * 3
    return pl.pallas_call(
        _attn_proj_kernel,
        grid=(b, s // tm),
        in_specs=[pl.BlockSpec((1, tm, d), lambda bi, i: (bi, i, 0)),
                  _const_spec(gains.shape), _const_spec(wq.shape), _const_spec(wkv.shape)],
        out_specs=out_specs,
        out_shape=out_shape,
        scratch_shapes=[pltpu.VMEM((qw // LANES, tm, LANES), F32),
                        pltpu.VMEM((2 * qw // LANES, tm, LANES), F32)],
        compiler_params=_params("parallel", "parallel"),
        name="attn_proj",
    )(x, gains, wq, wkv)


def _attn_kernel(slopes_ref, q_ref, k_ref, v_ref, o_ref, lse_ref, *, branch, dilation, qb):
    sub_len = q_ref.shape[2]
    n_blk = sub_len // qb
    n_pairs = q_ref.shape[3] // LANES
    col2 = lax.broadcasted_iota(jnp.int32, (qb, 2 * qb), 1)
    rel = (lax.broadcasted_iota(jnp.int32, (qb, 2 * qb), 0) - jnp.where(col2 < qb, col2, col2 - qb)).astype(F32)
    dist_d = jnp.where(rel >= 0, rel, -NEG)
    dist_p = jnp.where(rel <= 0, rel + float(qb), -NEG)
    first_head = lax.broadcasted_iota(jnp.int32, (1, 2 * qb), 1) < qb
    lane = lax.broadcasted_iota(jnp.int32, (qb, LANES), 1)
    head0 = lane < HEAD
    lane_row = lax.broadcasted_iota(jnp.int32, (1, LANES), 1)
    head_mask = [jnp.where(lane_row < HEAD, 1.0, 0.0).astype(BF16),
                 jnp.where(lane_row < HEAD, 0.0, 1.0).astype(BF16)]
    head_ones = [jnp.where(head0, 1.0, 0.0).astype(BF16), jnp.where(head0, 0.0, 1.0).astype(BF16)]

    def block(residues, i):
        cur = pl.ds(pl.multiple_of(i * qb, qb), qb)
        if n_blk > 1:
            prv = pl.ds(pl.multiple_of(jnp.maximum(i - 1, 0) * qb, qb), qb)
            has_prev = jnp.where(i > 0, 1.0, -NEG)

        def one_pair(r, p):
            lanes = slice(p * LANES, (p + 1) * LANES)
            slope = [slopes_ref[branch * HEADS_PER_BRANCH + 2 * p + j] * float(dilation) for j in range(2)]
            step = jnp.where(first_head, slope[0], slope[1])

            def keys(rows):
                k = k_ref[0, r, rows, lanes]
                return jnp.concatenate([k * head_mask[0], k * head_mask[1]], axis=0)

            def values(rows):
                v = v_ref[0, r, rows, lanes]
                return jnp.concatenate(
                    [jnp.concatenate([v * head_mask[0], head_ones[0]], axis=1),
                     jnp.concatenate([v * head_mask[1], head_ones[1]], axis=1)], axis=0)

            q = q_ref[0, r, cur, lanes]
            sd = _dot_nt(q, keys(cur)) - step * dist_d
            if n_blk > 1:
                sp = _dot_nt(q, keys(prv)) - (step * has_prev) * dist_p
            yield
            both = jnp.maximum(sd, sp) if n_blk > 1 else sd
            m = [jnp.max(both[:, :qb], axis=-1, keepdims=True), jnp.max(both[:, qb:], axis=-1, keepdims=True)]
            probs = lambda s: _bf(jnp.concatenate([jnp.exp(s[:, :qb] - m[0]), jnp.exp(s[:, qb:] - m[1])], axis=1))
            acc = _dot(probs(sd), values(cur))
            if n_blk > 1:
                acc = acc + _dot(probs(sp), values(prv))
            yield
            den = acc[:, LANES:]
            lse = jnp.where(head0, m[0], m[1]) + jnp.log(den)
            yield acc[:, :LANES] / den, lse[:, 0:1], lse[:, HEAD:HEAD + 1]

        res = _round_robin([one_pair(r, p) for r in residues for p in range(n_pairs)])
        for n, r in enumerate(residues):
            lse = jnp.zeros((qb, LANES), F32)
            for p in range(n_pairs):
                out, l_a, l_b = res[n * n_pairs + p]
                o_ref[0, r, cur, p * LANES:(p + 1) * LANES] = _bf(out)
                lse = jnp.where(lane == 2 * p, l_a, jnp.where(lane == 2 * p + 1, l_b, lse))
            lse_ref[0, r, cur, :] = lse

    n_res = q_ref.shape[1]
    if n_blk == 1:
        block(range(n_res), 0)
    else:
        for r in range(n_res):
            lax.fori_loop(0, n_blk, lambda i, c, r=r: (block([r], i), c)[1], 0,
                          unroll=2 if n_blk % 2 == 0 else 1)


def _attn_branch(slopes, q, k, v, branch):
    window, dilation = BRANCHES[branch]
    assert window % dilation == 0 and window // dilation == LANES
    b, _, sub_len, width = q.shape
    qb = min(LANES, sub_len)
    assert sub_len % qb == 0
    n_res = max(1, min(dilation, 512 // sub_len))
    assert dilation % n_res == 0
    spec = pl.BlockSpec((1, n_res, sub_len, width), lambda bi, ri: (bi, ri, 0, 0))
    return pl.pallas_call(
        functools.partial(_attn_kernel, branch=branch, dilation=dilation, qb=qb),
        grid=(b, dilation // n_res),
        in_specs=[pl.BlockSpec(memory_space=pltpu.SMEM)] + [spec] * 3,
        out_specs=[spec, pl.BlockSpec((1, n_res, sub_len, LANES), lambda bi, ri: (bi, ri, 0, 0))],
        out_shape=[jax.ShapeDtypeStruct(q.shape, BF16),
                   jax.ShapeDtypeStruct(q.shape[:3] + (LANES,), F32)],
        compiler_params=_params("parallel", "parallel"),
        name=f"dilated_attn_{branch}",
    )(slopes, q, k, v)


def _attn_out_kernel(o0, o1, o2, l0, l1, l2, x_ref, gain_ref, wo_ref, router_ref, rb_ref, expand_ref,
                     x_out, h_out, route_out, cnt_out, cnt_ref, o_scr, l_scr):
    @pl.when((pl.program_id(0) == 0) & (pl.program_id(1) == 0))
    def _():
        cnt_ref[...] = jnp.zeros_like(cnt_ref)

    tm = x_ref.shape[0]
    tiles_br = BRANCH_WIDTH // LANES
    for br, (o_ref, l_ref) in enumerate(((o0, l0), (o1, l1), (o2, l2))):
        dil = BRANCHES[br][1]
        for res in range(dil):
            rows = pl.ds(res, tm // dil, stride=dil) if dil > 1 else slice(None)
            l_scr[br, rows, :] = l_ref[0, res]
            for ti in range(tiles_br):
                lanes = slice(ti * LANES, (ti + 1) * LANES)
                o_scr[br, ti, rows, :] = o_ref[0, res, :, lanes].astype(F32)
    ls = [l_scr[br] for br in range(3)]
    m = jnp.maximum(jnp.maximum(ls[0], ls[1]), ls[2])
    ws = [jnp.exp(l - m) for l in ls]
    total = ws[0] + ws[1] + ws[2]
    spread = [_dot(_bf(ws[br] / total), expand_ref[...]) for br in range(3)]
    merged = []
    for ti in range(tiles_br):
        lanes = slice(ti * LANES, (ti + 1) * LANES)
        merged.append(_bf(sum(spread[br][:, lanes] * o_scr[br, ti] for br in range(3))))
    x = x_ref[...] + _dot(jnp.concatenate(merged, axis=1), wo_ref[...])
    x_out[...] = x
    h = _rms(x, gain_ref[...])
    h_out[...] = h
    hhi, hlo = _split2(h)
    both = _dot(hhi, router_ref[...])
    logits = both[:, :LANES] + both[:, LANES:] + _dot(hlo, router_ref[:, :LANES]) + rb_ref[...]
    lane = lax.broadcasted_iota(jnp.int32, logits.shape, 1)
    m1 = jnp.max(logits, axis=-1, keepdims=True)
    i1 = jnp.min(jnp.where(logits == m1, lane, LANES), axis=-1, keepdims=True)
    rest = jnp.where(lane == i1, -3e38, logits)
    m2 = jnp.max(rest, axis=-1, keepdims=True)
    i2 = jnp.min(jnp.where(rest == m2, lane, LANES), axis=-1, keepdims=True)
    ex = jnp.exp(m2 - m1)
    sel = jnp.where((lane == i1) | (lane == i2), 1.0, 0.0)
    rr = lax.broadcasted_iota(jnp.int32, (tm, tm), 0)
    cc = lax.broadcasted_iota(jnp.int32, (tm, tm), 1)
    before = jnp.where(rr > cc, 1.0, 0.0).astype(BF16)
    rank = cnt_ref[...] + _dot(before, _bf(sel))
    cnt_ref[...] += jnp.sum(sel, axis=0, keepdims=True)
    cnt_out[...] = cnt_ref[...]
    columns = [i1.astype(F32), i2.astype(F32),
               jnp.sum(jnp.where(lane == i1, rank, 0.0), axis=-1, keepdims=True),
               jnp.sum(jnp.where(lane == i2, rank, 0.0), axis=-1, keepdims=True),
               1.0 / (1.0 + ex), ex / (1.0 + ex)]
    route = jnp.zeros(logits.shape, F32)
    for j, col in enumerate(columns):
        route = jnp.where(lane == j, col, route)
    route_out[...] = route


def _attn_out(os_, ls_, x, gain, wo, router, rb, b, tm):
    t, d = x.shape
    tiles = t // (b * tm)
    tok = pl.BlockSpec((tm, d), lambda bi, i: (bi * tiles + i, 0))
    small = pl.BlockSpec((tm, LANES), lambda bi, i: (bi * tiles + i, 0))
    att = [pl.BlockSpec((1, dil, tm // dil, BRANCH_WIDTH), lambda bi, i: (bi, 0, i, 0))
           for _, dil in BRANCHES]
    lse = [pl.BlockSpec((1, dil, tm // dil, LANES), lambda bi, i: (bi, 0, i, 0)) for _, dil in BRANCHES]
    expand = (jnp.arange(LANES)[:, None] == jnp.arange(BRANCH_WIDTH)[None, :] // HEAD).astype(BF16)
    consts = [gain, wo, router, rb, expand]
    return pl.pallas_call(
        _attn_out_kernel,
        grid=(b, tiles),
        in_specs=att + lse + [tok] + [_const_spec(c.shape) for c in consts],
        out_specs=[tok, tok, small, _const_spec((1, LANES))],
        out_shape=[jax.ShapeDtypeStruct((t, d), F32), jax.ShapeDtypeStruct((t, d), F32),
                   jax.ShapeDtypeStruct((t, LANES), F32), jax.ShapeDtypeStruct((1, LANES), F32)],
        scratch_shapes=[pltpu.VMEM((1, LANES), F32),
                        pltpu.VMEM((len(BRANCHES), BRANCH_WIDTH // LANES, tm, LANES), F32),
                        pltpu.VMEM((len(BRANCHES), tm, LANES), F32)],
        compiler_params=_params("arbitrary", "arbitrary"),
        name="attn_out_route",
    )(*os_, *ls_, x, *consts)


DMA_UNROLL = 8


def _row_copy(src, src_row, dst, dst_row, sem):
    return pltpu.make_async_copy(src.at[pl.ds(src_row, 1)], dst.at[pl.ds(dst_row, 1)], sem)


def _moe_scatter_kernel(meta_ref, dest_ref, h_ref, out_ref, zeros, sem, zero_sem):
    n_tok = h_ref.shape[0]
    tile = zeros.shape[0]
    ends = lambda e: meta_ref[N_EXPERTS + e]

    def zero_fill(e):
        if e < N_EXPERTS:
            start, used = jnp.maximum(ends(e) - tile, 0), ends(e) > meta_ref[e]
        else:
            start = ends(N_EXPERTS - 1) + (e - N_EXPERTS) * tile
            used = start < out_ref.shape[0]
            start = jnp.minimum(start, out_ref.shape[0] - tile)
        dst = out_ref.at[pl.ds(pl.multiple_of(start, tile), tile)]
        return used, pltpu.make_async_copy(zeros, dst, zero_sem)

    @pl.when(pl.program_id(0) == 0)
    def _():
        zeros[...] = jnp.zeros_like(zeros)
        for e in range(2 * N_EXPERTS):
            used, copy = zero_fill(e)
            pl.when(used)(copy.start)
        for e in range(2 * N_EXPERTS):
            used, copy = zero_fill(e)
            pl.when(used)(copy.wait)

    def issue(j, carry):
        for k in range(2):
            _row_copy(h_ref, j, out_ref, dest_ref[2 * j + k], sem).start(priority=k)
        return carry

    def drain(j, carry):
        for _ in range(2):
            _row_copy(h_ref, 0, out_ref, 0, sem).wait()
        return carry

    lax.fori_loop(0, n_tok, issue, 0, unroll=DMA_UNROLL)
    lax.fori_loop(0, n_tok, drain, 0, unroll=DMA_UNROLL)


def _moe_scatter(meta, dest_flat, h, rows, ts, tile):
    t, d = h.shape
    return pl.pallas_call(
        _moe_scatter_kernel,
        grid=(t // ts,),
        in_specs=[pl.BlockSpec(memory_space=pltpu.SMEM),
                  pl.BlockSpec((2 * ts,), lambda i: (i,), memory_space=pltpu.SMEM),
                  pl.BlockSpec((ts, d), lambda i: (i, 0))],
        out_specs=pl.BlockSpec(memory_space=pl.ANY),
        out_shape=jax.ShapeDtypeStruct((rows, d), F32),
        scratch_shapes=[pltpu.VMEM((tile, d), F32), pltpu.SemaphoreType.DMA(()),
                        pltpu.SemaphoreType.DMA(())],
        compiler_params=_params("arbitrary"),
        name="moe_scatter",
    )(meta, dest_flat, h)


def _moe_ffn_kernel(src_ref, te_ref, nv_ref, x_ref, w1_ref, w3_ref, w2_ref, o_ref, acc_ref):
    del src_ref, te_ref
    i = pl.program_id(0)
    f = pl.program_id(1)

    @pl.when(i < nv_ref[0])
    def _():
        @pl.when(f == 0)
        def _():
            acc_ref[...] = jnp.zeros_like(acc_ref)

        h = _bf(x_ref[...])
        u = _dot(h, w1_ref[0])
        act = _bf(u * _sigmoid(u) * _dot(h, w3_ref[0]))
        acc_ref[...] += _dot(act, w2_ref[0])

        @pl.when(f == pl.num_programs(1) - 1)
        def _():
            o_ref[...] = acc_ref[...]

    @pl.when(i >= nv_ref[0])
    def _():
        o_ref[...] = jnp.zeros_like(o_ref)


def _moe_ffn(tile_src, tile_expert, n_valid, xs, w1, w3, w2, tm, tf):
    rows, d = xs.shape
    ff = w1.shape[2]
    tok = pl.BlockSpec((tm, d), lambda i, f, src, te, nv: (src[i], 0))
    return pl.pallas_call(
        _moe_ffn_kernel,
        grid_spec=pltpu.PrefetchScalarGridSpec(
            num_scalar_prefetch=3,
            grid=(rows // tm, ff // tf),
            in_specs=[tok,
                      pl.BlockSpec((1, d, tf), lambda i, f, src, te, nv: (te[i], 0, f)),
                      pl.BlockSpec((1, d, tf), lambda i, f, src, te, nv: (te[i], 0, f)),
                      pl.BlockSpec((1, tf, d), lambda i, f, src, te, nv: (te[i], f, 0))],
            out_specs=pl.BlockSpec((tm, d), lambda i, f, src, te, nv: (i, 0)),
            scratch_shapes=[pltpu.VMEM((tm, d), F32)],
        ),
        out_shape=jax.ShapeDtypeStruct((rows, d), F32),
        compiler_params=_params("arbitrary", "arbitrary"),
        name="moe_experts",
    )(tile_src, tile_expert, n_valid, xs, w1, w3, w2)


def _moe_combine_kernel(dest_ref, x_ref, gates_ref, gain_ref, y_hbm, o_ref, buf, sem):
    n_tok = x_ref.shape[0]

    def issue(j, carry):
        for k in range(2):
            _row_copy(y_hbm, dest_ref[2 * j + k], buf.at[k], j, sem).start(priority=k)
        return carry

    def drain(j, carry):
        for k in range(2):
            _row_copy(y_hbm, 0, buf.at[k], 0, sem).wait()
        return carry

    lax.fori_loop(0, n_tok, issue, 0, unroll=DMA_UNROLL)
    lax.fori_loop(0, n_tok, drain, 0, unroll=DMA_UNROLL)
    gates = gates_ref[...]
    x = x_ref[...] + gates[:, 4:5] * buf[0] + gates[:, 5:6] * buf[1]
    o_ref[...] = _rms(x, gain_ref[...])


def _moe_combine(dest_flat, x, route, gain, ys, ts):
    t, d = x.shape
    tok = pl.BlockSpec((ts, d), lambda i: (i, 0))
    return pl.pallas_call(
        _moe_combine_kernel,
        grid=(t // ts,),
        in_specs=[pl.BlockSpec((2 * ts,), lambda i: (i,), memory_space=pltpu.SMEM),
                  tok, pl.BlockSpec((ts, LANES), lambda i: (i, 0)), _const_spec(gain.shape),
                  pl.BlockSpec(memory_space=pl.ANY)],
        out_specs=tok,
        out_shape=jax.ShapeDtypeStruct((t, d), F32),
        scratch_shapes=[pltpu.VMEM((2, ts, d), F32), pltpu.SemaphoreType.DMA(())],
        compiler_params=_params("arbitrary"),
        name="moe_combine",
    )(dest_flat, x, route, gain, ys)


def _tile(n, want):
    t = min(n, want)
    assert n % t == 0
    return t


def _pad_cols(w, n):
    return jnp.pad(w, ((0, 0), (0, n - w.shape[1])))


def _pad_rows(w, n):
    return jnp.pad(w, ((0, n - w.shape[0]), (0, 0)))


def kernel(x, norm_gain, rwkv_mu, rwkv_wr, rwkv_wk, rwkv_wv, rwkv_w0, rwkv_w1, rwkv_w2, rwkv_a0, rwkv_a1, rwkv_a2, rwkv_g1, rwkv_g2, rwkv_k_k, rwkv_k_a, rwkv_r_k, rwkv_lnx_w, rwkv_lnx_b, rwkv_wo, kv_norm_gain, w_kv, attn_wq, attn_wo, ffn_w1, ffn_w3, ffn_w2, moe_router, moe_router_bias, moe_w1, moe_w3, moe_w2, final_norm_gain):
    b, s, d = x.shape
    t = b * s
    assert norm_gain.shape[0] == 2 and d % LANES == 0 and s % 8 == 0

    zeros = jnp.zeros((d,), F32)
    vec_pre = jnp.stack([norm_gain[0, 0], rwkv_w0[0], rwkv_a0[0], rwkv_k_k[0], rwkv_k_a[0],
                         zeros, zeros, zeros])
    tm_pre = _tile(s, 256)
    chunk = _tile(s, LANES)
    r, lw, k, v, kk, a, g = _rwkv_pre(
        x, vec_pre, rwkv_mu[0], _bf(rwkv_wr[0]), _bf(rwkv_wk[0]), _bf(rwkv_wv[0]),
        _bf(_pad_cols(rwkv_w1[0], LANES)), _bf(_pad_rows(rwkv_w2[0], LANES)),
        _bf(_pad_cols(rwkv_a1[0], LANES)), _bf(_pad_rows(rwkv_a2[0], LANES)),
        _bf(rwkv_g1[0]), _bf(rwkv_g2[0]), tm_pre)
    vec_post = jnp.stack([rwkv_lnx_w[0], rwkv_lnx_b[0], rwkv_r_k[0].reshape(d),
                          zeros, zeros, zeros, zeros, zeros])
    z = _wkv(r, lw, k, v, kk, a, vec_post, chunk=chunk, seq_tile=_tile(s, 2048), n_pairs=4)

    flat = lambda arr: arr.reshape(t, d)
    tm = _tile(t, 512)
    x1 = _rwkv_post(flat(z), flat(g), flat(x), _bf(rwkv_wo[0]), tm)

    ff = ffn_w1.shape[2]
    tf = ff // 2 if (ff // 2) % LANES == 0 else ff
    x2 = _ffn(x1, norm_gain[0, 1][None, :], _bf(ffn_w1[0]), _bf(ffn_w3[0]), _bf(ffn_w2[0]),
              _tile(t, 2 * FFN_SUB_ROWS), tf)

    n_slopes = len(BRANCHES) * HEADS_PER_BRANCH
    slopes = jnp.exp2(-ALIBI_MAX * (jnp.arange(n_slopes, dtype=F32) + 1.0) / n_slopes)
    gains = jnp.stack([norm_gain[1, 0], kv_norm_gain] + [zeros] * 6)
    tm_a = _tile(s, 512)
    qkv = _attn_proj(x2.reshape(b, s, d), gains, _bf(attn_wq[0]), _bf(w_kv), tm_a)
    os_, ls_ = [], []
    for br in range(len(BRANCHES)):
        o_br, l_br = _attn_branch(slopes, *qkv[3 * br:3 * br + 3], br)
        os_.append(o_br)
        ls_.append(l_br)

    router = jnp.concatenate(_split2(_pad_cols(moe_router[0], LANES)), axis=1)
    rbias = jnp.full((1, LANES), NEG, F32).at[0, :N_EXPERTS].set(moe_router_bias[0])
    x3, h4, route, cnt = _attn_out(os_, ls_, x2, norm_gain[1, 1][None, :], _bf(attn_wo[0]),
                                   router, rbias, b, tm_a)

    tm_e = 512 if t >= 4096 else 128
    ts = _tile(t, 512)
    counts = cnt[0, :N_EXPERTS].astype(jnp.int32)
    padded = ((counts + tm_e - 1) // tm_e) * tm_e
    ends = jnp.cumsum(padded)
    meta = jnp.concatenate([ends - padded, ends]).astype(jnp.int32)
    route_i = route[:, :4].astype(jnp.int32)
    group_start = jnp.sum(jnp.where(route_i[:, :2, None] == jnp.arange(N_EXPERTS), ends - padded, 0), axis=-1)
    dest = (group_start + route_i[:, 2:4]).astype(jnp.int32).reshape(2 * t)
    rows = 2 * t + N_EXPERTS * tm_e
    n_tiles = rows // tm_e
    n_valid = (ends[-1] // tm_e).astype(jnp.int32)
    tile_src = jnp.minimum(jnp.arange(n_tiles, dtype=jnp.int32), n_valid - 1)
    tile_expert = jnp.minimum(
        jnp.sum(tile_src[:, None] * tm_e >= ends[None, :], axis=1), N_EXPERTS - 1).astype(jnp.int32)

    xs = _moe_scatter(meta, dest, h4, rows, ts, tm_e)
    ffe = moe_w1.shape[3]
    tfe = ffe // 2 if (ffe // 2) % LANES == 0 else ffe
    ys = _moe_ffn(tile_src, tile_expert, n_valid.reshape(1), xs,
                  _bf(moe_w1[0]), _bf(moe_w3[0]), _bf(moe_w2[0]), tm_e, tfe)
    out = _moe_combine(dest, x3, route, final_norm_gain[None, :], ys, ts)
    return out.reshape(b, s, d)
```

```python
import functools
import math

import jax
import jax.numpy as jnp
from jax import lax
from jax.experimental import pallas as pl
from jax.experimental.pallas import tpu as pltpu

F32 = jnp.float32
BF16 = jnp.bfloat16

RMS_EPS = 1e-5
GN_EPS = 64e-5
HEAD = 64
LANES = 128
BRANCHES = ((128, 1), (512, 4), (2048, 16))
HEADS_PER_BRANCH = 8
ALIBI_MAX = 8.0
N_EXPERTS = 8
NEG = -1e30
VMEM_LIMIT_BYTES = 56 * 1024 * 1024


def _params(*sem):
    return pltpu.CompilerParams(dimension_semantics=sem, vmem_limit_bytes=VMEM_LIMIT_BYTES)


def _dot(a, b):
    return jnp.dot(a, b, preferred_element_type=F32)


def _dot_nt(a, b):
    return lax.dot_general(a, b, (((1,), (1,)), ((), ())), preferred_element_type=F32)


def _bf(x):
    return x.astype(BF16)


def _split2(x):
    hi = x.astype(BF16)
    lo = (x - hi.astype(F32)).astype(BF16)
    return hi, lo


def _sigmoid(z):
    return 1.0 / (1.0 + jnp.exp(-z))


def _rms(x, gain):
    return x * lax.rsqrt(jnp.mean(x * x, axis=-1, keepdims=True) + RMS_EPS) * gain


def _headsum(x, e, et):
    return _dot(_bf(_dot(_bf(x), e)), et)


def _const_spec(shape):
    nd = len(shape)
    return pl.BlockSpec(shape, lambda *_: (0,) * nd)


def _rwkv_pre_kernel(x_ref, xp_ref, vec_ref, mu_ref, wr_ref, wk_ref, wv_ref, w1_ref, w2_ref,
                     a1_ref, a2_ref, g1_ref, g2_ref,
                     r_out, lw_out, k_out, v_out, kk_out, a_out, g_out):
    i = pl.program_id(1)
    vec = vec_ref[...]
    gain, w0, a0, k_k, k_a = (vec[j:j + 1] for j in range(5))
    x = x_ref[0]
    h = _rms(x, gain)
    hp = _rms(xp_ref[0][7:8, :], gain)
    hp = jnp.where(i > 0, hp, 0.0)
    rows = lax.broadcasted_iota(jnp.int32, h.shape, 0)
    hprev = jnp.where(rows == 0, hp, pltpu.roll(h, 1, 0))
    xx = hprev - h
    mu = mu_ref[...]
    xr, xw, xk, xv, xa, xg = (_bf(h + xx * mu[j:j + 1]) for j in range(6))
    r = _dot(xr, wr_ref[...])
    k = _dot(xk, wk_ref[...])
    v = _dot(xv, wv_ref[...])
    wl = w0 + _dot(_bf(jnp.tanh(_dot(xw, w1_ref[...]))), w2_ref[...])
    a = _sigmoid(a0 + _dot(_bf(_dot(xa, a1_ref[...])), a2_ref[...]))
    g = _dot(_bf(_sigmoid(_dot(xg, g1_ref[...]))), g2_ref[...])
    lw_out[0] = _sigmoid(wl) * (-math.exp(-0.5))
    r_out[0] = _bf(r)
    k_out[0] = _bf(k * (1.0 + (a - 1.0) * k_a))
    v_out[0] = _bf(v)
    kk_out[0] = _bf(k * k_k)
    a_out[0] = _bf(a)
    g_out[0] = _bf(g)


def _rwkv_pre(x, vec, mu, wr, wk, wv, w1, w2, a1, a2, g1, g2, tm):
    b, s, d = x.shape
    tok = pl.BlockSpec((1, tm, d), lambda bi, i: (bi, i, 0))
    prev = pl.BlockSpec((1, 8, d), lambda bi, i: (bi, jnp.maximum(i * (tm // 8) - 1, 0), 0))
    consts = [vec, mu, wr, wk, wv, w1, w2, a1, a2, g1, g2]
    out = lambda dt: jax.ShapeDtypeStruct((b, s, d), dt)
    return pl.pallas_call(
        _rwkv_pre_kernel,
        grid=(b, s // tm),
        in_specs=[tok, prev] + [_const_spec(c.shape) for c in consts],
        out_specs=[tok] * 7,
        out_shape=[out(BF16), out(F32)] + [out(BF16)] * 5,
        compiler_params=_params("parallel", "arbitrary"),
        name="rwkv_pre",
    )(x, x, *consts)


def _round_robin(generators):
    results = [None] * len(generators)
    live = list(enumerate(generators))
    while live:
        still = []
        for idx, gen in live:
            try:
                results[idx] = next(gen)
                still.append((idx, gen))
            except StopIteration:
                pass
        live = still
    return results


def _wkv_kernel(r_ref, lw_ref, k_ref, v_ref, kk_ref, a_ref, vec_ref, y_ref, st_ref, raw_ref, *, chunk):
    c_len = chunk
    n_chunks = r_ref.shape[1] // c_len
    n_pairs = r_ref.shape[2] // LANES
    inv_steps = max(c_len.bit_length() - 2, 0)
    lane = lax.broadcasted_iota(jnp.int32, (c_len, LANES), 1)
    head0 = lane < HEAD
    rr = lax.broadcasted_iota(jnp.int32, (c_len, c_len), 0)
    cc = lax.broadcasted_iota(jnp.int32, (c_len, c_len), 1)
    lower = rr >= cc
    strict = rr > cc
    eye = jnp.where(rr == cc, 1.0, 0.0).astype(F32)
    r2 = lax.broadcasted_iota(jnp.int32, (LANES, LANES), 0)
    c2 = lax.broadcasted_iota(jnp.int32, (LANES, LANES), 1)
    same_head = (r2 < HEAD) == (c2 < HEAD)

    @pl.when(pl.program_id(2) == 0)
    def _():
        st_ref[...] = jnp.zeros_like(st_ref)
        raw_ref[...] = jnp.zeros_like(raw_ref)

    row_index = lax.broadcasted_iota(jnp.int32, (c_len, LANES), 0)
    scan_shifts = [1 << j for j in range(c_len.bit_length() - 1)]
    rows = lambda mat, j: mat[j * c_len:(j + 1) * c_len]
    own_head = lambda mat: jnp.where(head0, rows(mat, 0), rows(mat, 1))

    def head_sum(x):
        s0 = jnp.sum(jnp.where(head0, x, 0.0), axis=-1, keepdims=True)
        s1 = jnp.sum(jnp.where(head0, 0.0, x), axis=-1, keepdims=True)
        return jnp.where(head0, s0, s1)

    def chunk_pair(sl, lanes, state):
        cum = lw_ref[0, sl, lanes]
        for shift in scan_shifts:
            cum = cum + jnp.where(row_index >= shift, pltpu.roll(cum, shift, 0), 0.0)
        cum_prev = jnp.where(row_index == 0, 0.0, pltpu.roll(cum, 1, 0))
        cum_last = cum[c_len - 1:c_len, :]
        kk = kk_ref[0, sl, lanes].astype(F32)
        kk = kk / jnp.maximum(jnp.sqrt(head_sum(kk * kk)), 1e-12)
        b = kk * a_ref[0, sl, lanes].astype(F32)
        k = k_ref[0, sl, lanes].astype(F32)
        vb = v_ref[0, sl, lanes]
        e_neg = jnp.exp(-cum)
        e_end = jnp.exp(cum_last - cum)
        a_t = -kk * jnp.exp(cum_prev)
        r = r_ref[0, sl, lanes].astype(F32)
        r_t = r * jnp.exp(cum)
        lhs = _bf(jnp.concatenate([jnp.where(head0, a_t, 0.0), jnp.where(head0, 0.0, a_t),
                                   jnp.where(head0, r_t, 0.0), jnp.where(head0, 0.0, r_t)], axis=0))
        gram = _dot_nt(lhs, _bf(jnp.concatenate([b * e_neg, k * e_neg], axis=0)))
        from_state = _dot_nt(_bf(jnp.concatenate([a_t, r_t], axis=0)), _bf(state))
        yield
        gram_b, gram_k = gram[:, :c_len], gram[:, c_len:]
        from_v = _dot(_bf(jnp.concatenate(
            [jnp.where(strict, rows(gram_k, 0), 0.0), jnp.where(strict, rows(gram_k, 1), 0.0),
             jnp.where(lower, rows(gram_k, 2), 0.0), jnp.where(lower, rows(gram_k, 3), 0.0)], axis=0)), vb)
        m_rb = _bf(jnp.concatenate([jnp.where(lower, rows(gram_b, 2), 0.0),
                                    jnp.where(lower, rows(gram_b, 3), 0.0)], axis=0))
        w = rows(from_state, 0) + own_head(from_v[:2 * c_len])
        y1 = rows(from_state, 1) + own_head(from_v[2 * c_len:])
        pw = [jnp.where(strict, rows(gram_b, hd), 0.0) for hd in range(2)]
        inv = [eye + pw[hd] for hd in range(2)]
        if inv_steps:
            pw = [_dot(_bf(pw[hd]), _bf(pw[hd])) for hd in range(2)]
        for step in range(inv_steps):
            yield
            if step < inv_steps - 1:
                both = [_dot(_bf(jnp.concatenate([inv[hd], pw[hd]], axis=0)), _bf(pw[hd])) for hd in range(2)]
                inv = [inv[hd] + rows(both[hd], 0) for hd in range(2)]
                pw = [rows(both[hd], 1) for hd in range(2)]
            else:
                inv = [inv[hd] + _dot(_bf(inv[hd]), _bf(pw[hd])) for hd in range(2)]
        yield
        u = own_head(_dot(_bf(jnp.concatenate(inv, axis=0)), _bf(w)))
        yield
        y = y1 + own_head(_dot(m_rb, _bf(u)))
        v = vb.astype(F32)
        uv_t = _bf(jnp.transpose(jnp.concatenate([u, v], axis=0)))
        bk = _bf(jnp.concatenate([b * e_end, k * e_end], axis=0))
        yield y, state * jnp.exp(cum_last) + jnp.where(same_head, _dot(uv_t, bk), 0.0)

    def finish(sl, lanes, y):
        lnx_w, lnx_b, r_k = (vec_ref[j:j + 1, lanes] for j in range(3))
        r, k, v = (ref[0, sl, lanes].astype(F32) for ref in (r_ref, k_ref, v_ref))
        dy = y - head_sum(y) * (1.0 / HEAD)
        yn = dy * lax.rsqrt(head_sum(dy * dy) * (1.0 / HEAD) + GN_EPS) * lnx_w + lnx_b
        y_ref[0, sl, lanes] = _bf(yn + head_sum(r * k * r_k) * v)

    pair_lanes = [slice(p * LANES, (p + 1) * LANES) for p in range(n_pairs)]
    chunk_rows = lambda c: pl.ds(pl.multiple_of(c * c_len, c_len), c_len)

    def body(c, carry):
        before = chunk_rows(jnp.maximum(c - 1, 0))
        for p in range(n_pairs):
            finish(before, pair_lanes[p], raw_ref[p])
        sl = chunk_rows(c)
        results = _round_robin([chunk_pair(sl, pair_lanes[p], st_ref[p]) for p in range(n_pairs)])
        for p, (y, new_state) in enumerate(results):
            raw_ref[p] = y
            st_ref[p] = new_state
        return carry

    lax.fori_loop(0, n_chunks, body, 0, unroll=2 if n_chunks % 2 == 0 else 1)
    for p in range(n_pairs):
        finish(chunk_rows(n_chunks - 1), pair_lanes[p], raw_ref[p])


def _wkv(r, lw, k, v, kk, a, vec, chunk, seq_tile, n_pairs):
    b, s, d = r.shape
    width = n_pairs * LANES
    spec = pl.BlockSpec((1, seq_tile, width), lambda bi, hi, si: (bi, si, hi))
    return pl.pallas_call(
        functools.partial(_wkv_kernel, chunk=chunk),
        grid=(b, d // width, s // seq_tile),
        in_specs=[spec] * 6 + [pl.BlockSpec((vec.shape[0], width), lambda bi, hi, si: (0, hi))],
        out_specs=spec,
        out_shape=jax.ShapeDtypeStruct((b, s, d), BF16),
        scratch_shapes=[pltpu.VMEM((n_pairs, LANES, LANES), F32),
                        pltpu.VMEM((n_pairs, chunk, LANES), F32)],
        compiler_params=_params("parallel", "parallel", "arbitrary"),
        name="wkv7",
    )(r, lw, k, v, kk, a, vec)


def _rwkv_post_kernel(z_ref, g_ref, x_ref, wo_ref, o_ref):
    o_ref[...] = x_ref[...] + _dot(z_ref[...] * g_ref[...], wo_ref[...])


def _rwkv_post(z, g, x, wo, tm):
    t, d = x.shape
    tok = pl.BlockSpec((tm, d), lambda i: (i, 0))
    return pl.pallas_call(
        _rwkv_post_kernel,
        grid=(t // tm,),
        in_specs=[tok] * 3 + [_const_spec(wo.shape)],
        out_specs=tok,
        out_shape=jax.ShapeDtypeStruct((t, d), F32),
        compiler_params=_params("parallel"),
        name="rwkv_post",
    )(z, g, x, wo)


FFN_SUB_ROWS = 512


def _ffn_kernel(x_ref, gain_ref, w1_ref, w3_ref, w2_ref, o_ref, h_ref):
    @pl.when(pl.program_id(1) == 0)
    def _():
        h_ref[...] = _bf(_rms(x_ref[...], gain_ref[...]))
        o_ref[...] = x_ref[...]

    for start in range(0, h_ref.shape[0], FFN_SUB_ROWS):
        rows = slice(start, min(start + FFN_SUB_ROWS, h_ref.shape[0]))
        h = h_ref[rows, :]
        u = _dot(h, w1_ref[...])
        act = _bf(u * _sigmoid(u) * _dot(h, w3_ref[...]))
        o_ref[rows, :] += _dot(act, w2_ref[...])


def _ffn(x, gain, w1, w3, w2, tm, tf):
    t, d = x.shape
    ff = w1.shape[1]
    tok = pl.BlockSpec((tm, d), lambda i, f: (i, 0))
    return pl.pallas_call(
        _ffn_kernel,
        grid=(t // tm, ff // tf),
        in_specs=[tok, _const_spec(gain.shape),
                  pl.BlockSpec((d, tf), lambda i, f: (0, f)),
                  pl.BlockSpec((d, tf), lambda i, f: (0, f)),
                  pl.BlockSpec((tf, d), lambda i, f: (f, 0))],
        out_specs=tok,
        out_shape=jax.ShapeDtypeStruct((t, d), F32),
        scratch_shapes=[pltpu.VMEM((tm, d), BF16)],
        compiler_params=_params("parallel", "arbitrary"),
        name="ffn_dense",
    )(x, gain, w1, w3, w2)


BRANCH_WIDTH = HEADS_PER_BRANCH * HEAD


def _attn_proj_kernel(x_ref, gains_ref, wq_ref, wkv_ref, *refs):
    outs, (q_scr, kv_scr) = refs[:-2], refs[-2:]
    x = x_ref[0]
    tm = x.shape[0]
    n = x * lax.rsqrt(jnp.mean(x * x, axis=-1, keepdims=True) + RMS_EPS)
    gains = gains_ref[...]
    q_tiles = q_scr.shape[0]
    qw = q_tiles * LANES
    tiles_br = BRANCH_WIDTH // LANES
    hq = _bf(n * gains[0:1])
    hkv = _bf(n * gains[1:2])

    def project(h, w_ref, col, scr, first_tile, scale=None):
        val = _dot(h, w_ref[:, col:col + BRANCH_WIDTH])
        for ti in range(tiles_br):
            tile = val[:, ti * LANES:(ti + 1) * LANES]
            scr[first_tile + ti] = tile if scale is None else tile * scale

    def permute(out, dil, scr, first_tile):
        for res in range(dil):
            rows = pl.ds(res, tm // dil, stride=dil) if dil > 1 else slice(None)
            for ti in range(tiles_br):
                out[0, res, :, ti * LANES:(ti + 1) * LANES] = _bf(scr[first_tile + ti, rows, :])

    pending = None
    for which, (h, w_ref, base, scr, tile0, scale) in enumerate(
            ((hq, wq_ref, 0, q_scr, 0, 1.0 / HEAD ** 0.5), (hkv, wkv_ref, 0, kv_scr, 0, None),
             (hkv, wkv_ref, qw, kv_scr, q_tiles, None))):
        for br, (_, dil) in enumerate(BRANCHES):
            project(h, w_ref, base + br * BRANCH_WIDTH, scr, tile0 + br * tiles_br, scale)
            if pending is not None:
                permute(*pending)
            pending = (outs[3 * br + which], dil, scr, tile0 + br * tiles_br)
    permute(*pending)


def _attn_proj(x, gains, wq, wkv, tm):
    b, s, d = x.shape
    qw = wq.shape[1]
    out_specs, out_shape = [], []
    for _, dil in BRANCHES:
        assert tm % (16 * dil) == 0
        out_specs += [pl.BlockSpec((1, dil, tm // dil, BRANCH_WIDTH), lambda bi, i: (bi, 0, i, 0))] * 3
        out_shape += [jax.ShapeDtypeStruct((b, dil, s // dil, BRANCH_WIDTH), BF16)] * 3
    return pl.pallas_call(
        _attn_proj_kernel,
        grid=(b, s // tm),
        in_specs=[pl.BlockSpec((1, tm, d), lambda bi, i: (bi, i, 0)),
                  _const_spec(gains.shape), _const_spec(wq.shape), _const_spec(wkv.shape)],
        out_specs=out_specs,
        out_shape=out_shape,
        scratch_shapes=[pltpu.VMEM((qw // LANES, tm, LANES), F32),
                        pltpu.VMEM((2 * qw // LANES, tm, LANES), F32)],
        compiler_params=_params("parallel", "parallel"),
        name="attn_proj",
    )(x, gains, wq, wkv)


def _attn_kernel(slopes_ref, q_ref, k_ref, v_ref, o_ref, lse_ref, *, branch, dilation, qb):
    sub_len = q_ref.shape[2]
    n_blk = sub_len // qb
    n_pairs = q_ref.shape[3] // LANES
    col2 = lax.broadcasted_iota(jnp.int32, (qb, 2 * qb), 1)
    rel = (lax.broadcasted_iota(jnp.int32, (qb, 2 * qb), 0) - jnp.where(col2 < qb, col2, col2 - qb)).astype(F32)
    dist_d = jnp.where(rel >= 0, rel, -NEG)
    dist_p = jnp.where(rel <= 0, rel + float(qb), -NEG)
    first_head = lax.broadcasted_iota(jnp.int32, (1, 2 * qb), 1) < qb
    lane = lax.broadcasted_iota(jnp.int32, (qb, LANES), 1)
    head0 = lane < HEAD
    lane_row = lax.broadcasted_iota(jnp.int32, (1, LANES), 1)
    head_mask = [jnp.where(lane_row < HEAD, 1.0, 0.0).astype(BF16),
                 jnp.where(lane_row < HEAD, 0.0, 1.0).astype(BF16)]
    head_ones = [jnp.where(head0, 1.0, 0.0).astype(BF16), jnp.where(head0, 0.0, 1.0).astype(BF16)]

    def block(residues, i):
        cur = pl.ds(pl.multiple_of(i * qb, qb), qb)
        if n_blk > 1:
            prv = pl.ds(pl.multiple_of(jnp.maximum(i - 1, 0) * qb, qb), qb)
            has_prev = jnp.where(i > 0, 1.0, -NEG)

        def one_pair(r, p):
            lanes = slice(p * LANES, (p + 1) * LANES)
            slope = [slopes_ref[branch * HEADS_PER_BRANCH + 2 * p + j] * float(dilation) for j in range(2)]
            step = jnp.where(first_head, slope[0], slope[1])

            def keys(rows):
                k = k_ref[0, r, rows, lanes]
                return jnp.concatenate([k * head_mask[0], k * head_mask[1]], axis=0)

            def values(rows):
                v = v_ref[0, r, rows, lanes]
                return jnp.concatenate(
                    [jnp.concatenate([v * head_mask[0], head_ones[0]], axis=1),
                     jnp.concatenate([v * head_mask[1], head_ones[1]], axis=1)], axis=0)

            q = q_ref[0, r, cur, lanes]
            sd = _dot_nt(q, keys(cur)) - step * dist_d
            if n_blk > 1:
                sp = _dot_nt(q, keys(prv)) - (step * has_prev) * dist_p
            yield
            both = jnp.maximum(sd, sp) if n_blk > 1 else sd
            m = [jnp.max(both[:, :qb], axis=-1, keepdims=True), jnp.max(both[:, qb:], axis=-1, keepdims=True)]
            probs = lambda s: _bf(jnp.concatenate([jnp.exp(s[:, :qb] - m[0]), jnp.exp(s[:, qb:] - m[1])], axis=1))
            acc = _dot(probs(sd), values(cur))
            if n_blk > 1:
                acc = acc + _dot(probs(sp), values(prv))
            yield
            den = acc[:, LANES:]
            lse = jnp.where(head0, m[0], m[1]) + jnp.log(den)
            yield acc[:, :LANES] / den, lse[:, 0:1], lse[:, HEAD:HEAD + 1]

        res = _round_robin([one_pair(r, p) for r in residues for p in range(n_pairs)])
        for n, r in enumerate(residues):
            lse = jnp.zeros((qb, LANES), F32)
            for p in range(n_pairs):
                out, l_a, l_b = res[n * n_pairs + p]
                o_ref[0, r, cur, p * LANES:(p + 1) * LANES] = _bf(out)
                lse = jnp.where(lane == 2 * p, l_a, jnp.where(lane == 2 * p + 1, l_b, lse))
            lse_ref[0, r, cur, :] = lse

    n_res = q_ref.shape[1]
    if n_blk == 1:
        block(range(n_res), 0)
    else:
        for r in range(n_res):
            lax.fori_loop(0, n_blk, lambda i, c, r=r: (block([r], i), c)[1], 0,
                          unroll=2 if n_blk % 2 == 0 else 1)


def _attn_branch(slopes, q, k, v, branch):
    window, dilation = BRANCHES[branch]
    assert window % dilation == 0 and window // dilation == LANES
    b, _, sub_len, width = q.shape
    qb = min(LANES, sub_len)
    assert sub_len % qb == 0
    n_res = max(1, min(dilation, 512 // sub_len))
    assert dilation % n_res == 0
    spec = pl.BlockSpec((1, n_res, sub_len, width), lambda bi, ri: (bi, ri, 0, 0))
    return pl.pallas_call(
        functools.partial(_attn_kernel, branch=branch, dilation=dilation, qb=qb),
        grid=(b, dilation // n_res),
        in_specs=[pl.BlockSpec(memory_space=pltpu.SMEM)] + [spec] * 3,
        out_specs=[spec, pl.BlockSpec((1, n_res, sub_len, LANES), lambda bi, ri: (bi, ri, 0, 0))],
        out_shape=[jax.ShapeDtypeStruct(q.shape, BF16),
                   jax.ShapeDtypeStruct(q.shape[:3] + (LANES,), F32)],
        compiler_params=_params("parallel", "parallel"),
        name=f"dilated_attn_{branch}",
    )(slopes, q, k, v)


def _attn_out_kernel(o0, o1, o2, l0, l1, l2, x_ref, gain_ref, wo_ref, router_ref, rb_ref, expand_ref,
                     x_out, h_out, route_out, cnt_out, cnt_ref, o_scr, l_scr):
    @pl.when((pl.program_id(0) == 0) & (pl.program_id(1) == 0))
    def _():
        cnt_ref[...] = jnp.zeros_like(cnt_ref)

    tm = x_ref.shape[0]
    tiles_br = BRANCH_WIDTH // LANES
    for br, (o_ref, l_ref) in enumerate(((o0, l0), (o1, l1), (o2, l2))):
        dil = BRANCHES[br][1]
        for res in range(dil):
            rows = pl.ds(res, tm // dil, stride=dil) if dil > 1 else slice(None)
            l_scr[br, rows, :] = l_ref[0, res]
            for ti in range(tiles_br):
                lanes = slice(ti * LANES, (ti + 1) * LANES)
                o_scr[br, ti, rows, :] = o_ref[0, res, :, lanes].astype(F32)
    ls = [l_scr[br] for br in range(3)]
    m = jnp.maximum(jnp.maximum(ls[0], ls[1]), ls[2])
    ws = [jnp.exp(l - m) for l in ls]
    total = ws[0] + ws[1] + ws[2]
    spread = [_dot(_bf(ws[br] / total), expand_ref[...]) for br in range(3)]
    merged = []
    for ti in range(tiles_br):
        lanes = slice(ti * LANES, (ti + 1) * LANES)
        merged.append(_bf(sum(spread[br][:, lanes] * o_scr[br, ti] for br in range(3))))
    x = x_ref[...] + _dot(jnp.concatenate(merged, axis=1), wo_ref[...])
    x_out[...] = x
    h = _rms(x, gain_ref[...])
    h_out[...] = h
    hhi, hlo = _split2(h)
    both = _dot(hhi, router_ref[...])
    logits = both[:, :LANES] + both[:, LANES:] + _dot(hlo, router_ref[:, :LANES]) + rb_ref[...]
    lane = lax.broadcasted_iota(jnp.int32, logits.shape, 1)
    m1 = jnp.max(logits, axis=-1, keepdims=True)
    i1 = jnp.min(jnp.where(logits == m1, lane, LANES), axis=-1, keepdims=True)
    rest = jnp.where(lane == i1, -3e38, logits)
    m2 = jnp.max(rest, axis=-1, keepdims=True)
    i2 = jnp.min(jnp.where(rest == m2, lane, LANES), axis=-1, keepdims=True)
    ex = jnp.exp(m2 - m1)
    sel = jnp.where((lane == i1) | (lane == i2), 1.0, 0.0)
    rr = lax.broadcasted_iota(jnp.int32, (tm, tm), 0)
    cc = lax.broadcasted_iota(jnp.int32, (tm, tm), 1)
    before = jnp.where(rr > cc, 1.0, 0.0).astype(BF16)
    rank = cnt_ref[...] + _dot(before, _bf(sel))
    cnt_ref[...] += jnp.sum(sel, axis=0, keepdims=True)
    cnt_out[...] = cnt_ref[...]
    columns = [i1.astype(F32), i2.astype(F32),
               jnp.sum(jnp.where(lane == i1, rank, 0.0), axis=-1, keepdims=True),
               jnp.sum(jnp.where(lane == i2, rank, 0.0), axis=-1, keepdims=True),
               1.0 / (1.0 + ex), ex / (1.0 + ex)]
    route = jnp.zeros(logits.shape, F32)
    for j, col in enumerate(columns):
        route = jnp.where(lane == j, col, route)
    route_out[...] = route


def _attn_out(os_, ls_, x, gain, wo, router, rb, b, tm):
    t, d = x.shape
    tiles = t // (b * tm)
    tok = pl.BlockSpec((tm, d), lambda bi, i: (bi * tiles + i, 0))
    small = pl.BlockSpec((tm, LANES), lambda bi, i: (bi * tiles + i, 0))
    att = [pl.BlockSpec((1, dil, tm // dil, BRANCH_WIDTH), lambda bi, i: (bi, 0, i, 0))
           for _, dil in BRANCHES]
    lse = [pl.BlockSpec((1, dil, tm // dil, LANES), lambda bi, i: (bi, 0, i, 0)) for _, dil in BRANCHES]
    expand = (jnp.arange(LANES)[:, None] == jnp.arange(BRANCH_WIDTH)[None, :] // HEAD).astype(BF16)
    consts = [gain, wo, router, rb, expand]
    return pl.pallas_call(
        _attn_out_kernel,
        grid=(b, tiles),
        in_specs=att + lse + [tok] + [_const_spec(c.shape) for c in consts],
        out_specs=[tok, tok, small, _const_spec((1, LANES))],
        out_shape=[jax.ShapeDtypeStruct((t, d), F32), jax.ShapeDtypeStruct((t, d), F32),
                   jax.ShapeDtypeStruct((t, LANES), F32), jax.ShapeDtypeStruct((1, LANES), F32)],
        scratch_shapes=[pltpu.VMEM((1, LANES), F32),
                        pltpu.VMEM((len(BRANCHES), BRANCH_WIDTH // LANES, tm, LANES), F32),
                        pltpu.VMEM((len(BRANCHES), tm, LANES), F32)],
        compiler_params=_params("arbitrary", "arbitrary"),
        name="attn_out_route",
    )(*os_, *ls_, x, *consts)


DMA_UNROLL = 8


def _row_copy(src, src_row, dst, dst_row, sem):
    return pltpu.make_async_copy(src.at[pl.ds(src_row, 1)], dst.at[pl.ds(dst_row, 1)], sem)


def _moe_scatter_kernel(meta_ref, dest_ref, h_ref, out_ref, zeros, sem, zero_sem):
    n_tok = h_ref.shape[0]
    tile = zeros.shape[0]
    ends = lambda e: meta_ref[N_EXPERTS + e]

    def zero_fill(e):
        if e < N_EXPERTS:
            start, used = jnp.maximum(ends(e) - tile, 0), ends(e) > meta_ref[e]
        else:
            start = ends(N_EXPERTS - 1) + (e - N_EXPERTS) * tile
            used = start < out_ref.shape[0]
            start = jnp.minimum(start, out_ref.shape[0] - tile)
        dst = out_ref.at[pl.ds(pl.multiple_of(start, tile), tile)]
        return used, pltpu.make_async_copy(zeros, dst, zero_sem)

    @pl.when(pl.program_id(0) == 0)
    def _():
        zeros[...] = jnp.zeros_like(zeros)
        for e in range(2 * N_EXPERTS):
            used, copy = zero_fill(e)
            pl.when(used)(copy.start)
        for e in range(2 * N_EXPERTS):
            used, copy = zero_fill(e)
            pl.when(used)(copy.wait)

    def issue(j, carry):
        for k in range(2):
            _row_copy(h_ref, j, out_ref, dest_ref[2 * j + k], sem).start(priority=k)
        return carry

    def drain(j, carry):
        for _ in range(2):
            _row_copy(h_ref, 0, out_ref, 0, sem).wait()
        return carry

    lax.fori_loop(0, n_tok, issue, 0, unroll=DMA_UNROLL)
    lax.fori_loop(0, n_tok, drain, 0, unroll=DMA_UNROLL)


def _moe_scatter(meta, dest_flat, h, rows, ts, tile):
    t, d = h.shape
    return pl.pallas_call(
        _moe_scatter_kernel,
        grid=(t // ts,),
        in_specs=[pl.BlockSpec(memory_space=pltpu.SMEM),
                  pl.BlockSpec((2 * ts,), lambda i: (i,), memory_space=pltpu.SMEM),
                  pl.BlockSpec((ts, d), lambda i: (i, 0))],
        out_specs=pl.BlockSpec(memory_space=pl.ANY),
        out_shape=jax.ShapeDtypeStruct((rows, d), F32),
        scratch_shapes=[pltpu.VMEM((tile, d), F32), pltpu.SemaphoreType.DMA(()),
                        pltpu.SemaphoreType.DMA(())],
        compiler_params=_params("arbitrary"),
        name="moe_scatter",
    )(meta, dest_flat, h)


def _moe_ffn_kernel(src_ref, te_ref, nv_ref, x_ref, w1_ref, w3_ref, w2_ref, o_ref):
    del src_ref, te_ref
    @pl.when(pl.program_id(1) == 0)
    def _():
        o_ref[...] = jnp.zeros_like(o_ref)

    @pl.when(pl.program_id(0) < nv_ref[0])
    def _():
        h = _bf(x_ref[...])
        u = _dot(h, w1_ref[0])
        act = _bf(u * _sigmoid(u) * _dot(h, w3_ref[0]))
        o_ref[...] += _dot(act, w2_ref[0])


def _moe_ffn(tile_src, tile_expert, n_valid, xs, w1, w3, w2, tm, tf):
    rows, d = xs.shape
    ff = w1.shape[2]
    tok = pl.BlockSpec((tm, d), lambda i, f, src, te, nv: (src[i], 0))
    return pl.pallas_call(
        _moe_ffn_kernel,
        grid_spec=pltpu.PrefetchScalarGridSpec(
            num_scalar_prefetch=3,
            grid=(rows // tm, ff // tf),
            in_specs=[tok,
                      pl.BlockSpec((1, d, tf), lambda i, f, src, te, nv: (te[i], 0, f)),
                      pl.BlockSpec((1, d, tf), lambda i, f, src, te, nv: (te[i], 0, f)),
                      pl.BlockSpec((1, tf, d), lambda i, f, src, te, nv: (te[i], f, 0))],
            out_specs=pl.BlockSpec((tm, d), lambda i, f, src, te, nv: (i, 0)),
        ),
        out_shape=jax.ShapeDtypeStruct((rows, d), F32),
        compiler_params=_params("arbitrary", "arbitrary"),
        name="moe_experts",
    )(tile_src, tile_expert, n_valid, xs, w1, w3, w2)


def _moe_combine_kernel(dest_ref, x_ref, gates_ref, gain_ref, y_hbm, o_ref, buf, sem):
    n_tok = x_ref.shape[0]

    def issue(j, carry):
        for k in range(2):
            _row_copy(y_hbm, dest_ref[2 * j + k], buf.at[k], j, sem).start(priority=k)
        return carry

    def drain(j, carry):
        for k in range(2):
            _row_copy(y_hbm, 0, buf.at[k], 0, sem).wait()
        return carry

    lax.fori_loop(0, n_tok, issue, 0, unroll=DMA_UNROLL)
    lax.fori_loop(0, n_tok, drain, 0, unroll=DMA_UNROLL)
    gates = gates_ref[...]
    x = x_ref[...] + gates[:, 4:5] * buf[0] + gates[:, 5:6] * buf[1]
    o_ref[...] = _rms(x, gain_ref[...])


def _moe_combine(dest_flat, x, route, gain, ys, ts):
    t, d = x.shape
    tok = pl.BlockSpec((ts, d), lambda i: (i, 0))
    return pl.pallas_call(
        _moe_combine_kernel,
        grid=(t // ts,),
        in_specs=[pl.BlockSpec((2 * ts,), lambda i: (i,), memory_space=pltpu.SMEM),
                  tok, pl.BlockSpec((ts, LANES), lambda i: (i, 0)), _const_spec(gain.shape),
                  pl.BlockSpec(memory_space=pl.ANY)],
        out_specs=tok,
        out_shape=jax.ShapeDtypeStruct((t, d), F32),
        scratch_shapes=[pltpu.VMEM((2, ts, d), F32), pltpu.SemaphoreType.DMA(())],
        compiler_params=_params("arbitrary"),
        name="moe_combine",
    )(dest_flat, x, route, gain, ys)


def _tile(n, want):
    t = min(n, want)
    assert n % t == 0
    return t


def _pad_cols(w, n):
    return jnp.pad(w, ((0, 0), (0, n - w.shape[1])))


def _pad_rows(w, n):
    return jnp.pad(w, ((0, n - w.shape[0]), (0, 0)))


def kernel(x, norm_gain, rwkv_mu, rwkv_wr, rwkv_wk, rwkv_wv, rwkv_w0, rwkv_w1, rwkv_w2, rwkv_a0, rwkv_a1, rwkv_a2, rwkv_g1, rwkv_g2, rwkv_k_k, rwkv_k_a, rwkv_r_k, rwkv_lnx_w, rwkv_lnx_b, rwkv_wo, kv_norm_gain, w_kv, attn_wq, attn_wo, ffn_w1, ffn_w3, ffn_w2, moe_router, moe_router_bias, moe_w1, moe_w3, moe_w2, final_norm_gain):
    b, s, d = x.shape
    t = b * s
    assert norm_gain.shape[0] == 2 and d % LANES == 0 and s % 8 == 0

    zeros = jnp.zeros((d,), F32)
    vec_pre = jnp.stack([norm_gain[0, 0], rwkv_w0[0], rwkv_a0[0], rwkv_k_k[0], rwkv_k_a[0],
                         zeros, zeros, zeros])
    tm_pre = _tile(s, 256)
    chunk = _tile(s, LANES)
    r, lw, k, v, kk, a, g = _rwkv_pre(
        x, vec_pre, rwkv_mu[0], _bf(rwkv_wr[0]), _bf(rwkv_wk[0]), _bf(rwkv_wv[0]),
        _bf(_pad_cols(rwkv_w1[0], LANES)), _bf(_pad_rows(rwkv_w2[0], LANES)),
        _bf(_pad_cols(rwkv_a1[0], LANES)), _bf(_pad_rows(rwkv_a2[0], LANES)),
        _bf(rwkv_g1[0]), _bf(rwkv_g2[0]), tm_pre)
    vec_post = jnp.stack([rwkv_lnx_w[0], rwkv_lnx_b[0], rwkv_r_k[0].reshape(d),
                          zeros, zeros, zeros, zeros, zeros])
    z = _wkv(r, lw, k, v, kk, a, vec_post, chunk=chunk, seq_tile=_tile(s, 2048), n_pairs=4)

    flat = lambda arr: arr.reshape(t, d)
    tm = _tile(t, 512)
    x1 = _rwkv_post(flat(z), flat(g), flat(x), _bf(rwkv_wo[0]), tm)

    ff = ffn_w1.shape[2]
    tf = ff // 2 if (ff // 2) % LANES == 0 else ff
    x2 = _ffn(x1, norm_gain[0, 1][None, :], _bf(ffn_w1[0]), _bf(ffn_w3[0]), _bf(ffn_w2[0]),
              _tile(t, 2 * FFN_SUB_ROWS), tf)

    n_slopes = len(BRANCHES) * HEADS_PER_BRANCH
    slopes = jnp.exp2(-ALIBI_MAX * (jnp.arange(n_slopes, dtype=F32) + 1.0) / n_slopes)
    gains = jnp.stack([norm_gain[1, 0], kv_norm_gain] + [zeros] * 6)
    tm_a = _tile(s, 512)
    qkv = _attn_proj(x2.reshape(b, s, d), gains, _bf(attn_wq[0]), _bf(w_kv), tm_a)
    os_, ls_ = [], []
    for br in range(len(BRANCHES)):
        o_br, l_br = _attn_branch(slopes, *qkv[3 * br:3 * br + 3], br)
        os_.append(o_br)
        ls_.append(l_br)

    router = jnp.concatenate(_split2(_pad_cols(moe_router[0], LANES)), axis=1)
    rbias = jnp.full((1, LANES), NEG, F32).at[0, :N_EXPERTS].set(moe_router_bias[0])
    x3, h4, route, cnt = _attn_out(os_, ls_, x2, norm_gain[1, 1][None, :], _bf(attn_wo[0]),
                                   router, rbias, b, tm_a)

    tm_e = 512 if t >= 4096 else 128
    ts = _tile(t, 512)
    counts = cnt[0, :N_EXPERTS].astype(jnp.int32)
    padded = ((counts + tm_e - 1) // tm_e) * tm_e
    ends = jnp.cumsum(padded)
    meta = jnp.concatenate([ends - padded, ends]).astype(jnp.int32)
    route_i = route[:, :4].astype(jnp.int32)
    group_start = jnp.sum(jnp.where(route_i[:, :2, None] == jnp.arange(N_EXPERTS), ends - padded, 0), axis=-1)
    dest = (group_start + route_i[:, 2:4]).astype(jnp.int32).reshape(2 * t)
    rows = 2 * t + N_EXPERTS * tm_e
    n_tiles = rows // tm_e
    n_valid = (ends[-1] // tm_e).astype(jnp.int32)
    tile_src = jnp.minimum(jnp.arange(n_tiles, dtype=jnp.int32), n_valid - 1)
    tile_expert = jnp.minimum(
        jnp.sum(tile_src[:, None] * tm_e >= ends[None, :], axis=1), N_EXPERTS - 1).astype(jnp.int32)

    xs = _moe_scatter(meta, dest, h4, rows, ts, tm_e)
    ffe = moe_w1.shape[3]
    tfe = ffe // 2 if (ffe // 2) % LANES == 0 else ffe
    ys = _moe_ffn(tile_src, tile_expert, n_valid.reshape(1), xs,
                  _bf(moe_w1[0]), _bf(moe_w3[0]), _bf(moe_w2[0]), tm_e, tfe)
    out = _moe_combine(dest, x3, route, final_norm_gain[None, :], ys, ts)
    return out.reshape(b, s, d)
```

```python
import functools
import math

import jax
import jax.numpy as jnp
from jax import lax
from jax.experimental import pallas as pl
from jax.experimental.pallas import tpu as pltpu

F32 = jnp.float32
BF16 = jnp.bfloat16

RMS_EPS = 1e-5
GN_EPS = 64e-5
HEAD = 64
LANES = 128
BRANCHES = ((128, 1), (512, 4), (2048, 16))
HEADS_PER_BRANCH = 8
ALIBI_MAX = 8.0
N_EXPERTS = 8
NEG = -1e30
VMEM_LIMIT_BYTES = 56 * 1024 * 1024
TOKEN_TILE = 512
RWKV_PRE_TILE = 256
WKV_SEQ_TILE = 1024
WKV_PAIRS = 8


def _params(*sem):
    return pltpu.CompilerParams(dimension_semantics=sem, vmem_limit_bytes=VMEM_LIMIT_BYTES)


def _dot(a, b):
    return jnp.dot(a, b, preferred_element_type=F32)


def _dot_nt(a, b):
    return lax.dot_general(a, b, (((1,), (1,)), ((), ())), preferred_element_type=F32)


def _bf(x):
    return x.astype(BF16)


def _split2(x):
    hi = x.astype(BF16)
    lo = (x - hi.astype(F32)).astype(BF16)
    return hi, lo


def _sigmoid(z):
    return 1.0 / (1.0 + jnp.exp(-z))


def _rms(x, gain):
    return x * lax.rsqrt(jnp.mean(x * x, axis=-1, keepdims=True) + RMS_EPS) * gain


def _const_spec(shape):
    nd = len(shape)
    return pl.BlockSpec(shape, lambda *_: (0,) * nd)


def _rwkv_pre_kernel(x_ref, xp_ref, vec_ref, mu_ref, wr_ref, wk_ref, wv_ref, w1_ref, w2_ref,
                     a1_ref, a2_ref, g1_ref, g2_ref,
                     r_out, lw_out, k_out, v_out, kk_out, a_out, g_out):
    i = pl.program_id(1)
    vec = vec_ref[...]
    gain, w0, a0, k_k, k_a = (vec[j:j + 1] for j in range(5))
    x = x_ref[0]
    h = _rms(x, gain)
    hp = _rms(xp_ref[0][7:8, :], gain)
    hp = jnp.where(i > 0, hp, 0.0)
    rows = lax.broadcasted_iota(jnp.int32, h.shape, 0)
    hprev = jnp.where(rows == 0, hp, pltpu.roll(h, 1, 0))
    xx = hprev - h
    mu = mu_ref[...]
    xr, xw, xk, xv, xa, xg = (_bf(h + xx * mu[j:j + 1]) for j in range(6))
    r = _dot(xr, wr_ref[...])
    k = _dot(xk, wk_ref[...])
    v = _dot(xv, wv_ref[...])
    wl = w0 + _dot(_bf(jnp.tanh(_dot(xw, w1_ref[...]))), w2_ref[...])
    a = _sigmoid(a0 + _dot(_bf(_dot(xa, a1_ref[...])), a2_ref[...]))
    g = _dot(_bf(_sigmoid(_dot(xg, g1_ref[...]))), g2_ref[...])
    lw_out[0] = _sigmoid(wl) * (-math.exp(-0.5))
    r_out[0] = _bf(r)
    k_out[0] = _bf(k * (1.0 + (a - 1.0) * k_a))
    v_out[0] = _bf(v)
    kk_out[0] = _bf(k * k_k)
    a_out[0] = _bf(a)
    g_out[0] = _bf(g)


def _rwkv_pre(x, vec, mu, wr, wk, wv, w1, w2, a1, a2, g1, g2, tm):
    b, s, d = x.shape
    tok = pl.BlockSpec((1, tm, d), lambda bi, i: (bi, i, 0))
    prev = pl.BlockSpec((1, 8, d), lambda bi, i: (bi, jnp.maximum(i * (tm // 8) - 1, 0), 0))
    consts = [vec, mu, wr, wk, wv, w1, w2, a1, a2, g1, g2]
    out = lambda dt: jax.ShapeDtypeStruct((b, s, d), dt)
    return pl.pallas_call(
        _rwkv_pre_kernel,
        grid=(b, s // tm),
        in_specs=[tok, prev] + [_const_spec(c.shape) for c in consts],
        out_specs=[tok] * 7,
        out_shape=[out(BF16), out(F32)] + [out(BF16)] * 5,
        compiler_params=_params("parallel", "arbitrary"),
        name="rwkv_pre",
    )(x, x, *consts)


def _round_robin(generators):
    results = [None] * len(generators)
    live = list(enumerate(generators))
    while live:
        still = []
        for idx, gen in live:
            try:
                results[idx] = next(gen)
                still.append((idx, gen))
            except StopIteration:
                pass
        live = still
    return results


def _wkv_kernel(r_ref, lw_ref, k_ref, v_ref, kk_ref, a_ref, vec_ref, y_ref, st_ref, raw_ref, *, chunk):
    c_len = chunk
    n_chunks = r_ref.shape[1] // c_len
    n_pairs = r_ref.shape[2] // LANES
    inv_steps = max(c_len.bit_length() - 2, 0)
    lane = lax.broadcasted_iota(jnp.int32, (c_len, LANES), 1)
    head0 = lane < HEAD
    rr = lax.broadcasted_iota(jnp.int32, (c_len, c_len), 0)
    cc = lax.broadcasted_iota(jnp.int32, (c_len, c_len), 1)
    lower = rr >= cc
    strict = rr > cc
    eye = jnp.where(rr == cc, 1.0, 0.0).astype(F32)
    r2 = lax.broadcasted_iota(jnp.int32, (LANES, LANES), 0)
    c2 = lax.broadcasted_iota(jnp.int32, (LANES, LANES), 1)
    same_head = (r2 < HEAD) == (c2 < HEAD)

    @pl.when(pl.program_id(2) == 0)
    def _():
        st_ref[...] = jnp.zeros_like(st_ref)
        raw_ref[...] = jnp.zeros_like(raw_ref)

    row_index = lax.broadcasted_iota(jnp.int32, (c_len, LANES), 0)
    scan_shifts = [1 << j for j in range(c_len.bit_length() - 1)]
    rows = lambda mat, j: mat[j * c_len:(j + 1) * c_len]
    own_head = lambda mat: jnp.where(head0, rows(mat, 0), rows(mat, 1))

    def head_sum(x):
        s0 = jnp.sum(jnp.where(head0, x, 0.0), axis=-1, keepdims=True)
        s1 = jnp.sum(jnp.where(head0, 0.0, x), axis=-1, keepdims=True)
        return jnp.where(head0, s0, s1)

    def chunk_pair(sl, lanes, state):
        cum = lw_ref[0, sl, lanes]
        for shift in scan_shifts:
            cum = cum + jnp.where(row_index >= shift, pltpu.roll(cum, shift, 0), 0.0)
        cum_prev = jnp.where(row_index == 0, 0.0, pltpu.roll(cum, 1, 0))
        cum_last = cum[c_len - 1:c_len, :]
        kk = kk_ref[0, sl, lanes].astype(F32)
        kk = kk / jnp.maximum(jnp.sqrt(head_sum(kk * kk)), 1e-12)
        b = kk * a_ref[0, sl, lanes].astype(F32)
        k = k_ref[0, sl, lanes].astype(F32)
        vb = v_ref[0, sl, lanes]
        e_neg = jnp.exp(-cum)
        e_end = jnp.exp(cum_last - cum)
        a_t = -kk * jnp.exp(cum_prev)
        r_t = r_ref[0, sl, lanes].astype(F32) * jnp.exp(cum)
        lhs = _bf(jnp.concatenate([jnp.where(head0, a_t, 0.0), jnp.where(head0, 0.0, a_t),
                                   jnp.where(head0, r_t, 0.0), jnp.where(head0, 0.0, r_t)], axis=0))
        gram = _dot_nt(lhs, _bf(jnp.concatenate([b * e_neg, k * e_neg], axis=0)))
        from_state = _dot_nt(_bf(jnp.concatenate([a_t, r_t], axis=0)), _bf(state))
        yield
        gram_b, gram_k = gram[:, :c_len], gram[:, c_len:]
        from_v = _dot(_bf(jnp.concatenate(
            [jnp.where(strict, rows(gram_k, 0), 0.0), jnp.where(strict, rows(gram_k, 1), 0.0),
             jnp.where(lower, rows(gram_k, 2), 0.0), jnp.where(lower, rows(gram_k, 3), 0.0)], axis=0)), vb)
        m_rb = _bf(jnp.concatenate([jnp.where(lower, rows(gram_b, 2), 0.0),
                                    jnp.where(lower, rows(gram_b, 3), 0.0)], axis=0))
        w = rows(from_state, 0) + own_head(from_v[:2 * c_len])
        y1 = rows(from_state, 1) + own_head(from_v[2 * c_len:])
        pw = [jnp.where(strict, rows(gram_b, hd), 0.0) for hd in range(2)]
        inv = [eye + pw[hd] for hd in range(2)]
        if inv_steps:
            pw = [_dot(_bf(pw[hd]), _bf(pw[hd])) for hd in range(2)]
        for step in range(inv_steps):
            yield
            if step < inv_steps - 1:
                both = [_dot(_bf(jnp.concatenate([inv[hd], pw[hd]], axis=0)), _bf(pw[hd])) for hd in range(2)]
                inv = [inv[hd] + rows(both[hd], 0) for hd in range(2)]
                pw = [rows(both[hd], 1) for hd in range(2)]
            else:
                inv = [inv[hd] + _dot(_bf(inv[hd]), _bf(pw[hd])) for hd in range(2)]
        yield
        u = own_head(_dot(_bf(jnp.concatenate(inv, axis=0)), _bf(w)))
        yield
        y = y1 + own_head(_dot(m_rb, _bf(u)))
        uv_t = _bf(jnp.transpose(jnp.concatenate([u, vb.astype(F32)], axis=0)))
        bk = _bf(jnp.concatenate([b * e_end, k * e_end], axis=0))
        yield y, state * jnp.exp(cum_last) + jnp.where(same_head, _dot(uv_t, bk), 0.0)

    def finish(sl, lanes, y):
        lnx_w, lnx_b, r_k = (vec_ref[j:j + 1, lanes] for j in range(3))
        r, k, v = (ref[0, sl, lanes].astype(F32) for ref in (r_ref, k_ref, v_ref))
        dy = y - head_sum(y) * (1.0 / HEAD)
        yn = dy * lax.rsqrt(head_sum(dy * dy) * (1.0 / HEAD) + GN_EPS) * lnx_w + lnx_b
        y_ref[0, sl, lanes] = _bf(yn + head_sum(r * k * r_k) * v)

    pair_lanes = [slice(p * LANES, (p + 1) * LANES) for p in range(n_pairs)]
    chunk_rows = lambda c: pl.ds(pl.multiple_of(c * c_len, c_len), c_len)

    def body(c, carry):
        before = chunk_rows(jnp.maximum(c - 1, 0))
        for p in range(n_pairs):
            finish(before, pair_lanes[p], raw_ref[p])
        sl = chunk_rows(c)
        results = _round_robin([chunk_pair(sl, pair_lanes[p], st_ref[p]) for p in range(n_pairs)])
        for p, (y, new_state) in enumerate(results):
            raw_ref[p] = y
            st_ref[p] = new_state
        return carry

    lax.fori_loop(0, n_chunks, body, 0, unroll=2 if n_chunks % 2 == 0 else 1)
    for p in range(n_pairs):
        finish(chunk_rows(n_chunks - 1), pair_lanes[p], raw_ref[p])


def _wkv(r, lw, k, v, kk, a, vec, chunk, seq_tile, n_pairs):
    b, s, d = r.shape
    width = n_pairs * LANES
    spec = pl.BlockSpec((1, seq_tile, width), lambda bi, hi, si: (bi, si, hi))
    return pl.pallas_call(
        functools.partial(_wkv_kernel, chunk=chunk),
        grid=(b, d // width, s // seq_tile),
        in_specs=[spec] * 6 + [pl.BlockSpec((vec.shape[0], width), lambda bi, hi, si: (0, hi))],
        out_specs=spec,
        out_shape=jax.ShapeDtypeStruct((b, s, d), BF16),
        scratch_shapes=[pltpu.VMEM((n_pairs, LANES, LANES), F32),
                        pltpu.VMEM((n_pairs, chunk, LANES), F32)],
        compiler_params=_params("parallel", "parallel", "arbitrary"),
        name="wkv7",
    )(r, lw, k, v, kk, a, vec)


def _rwkv_post_kernel(z_ref, g_ref, x_ref, wo_ref, o_ref):
    o_ref[...] = x_ref[...] + _dot(z_ref[...] * g_ref[...], wo_ref[...])


def _rwkv_post(z, g, x, wo, tm):
    t, d = x.shape
    tok = pl.BlockSpec((tm, d), lambda i: (i, 0))
    return pl.pallas_call(
        _rwkv_post_kernel,
        grid=(t // tm,),
        in_specs=[tok] * 3 + [_const_spec(wo.shape)],
        out_specs=tok,
        out_shape=jax.ShapeDtypeStruct((t, d), F32),
        compiler_params=_params("parallel"),
        name="rwkv_post",
    )(z, g, x, wo)


FFN_SUB_ROWS = 512


def _ffn_kernel(x_ref, gain_ref, w1_ref, w3_ref, w2_ref, o_ref, h_ref):
    @pl.when(pl.program_id(1) == 0)
    def _():
        h_ref[...] = _bf(_rms(x_ref[...], gain_ref[...]))
        o_ref[...] = x_ref[...]

    for start in range(0, h_ref.shape[0], FFN_SUB_ROWS):
        rows = slice(start, min(start + FFN_SUB_ROWS, h_ref.shape[0]))
        h = h_ref[rows, :]
        u = _dot(h, w1_ref[...])
        act = _bf(u * _sigmoid(u) * _dot(h, w3_ref[...]))
        o_ref[rows, :] += _dot(act, w2_ref[...])


def _ffn(x, gain, w1, w3, w2, tm, tf):
    t, d = x.shape
    ff = w1.shape[1]
    tok = pl.BlockSpec((tm, d), lambda i, f: (i, 0))
    return pl.pallas_call(
        _ffn_kernel,
        grid=(t // tm, ff // tf),
        in_specs=[tok, _const_spec(gain.shape),
                  pl.BlockSpec((d, tf), lambda i, f: (0, f)),
                  pl.BlockSpec((d, tf), lambda i, f: (0, f)),
                  pl.BlockSpec((tf, d), lambda i, f: (f, 0))],
        out_specs=tok,
        out_shape=jax.ShapeDtypeStruct((t, d), F32),
        scratch_shapes=[pltpu.VMEM((tm, d), BF16)],
        compiler_params=_params("parallel", "arbitrary"),
        name="ffn_dense",
    )(x, gain, w1, w3, w2)


BRANCH_WIDTH = HEADS_PER_BRANCH * HEAD


def _attn_proj_kernel(x_ref, gains_ref, wq_ref, wkv_ref, *refs):
    outs, (q_scr, kv_scr) = refs[:-2], refs[-2:]
    x = x_ref[0]
    tm = x.shape[0]
    n = x * lax.rsqrt(jnp.mean(x * x, axis=-1, keepdims=True) + RMS_EPS)
    gains = gains_ref[...]
    q_tiles = q_scr.shape[0]
    qw = q_tiles * LANES
    tiles_br = BRANCH_WIDTH // LANES
    hq = _bf(n * gains[0:1])
    hkv = _bf(n * gains[1:2])

    def project(h, w_ref, col, scr, first_tile, scale=None):
        val = _dot(h, w_ref[:, col:col + BRANCH_WIDTH])
        for ti in range(tiles_br):
            tile = val[:, ti * LANES:(ti + 1) * LANES]
            scr[first_tile + ti] = tile if scale is None else tile * scale

    def permute(out, dil, scr, first_tile):
        for res in range(dil):
            rows = pl.ds(res, tm // dil, stride=dil) if dil > 1 else slice(None)
            for ti in range(tiles_br):
                out[0, res, :, ti * LANES:(ti + 1) * LANES] = _bf(scr[first_tile + ti, rows, :])

    pending = None
    for which, (h, w_ref, base, scr, tile0, scale) in enumerate(
            ((hq, wq_ref, 0, q_scr, 0, 1.0 / HEAD ** 0.5), (hkv, wkv_ref, 0, kv_scr, 0, None),
             (hkv, wkv_ref, qw, kv_scr, q_tiles, None))):
        for br, (_, dil) in enumerate(BRANCHES):
            project(h, w_ref, base + br * BRANCH_WIDTH, scr, tile0 + br * tiles_br, scale)
            if pending is not None:
                permute(*pending)
            pending = (outs[3 * br + which], dil, scr, tile0 + br * tiles_br)
    permute(*pending)


def _attn_proj(x, gains, wq, wkv, tm):
    b, s, d = x.shape
    qw = wq.shape[1]
    out_specs, out_shape = [], []
    for _, dil in BRANCHES:
        assert tm % (16 * dil) == 0
        out_specs += [pl.BlockSpec((1, dil, tm // dil, BRANCH_WIDTH), lambda bi, i: (bi, 0, i, 0))] * 3
        out_shape += [jax.ShapeDtypeStruct((b, dil, s // dil, BRANCH_WIDTH), BF16)] * 3
    return pl.pallas_call(
        _attn_proj_kernel,
        grid=(b, s // tm),
        in_specs=[pl.BlockSpec((1, tm, d), lambda bi, i: (bi, i, 0)),
                  _const_spec(gains.shape), _const_spec(wq.shape), _const_spec(wkv.shape)],
        out_specs=out_specs,
        out_shape=out_shape,
        scratch_shapes=[pltpu.VMEM((qw // LANES, tm, LANES), F32),
                        pltpu.VMEM((2 * qw // LANES, tm, LANES), F32)],
        compiler_params=_params("parallel", "parallel"),
        name="attn_proj",
    )(x, gains, wq, wkv)


def _attn_kernel(slopes_ref, q_ref, k_ref, v_ref, o_ref, lse_ref, *, branch, dilation, qb):
    sub_len = q_ref.shape[2]
    n_blk = sub_len // qb
    n_pairs = q_ref.shape[3] // LANES
    col2 = lax.broadcasted_iota(jnp.int32, (qb, 2 * qb), 1)
    rel = (lax.broadcasted_iota(jnp.int32, (qb, 2 * qb), 0) - jnp.where(col2 < qb, col2, col2 - qb)).astype(F32)
    dist_d = jnp.where(rel >= 0, rel, -NEG)
    dist_p = jnp.where(rel <= 0, rel + float(qb), -NEG)
    first_head = lax.broadcasted_iota(jnp.int32, (1, 2 * qb), 1) < qb
    lane = lax.broadcasted_iota(jnp.int32, (qb, LANES), 1)
    head0 = lane < HEAD
    lane_row = lax.broadcasted_iota(jnp.int32, (1, LANES), 1)
    head_mask = [jnp.where(lane_row < HEAD, 1.0, 0.0).astype(BF16),
                 jnp.where(lane_row < HEAD, 0.0, 1.0).astype(BF16)]
    head_ones = [jnp.where(head0, 1.0, 0.0).astype(BF16), jnp.where(head0, 0.0, 1.0).astype(BF16)]

    def block(residues, i):
        cur = pl.ds(pl.multiple_of(i * qb, qb), qb)
        if n_blk > 1:
            prv = pl.ds(pl.multiple_of(jnp.maximum(i - 1, 0) * qb, qb), qb)
            has_prev = jnp.where(i > 0, 1.0, -NEG)

        def one_pair(r, p):
            lanes = slice(p * LANES, (p + 1) * LANES)
            slope = [slopes_ref[branch * HEADS_PER_BRANCH + 2 * p + j] * float(dilation) for j in range(2)]
            step = jnp.where(first_head, slope[0], slope[1])

            def keys(rows):
                k = k_ref[0, r, rows, lanes]
                return jnp.concatenate([k * head_mask[0], k * head_mask[1]], axis=0)

            def values(rows):
                v = v_ref[0, r, rows, lanes]
                return jnp.concatenate(
                    [jnp.concatenate([v * head_mask[0], head_ones[0]], axis=1),
                     jnp.concatenate([v * head_mask[1], head_ones[1]], axis=1)], axis=0)

            q = q_ref[0, r, cur, lanes]
            sd = _dot_nt(q, keys(cur)) - step * dist_d
            if n_blk > 1:
                sp = _dot_nt(q, keys(prv)) - (step * has_prev) * dist_p
            yield
            both = jnp.maximum(sd, sp) if n_blk > 1 else sd
            m = [jnp.max(both[:, :qb], axis=-1, keepdims=True), jnp.max(both[:, qb:], axis=-1, keepdims=True)]
            probs = lambda s: _bf(jnp.concatenate([jnp.exp(s[:, :qb] - m[0]), jnp.exp(s[:, qb:] - m[1])], axis=1))
            acc = _dot(probs(sd), values(cur))
            if n_blk > 1:
                acc = acc + _dot(probs(sp), values(prv))
            yield
            den = acc[:, LANES:]
            lse = jnp.where(head0, m[0], m[1]) + jnp.log(den)
            yield acc[:, :LANES] / den, lse[:, 0:1], lse[:, HEAD:HEAD + 1]

        res = _round_robin([one_pair(r, p) for r in residues for p in range(n_pairs)])
        for n, r in enumerate(residues):
            lse = jnp.zeros((qb, LANES), F32)
            for p in range(n_pairs):
                out, l_a, l_b = res[n * n_pairs + p]
                o_ref[0, r, cur, p * LANES:(p + 1) * LANES] = _bf(out)
                lse = jnp.where(lane == 2 * p, l_a, jnp.where(lane == 2 * p + 1, l_b, lse))
            lse_ref[0, r, cur, :] = lse

    n_res = q_ref.shape[1]
    if n_blk == 1:
        block(range(n_res), 0)
    else:
        for r in range(n_res):
            lax.fori_loop(0, n_blk, lambda i, c, r=r: (block([r], i), c)[1], 0,
                          unroll=2 if n_blk % 2 == 0 else 1)


def _attn_branch(slopes, q, k, v, branch):
    window, dilation = BRANCHES[branch]
    assert window % dilation == 0 and window // dilation == LANES
    b, _, sub_len, width = q.shape
    qb = min(LANES, sub_len)
    assert sub_len % qb == 0
    n_res = max(1, min(dilation, TOKEN_TILE // sub_len))
    assert dilation % n_res == 0
    spec = pl.BlockSpec((1, n_res, sub_len, width), lambda bi, ri: (bi, ri, 0, 0))
    return pl.pallas_call(
        functools.partial(_attn_kernel, branch=branch, dilation=dilation, qb=qb),
        grid=(b, dilation // n_res),
        in_specs=[pl.BlockSpec(memory_space=pltpu.SMEM)] + [spec] * 3,
        out_specs=[spec, pl.BlockSpec((1, n_res, sub_len, LANES), lambda bi, ri: (bi, ri, 0, 0))],
        out_shape=[jax.ShapeDtypeStruct(q.shape, BF16),
                   jax.ShapeDtypeStruct(q.shape[:3] + (LANES,), F32)],
        compiler_params=_params("parallel", "parallel"),
        name=f"dilated_attn_{branch}",
    )(slopes, q, k, v)


def _attn_out_kernel(o0, o1, o2, l0, l1, l2, x_ref, gain_ref, wo_ref, router_ref, rb_ref, expand_ref,
                     x_out, h_out, route_out, cnt_out, cnt_ref, o_scr, l_scr):
    @pl.when((pl.program_id(0) == 0) & (pl.program_id(1) == 0))
    def _():
        cnt_ref[...] = jnp.zeros_like(cnt_ref)

    tm = x_ref.shape[0]
    tiles_br = BRANCH_WIDTH // LANES
    for br, (o_ref, l_ref) in enumerate(((o0, l0), (o1, l1), (o2, l2))):
        dil = BRANCHES[br][1]
        for res in range(dil):
            rows = pl.ds(res, tm // dil, stride=dil) if dil > 1 else slice(None)
            l_scr[br, rows, :] = l_ref[0, res]
            for ti in range(tiles_br):
                lanes = slice(ti * LANES, (ti + 1) * LANES)
                o_scr[br, ti, rows, :] = o_ref[0, res, :, lanes].astype(F32)
    ls = [l_scr[br] for br in range(3)]
    m = jnp.maximum(jnp.maximum(ls[0], ls[1]), ls[2])
    ws = [jnp.exp(l - m) for l in ls]
    total = ws[0] + ws[1] + ws[2]
    spread = [_dot(_bf(ws[br] / total), expand_ref[...]) for br in range(3)]
    merged = []
    for ti in range(tiles_br):
        lanes = slice(ti * LANES, (ti + 1) * LANES)
        merged.append(_bf(sum(spread[br][:, lanes] * o_scr[br, ti] for br in range(3))))
    x = x_ref[...] + _dot(jnp.concatenate(merged, axis=1), wo_ref[...])
    x_out[...] = x
    h = _rms(x, gain_ref[...])
    h_out[...] = h
    hhi, hlo = _split2(h)
    both = _dot(hhi, router_ref[...])
    logits = both[:, :LANES] + both[:, LANES:] + _dot(hlo, router_ref[:, :LANES]) + rb_ref[...]
    lane = lax.broadcasted_iota(jnp.int32, logits.shape, 1)
    m1 = jnp.max(logits, axis=-1, keepdims=True)
    i1 = jnp.min(jnp.where(logits == m1, lane, LANES), axis=-1, keepdims=True)
    rest = jnp.where(lane == i1, -3e38, logits)
    m2 = jnp.max(rest, axis=-1, keepdims=True)
    i2 = jnp.min(jnp.where(rest == m2, lane, LANES), axis=-1, keepdims=True)
    ex = jnp.exp(m2 - m1)
    sel = jnp.where((lane == i1) | (lane == i2), 1.0, 0.0)
    rr = lax.broadcasted_iota(jnp.int32, (tm, tm), 0)
    cc = lax.broadcasted_iota(jnp.int32, (tm, tm), 1)
    before = jnp.where(rr > cc, 1.0, 0.0).astype(BF16)
    rank = cnt_ref[...] + _dot(before, _bf(sel))
    cnt_ref[...] += jnp.sum(sel, axis=0, keepdims=True)
    cnt_out[...] = cnt_ref[...]
    columns = [i1.astype(F32), i2.astype(F32),
               jnp.sum(jnp.where(lane == i1, rank, 0.0), axis=-1, keepdims=True),
               jnp.sum(jnp.where(lane == i2, rank, 0.0), axis=-1, keepdims=True),
               1.0 / (1.0 + ex), ex / (1.0 + ex)]
    route = jnp.zeros(logits.shape, F32)
    for j, col in enumerate(columns):
        route = jnp.where(lane == j, col, route)
    route_out[...] = route


def _attn_out(os_, ls_, x, gain, wo, router, rb, b, tm):
    t, d = x.shape
    tiles = t // (b * tm)
    tok = pl.BlockSpec((tm, d), lambda bi, i: (bi * tiles + i, 0))
    small = pl.BlockSpec((tm, LANES), lambda bi, i: (bi * tiles + i, 0))
    att = [pl.BlockSpec((1, dil, tm // dil, BRANCH_WIDTH), lambda bi, i: (bi, 0, i, 0))
           for _, dil in BRANCHES]
    lse = [pl.BlockSpec((1, dil, tm // dil, LANES), lambda bi, i: (bi, 0, i, 0)) for _, dil in BRANCHES]
    expand = (jnp.arange(LANES)[:, None] == jnp.arange(BRANCH_WIDTH)[None, :] // HEAD).astype(BF16)
    consts = [gain, wo, router, rb, expand]
    return pl.pallas_call(
        _attn_out_kernel,
        grid=(b, tiles),
        in_specs=att + lse + [tok] + [_const_spec(c.shape) for c in consts],
        out_specs=[tok, tok, small, _const_spec((1, LANES))],
        out_shape=[jax.ShapeDtypeStruct((t, d), F32), jax.ShapeDtypeStruct((t, d), F32),
                   jax.ShapeDtypeStruct((t, LANES), F32), jax.ShapeDtypeStruct((1, LANES), F32)],
        scratch_shapes=[pltpu.VMEM((1, LANES), F32),
                        pltpu.VMEM((len(BRANCHES), BRANCH_WIDTH // LANES, tm, LANES), F32),
                        pltpu.VMEM((len(BRANCHES), tm, LANES), F32)],
        compiler_params=_params("arbitrary", "arbitrary"),
        name="attn_out_route",
    )(*os_, *ls_, x, *consts)


DMA_UNROLL = 8


def _row_copy(src, src_row, dst, dst_row, sem):
    return pltpu.make_async_copy(src.at[pl.ds(src_row, 1)], dst.at[pl.ds(dst_row, 1)], sem)


def _moe_scatter_kernel(meta_ref, dest_ref, h_ref, out_ref, zeros, sem, zero_sem):
    n_tok = h_ref.shape[0]
    tile = zeros.shape[0]
    ends = lambda e: meta_ref[N_EXPERTS + e]

    def zero_fill(e):
        if e < N_EXPERTS:
            start, used = jnp.maximum(ends(e) - tile, 0), ends(e) > meta_ref[e]
        else:
            start = ends(N_EXPERTS - 1) + (e - N_EXPERTS) * tile
            used = start < out_ref.shape[0]
            start = jnp.minimum(start, out_ref.shape[0] - tile)
        dst = out_ref.at[pl.ds(pl.multiple_of(start, tile), tile)]
        return used, pltpu.make_async_copy(zeros, dst, zero_sem)

    @pl.when(pl.program_id(0) == 0)
    def _():
        zeros[...] = jnp.zeros_like(zeros)
        for e in range(2 * N_EXPERTS):
            used, copy = zero_fill(e)
            pl.when(used)(copy.start)
        for e in range(2 * N_EXPERTS):
            used, copy = zero_fill(e)
            pl.when(used)(copy.wait)

    def issue(j, carry):
        for k in range(2):
            _row_copy(h_ref, j, out_ref, dest_ref[2 * j + k], sem).start(priority=k)
        return carry

    def drain(j, carry):
        for _ in range(2):
            _row_copy(h_ref, 0, out_ref, 0, sem).wait()
        return carry

    lax.fori_loop(0, n_tok, issue, 0, unroll=DMA_UNROLL)
    lax.fori_loop(0, n_tok, drain, 0, unroll=DMA_UNROLL)


def _moe_scatter(meta, dest_flat, h, rows, ts, tile):
    t, d = h.shape
    return pl.pallas_call(
        _moe_scatter_kernel,
        grid=(t // ts,),
        in_specs=[pl.BlockSpec(memory_space=pltpu.SMEM),
                  pl.BlockSpec((2 * ts,), lambda i: (i,), memory_space=pltpu.SMEM),
                  pl.BlockSpec((ts, d), lambda i: (i, 0))],
        out_specs=pl.BlockSpec(memory_space=pl.ANY),
        out_shape=jax.ShapeDtypeStruct((rows, d), F32),
        scratch_shapes=[pltpu.VMEM((tile, d), F32), pltpu.SemaphoreType.DMA(()),
                        pltpu.SemaphoreType.DMA(())],
        compiler_params=_params("arbitrary"),
        name="moe_scatter",
    )(meta, dest_flat, h)


def _moe_ffn_kernel(src_ref, te_ref, nv_ref, x_ref, w1_ref, w3_ref, w2_ref, o_ref):
    del src_ref, te_ref
    @pl.when(pl.program_id(1) == 0)
    def _():
        o_ref[...] = jnp.zeros_like(o_ref)

    @pl.when(pl.program_id(0) < nv_ref[0])
    def _():
        h = _bf(x_ref[...])
        u = _dot(h, w1_ref[0])
        act = _bf(u * _sigmoid(u) * _dot(h, w3_ref[0]))
        o_ref[...] += _dot(act, w2_ref[0])


def _moe_ffn(tile_src, tile_expert, n_valid, xs, w1, w3, w2, tm, tf):
    rows, d = xs.shape
    ff = w1.shape[2]
    tok = pl.BlockSpec((tm, d), lambda i, f, src, te, nv: (src[i], 0))
    return pl.pallas_call(
        _moe_ffn_kernel,
        grid_spec=pltpu.PrefetchScalarGridSpec(
            num_scalar_prefetch=3,
            grid=(rows // tm, ff // tf),
            in_specs=[tok,
                      pl.BlockSpec((1, d, tf), lambda i, f, src, te, nv: (te[i], 0, f)),
                      pl.BlockSpec((1, d, tf), lambda i, f, src, te, nv: (te[i], 0, f)),
                      pl.BlockSpec((1, tf, d), lambda i, f, src, te, nv: (te[i], f, 0))],
            out_specs=pl.BlockSpec((tm, d), lambda i, f, src, te, nv: (i, 0)),
        ),
        out_shape=jax.ShapeDtypeStruct((rows, d), F32),
        compiler_params=_params("arbitrary", "arbitrary"),
        name="moe_experts",
    )(tile_src, tile_expert, n_valid, xs, w1, w3, w2)


def _moe_combine_kernel(dest_ref, x_ref, gates_ref, gain_ref, y_hbm, o_ref, buf, sem):
    n_tok = x_ref.shape[0]

    def issue(j, carry):
        for k in range(2):
            _row_copy(y_hbm, dest_ref[2 * j + k], buf.at[k], j, sem).start(priority=k)
        return carry

    def drain(j, carry):
        for k in range(2):
            _row_copy(y_hbm, 0, buf.at[k], 0, sem).wait()
        return carry

    lax.fori_loop(0, n_tok, issue, 0, unroll=DMA_UNROLL)
    lax.fori_loop(0, n_tok, drain, 0, unroll=DMA_UNROLL)
    gates = gates_ref[...]
    x = x_ref[...] + gates[:, 4:5] * buf[0] + gates[:, 5:6] * buf[1]
    o_ref[...] = _rms(x, gain_ref[...])


def _moe_combine(dest_flat, x, route, gain, ys, ts):
    t, d = x.shape
    tok = pl.BlockSpec((ts, d), lambda i: (i, 0))
    return pl.pallas_call(
        _moe_combine_kernel,
        grid=(t // ts,),
        in_specs=[pl.BlockSpec((2 * ts,), lambda i: (i,), memory_space=pltpu.SMEM),
                  tok, pl.BlockSpec((ts, LANES), lambda i: (i, 0)), _const_spec(gain.shape),
                  pl.BlockSpec(memory_space=pl.ANY)],
        out_specs=tok,
        out_shape=jax.ShapeDtypeStruct((t, d), F32),
        scratch_shapes=[pltpu.VMEM((2, ts, d), F32), pltpu.SemaphoreType.DMA(())],
        compiler_params=_params("arbitrary"),
        name="moe_combine",
    )(dest_flat, x, route, gain, ys)


def _tile(n, want):
    t = min(n, want)
    assert n % t == 0
    return t


def _pad_cols(w, n):
    return jnp.pad(w, ((0, 0), (0, n - w.shape[1])))


def _pad_rows(w, n):
    return jnp.pad(w, ((0, n - w.shape[0]), (0, 0)))


def kernel(x, norm_gain, rwkv_mu, rwkv_wr, rwkv_wk, rwkv_wv, rwkv_w0, rwkv_w1, rwkv_w2, rwkv_a0, rwkv_a1, rwkv_a2, rwkv_g1, rwkv_g2, rwkv_k_k, rwkv_k_a, rwkv_r_k, rwkv_lnx_w, rwkv_lnx_b, rwkv_wo, kv_norm_gain, w_kv, attn_wq, attn_wo, ffn_w1, ffn_w3, ffn_w2, moe_router, moe_router_bias, moe_w1, moe_w3, moe_w2, final_norm_gain):
    b, s, d = x.shape
    t = b * s
    assert norm_gain.shape[0] == 2 and d % LANES == 0 and s % 8 == 0

    zeros = jnp.zeros((d,), F32)
    vec_pre = jnp.stack([norm_gain[0, 0], rwkv_w0[0], rwkv_a0[0], rwkv_k_k[0], rwkv_k_a[0],
                         zeros, zeros, zeros])
    tm_pre = _tile(s, RWKV_PRE_TILE)
    chunk = _tile(s, LANES)
    r, lw, k, v, kk, a, g = _rwkv_pre(
        x, vec_pre, rwkv_mu[0], _bf(rwkv_wr[0]), _bf(rwkv_wk[0]), _bf(rwkv_wv[0]),
        _bf(_pad_cols(rwkv_w1[0], LANES)), _bf(_pad_rows(rwkv_w2[0], LANES)),
        _bf(_pad_cols(rwkv_a1[0], LANES)), _bf(_pad_rows(rwkv_a2[0], LANES)),
        _bf(rwkv_g1[0]), _bf(rwkv_g2[0]), tm_pre)
    vec_post = jnp.stack([rwkv_lnx_w[0], rwkv_lnx_b[0], rwkv_r_k[0].reshape(d),
                          zeros, zeros, zeros, zeros, zeros])
    z = _wkv(r, lw, k, v, kk, a, vec_post, chunk=chunk, seq_tile=_tile(s, WKV_SEQ_TILE),
             n_pairs=WKV_PAIRS)

    flat = lambda arr: arr.reshape(t, d)
    tm = _tile(t, TOKEN_TILE)
    x1 = _rwkv_post(flat(z), flat(g), flat(x), _bf(rwkv_wo[0]), tm)

    ff = ffn_w1.shape[2]
    tf = ff // 2 if (ff // 2) % LANES == 0 else ff
    x2 = _ffn(x1, norm_gain[0, 1][None, :], _bf(ffn_w1[0]), _bf(ffn_w3[0]), _bf(ffn_w2[0]),
              _tile(t, 2 * FFN_SUB_ROWS), tf)

    n_slopes = len(BRANCHES) * HEADS_PER_BRANCH
    slopes = jnp.exp2(-ALIBI_MAX * (jnp.arange(n_slopes, dtype=F32) + 1.0) / n_slopes)
    gains = jnp.stack([norm_gain[1, 0], kv_norm_gain] + [zeros] * 6)
    tm_a = _tile(s, TOKEN_TILE)
    qkv = _attn_proj(x2.reshape(b, s, d), gains, _bf(attn_wq[0]), _bf(w_kv), tm_a)
    os_, ls_ = [], []
    for br in range(len(BRANCHES)):
        o_br, l_br = _attn_branch(slopes, *qkv[3 * br:3 * br + 3], br)
        os_.append(o_br)
        ls_.append(l_br)

    router = jnp.concatenate(_split2(_pad_cols(moe_router[0], LANES)), axis=1)
    rbias = jnp.full((1, LANES), NEG, F32).at[0, :N_EXPERTS].set(moe_router_bias[0])
    x3, h4, route, cnt = _attn_out(os_, ls_, x2, norm_gain[1, 1][None, :], _bf(attn_wo[0]),
                                   router, rbias, b, tm_a)

    tm_e = TOKEN_TILE if t >= N_EXPERTS * TOKEN_TILE else LANES
    ts = _tile(t, TOKEN_TILE)
    counts = cnt[0, :N_EXPERTS].astype(jnp.int32)
    padded = ((counts + tm_e - 1) // tm_e) * tm_e
    ends = jnp.cumsum(padded)
    meta = jnp.concatenate([ends - padded, ends]).astype(jnp.int32)
    route_i = route[:, :4].astype(jnp.int32)
    group_start = jnp.sum(jnp.where(route_i[:, :2, None] == jnp.arange(N_EXPERTS), ends - padded, 0), axis=-1)
    dest = (group_start + route_i[:, 2:4]).astype(jnp.int32).reshape(2 * t)
    rows = 2 * t + N_EXPERTS * tm_e
    n_tiles = rows // tm_e
    n_valid = (ends[-1] // tm_e).astype(jnp.int32)
    tile_src = jnp.minimum(jnp.arange(n_tiles, dtype=jnp.int32), n_valid - 1)
    tile_expert = jnp.minimum(
        jnp.sum(tile_src[:, None] * tm_e >= ends[None, :], axis=1), N_EXPERTS - 1).astype(jnp.int32)

    xs = _moe_scatter(meta, dest, h4, rows, ts, tm_e)
    ffe = moe_w1.shape[3]
    tfe = ffe // 2 if (ffe // 2) % LANES == 0 else ffe
    ys = _moe_ffn(tile_src, tile_expert, n_valid.reshape(1), xs,
                  _bf(moe_w1[0]), _bf(moe_w3[0]), _bf(moe_w2[0]), tm_e, tfe)
    out = _moe_combine(dest, x3, route, final_norm_gain[None, :], ys, ts)
    return out.reshape(b, s, d)
```

```python
import functools
import math

import jax
import jax.numpy as jnp
from jax import lax
from jax.experimental import pallas as pl
from jax.experimental.pallas import tpu as pltpu

F32 = jnp.float32
BF16 = jnp.bfloat16

RMS_EPS = 1e-5
GN_EPS = 64e-5
HEAD = 64
LANES = 128
BRANCHES = ((128, 1), (512, 4), (2048, 16))
HEADS_PER_BRANCH = 8
ALIBI_MAX = 8.0
N_EXPERTS = 8
NEG = -1e30
VMEM_LIMIT_BYTES = 56 * 1024 * 1024
TOKEN_TILE = 512
RWKV_PRE_TILE = 256
WKV_SEQ_TILE = 1024
WKV_PAIRS = 8


def _params(*sem):
    return pltpu.CompilerParams(dimension_semantics=sem, vmem_limit_bytes=VMEM_LIMIT_BYTES)


def _dot(a, b):
    return jnp.dot(a, b, preferred_element_type=F32)


def _dot_nt(a, b):
    return lax.dot_general(a, b, (((1,), (1,)), ((), ())), preferred_element_type=F32)


def _bf(x):
    return x.astype(BF16)


def _split2(x):
    hi = x.astype(BF16)
    lo = (x - hi.astype(F32)).astype(BF16)
    return hi, lo


def _sigmoid(z):
    return 1.0 / (1.0 + jnp.exp(-z))


def _rms(x, gain):
    return x * lax.rsqrt(jnp.mean(x * x, axis=-1, keepdims=True) + RMS_EPS) * gain


def _const_spec(shape):
    nd = len(shape)
    return pl.BlockSpec(shape, lambda *_: (0,) * nd)


def _rwkv_pre_kernel(x_ref, xp_ref, vec_ref, mu_ref, wr_ref, wk_ref, wv_ref, w1_ref, w2_ref,
                     a1_ref, a2_ref, g1_ref, g2_ref,
                     r_out, lw_out, k_out, v_out, kk_out, a_out, g_out):
    i = pl.program_id(1)
    vec = vec_ref[...]
    gain, w0, a0, k_k, k_a = (vec[j:j + 1] for j in range(5))
    x = x_ref[0]
    h = _rms(x, gain)
    hp = _rms(xp_ref[0][7:8, :], gain)
    hp = jnp.where(i > 0, hp, 0.0)
    rows = lax.broadcasted_iota(jnp.int32, h.shape, 0)
    hprev = jnp.where(rows == 0, hp, pltpu.roll(h, 1, 0))
    xx = hprev - h
    mu = mu_ref[...]
    xr, xw, xk, xv, xa, xg = (_bf(h + xx * mu[j:j + 1]) for j in range(6))
    w_low = _dot(xw, w1_ref[...])
    a_low = _dot(xa, a1_ref[...])
    g_low = _dot(xg, g1_ref[...])
    r = _dot(xr, wr_ref[...])
    k = _dot(xk, wk_ref[...])
    v = _dot(xv, wv_ref[...])
    wl = w0 + _dot(_bf(jnp.tanh(w_low)), w2_ref[...])
    a = _sigmoid(a0 + _dot(_bf(a_low), a2_ref[...]))
    g = _dot(_bf(_sigmoid(g_low)), g2_ref[...])
    lw_out[0] = _sigmoid(wl) * (-math.exp(-0.5))
    r_out[0] = _bf(r)
    k_out[0] = _bf(k * (1.0 + (a - 1.0) * k_a))
    v_out[0] = _bf(v)
    kk_out[0] = _bf(k * k_k)
    a_out[0] = _bf(a)
    g_out[0] = _bf(g)


def _rwkv_pre(x, vec, mu, wr, wk, wv, w1, w2, a1, a2, g1, g2, tm):
    b, s, d = x.shape
    tok = pl.BlockSpec((1, tm, d), lambda bi, i: (bi, i, 0))
    prev = pl.BlockSpec((1, 8, d), lambda bi, i: (bi, jnp.maximum(i * (tm // 8) - 1, 0), 0))
    consts = [vec, mu, wr, wk, wv, w1, w2, a1, a2, g1, g2]
    out = lambda dt: jax.ShapeDtypeStruct((b, s, d), dt)
    return pl.pallas_call(
        _rwkv_pre_kernel,
        grid=(b, s // tm),
        in_specs=[tok, prev] + [_const_spec(c.shape) for c in consts],
        out_specs=[tok] * 7,
        out_shape=[out(BF16), out(F32)] + [out(BF16)] * 5,
        compiler_params=_params("parallel", "arbitrary"),
        name="rwkv_pre",
    )(x, x, *consts)


def _round_robin(generators):
    results = [None] * len(generators)
    live = list(enumerate(generators))
    while live:
        still = []
        for idx, gen in live:
            try:
                results[idx] = next(gen)
                still.append((idx, gen))
            except StopIteration:
                pass
        live = still
    return results


def _wkv_kernel(r_ref, lw_ref, k_ref, v_ref, kk_ref, a_ref, vec_ref, y_ref, st_ref, raw_ref, *, chunk):
    c_len = chunk
    n_chunks = r_ref.shape[1] // c_len
    n_pairs = r_ref.shape[2] // LANES
    inv_steps = max(c_len.bit_length() - 2, 0)
    lane = lax.broadcasted_iota(jnp.int32, (c_len, LANES), 1)
    head0 = lane < HEAD
    rr = lax.broadcasted_iota(jnp.int32, (c_len, c_len), 0)
    cc = lax.broadcasted_iota(jnp.int32, (c_len, c_len), 1)
    lower = rr >= cc
    strict = rr > cc
    eye = jnp.where(rr == cc, 1.0, 0.0).astype(F32)
    r2 = lax.broadcasted_iota(jnp.int32, (LANES, LANES), 0)
    c2 = lax.broadcasted_iota(jnp.int32, (LANES, LANES), 1)
    same_head = (r2 < HEAD) == (c2 < HEAD)

    @pl.when(pl.program_id(2) == 0)
    def _():
        st_ref[...] = jnp.zeros_like(st_ref)
        raw_ref[...] = jnp.zeros_like(raw_ref)

    row_index = lax.broadcasted_iota(jnp.int32, (c_len, LANES), 0)
    scan_shifts = [1 << j for j in range(c_len.bit_length() - 1)]
    rows = lambda mat, j: mat[j * c_len:(j + 1) * c_len]
    own_head = lambda mat: jnp.where(head0, rows(mat, 0), rows(mat, 1))

    def head_sum(x):
        s0 = jnp.sum(jnp.where(head0, x, 0.0), axis=-1, keepdims=True)
        s1 = jnp.sum(jnp.where(head0, 0.0, x), axis=-1, keepdims=True)
        return jnp.where(head0, s0, s1)

    def chunk_pair(sl, lanes, state):
        cum = lw_ref[0, sl, lanes]
        for shift in scan_shifts:
            cum = cum + jnp.where(row_index >= shift, pltpu.roll(cum, shift, 0), 0.0)
        cum_prev = jnp.where(row_index == 0, 0.0, pltpu.roll(cum, 1, 0))
        cum_last = cum[c_len - 1:c_len, :]
        kk = kk_ref[0, sl, lanes].astype(F32)
        kk = kk / jnp.maximum(jnp.sqrt(head_sum(kk * kk)), 1e-12)
        b = kk * a_ref[0, sl, lanes].astype(F32)
        k = k_ref[0, sl, lanes].astype(F32)
        vb = v_ref[0, sl, lanes]
        e_neg = jnp.exp(-cum)
        e_end = jnp.exp(cum_last - cum)
        a_t = -kk * jnp.exp(cum_prev)
        r_t = r_ref[0, sl, lanes].astype(F32) * jnp.exp(cum)
        lhs = _bf(jnp.concatenate([jnp.where(head0, a_t, 0.0), jnp.where(head0, 0.0, a_t),
                                   jnp.where(head0, r_t, 0.0), jnp.where(head0, 0.0, r_t)], axis=0))
        gram = _dot_nt(lhs, _bf(jnp.concatenate([b * e_neg, k * e_neg], axis=0)))
        from_state = _dot_nt(_bf(jnp.concatenate([a_t, r_t], axis=0)), _bf(state))
        yield
        gram_b, gram_k = gram[:, :c_len], gram[:, c_len:]
        from_v = _dot(_bf(jnp.concatenate(
            [jnp.where(strict, rows(gram_k, 0), 0.0), jnp.where(strict, rows(gram_k, 1), 0.0),
             jnp.where(lower, rows(gram_k, 2), 0.0), jnp.where(lower, rows(gram_k, 3), 0.0)], axis=0)), vb)
        m_rb = _bf(jnp.concatenate([jnp.where(lower, rows(gram_b, 2), 0.0),
                                    jnp.where(lower, rows(gram_b, 3), 0.0)], axis=0))
        w = rows(from_state, 0) + own_head(from_v[:2 * c_len])
        y1 = rows(from_state, 1) + own_head(from_v[2 * c_len:])
        pw = [jnp.where(strict, rows(gram_b, hd), 0.0) for hd in range(2)]
        inv = [eye + pw[hd] for hd in range(2)]
        if inv_steps:
            pw = [_dot(_bf(pw[hd]), _bf(pw[hd])) for hd in range(2)]
        for step in range(inv_steps):
            yield
            if step < inv_steps - 1:
                both = [_dot(_bf(jnp.concatenate([inv[hd], pw[hd]], axis=0)), _bf(pw[hd])) for hd in range(2)]
                inv = [inv[hd] + rows(both[hd], 0) for hd in range(2)]
                pw = [rows(both[hd], 1) for hd in range(2)]
            else:
                inv = [inv[hd] + _dot(_bf(inv[hd]), _bf(pw[hd])) for hd in range(2)]
        yield
        u = own_head(_dot(_bf(jnp.concatenate(inv, axis=0)), _bf(w)))
        yield
        y = y1 + own_head(_dot(m_rb, _bf(u)))
        uv_t = _bf(jnp.transpose(jnp.concatenate([u, vb.astype(F32)], axis=0)))
        bk = _bf(jnp.concatenate([b * e_end, k * e_end], axis=0))
        yield y, state * jnp.exp(cum_last) + jnp.where(same_head, _dot(uv_t, bk), 0.0)

    def finish(sl, lanes, y):
        lnx_w, lnx_b, r_k = (vec_ref[j:j + 1, lanes] for j in range(3))
        r, k, v = (ref[0, sl, lanes].astype(F32) for ref in (r_ref, k_ref, v_ref))
        dy = y - head_sum(y) * (1.0 / HEAD)
        yn = dy * lax.rsqrt(head_sum(dy * dy) * (1.0 / HEAD) + GN_EPS) * lnx_w + lnx_b
        y_ref[0, sl, lanes] = _bf(yn + head_sum(r * k * r_k) * v)

    pair_lanes = [slice(p * LANES, (p + 1) * LANES) for p in range(n_pairs)]
    chunk_rows = lambda c: pl.ds(pl.multiple_of(c * c_len, c_len), c_len)

    def body(c, carry):
        before = chunk_rows(jnp.maximum(c - 1, 0))
        for p in range(n_pairs):
            finish(before, pair_lanes[p], raw_ref[p])
        sl = chunk_rows(c)
        results = _round_robin([chunk_pair(sl, pair_lanes[p], st_ref[p]) for p in range(n_pairs)])
        for p, (y, new_state) in enumerate(results):
            raw_ref[p] = y
            st_ref[p] = new_state
        return carry

    lax.fori_loop(0, n_chunks, body, 0, unroll=2 if n_chunks % 2 == 0 else 1)
    for p in range(n_pairs):
        finish(chunk_rows(n_chunks - 1), pair_lanes[p], raw_ref[p])


def _wkv(r, lw, k, v, kk, a, vec, chunk, seq_tile, n_pairs):
    b, s, d = r.shape
    width = n_pairs * LANES
    spec = pl.BlockSpec((1, seq_tile, width), lambda bi, hi, si: (bi, si, hi))
    return pl.pallas_call(
        functools.partial(_wkv_kernel, chunk=chunk),
        grid=(b, d // width, s // seq_tile),
        in_specs=[spec] * 6 + [pl.BlockSpec((vec.shape[0], width), lambda bi, hi, si: (0, hi))],
        out_specs=spec,
        out_shape=jax.ShapeDtypeStruct((b, s, d), BF16),
        scratch_shapes=[pltpu.VMEM((n_pairs, LANES, LANES), F32),
                        pltpu.VMEM((n_pairs, chunk, LANES), F32)],
        compiler_params=_params("parallel", "parallel", "arbitrary"),
        name="wkv7",
    )(r, lw, k, v, kk, a, vec)


def _rwkv_post_kernel(z_ref, g_ref, x_ref, wo_ref, o_ref):
    o_ref[...] = x_ref[...] + _dot(z_ref[...] * g_ref[...], wo_ref[...])


def _rwkv_post(z, g, x, wo, tm):
    t, d = x.shape
    tok = pl.BlockSpec((tm, d), lambda i: (i, 0))
    return pl.pallas_call(
        _rwkv_post_kernel,
        grid=(t // tm,),
        in_specs=[tok] * 3 + [_const_spec(wo.shape)],
        out_specs=tok,
        out_shape=jax.ShapeDtypeStruct((t, d), F32),
        compiler_params=_params("parallel"),
        name="rwkv_post",
    )(z, g, x, wo)


FFN_SUB_ROWS = 512


def _ffn_kernel(x_ref, gain_ref, w1_ref, w3_ref, w2_ref, o_ref, h_ref):
    @pl.when(pl.program_id(1) == 0)
    def _():
        h_ref[...] = _bf(_rms(x_ref[...], gain_ref[...]))
        o_ref[...] = x_ref[...]

    for start in range(0, h_ref.shape[0], FFN_SUB_ROWS):
        rows = slice(start, min(start + FFN_SUB_ROWS, h_ref.shape[0]))
        h = h_ref[rows, :]
        u = _dot(h, w1_ref[...])
        act = _bf(u * _sigmoid(u) * _dot(h, w3_ref[...]))
        o_ref[rows, :] += _dot(act, w2_ref[...])


def _ffn(x, gain, w1, w3, w2, tm, tf):
    t, d = x.shape
    ff = w1.shape[1]
    tok = pl.BlockSpec((tm, d), lambda i, f: (i, 0))
    return pl.pallas_call(
        _ffn_kernel,
        grid=(t // tm, ff // tf),
        in_specs=[tok, _const_spec(gain.shape),
                  pl.BlockSpec((d, tf), lambda i, f: (0, f)),
                  pl.BlockSpec((d, tf), lambda i, f: (0, f)),
                  pl.BlockSpec((tf, d), lambda i, f: (f, 0))],
        out_specs=tok,
        out_shape=jax.ShapeDtypeStruct((t, d), F32),
        scratch_shapes=[pltpu.VMEM((tm, d), BF16)],
        compiler_params=_params("parallel", "arbitrary"),
        name="ffn_dense",
    )(x, gain, w1, w3, w2)


BRANCH_WIDTH = HEADS_PER_BRANCH * HEAD


def _attn_proj_kernel(x_ref, gains_ref, wq_ref, wkv_ref, *refs):
    outs, (q_scr, kv_scr) = refs[:-2], refs[-2:]
    x = x_ref[0]
    tm = x.shape[0]
    n = x * lax.rsqrt(jnp.mean(x * x, axis=-1, keepdims=True) + RMS_EPS)
    gains = gains_ref[...]
    q_tiles = q_scr.shape[0]
    qw = q_tiles * LANES
    tiles_br = BRANCH_WIDTH // LANES
    hq = _bf(n * gains[0:1])
    hkv = _bf(n * gains[1:2])

    def project(h, w_ref, col, scr, first_tile, scale=None):
        val = _dot(h, w_ref[:, col:col + BRANCH_WIDTH])
        for ti in range(tiles_br):
            tile = val[:, ti * LANES:(ti + 1) * LANES]
            scr[first_tile + ti] = tile if scale is None else tile * scale

    def permute(out, dil, scr, first_tile):
        for res in range(dil):
            rows = pl.ds(res, tm // dil, stride=dil) if dil > 1 else slice(None)
            for ti in range(tiles_br):
                out[0, res, :, ti * LANES:(ti + 1) * LANES] = _bf(scr[first_tile + ti, rows, :])

    pending = None
    for which, (h, w_ref, base, scr, tile0, scale) in enumerate(
            ((hq, wq_ref, 0, q_scr, 0, 1.0 / HEAD ** 0.5), (hkv, wkv_ref, 0, kv_scr, 0, None),
             (hkv, wkv_ref, qw, kv_scr, q_tiles, None))):
        for br, (_, dil) in enumerate(BRANCHES):
            project(h, w_ref, base + br * BRANCH_WIDTH, scr, tile0 + br * tiles_br, scale)
            if pending is not None:
                permute(*pending)
            pending = (outs[3 * br + which], dil, scr, tile0 + br * tiles_br)
    permute(*pending)


def _attn_proj(x, gains, wq, wkv, tm):
    b, s, d = x.shape
    qw = wq.shape[1]
    out_specs, out_shape = [], []
    for _, dil in BRANCHES:
        assert tm % (16 * dil) == 0
        out_specs += [pl.BlockSpec((1, dil, tm // dil, BRANCH_WIDTH), lambda bi, i: (bi, 0, i, 0))] * 3
        out_shape += [jax.ShapeDtypeStruct((b, dil, s // dil, BRANCH_WIDTH), BF16)] * 3
    return pl.pallas_call(
        _attn_proj_kernel,
        grid=(b, s // tm),
        in_specs=[pl.BlockSpec((1, tm, d), lambda bi, i: (bi, i, 0)),
                  _const_spec(gains.shape), _const_spec(wq.shape), _const_spec(wkv.shape)],
        out_specs=out_specs,
        out_shape=out_shape,
        scratch_shapes=[pltpu.VMEM((qw // LANES, tm, LANES), F32),
                        pltpu.VMEM((2 * qw // LANES, tm, LANES), F32)],
        compiler_params=_params("parallel", "parallel"),
        name="attn_proj",
    )(x, gains, wq, wkv)


def _attn_kernel(slopes_ref, q_ref, k_ref, v_ref, o_ref, lse_ref, *, branch, dilation, qb):
    sub_len = q_ref.shape[2]
    n_blk = sub_len // qb
    n_pairs = q_ref.shape[3] // LANES
    col2 = lax.broadcasted_iota(jnp.int32, (qb, 2 * qb), 1)
    rel = (lax.broadcasted_iota(jnp.int32, (qb, 2 * qb), 0) - jnp.where(col2 < qb, col2, col2 - qb)).astype(F32)
    dist_d = jnp.where(rel >= 0, rel, -NEG)
    dist_p = jnp.where(rel <= 0, rel + float(qb), -NEG)
    first_head = lax.broadcasted_iota(jnp.int32, (1, 2 * qb), 1) < qb
    lane = lax.broadcasted_iota(jnp.int32, (qb, LANES), 1)
    head0 = lane < HEAD
    lane_row = lax.broadcasted_iota(jnp.int32, (1, LANES), 1)
    head_mask = [jnp.where(lane_row < HEAD, 1.0, 0.0).astype(BF16),
                 jnp.where(lane_row < HEAD, 0.0, 1.0).astype(BF16)]
    head_ones = [jnp.where(head0, 1.0, 0.0).astype(BF16), jnp.where(head0, 0.0, 1.0).astype(BF16)]

    block_rows = lambda i: pl.ds(pl.multiple_of(i * qb, qb), qb)

    def blocks(items):
        def one_pair(r, i, p):
            cur = block_rows(i)
            if n_blk > 1:
                prv = block_rows(jnp.maximum(i - 1, 0))
                has_prev = jnp.where(i > 0, 1.0, -NEG)
            lanes = slice(p * LANES, (p + 1) * LANES)
            slope = [slopes_ref[branch * HEADS_PER_BRANCH + 2 * p + j] * float(dilation) for j in range(2)]
            step = jnp.where(first_head, slope[0], slope[1])

            def keys(rows):
                k = k_ref[0, r, rows, lanes]
                return jnp.concatenate([k * head_mask[0], k * head_mask[1]], axis=0)

            def values(rows):
                v = v_ref[0, r, rows, lanes]
                return jnp.concatenate(
                    [jnp.concatenate([v * head_mask[0], head_ones[0]], axis=1),
                     jnp.concatenate([v * head_mask[1], head_ones[1]], axis=1)], axis=0)

            q = q_ref[0, r, cur, lanes]
            sd = _dot_nt(q, keys(cur)) - step * dist_d
            if n_blk > 1:
                sp = _dot_nt(q, keys(prv)) - (step * has_prev) * dist_p
            yield
            both = jnp.maximum(sd, sp) if n_blk > 1 else sd
            m = [jnp.max(both[:, :qb], axis=-1, keepdims=True), jnp.max(both[:, qb:], axis=-1, keepdims=True)]
            probs = lambda s: _bf(jnp.concatenate([jnp.exp(s[:, :qb] - m[0]), jnp.exp(s[:, qb:] - m[1])], axis=1))
            acc = _dot(probs(sd), values(cur))
            if n_blk > 1:
                acc = acc + _dot(probs(sp), values(prv))
            yield
            den = acc[:, LANES:]
            lse = jnp.where(head0, m[0], m[1]) + jnp.log(den)
            yield acc[:, :LANES] / den, lse[:, 0:1], lse[:, HEAD:HEAD + 1]

        res = _round_robin([one_pair(r, i, p) for r, i in items for p in range(n_pairs)])
        for n, (r, i) in enumerate(items):
            lse = jnp.zeros((qb, LANES), F32)
            for p in range(n_pairs):
                out, l_a, l_b = res[n * n_pairs + p]
                o_ref[0, r, block_rows(i), p * LANES:(p + 1) * LANES] = _bf(out)
                lse = jnp.where(lane == 2 * p, l_a, jnp.where(lane == 2 * p + 1, l_b, lse))
            lse_ref[0, r, block_rows(i), :] = lse

    n_res = q_ref.shape[1]
    if n_blk == 1:
        blocks([(r, 0) for r in range(n_res)])
    else:
        group = next(g for g in (4, 2, 1) if n_blk % g == 0)
        for r in range(n_res):
            lax.fori_loop(0, n_blk // group,
                          lambda j, c, r=r: (blocks([(r, group * j + g) for g in range(group)]), c)[1], 0)


def _attn_branch(slopes, q, k, v, branch):
    window, dilation = BRANCHES[branch]
    assert window % dilation == 0 and window // dilation == LANES
    b, _, sub_len, width = q.shape
    qb = min(LANES, sub_len)
    assert sub_len % qb == 0
    n_res = max(1, min(dilation, TOKEN_TILE // sub_len))
    assert dilation % n_res == 0
    spec = pl.BlockSpec((1, n_res, sub_len, width), lambda bi, ri: (bi, ri, 0, 0))
    return pl.pallas_call(
        functools.partial(_attn_kernel, branch=branch, dilation=dilation, qb=qb),
        grid=(b, dilation // n_res),
        in_specs=[pl.BlockSpec(memory_space=pltpu.SMEM)] + [spec] * 3,
        out_specs=[spec, pl.BlockSpec((1, n_res, sub_len, LANES), lambda bi, ri: (bi, ri, 0, 0))],
        out_shape=[jax.ShapeDtypeStruct(q.shape, BF16),
                   jax.ShapeDtypeStruct(q.shape[:3] + (LANES,), F32)],
        compiler_params=_params("parallel", "parallel"),
        name=f"dilated_attn_{branch}",
    )(slopes, q, k, v)


def _attn_out_kernel(o0, o1, o2, l0, l1, l2, x_ref, gain_ref, wo_ref, router_ref, rb_ref, expand_ref,
                     x_out, h_out, route_out, cnt_out, cnt_ref, o_scr, l_scr):
    @pl.when((pl.program_id(0) == 0) & (pl.program_id(1) == 0))
    def _():
        cnt_ref[...] = jnp.zeros_like(cnt_ref)

    tm = x_ref.shape[0]
    tiles_br = BRANCH_WIDTH // LANES
    for br, (o_ref, l_ref) in enumerate(((o0, l0), (o1, l1), (o2, l2))):
        dil = BRANCHES[br][1]
        for res in range(dil):
            rows = pl.ds(res, tm // dil, stride=dil) if dil > 1 else slice(None)
            l_scr[br, rows, :] = l_ref[0, res]
            for ti in range(tiles_br):
                lanes = slice(ti * LANES, (ti + 1) * LANES)
                o_scr[br, ti, rows, :] = o_ref[0, res, :, lanes].astype(F32)
    ls = [l_scr[br] for br in range(3)]
    m = jnp.maximum(jnp.maximum(ls[0], ls[1]), ls[2])
    ws = [jnp.exp(l - m) for l in ls]
    total = ws[0] + ws[1] + ws[2]
    spread = [_dot(_bf(ws[br] / total), expand_ref[...]) for br in range(3)]
    merged = []
    for ti in range(tiles_br):
        lanes = slice(ti * LANES, (ti + 1) * LANES)
        merged.append(_bf(sum(spread[br][:, lanes] * o_scr[br, ti] for br in range(3))))
    x = x_ref[...] + _dot(jnp.concatenate(merged, axis=1), wo_ref[...])
    x_out[...] = x
    h = _rms(x, gain_ref[...])
    h_out[...] = h
    hhi, hlo = _split2(h)
    both = _dot(hhi, router_ref[...])
    logits = both[:, :LANES] + both[:, LANES:] + _dot(hlo, router_ref[:, :LANES]) + rb_ref[...]
    lane = lax.broadcasted_iota(jnp.int32, logits.shape, 1)
    m1 = jnp.max(logits, axis=-1, keepdims=True)
    i1 = jnp.min(jnp.where(logits == m1, lane, LANES), axis=-1, keepdims=True)
    rest = jnp.where(lane == i1, -3e38, logits)
    m2 = jnp.max(rest, axis=-1, keepdims=True)
    i2 = jnp.min(jnp.where(rest == m2, lane, LANES), axis=-1, keepdims=True)
    ex = jnp.exp(m2 - m1)
    sel = jnp.where((lane == i1) | (lane == i2), 1.0, 0.0)
    rr = lax.broadcasted_iota(jnp.int32, (tm, tm), 0)
    cc = lax.broadcasted_iota(jnp.int32, (tm, tm), 1)
    before = jnp.where(rr > cc, 1.0, 0.0).astype(BF16)
    rank = cnt_ref[...] + _dot(before, _bf(sel))
    cnt_ref[...] += jnp.sum(sel, axis=0, keepdims=True)
    cnt_out[...] = cnt_ref[...]
    columns = [i1.astype(F32), i2.astype(F32),
               jnp.sum(jnp.where(lane == i1, rank, 0.0), axis=-1, keepdims=True),
               jnp.sum(jnp.where(lane == i2, rank, 0.0), axis=-1, keepdims=True),
               1.0 / (1.0 + ex), ex / (1.0 + ex)]
    route = jnp.zeros(logits.shape, F32)
    for j, col in enumerate(columns):
        route = jnp.where(lane == j, col, route)
    route_out[...] = route


def _attn_out(os_, ls_, x, gain, wo, router, rb, b, tm):
    t, d = x.shape
    tiles = t // (b * tm)
    tok = pl.BlockSpec((tm, d), lambda bi, i: (bi * tiles + i, 0))
    small = pl.BlockSpec((tm, LANES), lambda bi, i: (bi * tiles + i, 0))
    att = [pl.BlockSpec((1, dil, tm // dil, BRANCH_WIDTH), lambda bi, i: (bi, 0, i, 0))
           for _, dil in BRANCHES]
    lse = [pl.BlockSpec((1, dil, tm // dil, LANES), lambda bi, i: (bi, 0, i, 0)) for _, dil in BRANCHES]
    expand = (jnp.arange(LANES)[:, None] == jnp.arange(BRANCH_WIDTH)[None, :] // HEAD).astype(BF16)
    consts = [gain, wo, router, rb, expand]
    return pl.pallas_call(
        _attn_out_kernel,
        grid=(b, tiles),
        in_specs=att + lse + [tok] + [_const_spec(c.shape) for c in consts],
        out_specs=[tok, tok, small, _const_spec((1, LANES))],
        out_shape=[jax.ShapeDtypeStruct((t, d), F32), jax.ShapeDtypeStruct((t, d), F32),
                   jax.ShapeDtypeStruct((t, LANES), F32), jax.ShapeDtypeStruct((1, LANES), F32)],
        scratch_shapes=[pltpu.VMEM((1, LANES), F32),
                        pltpu.VMEM((len(BRANCHES), BRANCH_WIDTH // LANES, tm, LANES), F32),
                        pltpu.VMEM((len(BRANCHES), tm, LANES), F32)],
        compiler_params=_params("arbitrary", "arbitrary"),
        name="attn_out_route",
    )(*os_, *ls_, x, *consts)


DMA_UNROLL = 8


def _row_copy(src, src_row, dst, dst_row, sem):
    return pltpu.make_async_copy(src.at[pl.ds(src_row, 1)], dst.at[pl.ds(dst_row, 1)], sem)


def _moe_scatter_kernel(meta_ref, dest_ref, h_ref, out_ref, zeros, sem, zero_sem):
    n_tok = h_ref.shape[0]
    tile = zeros.shape[0]
    ends = lambda e: meta_ref[N_EXPERTS + e]

    def zero_fill(e):
        if e < N_EXPERTS:
            start, used = jnp.maximum(ends(e) - tile, 0), ends(e) > meta_ref[e]
        else:
            start = ends(N_EXPERTS - 1) + (e - N_EXPERTS) * tile
            used = start < out_ref.shape[0]
            start = jnp.minimum(start, out_ref.shape[0] - tile)
        dst = out_ref.at[pl.ds(pl.multiple_of(start, tile), tile)]
        return used, pltpu.make_async_copy(zeros, dst, zero_sem)

    @pl.when(pl.program_id(0) == 0)
    def _():
        zeros[...] = jnp.zeros_like(zeros)
        for e in range(2 * N_EXPERTS):
            used, copy = zero_fill(e)
            pl.when(used)(copy.start)
        for e in range(2 * N_EXPERTS):
            used, copy = zero_fill(e)
            pl.when(used)(copy.wait)

    def issue(j, carry):
        for k in range(2):
            _row_copy(h_ref, j, out_ref, dest_ref[2 * j + k], sem).start(priority=k)
        return carry

    def drain(j, carry):
        for _ in range(2):
            _row_copy(h_ref, 0, out_ref, 0, sem).wait()
        return carry

    lax.fori_loop(0, n_tok, issue, 0, unroll=DMA_UNROLL)
    lax.fori_loop(0, n_tok, drain, 0, unroll=DMA_UNROLL)


def _moe_scatter(meta, dest_flat, h, rows, ts, tile):
    t, d = h.shape
    return pl.pallas_call(
        _moe_scatter_kernel,
        grid=(t // ts,),
        in_specs=[pl.BlockSpec(memory_space=pltpu.SMEM),
                  pl.BlockSpec((2 * ts,), lambda i: (i,), memory_space=pltpu.SMEM),
                  pl.BlockSpec((ts, d), lambda i: (i, 0))],
        out_specs=pl.BlockSpec(memory_space=pl.ANY),
        out_shape=jax.ShapeDtypeStruct((rows, d), F32),
        scratch_shapes=[pltpu.VMEM((tile, d), F32), pltpu.SemaphoreType.DMA(()),
                        pltpu.SemaphoreType.DMA(())],
        compiler_params=_params("arbitrary"),
        name="moe_scatter",
    )(meta, dest_flat, h)


def _moe_ffn_kernel(src_ref, te_ref, nv_ref, x_ref, w1_ref, w3_ref, w2_ref, o_ref):
    del src_ref, te_ref
    @pl.when(pl.program_id(1) == 0)
    def _():
        o_ref[...] = jnp.zeros_like(o_ref)

    @pl.when(pl.program_id(0) < nv_ref[0])
    def _():
        h = _bf(x_ref[...])
        u = _dot(h, w1_ref[0])
        act = _bf(u * _sigmoid(u) * _dot(h, w3_ref[0]))
        o_ref[...] += _dot(act, w2_ref[0])


def _moe_ffn(tile_src, tile_expert, n_valid, xs, w1, w3, w2, tm, tf):
    rows, d = xs.shape
    ff = w1.shape[2]
    tok = pl.BlockSpec((tm, d), lambda i, f, src, te, nv: (src[i], 0))
    return pl.pallas_call(
        _moe_ffn_kernel,
        grid_spec=pltpu.PrefetchScalarGridSpec(
            num_scalar_prefetch=3,
            grid=(rows // tm, ff // tf),
            in_specs=[tok,
                      pl.BlockSpec((1, d, tf), lambda i, f, src, te, nv: (te[i], 0, f)),
                      pl.BlockSpec((1, d, tf), lambda i, f, src, te, nv: (te[i], 0, f)),
                      pl.BlockSpec((1, tf, d), lambda i, f, src, te, nv: (te[i], f, 0))],
            out_specs=pl.BlockSpec((tm, d), lambda i, f, src, te, nv: (i, 0)),
        ),
        out_shape=jax.ShapeDtypeStruct((rows, d), F32),
        compiler_params=_params("arbitrary", "arbitrary"),
        name="moe_experts",
    )(tile_src, tile_expert, n_valid, xs, w1, w3, w2)


def _moe_combine_kernel(dest_ref, x_ref, gates_ref, gain_ref, y_hbm, o_ref, buf, sem):
    n_tok = x_ref.shape[0]

    def issue(j, carry):
        for k in range(2):
            _row_copy(y_hbm, dest_ref[2 * j + k], buf.at[k], j, sem).start(priority=k)
        return carry

    def drain(j, carry):
        for k in range(2):
            _row_copy(y_hbm, 0, buf.at[k], 0, sem).wait()
        return carry

    lax.fori_loop(0, n_tok, issue, 0, unroll=DMA_UNROLL)
    lax.fori_loop(0, n_tok, drain, 0, unroll=DMA_UNROLL)
    gates = gates_ref[...]
    x = x_ref[...] + gates[:, 4:5] * buf[0] + gates[:, 5:6] * buf[1]
    o_ref[...] = _rms(x, gain_ref[...])


def _moe_combine(dest_flat, x, route, gain, ys, ts):
    t, d = x.shape
    tok = pl.BlockSpec((ts, d), lambda i: (i, 0))
    return pl.pallas_call(
        _moe_combine_kernel,
        grid=(t // ts,),
        in_specs=[pl.BlockSpec((2 * ts,), lambda i: (i,), memory_space=pltpu.SMEM),
                  tok, pl.BlockSpec((ts, LANES), lambda i: (i, 0)), _const_spec(gain.shape),
                  pl.BlockSpec(memory_space=pl.ANY)],
        out_specs=tok,
        out_shape=jax.ShapeDtypeStruct((t, d), F32),
        scratch_shapes=[pltpu.VMEM((2, ts, d), F32), pltpu.SemaphoreType.DMA(())],
        compiler_params=_params("arbitrary"),
        name="moe_combine",
    )(dest_flat, x, route, gain, ys)


def _tile(n, want):
    t = min(n, want)
    assert n % t == 0
    return t


def _pad_cols(w, n):
    return jnp.pad(w, ((0, 0), (0, n - w.shape[1])))


def _pad_rows(w, n):
    return jnp.pad(w, ((0, n - w.shape[0]), (0, 0)))


def kernel(x, norm_gain, rwkv_mu, rwkv_wr, rwkv_wk, rwkv_wv, rwkv_w0, rwkv_w1, rwkv_w2, rwkv_a0, rwkv_a1, rwkv_a2, rwkv_g1, rwkv_g2, rwkv_k_k, rwkv_k_a, rwkv_r_k, rwkv_lnx_w, rwkv_lnx_b, rwkv_wo, kv_norm_gain, w_kv, attn_wq, attn_wo, ffn_w1, ffn_w3, ffn_w2, moe_router, moe_router_bias, moe_w1, moe_w3, moe_w2, final_norm_gain):
    b, s, d = x.shape
    t = b * s
    assert norm_gain.shape[0] == 2 and d % LANES == 0 and s % 8 == 0

    zeros = jnp.zeros((d,), F32)
    vec_pre = jnp.stack([norm_gain[0, 0], rwkv_w0[0], rwkv_a0[0], rwkv_k_k[0], rwkv_k_a[0],
                         zeros, zeros, zeros])
    tm_pre = _tile(s, RWKV_PRE_TILE)
    chunk = _tile(s, LANES)
    r, lw, k, v, kk, a, g = _rwkv_pre(
        x, vec_pre, rwkv_mu[0], _bf(rwkv_wr[0]), _bf(rwkv_wk[0]), _bf(rwkv_wv[0]),
        _bf(_pad_cols(rwkv_w1[0], LANES)), _bf(_pad_rows(rwkv_w2[0], LANES)),
        _bf(_pad_cols(rwkv_a1[0], LANES)), _bf(_pad_rows(rwkv_a2[0], LANES)),
        _bf(rwkv_g1[0]), _bf(rwkv_g2[0]), tm_pre)
    vec_post = jnp.stack([rwkv_lnx_w[0], rwkv_lnx_b[0], rwkv_r_k[0].reshape(d),
                          zeros, zeros, zeros, zeros, zeros])
    z = _wkv(r, lw, k, v, kk, a, vec_post, chunk=chunk, seq_tile=_tile(s, WKV_SEQ_TILE),
             n_pairs=WKV_PAIRS)

    flat = lambda arr: arr.reshape(t, d)
    tm = _tile(t, TOKEN_TILE)
    x1 = _rwkv_post(flat(z), flat(g), flat(x), _bf(rwkv_wo[0]), tm)

    ff = ffn_w1.shape[2]
    tf = ff // 2 if (ff // 2) % LANES == 0 else ff
    x2 = _ffn(x1, norm_gain[0, 1][None, :], _bf(ffn_w1[0]), _bf(ffn_w3[0]), _bf(ffn_w2[0]),
              _tile(t, 2 * FFN_SUB_ROWS), tf)

    n_slopes = len(BRANCHES) * HEADS_PER_BRANCH
    slopes = jnp.exp2(-ALIBI_MAX * (jnp.arange(n_slopes, dtype=F32) + 1.0) / n_slopes)
    gains = jnp.stack([norm_gain[1, 0], kv_norm_gain] + [zeros] * 6)
    tm_a = _tile(s, TOKEN_TILE)
    qkv = _attn_proj(x2.reshape(b, s, d), gains, _bf(attn_wq[0]), _bf(w_kv), tm_a)
    os_, ls_ = [], []
    for br in range(len(BRANCHES)):
        o_br, l_br = _attn_branch(slopes, *qkv[3 * br:3 * br + 3], br)
        os_.append(o_br)
        ls_.append(l_br)

    router = jnp.concatenate(_split2(_pad_cols(moe_router[0], LANES)), axis=1)
    rbias = jnp.full((1, LANES), NEG, F32).at[0, :N_EXPERTS].set(moe_router_bias[0])
    x3, h4, route, cnt = _attn_out(os_, ls_, x2, norm_gain[1, 1][None, :], _bf(attn_wo[0]),
                                   router, rbias, b, tm_a)

    tm_e = TOKEN_TILE if t >= N_EXPERTS * TOKEN_TILE else LANES
    ts = _tile(t, TOKEN_TILE)
    counts = cnt[0, :N_EXPERTS].astype(jnp.int32)
    padded = ((counts + tm_e - 1) // tm_e) * tm_e
    ends = jnp.cumsum(padded)
    meta = jnp.concatenate([ends - padded, ends]).astype(jnp.int32)
    route_i = route[:, :4].astype(jnp.int32)
    group_start = jnp.sum(jnp.where(route_i[:, :2, None] == jnp.arange(N_EXPERTS), ends - padded, 0), axis=-1)
    dest = (group_start + route_i[:, 2:4]).astype(jnp.int32).reshape(2 * t)
    rows = 2 * t + N_EXPERTS * tm_e
    n_tiles = rows // tm_e
    n_valid = (ends[-1] // tm_e).astype(jnp.int32)
    tile_src = jnp.minimum(jnp.arange(n_tiles, dtype=jnp.int32), n_valid - 1)
    tile_expert = jnp.minimum(
        jnp.sum(tile_src[:, None] * tm_e >= ends[None, :], axis=1), N_EXPERTS - 1).astype(jnp.int32)

    xs = _moe_scatter(meta, dest, h4, rows, ts, tm_e)
    ffe = moe_w1.shape[3]
    tfe = ffe // 2 if (ffe // 2) % LANES == 0 else ffe
    ys = _moe_ffn(tile_src, tile_expert, n_valid.reshape(1), xs,
                  _bf(moe_w1[0]), _bf(moe_w3[0]), _bf(moe_w2[0]), tm_e, tfe)
    out = _moe_combine(dest, x3, route, final_norm_gain[None, :], ys, ts)
    return out.reshape(b, s, d)
```

```python
import functools
import math

import jax
import jax.numpy as jnp
from jax import lax
from jax.experimental import pallas as pl
from jax.experimental.pallas import tpu as pltpu

F32 = jnp.float32
BF16 = jnp.bfloat16

RMS_EPS = 1e-5
GN_EPS = 64e-5
HEAD = 64
LANES = 128
BRANCHES = ((128, 1), (512, 4), (2048, 16))
HEADS_PER_BRANCH = 8
ALIBI_MAX = 8.0
N_EXPERTS = 8
NEG = -1e30
VMEM_LIMIT_BYTES = 56 * 1024 * 1024
TOKEN_TILE = 512
RWKV_PRE_TILE = 512
WKV_SEQ_TILE = 1024
WKV_PAIRS = 8


def _params(*sem):
    return pltpu.CompilerParams(dimension_semantics=sem, vmem_limit_bytes=VMEM_LIMIT_BYTES)


def _dot(a, b):
    return jnp.dot(a, b, preferred_element_type=F32)


def _dot_nt(a, b):
    return lax.dot_general(a, b, (((1,), (1,)), ((), ())), preferred_element_type=F32)


def _bf(x):
    return x.astype(BF16)


def _split2(x):
    hi = x.astype(BF16)
    lo = (x - hi.astype(F32)).astype(BF16)
    return hi, lo


def _sigmoid(z):
    return 1.0 / (1.0 + jnp.exp(-z))


def _rms(x, gain):
    return x * lax.rsqrt(jnp.mean(x * x, axis=-1, keepdims=True) + RMS_EPS) * gain


def _const_spec(shape):
    nd = len(shape)
    return pl.BlockSpec(shape, lambda *_: (0,) * nd)


def _rwkv_pre_kernel(x_ref, xp_ref, vec_ref, mu_ref, wr_ref, wk_ref, wv_ref, w1_ref, w2_ref,
                     a1_ref, a2_ref, g1_ref, g2_ref,
                     r_out, lw_out, k_out, v_out, kk_out, a_out, g_out):
    i = pl.program_id(1)
    vec = vec_ref[...]
    gain, w0, a0, k_k, k_a = (vec[j:j + 1] for j in range(5))
    x = x_ref[0]
    h = _rms(x, gain)
    hp = _rms(xp_ref[0][7:8, :], gain)
    hp = jnp.where(i > 0, hp, 0.0)
    rows = lax.broadcasted_iota(jnp.int32, h.shape, 0)
    hprev = jnp.where(rows == 0, hp, pltpu.roll(h, 1, 0))
    xx = hprev - h
    mu = mu_ref[...]
    xr, xw, xk, xv, xa, xg = (_bf(h + xx * mu[j:j + 1]) for j in range(6))
    w_low = _dot(xw, w1_ref[...])
    a_low = _dot(xa, a1_ref[...])
    g_low = _dot(xg, g1_ref[...])
    r = _dot(xr, wr_ref[...])
    k = _dot(xk, wk_ref[...])
    v = _dot(xv, wv_ref[...])
    wl = w0 + _dot(_bf(jnp.tanh(w_low)), w2_ref[...])
    a = _sigmoid(a0 + _dot(_bf(a_low), a2_ref[...]))
    g = _dot(_bf(_sigmoid(g_low)), g2_ref[...])
    lw_out[0] = _sigmoid(wl) * (-math.exp(-0.5))
    r_out[0] = _bf(r)
    k_out[0] = _bf(k * (1.0 + (a - 1.0) * k_a))
    v_out[0] = _bf(v)
    kk_out[0] = _bf(k * k_k)
    a_out[0] = _bf(a)
    g_out[0] = _bf(g)


def _rwkv_pre(x, vec, mu, wr, wk, wv, w1, w2, a1, a2, g1, g2, tm):
    b, s, d = x.shape
    tok = pl.BlockSpec((1, tm, d), lambda bi, i: (bi, i, 0))
    prev = pl.BlockSpec((1, 8, d), lambda bi, i: (bi, jnp.maximum(i * (tm // 8) - 1, 0), 0))
    consts = [vec, mu, wr, wk, wv, w1, w2, a1, a2, g1, g2]
    out = lambda dt: jax.ShapeDtypeStruct((b, s, d), dt)
    return pl.pallas_call(
        _rwkv_pre_kernel,
        grid=(b, s // tm),
        in_specs=[tok, prev] + [_const_spec(c.shape) for c in consts],
        out_specs=[tok] * 7,
        out_shape=[out(BF16), out(F32)] + [out(BF16)] * 5,
        compiler_params=_params("parallel", "arbitrary"),
        name="rwkv_pre",
    )(x, x, *consts)


def _round_robin(generators):
    results = [None] * len(generators)
    live = list(enumerate(generators))
    while live:
        still = []
        for idx, gen in live:
            try:
                results[idx] = next(gen)
                still.append((idx, gen))
            except StopIteration:
                pass
        live = still
    return results


def _wkv_kernel(r_ref, lw_ref, k_ref, v_ref, kk_ref, a_ref, vec_ref, y_ref, st_ref, raw_ref, *, chunk):
    c_len = chunk
    n_chunks = r_ref.shape[1] // c_len
    n_pairs = r_ref.shape[2] // LANES
    inv_steps = max(c_len.bit_length() - 2, 0)
    lane = lax.broadcasted_iota(jnp.int32, (c_len, LANES), 1)
    head0 = lane < HEAD
    rr = lax.broadcasted_iota(jnp.int32, (c_len, c_len), 0)
    cc = lax.broadcasted_iota(jnp.int32, (c_len, c_len), 1)
    lower = rr >= cc
    strict = rr > cc
    eye = jnp.where(rr == cc, 1.0, 0.0).astype(F32)
    r2 = lax.broadcasted_iota(jnp.int32, (LANES, LANES), 0)
    c2 = lax.broadcasted_iota(jnp.int32, (LANES, LANES), 1)
    same_head = (r2 < HEAD) == (c2 < HEAD)

    @pl.when(pl.program_id(2) == 0)
    def _():
        st_ref[...] = jnp.zeros_like(st_ref)
        raw_ref[...] = jnp.zeros_like(raw_ref)

    row_index = lax.broadcasted_iota(jnp.int32, (c_len, LANES), 0)
    scan_shifts = [1 << j for j in range(c_len.bit_length() - 1)]
    rows = lambda mat, j: mat[j * c_len:(j + 1) * c_len]
    own_head = lambda mat: jnp.where(head0, rows(mat, 0), rows(mat, 1))

    def head_sum(x):
        s0 = jnp.sum(jnp.where(head0, x, 0.0), axis=-1, keepdims=True)
        s1 = jnp.sum(jnp.where(head0, 0.0, x), axis=-1, keepdims=True)
        return jnp.where(head0, s0, s1)

    def chunk_pair(sl, lanes, state):
        cum = lw_ref[0, sl, lanes]
        for shift in scan_shifts:
            cum = cum + jnp.where(row_index >= shift, pltpu.roll(cum, shift, 0), 0.0)
        cum_prev = jnp.where(row_index == 0, 0.0, pltpu.roll(cum, 1, 0))
        cum_last = cum[c_len - 1:c_len, :]
        kk = kk_ref[0, sl, lanes].astype(F32)
        kk = kk / jnp.maximum(jnp.sqrt(head_sum(kk * kk)), 1e-12)
        b = kk * a_ref[0, sl, lanes].astype(F32)
        k = k_ref[0, sl, lanes].astype(F32)
        vb = v_ref[0, sl, lanes]
        e_neg = jnp.exp(-cum)
        e_end = jnp.exp(cum_last - cum)
        a_t = -kk * jnp.exp(cum_prev)
        r_t = r_ref[0, sl, lanes].astype(F32) * jnp.exp(cum)
        lhs = _bf(jnp.concatenate([jnp.where(head0, a_t, 0.0), jnp.where(head0, 0.0, a_t),
                                   jnp.where(head0, r_t, 0.0), jnp.where(head0, 0.0, r_t)], axis=0))
        gram = _dot_nt(lhs, _bf(jnp.concatenate([b * e_neg, k * e_neg], axis=0)))
        from_state = _dot_nt(_bf(jnp.concatenate([a_t, r_t], axis=0)), _bf(state))
        yield
        gram_b, gram_k = gram[:, :c_len], gram[:, c_len:]
        from_v = _dot(_bf(jnp.concatenate(
            [jnp.where(strict, rows(gram_k, 0), 0.0), jnp.where(strict, rows(gram_k, 1), 0.0),
             jnp.where(lower, rows(gram_k, 2), 0.0), jnp.where(lower, rows(gram_k, 3), 0.0)], axis=0)), vb)
        m_rb = _bf(jnp.concatenate([jnp.where(lower, rows(gram_b, 2), 0.0),
                                    jnp.where(lower, rows(gram_b, 3), 0.0)], axis=0))
        w = rows(from_state, 0) + own_head(from_v[:2 * c_len])
        y1 = rows(from_state, 1) + own_head(from_v[2 * c_len:])
        pw = [jnp.where(strict, rows(gram_b, hd), 0.0) for hd in range(2)]
        inv = [eye + pw[hd] for hd in range(2)]
        if inv_steps:
            pw = [_dot(_bf(pw[hd]), _bf(pw[hd])) for hd in range(2)]
        for step in range(inv_steps):
            yield
            if step < inv_steps - 1:
                both = [_dot(_bf(jnp.concatenate([inv[hd], pw[hd]], axis=0)), _bf(pw[hd])) for hd in range(2)]
                inv = [inv[hd] + rows(both[hd], 0) for hd in range(2)]
                pw = [rows(both[hd], 1) for hd in range(2)]
            else:
                inv = [inv[hd] + _dot(_bf(inv[hd]), _bf(pw[hd])) for hd in range(2)]
        yield
        u = own_head(_dot(_bf(jnp.concatenate(inv, axis=0)), _bf(w)))
        yield
        y = y1 + own_head(_dot(m_rb, _bf(u)))
        uv_t = _bf(jnp.transpose(jnp.concatenate([u, vb.astype(F32)], axis=0)))
        bk = _bf(jnp.concatenate([b * e_end, k * e_end], axis=0))
        yield y, state * jnp.exp(cum_last) + jnp.where(same_head, _dot(uv_t, bk), 0.0)

    def finish(sl, lanes, y):
        lnx_w, lnx_b, r_k = (vec_ref[j:j + 1, lanes] for j in range(3))
        r, k, v = (ref[0, sl, lanes].astype(F32) for ref in (r_ref, k_ref, v_ref))
        dy = y - head_sum(y) * (1.0 / HEAD)
        yn = dy * lax.rsqrt(head_sum(dy * dy) * (1.0 / HEAD) + GN_EPS) * lnx_w + lnx_b
        y_ref[0, sl, lanes] = _bf(yn + head_sum(r * k * r_k) * v)

    pair_lanes = [slice(p * LANES, (p + 1) * LANES) for p in range(n_pairs)]
    chunk_rows = lambda c: pl.ds(pl.multiple_of(c * c_len, c_len), c_len)

    def body(c, carry):
        before = chunk_rows(jnp.maximum(c - 1, 0))
        for p in range(n_pairs):
            finish(before, pair_lanes[p], raw_ref[p])
        sl = chunk_rows(c)
        results = _round_robin([chunk_pair(sl, pair_lanes[p], st_ref[p]) for p in range(n_pairs)])
        for p, (y, new_state) in enumerate(results):
            raw_ref[p] = y
            st_ref[p] = new_state
        return carry

    lax.fori_loop(0, n_chunks, body, 0, unroll=2 if n_chunks % 2 == 0 else 1)
    for p in range(n_pairs):
        finish(chunk_rows(n_chunks - 1), pair_lanes[p], raw_ref[p])


def _wkv(r, lw, k, v, kk, a, vec, chunk, seq_tile, n_pairs):
    b, s, d = r.shape
    width = n_pairs * LANES
    spec = pl.BlockSpec((1, seq_tile, width), lambda bi, hi, si: (bi, si, hi))
    return pl.pallas_call(
        functools.partial(_wkv_kernel, chunk=chunk),
        grid=(b, d // width, s // seq_tile),
        in_specs=[spec] * 6 + [pl.BlockSpec((vec.shape[0], width), lambda bi, hi, si: (0, hi))],
        out_specs=spec,
        out_shape=jax.ShapeDtypeStruct((b, s, d), BF16),
        scratch_shapes=[pltpu.VMEM((n_pairs, LANES, LANES), F32),
                        pltpu.VMEM((n_pairs, chunk, LANES), F32)],
        compiler_params=_params("parallel", "parallel", "arbitrary"),
        name="wkv7",
    )(r, lw, k, v, kk, a, vec)


def _rwkv_post_kernel(z_ref, g_ref, x_ref, wo_ref, o_ref):
    o_ref[...] = x_ref[...] + _dot(z_ref[...] * g_ref[...], wo_ref[...])


def _rwkv_post(z, g, x, wo, tm):
    t, d = x.shape
    tok = pl.BlockSpec((tm, d), lambda i: (i, 0))
    return pl.pallas_call(
        _rwkv_post_kernel,
        grid=(t // tm,),
        in_specs=[tok] * 3 + [_const_spec(wo.shape)],
        out_specs=tok,
        out_shape=jax.ShapeDtypeStruct((t, d), F32),
        compiler_params=_params("parallel"),
        name="rwkv_post",
    )(z, g, x, wo)


FFN_SUB_ROWS = 512


def _ffn_kernel(x_ref, gain_ref, w1_ref, w3_ref, w2_ref, o_ref, h_ref):
    @pl.when(pl.program_id(1) == 0)
    def _():
        h_ref[...] = _bf(_rms(x_ref[...], gain_ref[...]))
        o_ref[...] = x_ref[...]

    for start in range(0, h_ref.shape[0], FFN_SUB_ROWS):
        rows = slice(start, min(start + FFN_SUB_ROWS, h_ref.shape[0]))
        h = h_ref[rows, :]
        u = _dot(h, w1_ref[...])
        act = _bf(u * _sigmoid(u) * _dot(h, w3_ref[...]))
        o_ref[rows, :] += _dot(act, w2_ref[...])


def _ffn(x, gain, w1, w3, w2, tm, tf):
    t, d = x.shape
    ff = w1.shape[1]
    tok = pl.BlockSpec((tm, d), lambda i, f: (i, 0))
    return pl.pallas_call(
        _ffn_kernel,
        grid=(t // tm, ff // tf),
        in_specs=[tok, _const_spec(gain.shape),
                  pl.BlockSpec((d, tf), lambda i, f: (0, f)),
                  pl.BlockSpec((d, tf), lambda i, f: (0, f)),
                  pl.BlockSpec((tf, d), lambda i, f: (f, 0))],
        out_specs=tok,
        out_shape=jax.ShapeDtypeStruct((t, d), F32),
        scratch_shapes=[pltpu.VMEM((tm, d), BF16)],
        compiler_params=_params("parallel", "arbitrary"),
        name="ffn_dense",
    )(x, gain, w1, w3, w2)


BRANCH_WIDTH = HEADS_PER_BRANCH * HEAD


def _attn_proj_kernel(x_ref, gains_ref, wq_ref, wkv_ref, *refs):
    outs, (q_scr, kv_scr) = refs[:-2], refs[-2:]
    x = x_ref[0]
    tm = x.shape[0]
    n = x * lax.rsqrt(jnp.mean(x * x, axis=-1, keepdims=True) + RMS_EPS)
    gains = gains_ref[...]
    q_tiles = q_scr.shape[0]
    qw = q_tiles * LANES
    tiles_br = BRANCH_WIDTH // LANES
    hq = _bf(n * gains[0:1])
    hkv = _bf(n * gains[1:2])

    def project(h, w_ref, col, scr, first_tile, scale=None):
        val = _dot(h, w_ref[:, col:col + BRANCH_WIDTH])
        for ti in range(tiles_br):
            tile = val[:, ti * LANES:(ti + 1) * LANES]
            scr[first_tile + ti] = tile if scale is None else tile * scale

    def permute(out, dil, scr, first_tile):
        for res in range(dil):
            rows = pl.ds(res, tm // dil, stride=dil) if dil > 1 else slice(None)
            for ti in range(tiles_br):
                out[0, res, :, ti * LANES:(ti + 1) * LANES] = _bf(scr[first_tile + ti, rows, :])

    pending = None
    for which, (h, w_ref, base, scr, tile0, scale) in enumerate(
            ((hq, wq_ref, 0, q_scr, 0, 1.0 / HEAD ** 0.5), (hkv, wkv_ref, 0, kv_scr, 0, None),
             (hkv, wkv_ref, qw, kv_scr, q_tiles, None))):
        for br, (_, dil) in enumerate(BRANCHES):
            project(h, w_ref, base + br * BRANCH_WIDTH, scr, tile0 + br * tiles_br, scale)
            if pending is not None:
                permute(*pending)
            pending = (outs[3 * br + which], dil, scr, tile0 + br * tiles_br)
    permute(*pending)


def _attn_proj(x, gains, wq, wkv, tm):
    b, s, d = x.shape
    qw = wq.shape[1]
    out_specs, out_shape = [], []
    for _, dil in BRANCHES:
        assert tm % (16 * dil) == 0
        out_specs += [pl.BlockSpec((1, dil, tm // dil, BRANCH_WIDTH), lambda bi, i: (bi, 0, i, 0))] * 3
        out_shape += [jax.ShapeDtypeStruct((b, dil, s // dil, BRANCH_WIDTH), BF16)] * 3
    return pl.pallas_call(
        _attn_proj_kernel,
        grid=(b, s // tm),
        in_specs=[pl.BlockSpec((1, tm, d), lambda bi, i: (bi, i, 0)),
                  _const_spec(gains.shape), _const_spec(wq.shape), _const_spec(wkv.shape)],
        out_specs=out_specs,
        out_shape=out_shape,
        scratch_shapes=[pltpu.VMEM((qw // LANES, tm, LANES), F32),
                        pltpu.VMEM((2 * qw // LANES, tm, LANES), F32)],
        compiler_params=_params("parallel", "parallel"),
        name="attn_proj",
    )(x, gains, wq, wkv)


def _attn_kernel(slopes_ref, q_ref, k_ref, v_ref, o_ref, lse_ref, *, branch, dilation, qb):
    sub_len = q_ref.shape[2]
    n_blk = sub_len // qb
    n_pairs = q_ref.shape[3] // LANES
    col2 = lax.broadcasted_iota(jnp.int32, (qb, 2 * qb), 1)
    rel = (lax.broadcasted_iota(jnp.int32, (qb, 2 * qb), 0) - jnp.where(col2 < qb, col2, col2 - qb)).astype(F32)
    dist_d = jnp.where(rel >= 0, rel, -NEG)
    dist_p = jnp.where(rel <= 0, rel + float(qb), -NEG)
    first_head = lax.broadcasted_iota(jnp.int32, (1, 2 * qb), 1) < qb
    lane = lax.broadcasted_iota(jnp.int32, (qb, LANES), 1)
    head0 = lane < HEAD
    lane_row = lax.broadcasted_iota(jnp.int32, (1, LANES), 1)
    head_mask = [jnp.where(lane_row < HEAD, 1.0, 0.0).astype(BF16),
                 jnp.where(lane_row < HEAD, 0.0, 1.0).astype(BF16)]
    head_ones = [jnp.where(head0, 1.0, 0.0).astype(BF16), jnp.where(head0, 0.0, 1.0).astype(BF16)]

    block_rows = lambda i: pl.ds(pl.multiple_of(i * qb, qb), qb)

    def blocks(items):
        def one_pair(r, i, p):
            cur = block_rows(i)
            if n_blk > 1:
                prv = block_rows(jnp.maximum(i - 1, 0))
                has_prev = jnp.where(i > 0, 1.0, -NEG)
            lanes = slice(p * LANES, (p + 1) * LANES)
            slope = [slopes_ref[branch * HEADS_PER_BRANCH + 2 * p + j] * float(dilation) for j in range(2)]
            step = jnp.where(first_head, slope[0], slope[1])

            def keys(rows):
                k = k_ref[0, r, rows, lanes]
                return jnp.concatenate([k * head_mask[0], k * head_mask[1]], axis=0)

            def values(rows):
                v = v_ref[0, r, rows, lanes]
                return jnp.concatenate(
                    [jnp.concatenate([v * head_mask[0], head_ones[0]], axis=1),
                     jnp.concatenate([v * head_mask[1], head_ones[1]], axis=1)], axis=0)

            q = q_ref[0, r, cur, lanes]
            sd = _dot_nt(q, keys(cur)) - step * dist_d
            if n_blk > 1:
                sp = _dot_nt(q, keys(prv)) - (step * has_prev) * dist_p
            yield
            both = jnp.maximum(sd, sp) if n_blk > 1 else sd
            m = [jnp.max(both[:, :qb], axis=-1, keepdims=True), jnp.max(both[:, qb:], axis=-1, keepdims=True)]
            probs = lambda s: _bf(jnp.concatenate([jnp.exp(s[:, :qb] - m[0]), jnp.exp(s[:, qb:] - m[1])], axis=1))
            acc = _dot(probs(sd), values(cur))
            if n_blk > 1:
                acc = acc + _dot(probs(sp), values(prv))
            yield
            den = acc[:, LANES:]
            lse = jnp.where(head0, m[0], m[1]) + jnp.log(den)
            yield acc[:, :LANES] / den, lse[:, 0:1], lse[:, HEAD:HEAD + 1]

        res = _round_robin([one_pair(r, i, p) for r, i in items for p in range(n_pairs)])
        for n, (r, i) in enumerate(items):
            lse = jnp.zeros((qb, LANES), F32)
            for p in range(n_pairs):
                out, l_a, l_b = res[n * n_pairs + p]
                o_ref[0, r, block_rows(i), p * LANES:(p + 1) * LANES] = _bf(out)
                lse = jnp.where(lane == 2 * p, l_a, jnp.where(lane == 2 * p + 1, l_b, lse))
            lse_ref[0, r, block_rows(i), :] = lse

    n_res = q_ref.shape[1]
    if n_blk == 1:
        blocks([(r, 0) for r in range(n_res)])
    else:
        group = next(g for g in (4, 2, 1) if n_blk % g == 0)
        for r in range(n_res):
            lax.fori_loop(0, n_blk // group,
                          lambda j, c, r=r: (blocks([(r, group * j + g) for g in range(group)]), c)[1], 0)


def _attn_branch(slopes, q, k, v, branch):
    window, dilation = BRANCHES[branch]
    assert window % dilation == 0 and window // dilation == LANES
    b, _, sub_len, width = q.shape
    qb = min(LANES, sub_len)
    assert sub_len % qb == 0
    n_res = max(1, min(dilation, TOKEN_TILE // sub_len))
    assert dilation % n_res == 0
    spec = pl.BlockSpec((1, n_res, sub_len, width), lambda bi, ri: (bi, ri, 0, 0))
    return pl.pallas_call(
        functools.partial(_attn_kernel, branch=branch, dilation=dilation, qb=qb),
        grid=(b, dilation // n_res),
        in_specs=[pl.BlockSpec(memory_space=pltpu.SMEM)] + [spec] * 3,
        out_specs=[spec, pl.BlockSpec((1, n_res, sub_len, LANES), lambda bi, ri: (bi, ri, 0, 0))],
        out_shape=[jax.ShapeDtypeStruct(q.shape, BF16),
                   jax.ShapeDtypeStruct(q.shape[:3] + (LANES,), F32)],
        compiler_params=_params("parallel", "parallel"),
        name=f"dilated_attn_{branch}",
    )(slopes, q, k, v)


def _attn_out_kernel(o0, o1, o2, l0, l1, l2, x_ref, gain_ref, wo_ref, router_ref, rb_ref, expand_ref,
                     x_out, h_out, route_out, cnt_out, cnt_ref, o_scr, l_scr):
    @pl.when((pl.program_id(0) == 0) & (pl.program_id(1) == 0))
    def _():
        cnt_ref[...] = jnp.zeros_like(cnt_ref)

    tm = x_ref.shape[0]
    tiles_br = BRANCH_WIDTH // LANES
    for br, (o_ref, l_ref) in enumerate(((o0, l0), (o1, l1), (o2, l2))):
        dil = BRANCHES[br][1]
        for res in range(dil):
            rows = pl.ds(res, tm // dil, stride=dil) if dil > 1 else slice(None)
            l_scr[br, rows, :] = l_ref[0, res]
            for ti in range(tiles_br):
                lanes = slice(ti * LANES, (ti + 1) * LANES)
                o_scr[br, ti, rows, :] = o_ref[0, res, :, lanes].astype(F32)
    ls = [l_scr[br] for br in range(3)]
    m = jnp.maximum(jnp.maximum(ls[0], ls[1]), ls[2])
    ws = [jnp.exp(l - m) for l in ls]
    total = ws[0] + ws[1] + ws[2]
    spread = [_dot(_bf(ws[br] / total), expand_ref[...]) for br in range(3)]
    merged = []
    for ti in range(tiles_br):
        lanes = slice(ti * LANES, (ti + 1) * LANES)
        merged.append(_bf(sum(spread[br][:, lanes] * o_scr[br, ti] for br in range(3))))
    x = x_ref[...] + _dot(jnp.concatenate(merged, axis=1), wo_ref[...])
    x_out[...] = x
    h = _rms(x, gain_ref[...])
    h_out[...] = h
    hhi, hlo = _split2(h)
    both = _dot(hhi, router_ref[...])
    logits = both[:, :LANES] + both[:, LANES:] + _dot(hlo, router_ref[:, :LANES]) + rb_ref[...]
    lane = lax.broadcasted_iota(jnp.int32, logits.shape, 1)
    m1 = jnp.max(logits, axis=-1, keepdims=True)
    i1 = jnp.min(jnp.where(logits == m1, lane, LANES), axis=-1, keepdims=True)
    rest = jnp.where(lane == i1, -3e38, logits)
    m2 = jnp.max(rest, axis=-1, keepdims=True)
    i2 = jnp.min(jnp.where(rest == m2, lane, LANES), axis=-1, keepdims=True)
    ex = jnp.exp(m2 - m1)
    sel = jnp.where((lane == i1) | (lane == i2), 1.0, 0.0)
    rr = lax.broadcasted_iota(jnp.int32, (tm, tm), 0)
    cc = lax.broadcasted_iota(jnp.int32, (tm, tm), 1)
    before = jnp.where(rr > cc, 1.0, 0.0).astype(BF16)
    rank = cnt_ref[...] + _dot(before, _bf(sel))
    cnt_ref[...] += jnp.sum(sel, axis=0, keepdims=True)
    cnt_out[...] = cnt_ref[...]
    columns = [i1.astype(F32), i2.astype(F32),
               jnp.sum(jnp.where(lane == i1, rank, 0.0), axis=-1, keepdims=True),
               jnp.sum(jnp.where(lane == i2, rank, 0.0), axis=-1, keepdims=True),
               1.0 / (1.0 + ex), ex / (1.0 + ex)]
    route = jnp.zeros(logits.shape, F32)
    for j, col in enumerate(columns):
        route = jnp.where(lane == j, col, route)
    route_out[...] = route


def _attn_out(os_, ls_, x, gain, wo, router, rb, b, tm):
    t, d = x.shape
    tiles = t // (b * tm)
    tok = pl.BlockSpec((tm, d), lambda bi, i: (bi * tiles + i, 0))
    small = pl.BlockSpec((tm, LANES), lambda bi, i: (bi * tiles + i, 0))
    att = [pl.BlockSpec((1, dil, tm // dil, BRANCH_WIDTH), lambda bi, i: (bi, 0, i, 0))
           for _, dil in BRANCHES]
    lse = [pl.BlockSpec((1, dil, tm // dil, LANES), lambda bi, i: (bi, 0, i, 0)) for _, dil in BRANCHES]
    expand = (jnp.arange(LANES)[:, None] == jnp.arange(BRANCH_WIDTH)[None, :] // HEAD).astype(BF16)
    consts = [gain, wo, router, rb, expand]
    return pl.pallas_call(
        _attn_out_kernel,
        grid=(b, tiles),
        in_specs=att + lse + [tok] + [_const_spec(c.shape) for c in consts],
        out_specs=[tok, tok, small, _const_spec((1, LANES))],
        out_shape=[jax.ShapeDtypeStruct((t, d), F32), jax.ShapeDtypeStruct((t, d), F32),
                   jax.ShapeDtypeStruct((t, LANES), F32), jax.ShapeDtypeStruct((1, LANES), F32)],
        scratch_shapes=[pltpu.VMEM((1, LANES), F32),
                        pltpu.VMEM((len(BRANCHES), BRANCH_WIDTH // LANES, tm, LANES), F32),
                        pltpu.VMEM((len(BRANCHES), tm, LANES), F32)],
        compiler_params=_params("arbitrary", "arbitrary"),
        name="attn_out_route",
    )(*os_, *ls_, x, *consts)


DMA_UNROLL = 8


def _row_copy(src, src_row, dst, dst_row, sem):
    return pltpu.make_async_copy(src.at[pl.ds(src_row, 1)], dst.at[pl.ds(dst_row, 1)], sem)


def _moe_scatter_kernel(meta_ref, dest_ref, h_ref, out_ref, zeros, sem, zero_sem):
    n_tok = h_ref.shape[0]
    tile = zeros.shape[0]
    ends = lambda e: meta_ref[N_EXPERTS + e]

    def zero_fill(e):
        if e < N_EXPERTS:
            start, used = jnp.maximum(ends(e) - tile, 0), ends(e) > meta_ref[e]
        else:
            start = ends(N_EXPERTS - 1) + (e - N_EXPERTS) * tile
            used = start < out_ref.shape[0]
            start = jnp.minimum(start, out_ref.shape[0] - tile)
        dst = out_ref.at[pl.ds(pl.multiple_of(start, tile), tile)]
        return used, pltpu.make_async_copy(zeros, dst, zero_sem)

    @pl.when(pl.program_id(0) == 0)
    def _():
        zeros[...] = jnp.zeros_like(zeros)
        for e in range(2 * N_EXPERTS):
            used, copy = zero_fill(e)
            pl.when(used)(copy.start)
        for e in range(2 * N_EXPERTS):
            used, copy = zero_fill(e)
            pl.when(used)(copy.wait)

    def issue(j, carry):
        for k in range(2):
            _row_copy(h_ref, j, out_ref, dest_ref[2 * j + k], sem).start(priority=k)
        return carry

    def drain(j, carry):
        for _ in range(2):
            _row_copy(h_ref, 0, out_ref, 0, sem).wait()
        return carry

    lax.fori_loop(0, n_tok, issue, 0, unroll=DMA_UNROLL)
    lax.fori_loop(0, n_tok, drain, 0, unroll=DMA_UNROLL)


def _moe_scatter(meta, dest_flat, h, rows, ts, tile):
    t, d = h.shape
    return pl.pallas_call(
        _moe_scatter_kernel,
        grid=(t // ts,),
        in_specs=[pl.BlockSpec(memory_space=pltpu.SMEM),
                  pl.BlockSpec((2 * ts,), lambda i: (i,), memory_space=pltpu.SMEM),
                  pl.BlockSpec((ts, d), lambda i: (i, 0))],
        out_specs=pl.BlockSpec(memory_space=pl.ANY),
        out_shape=jax.ShapeDtypeStruct((rows, d), F32),
        scratch_shapes=[pltpu.VMEM((tile, d), F32), pltpu.SemaphoreType.DMA(()),
                        pltpu.SemaphoreType.DMA(())],
        compiler_params=_params("arbitrary"),
        name="moe_scatter",
    )(meta, dest_flat, h)


def _moe_ffn_kernel(src_ref, te_ref, nv_ref, x_ref, w1_ref, w3_ref, w2_ref, o_ref):
    del src_ref, te_ref
    @pl.when(pl.program_id(1) == 0)
    def _():
        o_ref[...] = jnp.zeros_like(o_ref)

    @pl.when(pl.program_id(0) < nv_ref[0])
    def _():
        h = _bf(x_ref[...])
        u = _dot(h, w1_ref[0])
        act = _bf(u * _sigmoid(u) * _dot(h, w3_ref[0]))
        o_ref[...] += _dot(act, w2_ref[0])


def _moe_ffn(tile_src, tile_expert, n_valid, xs, w1, w3, w2, tm, tf):
    rows, d = xs.shape
    ff = w1.shape[2]
    tok = pl.BlockSpec((tm, d), lambda i, f, src, te, nv: (src[i], 0))
    return pl.pallas_call(
        _moe_ffn_kernel,
        grid_spec=pltpu.PrefetchScalarGridSpec(
            num_scalar_prefetch=3,
            grid=(rows // tm, ff // tf),
            in_specs=[tok,
                      pl.BlockSpec((1, d, tf), lambda i, f, src, te, nv: (te[i], 0, f)),
                      pl.BlockSpec((1, d, tf), lambda i, f, src, te, nv: (te[i], 0, f)),
                      pl.BlockSpec((1, tf, d), lambda i, f, src, te, nv: (te[i], f, 0))],
            out_specs=pl.BlockSpec((tm, d), lambda i, f, src, te, nv: (i, 0)),
        ),
        out_shape=jax.ShapeDtypeStruct((rows, d), F32),
        compiler_params=_params("arbitrary", "arbitrary"),
        name="moe_experts",
    )(tile_src, tile_expert, n_valid, xs, w1, w3, w2)


def _moe_combine_kernel(dest_ref, x_ref, gates_ref, gain_ref, y_hbm, o_ref, buf, sem):
    n_tok = x_ref.shape[0]

    def issue(j, carry):
        for k in range(2):
            _row_copy(y_hbm, dest_ref[2 * j + k], buf.at[k], j, sem).start(priority=k)
        return carry

    def drain(j, carry):
        for k in range(2):
            _row_copy(y_hbm, 0, buf.at[k], 0, sem).wait()
        return carry

    lax.fori_loop(0, n_tok, issue, 0, unroll=DMA_UNROLL)
    lax.fori_loop(0, n_tok, drain, 0, unroll=DMA_UNROLL)
    gates = gates_ref[...]
    x = x_ref[...] + gates[:, 4:5] * buf[0] + gates[:, 5:6] * buf[1]
    o_ref[...] = _rms(x, gain_ref[...])


def _moe_combine(dest_flat, x, route, gain, ys, ts):
    t, d = x.shape
    tok = pl.BlockSpec((ts, d), lambda i: (i, 0))
    return pl.pallas_call(
        _moe_combine_kernel,
        grid=(t // ts,),
        in_specs=[pl.BlockSpec((2 * ts,), lambda i: (i,), memory_space=pltpu.SMEM),
                  tok, pl.BlockSpec((ts, LANES), lambda i: (i, 0)), _const_spec(gain.shape),
                  pl.BlockSpec(memory_space=pl.ANY)],
        out_specs=tok,
        out_shape=jax.ShapeDtypeStruct((t, d), F32),
        scratch_shapes=[pltpu.VMEM((2, ts, d), F32), pltpu.SemaphoreType.DMA(())],
        compiler_params=_params("arbitrary"),
        name="moe_combine",
    )(dest_flat, x, route, gain, ys)


def _tile(n, want):
    t = min(n, want)
    assert n % t == 0
    return t


def _pad_cols(w, n):
    return jnp.pad(w, ((0, 0), (0, n - w.shape[1])))


def _pad_rows(w, n):
    return jnp.pad(w, ((0, n - w.shape[0]), (0, 0)))


def kernel(x, norm_gain, rwkv_mu, rwkv_wr, rwkv_wk, rwkv_wv, rwkv_w0, rwkv_w1, rwkv_w2, rwkv_a0, rwkv_a1, rwkv_a2, rwkv_g1, rwkv_g2, rwkv_k_k, rwkv_k_a, rwkv_r_k, rwkv_lnx_w, rwkv_lnx_b, rwkv_wo, kv_norm_gain, w_kv, attn_wq, attn_wo, ffn_w1, ffn_w3, ffn_w2, moe_router, moe_router_bias, moe_w1, moe_w3, moe_w2, final_norm_gain):
    b, s, d = x.shape
    t = b * s
    assert norm_gain.shape[0] == 2 and d % LANES == 0 and s % 8 == 0

    zeros = jnp.zeros((d,), F32)
    vec_pre = jnp.stack([norm_gain[0, 0], rwkv_w0[0], rwkv_a0[0], rwkv_k_k[0], rwkv_k_a[0],
                         zeros, zeros, zeros])
    tm_pre = _tile(s, RWKV_PRE_TILE)
    chunk = _tile(s, LANES)
    r, lw, k, v, kk, a, g = _rwkv_pre(
        x, vec_pre, rwkv_mu[0], _bf(rwkv_wr[0]), _bf(rwkv_wk[0]), _bf(rwkv_wv[0]),
        _bf(_pad_cols(rwkv_w1[0], LANES)), _bf(_pad_rows(rwkv_w2[0], LANES)),
        _bf(_pad_cols(rwkv_a1[0], LANES)), _bf(_pad_rows(rwkv_a2[0], LANES)),
        _bf(rwkv_g1[0]), _bf(rwkv_g2[0]), tm_pre)
    vec_post = jnp.stack([rwkv_lnx_w[0], rwkv_lnx_b[0], rwkv_r_k[0].reshape(d),
                          zeros, zeros, zeros, zeros, zeros])
    z = _wkv(r, lw, k, v, kk, a, vec_post, chunk=chunk, seq_tile=_tile(s, WKV_SEQ_TILE),
             n_pairs=WKV_PAIRS)

    flat = lambda arr: arr.reshape(t, d)
    tm = _tile(t, TOKEN_TILE)
    x1 = _rwkv_post(flat(z), flat(g), flat(x), _bf(rwkv_wo[0]), tm)

    ff = ffn_w1.shape[2]
    tf = ff // 2 if (ff // 2) % LANES == 0 else ff
    x2 = _ffn(x1, norm_gain[0, 1][None, :], _bf(ffn_w1[0]), _bf(ffn_w3[0]), _bf(ffn_w2[0]),
              _tile(t, 2 * FFN_SUB_ROWS), tf)

    n_slopes = len(BRANCHES) * HEADS_PER_BRANCH
    slopes = jnp.exp2(-ALIBI_MAX * (jnp.arange(n_slopes, dtype=F32) + 1.0) / n_slopes)
    gains = jnp.stack([norm_gain[1, 0], kv_norm_gain] + [zeros] * 6)
    tm_a = _tile(s, TOKEN_TILE)
    qkv = _attn_proj(x2.reshape(b, s, d), gains, _bf(attn_wq[0]), _bf(w_kv), tm_a)
    os_, ls_ = [], []
    for br in range(len(BRANCHES)):
        o_br, l_br = _attn_branch(slopes, *qkv[3 * br:3 * br + 3], br)
        os_.append(o_br)
        ls_.append(l_br)

    router = jnp.concatenate(_split2(_pad_cols(moe_router[0], LANES)), axis=1)
    rbias = jnp.full((1, LANES), NEG, F32).at[0, :N_EXPERTS].set(moe_router_bias[0])
    x3, h4, route, cnt = _attn_out(os_, ls_, x2, norm_gain[1, 1][None, :], _bf(attn_wo[0]),
                                   router, rbias, b, tm_a)

    tm_e = TOKEN_TILE if t >= N_EXPERTS * TOKEN_TILE else LANES
    ts = _tile(t, TOKEN_TILE)
    counts = cnt[0, :N_EXPERTS].astype(jnp.int32)
    padded = ((counts + tm_e - 1) // tm_e) * tm_e
    ends = jnp.cumsum(padded)
    meta = jnp.concatenate([ends - padded, ends]).astype(jnp.int32)
    route_i = route[:, :4].astype(jnp.int32)
    group_start = jnp.sum(jnp.where(route_i[:, :2, None] == jnp.arange(N_EXPERTS), ends - padded, 0), axis=-1)
    dest = (group_start + route_i[:, 2:4]).astype(jnp.int32).reshape(2 * t)
    rows = 2 * t + N_EXPERTS * tm_e
    n_tiles = rows // tm_e
    n_valid = (ends[-1] // tm_e).astype(jnp.int32)
    tile_src = jnp.minimum(jnp.arange(n_tiles, dtype=jnp.int32), n_valid - 1)
    tile_expert = jnp.minimum(
        jnp.sum(tile_src[:, None] * tm_e >= ends[None, :], axis=1), N_EXPERTS - 1).astype(jnp.int32)

    xs = _moe_scatter(meta, dest, h4, rows, ts, tm_e)
    ffe = moe_w1.shape[3]
    tfe = ffe // 2 if (ffe // 2) % LANES == 0 else ffe
    ys = _moe_ffn(tile_src, tile_expert, n_valid.reshape(1), xs,
                  _bf(moe_w1[0]), _bf(moe_w3[0]), _bf(moe_w2[0]), tm_e, tfe)
    out = _moe_combine(dest, x3, route, final_norm_gain[None, :], ys, ts)
    return out.reshape(b, s, d)
```

```python
import functools
import math

import jax
import jax.numpy as jnp
from jax import lax
from jax.experimental import pallas as pl
from jax.experimental.pallas import tpu as pltpu

F32 = jnp.float32
BF16 = jnp.bfloat16

RMS_EPS = 1e-5
GN_EPS = 64e-5
HEAD = 64
LANES = 128
SUBLANES = 8
BRANCHES = ((128, 1), (512, 4), (2048, 16))
HEADS_PER_BRANCH = 8
ALIBI_MAX = 8.0
N_EXPERTS = 8
NEG = -1e30
VMEM_LIMIT_BYTES = 56 * 1024 * 1024
TOKEN_TILE = 512
RWKV_PRE_TILE = 512
WKV_SEQ_TILE = 1024
WKV_PAIRS = 8


def _params(*sem):
    return pltpu.CompilerParams(dimension_semantics=sem, vmem_limit_bytes=VMEM_LIMIT_BYTES)


def _dot(a, b):
    return jnp.dot(a, b, preferred_element_type=F32)


def _dot_nt(a, b):
    return lax.dot_general(a, b, (((1,), (1,)), ((), ())), preferred_element_type=F32)


def _bf(x):
    return x.astype(BF16)


def _split2(x):
    hi = x.astype(BF16)
    lo = (x - hi.astype(F32)).astype(BF16)
    return hi, lo


def _sigmoid(z):
    return 1.0 / (1.0 + jnp.exp(-z))


def _rms(x, gain):
    return x * lax.rsqrt(jnp.mean(x * x, axis=-1, keepdims=True) + RMS_EPS) * gain


def _const_spec(shape):
    nd = len(shape)
    return pl.BlockSpec(shape, lambda *_: (0,) * nd)


def _rwkv_pre_kernel(x_ref, xp_ref, vec_ref, mu_ref, wr_ref, wk_ref, wv_ref, w1_ref, w2_ref,
                     a1_ref, a2_ref, g1_ref, g2_ref,
                     r_out, lw_out, k_out, v_out, kk_out, a_out, g_out):
    i = pl.program_id(1)
    vec = vec_ref[...]
    gain, w0, a0, k_k, k_a = (vec[j:j + 1] for j in range(5))
    x = x_ref[0]
    h = _rms(x, gain)
    hp = _rms(xp_ref[0][SUBLANES - 1:SUBLANES, :], gain)
    hp = jnp.where(i > 0, hp, 0.0)
    rows = lax.broadcasted_iota(jnp.int32, h.shape, 0)
    hprev = jnp.where(rows == 0, hp, pltpu.roll(h, 1, 0))
    xx = hprev - h
    mu = mu_ref[...]
    xr, xw, xk, xv, xa, xg = (_bf(h + xx * mu[j:j + 1]) for j in range(6))
    w_low = _dot(xw, w1_ref[...])
    a_low = _dot(xa, a1_ref[...])
    g_low = _dot(xg, g1_ref[...])
    r = _dot(xr, wr_ref[...])
    k = _dot(xk, wk_ref[...])
    v = _dot(xv, wv_ref[...])
    wl = w0 + _dot(_bf(jnp.tanh(w_low)), w2_ref[...])
    a = _sigmoid(a0 + _dot(_bf(a_low), a2_ref[...]))
    g = _dot(_bf(_sigmoid(g_low)), g2_ref[...])
    lw_out[0] = _sigmoid(wl) * (-math.exp(-0.5))
    r_out[0] = _bf(r)
    k_out[0] = _bf(k * (1.0 + (a - 1.0) * k_a))
    v_out[0] = _bf(v)
    kk_out[0] = _bf(k * k_k)
    a_out[0] = _bf(a)
    g_out[0] = _bf(g)


def _rwkv_pre(x, vec, mu, wr, wk, wv, w1, w2, a1, a2, g1, g2, tm):
    b, s, d = x.shape
    tok = pl.BlockSpec((1, tm, d), lambda bi, i: (bi, i, 0))
    prev = pl.BlockSpec((1, SUBLANES, d), lambda bi, i: (bi, jnp.maximum(i * (tm // SUBLANES) - 1, 0), 0))
    consts = [vec, mu, wr, wk, wv, w1, w2, a1, a2, g1, g2]
    out = lambda dt: jax.ShapeDtypeStruct((b, s, d), dt)
    return pl.pallas_call(
        _rwkv_pre_kernel,
        grid=(b, s // tm),
        in_specs=[tok, prev] + [_const_spec(c.shape) for c in consts],
        out_specs=[tok] * 7,
        out_shape=[out(BF16), out(F32)] + [out(BF16)] * 5,
        compiler_params=_params("parallel", "arbitrary"),
        name="rwkv_pre",
    )(x, x, *consts)


def _round_robin(generators):
    results = [None] * len(generators)
    live = list(enumerate(generators))
    while live:
        still = []
        for idx, gen in live:
            try:
                results[idx] = next(gen)
                still.append((idx, gen))
            except StopIteration:
                pass
        live = still
    return results


def _wkv_kernel(r_ref, lw_ref, k_ref, v_ref, kk_ref, a_ref, vec_ref, y_ref, st_ref, raw_ref, *, chunk):
    c_len = chunk
    n_chunks = r_ref.shape[1] // c_len
    n_pairs = r_ref.shape[2] // LANES
    inv_steps = max(c_len.bit_length() - 2, 0)
    lane = lax.broadcasted_iota(jnp.int32, (c_len, LANES), 1)
    head0 = lane < HEAD
    rr = lax.broadcasted_iota(jnp.int32, (c_len, c_len), 0)
    cc = lax.broadcasted_iota(jnp.int32, (c_len, c_len), 1)
    lower = rr >= cc
    strict = rr > cc
    eye = jnp.where(rr == cc, 1.0, 0.0).astype(F32)
    r2 = lax.broadcasted_iota(jnp.int32, (LANES, LANES), 0)
    c2 = lax.broadcasted_iota(jnp.int32, (LANES, LANES), 1)
    same_head = (r2 < HEAD) == (c2 < HEAD)

    @pl.when(pl.program_id(2) == 0)
    def _():
        st_ref[...] = jnp.zeros_like(st_ref)
        raw_ref[...] = jnp.zeros_like(raw_ref)

    row_index = lax.broadcasted_iota(jnp.int32, (c_len, LANES), 0)
    scan_shifts = [1 << j for j in range(c_len.bit_length() - 1)]
    rows = lambda mat, j: mat[j * c_len:(j + 1) * c_len]
    own_head = lambda mat: jnp.where(head0, rows(mat, 0), rows(mat, 1))

    def head_sum(x):
        s0 = jnp.sum(jnp.where(head0, x, 0.0), axis=-1, keepdims=True)
        s1 = jnp.sum(jnp.where(head0, 0.0, x), axis=-1, keepdims=True)
        return jnp.where(head0, s0, s1)

    def chunk_pair(sl, lanes, state):
        cum = lw_ref[0, sl, lanes]
        for shift in scan_shifts:
            cum = cum + jnp.where(row_index >= shift, pltpu.roll(cum, shift, 0), 0.0)
        cum_prev = jnp.where(row_index == 0, 0.0, pltpu.roll(cum, 1, 0))
        cum_last = cum[c_len - 1:c_len, :]
        kk = kk_ref[0, sl, lanes].astype(F32)
        kk = kk / jnp.maximum(jnp.sqrt(head_sum(kk * kk)), 1e-12)
        b = kk * a_ref[0, sl, lanes].astype(F32)
        k = k_ref[0, sl, lanes].astype(F32)
        vb = v_ref[0, sl, lanes]
        e_neg = jnp.exp(-cum)
        e_end = jnp.exp(cum_last - cum)
        a_t = -kk * jnp.exp(cum_prev)
        r_t = r_ref[0, sl, lanes].astype(F32) * jnp.exp(cum)
        lhs = _bf(jnp.concatenate([jnp.where(head0, a_t, 0.0), jnp.where(head0, 0.0, a_t),
                                   jnp.where(head0, r_t, 0.0), jnp.where(head0, 0.0, r_t)], axis=0))
        gram = _dot_nt(lhs, _bf(jnp.concatenate([b * e_neg, k * e_neg], axis=0)))
        from_state = _dot_nt(_bf(jnp.concatenate([a_t, r_t], axis=0)), _bf(state))
        yield
        gram_b, gram_k = gram[:, :c_len], gram[:, c_len:]
        from_v = _dot(_bf(jnp.concatenate(
            [jnp.where(strict, rows(gram_k, 0), 0.0), jnp.where(strict, rows(gram_k, 1), 0.0),
             jnp.where(lower, rows(gram_k, 2), 0.0), jnp.where(lower, rows(gram_k, 3), 0.0)], axis=0)), vb)
        m_rb = _bf(jnp.concatenate([jnp.where(lower, rows(gram_b, 2), 0.0),
                                    jnp.where(lower, rows(gram_b, 3), 0.0)], axis=0))
        w = rows(from_state, 0) + own_head(from_v[:2 * c_len])
        y1 = rows(from_state, 1) + own_head(from_v[2 * c_len:])
        pw = [jnp.where(strict, rows(gram_b, hd), 0.0) for hd in range(2)]
        inv = [eye + pw[hd] for hd in range(2)]
        if inv_steps:
            pw = [_dot(_bf(pw[hd]), _bf(pw[hd])) for hd in range(2)]
        for step in range(inv_steps):
            yield
            if step < inv_steps - 1:
                both = [_dot(_bf(jnp.concatenate([inv[hd], pw[hd]], axis=0)), _bf(pw[hd])) for hd in range(2)]
                inv = [inv[hd] + rows(both[hd], 0) for hd in range(2)]
                pw = [rows(both[hd], 1) for hd in range(2)]
            else:
                inv = [inv[hd] + _dot(_bf(inv[hd]), _bf(pw[hd])) for hd in range(2)]
        yield
        u = own_head(_dot(_bf(jnp.concatenate(inv, axis=0)), _bf(w)))
        yield
        y = y1 + own_head(_dot(m_rb, _bf(u)))
        uv_t = _bf(jnp.transpose(jnp.concatenate([u, vb.astype(F32)], axis=0)))
        bk = _bf(jnp.concatenate([b * e_end, k * e_end], axis=0))
        yield y, state * jnp.exp(cum_last) + jnp.where(same_head, _dot(uv_t, bk), 0.0)

    def finish(sl, lanes, y):
        lnx_w, lnx_b, r_k = (vec_ref[j:j + 1, lanes] for j in range(3))
        r, k, v = (ref[0, sl, lanes].astype(F32) for ref in (r_ref, k_ref, v_ref))
        dy = y - head_sum(y) * (1.0 / HEAD)
        yn = dy * lax.rsqrt(head_sum(dy * dy) * (1.0 / HEAD) + GN_EPS) * lnx_w + lnx_b
        y_ref[0, sl, lanes] = _bf(yn + head_sum(r * k * r_k) * v)

    pair_lanes = [slice(p * LANES, (p + 1) * LANES) for p in range(n_pairs)]
    chunk_rows = lambda c: pl.ds(pl.multiple_of(c * c_len, c_len), c_len)

    def body(c, carry):
        before = chunk_rows(jnp.maximum(c - 1, 0))
        for p in range(n_pairs):
            finish(before, pair_lanes[p], raw_ref[p])
        sl = chunk_rows(c)
        results = _round_robin([chunk_pair(sl, pair_lanes[p], st_ref[p]) for p in range(n_pairs)])
        for p, (y, new_state) in enumerate(results):
            raw_ref[p] = y
            st_ref[p] = new_state
        return carry

    lax.fori_loop(0, n_chunks, body, 0, unroll=2 if n_chunks % 2 == 0 else 1)
    for p in range(n_pairs):
        finish(chunk_rows(n_chunks - 1), pair_lanes[p], raw_ref[p])


def _wkv(r, lw, k, v, kk, a, vec, chunk, seq_tile, n_pairs):
    b, s, d = r.shape
    width = n_pairs * LANES
    spec = pl.BlockSpec((1, seq_tile, width), lambda bi, hi, si: (bi, si, hi))
    return pl.pallas_call(
        functools.partial(_wkv_kernel, chunk=chunk),
        grid=(b, d // width, s // seq_tile),
        in_specs=[spec] * 6 + [pl.BlockSpec((vec.shape[0], width), lambda bi, hi, si: (0, hi))],
        out_specs=spec,
        out_shape=jax.ShapeDtypeStruct((b, s, d), BF16),
        scratch_shapes=[pltpu.VMEM((n_pairs, LANES, LANES), F32),
                        pltpu.VMEM((n_pairs, chunk, LANES), F32)],
        compiler_params=_params("parallel", "parallel", "arbitrary"),
        name="wkv7",
    )(r, lw, k, v, kk, a, vec)


def _rwkv_post_kernel(z_ref, g_ref, x_ref, wo_ref, o_ref):
    o_ref[...] = x_ref[...] + _dot(z_ref[...] * g_ref[...], wo_ref[...])


def _rwkv_post(z, g, x, wo, tm):
    t, d = x.shape
    tok = pl.BlockSpec((tm, d), lambda i: (i, 0))
    return pl.pallas_call(
        _rwkv_post_kernel,
        grid=(t // tm,),
        in_specs=[tok] * 3 + [_const_spec(wo.shape)],
        out_specs=tok,
        out_shape=jax.ShapeDtypeStruct((t, d), F32),
        compiler_params=_params("parallel"),
        name="rwkv_post",
    )(z, g, x, wo)


FFN_SUB_ROWS = 512


def _ffn_kernel(x_ref, gain_ref, w1_ref, w3_ref, w2_ref, o_ref, h_ref):
    @pl.when(pl.program_id(1) == 0)
    def _():
        h_ref[...] = _bf(_rms(x_ref[...], gain_ref[...]))
        o_ref[...] = x_ref[...]

    for start in range(0, h_ref.shape[0], FFN_SUB_ROWS):
        rows = slice(start, min(start + FFN_SUB_ROWS, h_ref.shape[0]))
        h = h_ref[rows, :]
        u = _dot(h, w1_ref[...])
        act = _bf(u * _sigmoid(u) * _dot(h, w3_ref[...]))
        o_ref[rows, :] += _dot(act, w2_ref[...])


def _ffn(x, gain, w1, w3, w2, tm, tf):
    t, d = x.shape
    ff = w1.shape[1]
    tok = pl.BlockSpec((tm, d), lambda i, f: (i, 0))
    return pl.pallas_call(
        _ffn_kernel,
        grid=(t // tm, ff // tf),
        in_specs=[tok, _const_spec(gain.shape),
                  pl.BlockSpec((d, tf), lambda i, f: (0, f)),
                  pl.BlockSpec((d, tf), lambda i, f: (0, f)),
                  pl.BlockSpec((tf, d), lambda i, f: (f, 0))],
        out_specs=tok,
        out_shape=jax.ShapeDtypeStruct((t, d), F32),
        scratch_shapes=[pltpu.VMEM((tm, d), BF16)],
        compiler_params=_params("parallel", "arbitrary"),
        name="ffn_dense",
    )(x, gain, w1, w3, w2)


BRANCH_WIDTH = HEADS_PER_BRANCH * HEAD


def _attn_proj_kernel(x_ref, gains_ref, wq_ref, wkv_ref, *refs):
    outs, (q_scr, kv_scr) = refs[:-2], refs[-2:]
    x = x_ref[0]
    tm = x.shape[0]
    n = x * lax.rsqrt(jnp.mean(x * x, axis=-1, keepdims=True) + RMS_EPS)
    gains = gains_ref[...]
    q_tiles = q_scr.shape[0]
    qw = q_tiles * LANES
    tiles_br = BRANCH_WIDTH // LANES
    hq = _bf(n * gains[0:1])
    hkv = _bf(n * gains[1:2])

    def project(h, w_ref, col, scr, first_tile, scale=None):
        val = _dot(h, w_ref[:, col:col + BRANCH_WIDTH])
        for ti in range(tiles_br):
            tile = val[:, ti * LANES:(ti + 1) * LANES]
            scr[first_tile + ti] = tile if scale is None else tile * scale

    def permute(out, dil, scr, first_tile):
        for res in range(dil):
            rows = pl.ds(res, tm // dil, stride=dil) if dil > 1 else slice(None)
            for ti in range(tiles_br):
                out[0, res, :, ti * LANES:(ti + 1) * LANES] = _bf(scr[first_tile + ti, rows, :])

    pending = None
    for which, (h, w_ref, base, scr, tile0, scale) in enumerate(
            ((hq, wq_ref, 0, q_scr, 0, 1.0 / HEAD ** 0.5), (hkv, wkv_ref, 0, kv_scr, 0, None),
             (hkv, wkv_ref, qw, kv_scr, q_tiles, None))):
        for br, (_, dil) in enumerate(BRANCHES):
            project(h, w_ref, base + br * BRANCH_WIDTH, scr, tile0 + br * tiles_br, scale)
            if pending is not None:
                permute(*pending)
            pending = (outs[3 * br + which], dil, scr, tile0 + br * tiles_br)
    permute(*pending)


def _attn_proj(x, gains, wq, wkv, tm):
    b, s, d = x.shape
    qw = wq.shape[1]
    out_specs, out_shape = [], []
    for _, dil in BRANCHES:
        assert tm % (2 * SUBLANES * dil) == 0
        out_specs += [pl.BlockSpec((1, dil, tm // dil, BRANCH_WIDTH), lambda bi, i: (bi, 0, i, 0))] * 3
        out_shape += [jax.ShapeDtypeStruct((b, dil, s // dil, BRANCH_WIDTH), BF16)] * 3
    return pl.pallas_call(
        _attn_proj_kernel,
        grid=(b, s // tm),
        in_specs=[pl.BlockSpec((1, tm, d), lambda bi, i: (bi, i, 0)),
                  _const_spec(gains.shape), _const_spec(wq.shape), _const_spec(wkv.shape)],
        out_specs=out_specs,
        out_shape=out_shape,
        scratch_shapes=[pltpu.VMEM((qw // LANES, tm, LANES), F32),
                        pltpu.VMEM((2 * qw // LANES, tm, LANES), F32)],
        compiler_params=_params("parallel", "parallel"),
        name="attn_proj",
    )(x, gains, wq, wkv)


def _attn_kernel(slopes_ref, q_ref, k_ref, v_ref, o_ref, lse_ref, *, branch, dilation, qb):
    sub_len = q_ref.shape[2]
    n_blk = sub_len // qb
    n_pairs = q_ref.shape[3] // LANES
    col2 = lax.broadcasted_iota(jnp.int32, (qb, 2 * qb), 1)
    rel = (lax.broadcasted_iota(jnp.int32, (qb, 2 * qb), 0) - jnp.where(col2 < qb, col2, col2 - qb)).astype(F32)
    dist_d = jnp.where(rel >= 0, rel, -NEG)
    dist_p = jnp.where(rel <= 0, rel + float(qb), -NEG)
    first_head = lax.broadcasted_iota(jnp.int32, (1, 2 * qb), 1) < qb
    lane = lax.broadcasted_iota(jnp.int32, (qb, LANES), 1)
    head0 = lane < HEAD
    lane_row = lax.broadcasted_iota(jnp.int32, (1, LANES), 1)
    head_mask = [jnp.where(lane_row < HEAD, 1.0, 0.0).astype(BF16),
                 jnp.where(lane_row < HEAD, 0.0, 1.0).astype(BF16)]
    head_ones = [jnp.where(head0, 1.0, 0.0).astype(BF16), jnp.where(head0, 0.0, 1.0).astype(BF16)]

    block_rows = lambda i: pl.ds(pl.multiple_of(i * qb, qb), qb)

    def blocks(items):
        def one_pair(r, i, p):
            cur = block_rows(i)
            if n_blk > 1:
                prv = block_rows(jnp.maximum(i - 1, 0))
                has_prev = jnp.where(i > 0, 1.0, -NEG)
            lanes = slice(p * LANES, (p + 1) * LANES)
            slope = [slopes_ref[branch * HEADS_PER_BRANCH + 2 * p + j] * float(dilation) for j in range(2)]
            step = jnp.where(first_head, slope[0], slope[1])

            def keys(rows):
                k = k_ref[0, r, rows, lanes]
                return jnp.concatenate([k * head_mask[0], k * head_mask[1]], axis=0)

            def values(rows):
                v = v_ref[0, r, rows, lanes]
                return jnp.concatenate(
                    [jnp.concatenate([v * head_mask[0], head_ones[0]], axis=1),
                     jnp.concatenate([v * head_mask[1], head_ones[1]], axis=1)], axis=0)

            q = q_ref[0, r, cur, lanes]
            sd = _dot_nt(q, keys(cur)) - step * dist_d
            if n_blk > 1:
                sp = _dot_nt(q, keys(prv)) - (step * has_prev) * dist_p
            yield
            both = jnp.maximum(sd, sp) if n_blk > 1 else sd
            m = [jnp.max(both[:, :qb], axis=-1, keepdims=True), jnp.max(both[:, qb:], axis=-1, keepdims=True)]
            probs = lambda s: _bf(jnp.concatenate([jnp.exp(s[:, :qb] - m[0]), jnp.exp(s[:, qb:] - m[1])], axis=1))
            acc = _dot(probs(sd), values(cur))
            if n_blk > 1:
                acc = acc + _dot(probs(sp), values(prv))
            yield
            den = acc[:, LANES:]
            lse = jnp.where(head0, m[0], m[1]) + jnp.log(den)
            yield acc[:, :LANES] / den, lse[:, 0:1], lse[:, HEAD:HEAD + 1]

        res = _round_robin([one_pair(r, i, p) for r, i in items for p in range(n_pairs)])
        for n, (r, i) in enumerate(items):
            lse = jnp.zeros((qb, LANES), F32)
            for p in range(n_pairs):
                out, l_a, l_b = res[n * n_pairs + p]
                o_ref[0, r, block_rows(i), p * LANES:(p + 1) * LANES] = _bf(out)
                lse = jnp.where(lane == 2 * p, l_a, jnp.where(lane == 2 * p + 1, l_b, lse))
            lse_ref[0, r, block_rows(i), :] = lse

    n_res = q_ref.shape[1]
    if n_blk == 1:
        blocks([(r, 0) for r in range(n_res)])
    else:
        group = next(g for g in (4, 2, 1) if n_blk % g == 0)
        for r in range(n_res):
            lax.fori_loop(0, n_blk // group,
                          lambda j, c, r=r: (blocks([(r, group * j + g) for g in range(group)]), c)[1], 0)


def _attn_branch(slopes, q, k, v, branch):
    window, dilation = BRANCHES[branch]
    assert window % dilation == 0 and window // dilation == LANES
    b, _, sub_len, width = q.shape
    qb = min(LANES, sub_len)
    assert sub_len % qb == 0
    n_res = max(1, min(dilation, TOKEN_TILE // sub_len))
    assert dilation % n_res == 0
    spec = pl.BlockSpec((1, n_res, sub_len, width), lambda bi, ri: (bi, ri, 0, 0))
    return pl.pallas_call(
        functools.partial(_attn_kernel, branch=branch, dilation=dilation, qb=qb),
        grid=(b, dilation // n_res),
        in_specs=[pl.BlockSpec(memory_space=pltpu.SMEM)] + [spec] * 3,
        out_specs=[spec, pl.BlockSpec((1, n_res, sub_len, LANES), lambda bi, ri: (bi, ri, 0, 0))],
        out_shape=[jax.ShapeDtypeStruct(q.shape, BF16),
                   jax.ShapeDtypeStruct(q.shape[:3] + (LANES,), F32)],
        compiler_params=_params("parallel", "parallel"),
        name=f"dilated_attn_{branch}",
    )(slopes, q, k, v)


def _attn_out_kernel(o0, o1, o2, l0, l1, l2, x_ref, gain_ref, wo_ref, router_ref, rb_ref, expand_ref,
                     x_out, h_out, route_out, cnt_out, cnt_ref, o_scr, l_scr):
    @pl.when((pl.program_id(0) == 0) & (pl.program_id(1) == 0))
    def _():
        cnt_ref[...] = jnp.zeros_like(cnt_ref)

    tm = x_ref.shape[0]
    tiles_br = BRANCH_WIDTH // LANES
    for br, (o_ref, l_ref) in enumerate(((o0, l0), (o1, l1), (o2, l2))):
        dil = BRANCHES[br][1]
        for res in range(dil):
            rows = pl.ds(res, tm // dil, stride=dil) if dil > 1 else slice(None)
            l_scr[br, rows, :] = l_ref[0, res]
            for ti in range(tiles_br):
                lanes = slice(ti * LANES, (ti + 1) * LANES)
                o_scr[br, ti, rows, :] = o_ref[0, res, :, lanes].astype(F32)
    ls = [l_scr[br] for br in range(3)]
    m = jnp.maximum(jnp.maximum(ls[0], ls[1]), ls[2])
    ws = [jnp.exp(l - m) for l in ls]
    total = ws[0] + ws[1] + ws[2]
    spread = [_dot(_bf(ws[br] / total), expand_ref[...]) for br in range(3)]
    merged = []
    for ti in range(tiles_br):
        lanes = slice(ti * LANES, (ti + 1) * LANES)
        merged.append(_bf(sum(spread[br][:, lanes] * o_scr[br, ti] for br in range(3))))
    x = x_ref[...] + _dot(jnp.concatenate(merged, axis=1), wo_ref[...])
    x_out[...] = x
    h = _rms(x, gain_ref[...])
    h_out[...] = h
    hhi, hlo = _split2(h)
    both = _dot(hhi, router_ref[...])
    logits = both[:, :LANES] + both[:, LANES:] + _dot(hlo, router_ref[:, :LANES]) + rb_ref[...]
    lane = lax.broadcasted_iota(jnp.int32, logits.shape, 1)
    m1 = jnp.max(logits, axis=-1, keepdims=True)
    i1 = jnp.min(jnp.where(logits == m1, lane, LANES), axis=-1, keepdims=True)
    rest = jnp.where(lane == i1, 2 * NEG, logits)
    m2 = jnp.max(rest, axis=-1, keepdims=True)
    i2 = jnp.min(jnp.where(rest == m2, lane, LANES), axis=-1, keepdims=True)
    ex = jnp.exp(m2 - m1)
    sel = jnp.where((lane == i1) | (lane == i2), 1.0, 0.0)
    rr = lax.broadcasted_iota(jnp.int32, (tm, tm), 0)
    cc = lax.broadcasted_iota(jnp.int32, (tm, tm), 1)
    before = jnp.where(rr > cc, 1.0, 0.0).astype(BF16)
    rank = cnt_ref[...] + _dot(before, _bf(sel))
    cnt_ref[...] += jnp.sum(sel, axis=0, keepdims=True)
    cnt_out[...] = cnt_ref[...]
    columns = [i1.astype(F32), i2.astype(F32),
               jnp.sum(jnp.where(lane == i1, rank, 0.0), axis=-1, keepdims=True),
               jnp.sum(jnp.where(lane == i2, rank, 0.0), axis=-1, keepdims=True),
               1.0 / (1.0 + ex), ex / (1.0 + ex)]
    route = jnp.zeros(logits.shape, F32)
    for j, col in enumerate(columns):
        route = jnp.where(lane == j, col, route)
    route_out[...] = route


def _attn_out(os_, ls_, x, gain, wo, router, rb, b, tm):
    t, d = x.shape
    tiles = t // (b * tm)
    tok = pl.BlockSpec((tm, d), lambda bi, i: (bi * tiles + i, 0))
    small = pl.BlockSpec((tm, LANES), lambda bi, i: (bi * tiles + i, 0))
    att = [pl.BlockSpec((1, dil, tm // dil, BRANCH_WIDTH), lambda bi, i: (bi, 0, i, 0))
           for _, dil in BRANCHES]
    lse = [pl.BlockSpec((1, dil, tm // dil, LANES), lambda bi, i: (bi, 0, i, 0)) for _, dil in BRANCHES]
    expand = (jnp.arange(LANES)[:, None] == jnp.arange(BRANCH_WIDTH)[None, :] // HEAD).astype(BF16)
    consts = [gain, wo, router, rb, expand]
    return pl.pallas_call(
        _attn_out_kernel,
        grid=(b, tiles),
        in_specs=att + lse + [tok] + [_const_spec(c.shape) for c in consts],
        out_specs=[tok, tok, small, _const_spec((1, LANES))],
        out_shape=[jax.ShapeDtypeStruct((t, d), F32), jax.ShapeDtypeStruct((t, d), F32),
                   jax.ShapeDtypeStruct((t, LANES), F32), jax.ShapeDtypeStruct((1, LANES), F32)],
        scratch_shapes=[pltpu.VMEM((1, LANES), F32),
                        pltpu.VMEM((len(BRANCHES), BRANCH_WIDTH // LANES, tm, LANES), F32),
                        pltpu.VMEM((len(BRANCHES), tm, LANES), F32)],
        compiler_params=_params("arbitrary", "arbitrary"),
        name="attn_out_route",
    )(*os_, *ls_, x, *consts)


DMA_UNROLL = 8


def _row_copy(src, src_row, dst, dst_row, sem):
    return pltpu.make_async_copy(src.at[pl.ds(src_row, 1)], dst.at[pl.ds(dst_row, 1)], sem)


def _moe_scatter_kernel(meta_ref, dest_ref, h_ref, out_ref, zeros, sem, zero_sem):
    n_tok = h_ref.shape[0]
    tile = zeros.shape[0]
    ends = lambda e: meta_ref[N_EXPERTS + e]

    def zero_fill(e):
        if e < N_EXPERTS:
            start, used = jnp.maximum(ends(e) - tile, 0), ends(e) > meta_ref[e]
        else:
            start = ends(N_EXPERTS - 1) + (e - N_EXPERTS) * tile
            used = start < out_ref.shape[0]
            start = jnp.minimum(start, out_ref.shape[0] - tile)
        dst = out_ref.at[pl.ds(pl.multiple_of(start, tile), tile)]
        return used, pltpu.make_async_copy(zeros, dst, zero_sem)

    @pl.when(pl.program_id(0) == 0)
    def _():
        zeros[...] = jnp.zeros_like(zeros)
        for e in range(2 * N_EXPERTS):
            used, copy = zero_fill(e)
            pl.when(used)(copy.start)
        for e in range(2 * N_EXPERTS):
            used, copy = zero_fill(e)
            pl.when(used)(copy.wait)

    def issue(j, carry):
        for k in range(2):
            _row_copy(h_ref, j, out_ref, dest_ref[2 * j + k], sem).start(priority=k)
        return carry

    def drain(j, carry):
        for _ in range(2):
            _row_copy(h_ref, 0, out_ref, 0, sem).wait()
        return carry

    lax.fori_loop(0, n_tok, issue, 0, unroll=DMA_UNROLL)
    lax.fori_loop(0, n_tok, drain, 0, unroll=DMA_UNROLL)


def _moe_scatter(meta, dest_flat, h, rows, ts, tile):
    t, d = h.shape
    return pl.pallas_call(
        _moe_scatter_kernel,
        grid=(t // ts,),
        in_specs=[pl.BlockSpec(memory_space=pltpu.SMEM),
                  pl.BlockSpec((2 * ts,), lambda i: (i,), memory_space=pltpu.SMEM),
                  pl.BlockSpec((ts, d), lambda i: (i, 0))],
        out_specs=pl.BlockSpec(memory_space=pl.ANY),
        out_shape=jax.ShapeDtypeStruct((rows, d), F32),
        scratch_shapes=[pltpu.VMEM((tile, d), F32), pltpu.SemaphoreType.DMA(()),
                        pltpu.SemaphoreType.DMA(())],
        compiler_params=_params("arbitrary"),
        name="moe_scatter",
    )(meta, dest_flat, h)


def _moe_ffn_kernel(src_ref, te_ref, nv_ref, x_ref, w1_ref, w3_ref, w2_ref, o_ref):
    del src_ref, te_ref
    @pl.when(pl.program_id(1) == 0)
    def _():
        o_ref[...] = jnp.zeros_like(o_ref)

    @pl.when(pl.program_id(0) < nv_ref[0])
    def _():
        h = _bf(x_ref[...])
        u = _dot(h, w1_ref[0])
        act = _bf(u * _sigmoid(u) * _dot(h, w3_ref[0]))
        o_ref[...] += _dot(act, w2_ref[0])


def _moe_ffn(tile_src, tile_expert, n_valid, xs, w1, w3, w2, tm, tf):
    rows, d = xs.shape
    ff = w1.shape[2]
    tok = pl.BlockSpec((tm, d), lambda i, f, src, te, nv: (src[i], 0))
    return pl.pallas_call(
        _moe_ffn_kernel,
        grid_spec=pltpu.PrefetchScalarGridSpec(
            num_scalar_prefetch=3,
            grid=(rows // tm, ff // tf),
            in_specs=[tok,
                      pl.BlockSpec((1, d, tf), lambda i, f, src, te, nv: (te[i], 0, f)),
                      pl.BlockSpec((1, d, tf), lambda i, f, src, te, nv: (te[i], 0, f)),
                      pl.BlockSpec((1, tf, d), lambda i, f, src, te, nv: (te[i], f, 0))],
            out_specs=pl.BlockSpec((tm, d), lambda i, f, src, te, nv: (i, 0)),
        ),
        out_shape=jax.ShapeDtypeStruct((rows, d), F32),
        compiler_params=_params("arbitrary", "arbitrary"),
        name="moe_experts",
    )(tile_src, tile_expert, n_valid, xs, w1, w3, w2)


def _moe_combine_kernel(dest_ref, x_ref, gates_ref, gain_ref, y_hbm, o_ref, buf, sem):
    n_tok = x_ref.shape[0]

    def issue(j, carry):
        for k in range(2):
            _row_copy(y_hbm, dest_ref[2 * j + k], buf.at[k], j, sem).start(priority=k)
        return carry

    def drain(j, carry):
        for k in range(2):
            _row_copy(y_hbm, 0, buf.at[k], 0, sem).wait()
        return carry

    lax.fori_loop(0, n_tok, issue, 0, unroll=DMA_UNROLL)
    lax.fori_loop(0, n_tok, drain, 0, unroll=DMA_UNROLL)
    gates = gates_ref[...]
    x = x_ref[...] + gates[:, 4:5] * buf[0] + gates[:, 5:6] * buf[1]
    o_ref[...] = _rms(x, gain_ref[...])


def _moe_combine(dest_flat, x, route, gain, ys, ts):
    t, d = x.shape
    tok = pl.BlockSpec((ts, d), lambda i: (i, 0))
    return pl.pallas_call(
        _moe_combine_kernel,
        grid=(t // ts,),
        in_specs=[pl.BlockSpec((2 * ts,), lambda i: (i,), memory_space=pltpu.SMEM),
                  tok, pl.BlockSpec((ts, LANES), lambda i: (i, 0)), _const_spec(gain.shape),
                  pl.BlockSpec(memory_space=pl.ANY)],
        out_specs=tok,
        out_shape=jax.ShapeDtypeStruct((t, d), F32),
        scratch_shapes=[pltpu.VMEM((2, ts, d), F32), pltpu.SemaphoreType.DMA(())],
        compiler_params=_params("arbitrary"),
        name="moe_combine",
    )(dest_flat, x, route, gain, ys)


def _tile(n, want):
    t = min(n, want)
    assert n % t == 0
    return t


def _pad_cols(w, n):
    return jnp.pad(w, ((0, 0), (0, n - w.shape[1])))


def _pad_rows(w, n):
    return jnp.pad(w, ((0, n - w.shape[0]), (0, 0)))


def kernel(x, norm_gain, rwkv_mu, rwkv_wr, rwkv_wk, rwkv_wv, rwkv_w0, rwkv_w1, rwkv_w2, rwkv_a0, rwkv_a1, rwkv_a2, rwkv_g1, rwkv_g2, rwkv_k_k, rwkv_k_a, rwkv_r_k, rwkv_lnx_w, rwkv_lnx_b, rwkv_wo, kv_norm_gain, w_kv, attn_wq, attn_wo, ffn_w1, ffn_w3, ffn_w2, moe_router, moe_router_bias, moe_w1, moe_w3, moe_w2, final_norm_gain):
    b, s, d = x.shape
    t = b * s
    assert norm_gain.shape[0] == 2 and d % LANES == 0 and s % SUBLANES == 0

    zeros = jnp.zeros((d,), F32)
    vec_pre = jnp.stack([norm_gain[0, 0], rwkv_w0[0], rwkv_a0[0], rwkv_k_k[0], rwkv_k_a[0],
                         zeros, zeros, zeros])
    tm_pre = _tile(s, RWKV_PRE_TILE)
    chunk = _tile(s, LANES)
    r, lw, k, v, kk, a, g = _rwkv_pre(
        x, vec_pre, rwkv_mu[0], _bf(rwkv_wr[0]), _bf(rwkv_wk[0]), _bf(rwkv_wv[0]),
        _bf(_pad_cols(rwkv_w1[0], LANES)), _bf(_pad_rows(rwkv_w2[0], LANES)),
        _bf(_pad_cols(rwkv_a1[0], LANES)), _bf(_pad_rows(rwkv_a2[0], LANES)),
        _bf(rwkv_g1[0]), _bf(rwkv_g2[0]), tm_pre)
    vec_post = jnp.stack([rwkv_lnx_w[0], rwkv_lnx_b[0], rwkv_r_k[0].reshape(d),
                          zeros, zeros, zeros, zeros, zeros])
    z = _wkv(r, lw, k, v, kk, a, vec_post, chunk=chunk, seq_tile=_tile(s, WKV_SEQ_TILE),
             n_pairs=WKV_PAIRS)

    flat = lambda arr: arr.reshape(t, d)
    tm = _tile(t, TOKEN_TILE)
    x1 = _rwkv_post(flat(z), flat(g), flat(x), _bf(rwkv_wo[0]), tm)

    ff = ffn_w1.shape[2]
    tf = ff // 2 if (ff // 2) % LANES == 0 else ff
    x2 = _ffn(x1, norm_gain[0, 1][None, :], _bf(ffn_w1[0]), _bf(ffn_w3[0]), _bf(ffn_w2[0]),
              _tile(t, 2 * FFN_SUB_ROWS), tf)

    n_slopes = len(BRANCHES) * HEADS_PER_BRANCH
    slopes = jnp.exp2(-ALIBI_MAX * (jnp.arange(n_slopes, dtype=F32) + 1.0) / n_slopes)
    gains = jnp.stack([norm_gain[1, 0], kv_norm_gain] + [zeros] * 6)
    tm_a = _tile(s, TOKEN_TILE)
    qkv = _attn_proj(x2.reshape(b, s, d), gains, _bf(attn_wq[0]), _bf(w_kv), tm_a)
    os_, ls_ = [], []
    for br in range(len(BRANCHES)):
        o_br, l_br = _attn_branch(slopes, *qkv[3 * br:3 * br + 3], br)
        os_.append(o_br)
        ls_.append(l_br)

    router = jnp.concatenate(_split2(_pad_cols(moe_router[0], LANES)), axis=1)
    rbias = jnp.full((1, LANES), NEG, F32).at[0, :N_EXPERTS].set(moe_router_bias[0])
    x3, h4, route, cnt = _attn_out(os_, ls_, x2, norm_gain[1, 1][None, :], _bf(attn_wo[0]),
                                   router, rbias, b, tm_a)

    tm_e = TOKEN_TILE if t >= N_EXPERTS * TOKEN_TILE else LANES
    ts = _tile(t, TOKEN_TILE)
    counts = cnt[0, :N_EXPERTS].astype(jnp.int32)
    padded = ((counts + tm_e - 1) // tm_e) * tm_e
    ends = jnp.cumsum(padded)
    meta = jnp.concatenate([ends - padded, ends]).astype(jnp.int32)
    route_i = route[:, :4].astype(jnp.int32)
    group_start = jnp.sum(jnp.where(route_i[:, :2, None] == jnp.arange(N_EXPERTS), ends - padded, 0), axis=-1)
    dest = (group_start + route_i[:, 2:4]).astype(jnp.int32).reshape(2 * t)
    rows = 2 * t + N_EXPERTS * tm_e
    n_tiles = rows // tm_e
    n_valid = (ends[-1] // tm_e).astype(jnp.int32)
    tile_src = jnp.minimum(jnp.arange(n_tiles, dtype=jnp.int32), n_valid - 1)
    tile_expert = jnp.minimum(
        jnp.sum(tile_src[:, None] * tm_e >= ends[None, :], axis=1), N_EXPERTS - 1).astype(jnp.int32)

    xs = _moe_scatter(meta, dest, h4, rows, ts, tm_e)
    ffe = moe_w1.shape[3]
    tfe = ffe // 2 if (ffe // 2) % LANES == 0 else ffe
    ys = _moe_ffn(tile_src, tile_expert, n_valid.reshape(1), xs,
                  _bf(moe_w1[0]), _bf(moe_w3[0]), _bf(moe_w2[0]), tm_e, tfe)
    out = _moe_combine(dest, x3, route, final_norm_gain[None, :], ys, ts)
    return out.reshape(b, s, d)
```

```python
import functools
import math

import jax
import jax.numpy as jnp
from jax import lax
from jax.experimental import pallas as pl
from jax.experimental.pallas import tpu as pltpu

F32 = jnp.float32
BF16 = jnp.bfloat16

RMS_EPS = 1e-5
GN_EPS = 64e-5
HEAD = 64
LANES = 128
SUBLANES = 8
BRANCHES = ((128, 1), (512, 4), (2048, 16))
HEADS_PER_BRANCH = 8
ALIBI_MAX = 8.0
N_EXPERTS = 8
NEG = -1e30
VMEM_LIMIT_BYTES = 56 * 1024 * 1024
TOKEN_TILE = 512
RWKV_PRE_TILE = 512
WKV_SEQ_TILE = 1024
WKV_PAIRS = 8


def _params(*sem):
    return pltpu.CompilerParams(dimension_semantics=sem, vmem_limit_bytes=VMEM_LIMIT_BYTES)


def _dot(a, b):
    return jnp.dot(a, b, preferred_element_type=F32)


def _dot_nt(a, b):
    return lax.dot_general(a, b, (((1,), (1,)), ((), ())), preferred_element_type=F32)


def _bf(x):
    return x.astype(BF16)


def _split2(x):
    hi = x.astype(BF16)
    lo = (x - hi.astype(F32)).astype(BF16)
    return hi, lo


def _sigmoid(z):
    return 1.0 / (1.0 + jnp.exp(-z))


def _rms(x, gain):
    return x * lax.rsqrt(jnp.mean(x * x, axis=-1, keepdims=True) + RMS_EPS) * gain


def _const_spec(shape):
    nd = len(shape)
    return pl.BlockSpec(shape, lambda *_: (0,) * nd)


def _rwkv_pre_kernel(x_ref, xp_ref, vec_ref, mu_ref, wr_ref, wk_ref, wv_ref, w1_ref, w2_ref,
                     a1_ref, a2_ref, g1_ref, g2_ref,
                     r_out, lw_out, k_out, v_out, kk_out, a_out, g_out):
    i = pl.program_id(1)
    vec = vec_ref[...]
    gain, w0, a0, k_k, k_a = (vec[j:j + 1] for j in range(5))
    x = x_ref[0]
    h = _rms(x, gain)
    hp = _rms(xp_ref[0][SUBLANES - 1:SUBLANES, :], gain)
    hp = jnp.where(i > 0, hp, 0.0)
    rows = lax.broadcasted_iota(jnp.int32, h.shape, 0)
    hprev = jnp.where(rows == 0, hp, pltpu.roll(h, 1, 0))
    xx = hprev - h
    mu = mu_ref[...]
    xr, xw, xk, xv, xa, xg = (_bf(h + xx * mu[j:j + 1]) for j in range(6))
    w_low = _dot(xw, w1_ref[...])
    a_low = _dot(xa, a1_ref[...])
    g_low = _dot(xg, g1_ref[...])
    r = _dot(xr, wr_ref[...])
    k = _dot(xk, wk_ref[...])
    v = _dot(xv, wv_ref[...])
    wl = w0 + _dot(_bf(jnp.tanh(w_low)), w2_ref[...])
    a = _sigmoid(a0 + _dot(_bf(a_low), a2_ref[...]))
    g = _dot(_bf(_sigmoid(g_low)), g2_ref[...])
    lw_out[0] = _sigmoid(wl) * (-math.exp(-0.5))
    r_out[0] = _bf(r)
    k_out[0] = _bf(k * (1.0 + (a - 1.0) * k_a))
    v_out[0] = _bf(v)
    kk_out[0] = _bf(k * k_k)
    a_out[0] = _bf(a)
    g_out[0] = _bf(g)


def _rwkv_pre(x, vec, mu, wr, wk, wv, w1, w2, a1, a2, g1, g2, tm):
    b, s, d = x.shape
    tok = pl.BlockSpec((1, tm, d), lambda bi, i: (bi, i, 0))
    prev = pl.BlockSpec((1, SUBLANES, d), lambda bi, i: (bi, jnp.maximum(i * (tm // SUBLANES) - 1, 0), 0))
    consts = [vec, mu, wr, wk, wv, w1, w2, a1, a2, g1, g2]
    out = lambda dt: jax.ShapeDtypeStruct((b, s, d), dt)
    return pl.pallas_call(
        _rwkv_pre_kernel,
        grid=(b, s // tm),
        in_specs=[tok, prev] + [_const_spec(c.shape) for c in consts],
        out_specs=[tok] * 7,
        out_shape=[out(BF16), out(F32)] + [out(BF16)] * 5,
        compiler_params=_params("parallel", "arbitrary"),
        name="rwkv_pre",
    )(x, x, *consts)


def _round_robin(generators):
    results = [None] * len(generators)
    live = list(enumerate(generators))
    while live:
        still = []
        for idx, gen in live:
            try:
                results[idx] = next(gen)
                still.append((idx, gen))
            except StopIteration:
                pass
        live = still
    return results


def _wkv_kernel(r_ref, lw_ref, k_ref, v_ref, kk_ref, a_ref, vec_ref, y_ref, st_ref, raw_ref, *, chunk):
    c_len = chunk
    n_chunks = r_ref.shape[1] // c_len
    n_pairs = r_ref.shape[2] // LANES
    inv_steps = max(c_len.bit_length() - 2, 0)
    lane = lax.broadcasted_iota(jnp.int32, (c_len, LANES), 1)
    head0 = lane < HEAD
    rr = lax.broadcasted_iota(jnp.int32, (c_len, c_len), 0)
    cc = lax.broadcasted_iota(jnp.int32, (c_len, c_len), 1)
    lower = rr >= cc
    strict = rr > cc
    eye = jnp.where(rr == cc, 1.0, 0.0).astype(F32)
    r2 = lax.broadcasted_iota(jnp.int32, (LANES, LANES), 0)
    c2 = lax.broadcasted_iota(jnp.int32, (LANES, LANES), 1)
    same_head = (r2 < HEAD) == (c2 < HEAD)

    @pl.when(pl.program_id(2) == 0)
    def _():
        st_ref[...] = jnp.zeros_like(st_ref)
        raw_ref[...] = jnp.zeros_like(raw_ref)

    row_index = lax.broadcasted_iota(jnp.int32, (c_len, LANES), 0)
    scan_shifts = [1 << j for j in range(c_len.bit_length() - 1)]
    rows = lambda mat, j: mat[j * c_len:(j + 1) * c_len]
    own_head = lambda mat: jnp.where(head0, rows(mat, 0), rows(mat, 1))

    def head_sum(x):
        s0 = jnp.sum(jnp.where(head0, x, 0.0), axis=-1, keepdims=True)
        s1 = jnp.sum(jnp.where(head0, 0.0, x), axis=-1, keepdims=True)
        return jnp.where(head0, s0, s1)

    def chunk_pair(sl, lanes, state):
        cum = lw_ref[0, sl, lanes]
        for shift in scan_shifts:
            cum = cum + jnp.where(row_index >= shift, pltpu.roll(cum, shift, 0), 0.0)
        cum_prev = jnp.where(row_index == 0, 0.0, pltpu.roll(cum, 1, 0))
        cum_last = cum[c_len - 1:c_len, :]
        kk = kk_ref[0, sl, lanes].astype(F32)
        kk = kk / jnp.maximum(jnp.sqrt(head_sum(kk * kk)), 1e-12)
        b = kk * a_ref[0, sl, lanes].astype(F32)
        k = k_ref[0, sl, lanes].astype(F32)
        vb = v_ref[0, sl, lanes]
        e_neg = jnp.exp(-cum)
        e_end = jnp.exp(cum_last - cum)
        a_t = -kk * jnp.exp(cum_prev)
        r_t = r_ref[0, sl, lanes].astype(F32) * jnp.exp(cum)
        lhs = _bf(jnp.concatenate([jnp.where(head0, a_t, 0.0), jnp.where(head0, 0.0, a_t),
                                   jnp.where(head0, r_t, 0.0), jnp.where(head0, 0.0, r_t)], axis=0))
        gram = _dot_nt(lhs, _bf(jnp.concatenate([b * e_neg, k * e_neg], axis=0)))
        from_state = _dot_nt(_bf(jnp.concatenate([a_t, r_t], axis=0)), _bf(state))
        yield
        gram_b, gram_k = gram[:, :c_len], gram[:, c_len:]
        from_v = _dot(_bf(jnp.concatenate(
            [jnp.where(strict, rows(gram_k, 0), 0.0), jnp.where(strict, rows(gram_k, 1), 0.0),
             jnp.where(lower, rows(gram_k, 2), 0.0), jnp.where(lower, rows(gram_k, 3), 0.0)], axis=0)), vb)
        m_rb = _bf(jnp.concatenate([jnp.where(lower, rows(gram_b, 2), 0.0),
                                    jnp.where(lower, rows(gram_b, 3), 0.0)], axis=0))
        w = rows(from_state, 0) + own_head(from_v[:2 * c_len])
        y1 = rows(from_state, 1) + own_head(from_v[2 * c_len:])
        pw = [jnp.where(strict, rows(gram_b, hd), 0.0) for hd in range(2)]
        inv = [eye + pw[hd] for hd in range(2)]
        if inv_steps:
            pw = [_dot(_bf(pw[hd]), _bf(pw[hd])) for hd in range(2)]
        for step in range(inv_steps):
            yield
            if step < inv_steps - 1:
                both = [_dot(_bf(jnp.concatenate([inv[hd], pw[hd]], axis=0)), _bf(pw[hd])) for hd in range(2)]
                inv = [inv[hd] + rows(both[hd], 0) for hd in range(2)]
                pw = [rows(both[hd], 1) for hd in range(2)]
            else:
                inv = [inv[hd] + _dot(_bf(inv[hd]), _bf(pw[hd])) for hd in range(2)]
        yield
        u = own_head(_dot(_bf(jnp.concatenate(inv, axis=0)), _bf(w)))
        yield
        y = y1 + own_head(_dot(m_rb, _bf(u)))
        uv_t = _bf(jnp.transpose(jnp.concatenate([u, vb.astype(F32)], axis=0)))
        bk = _bf(jnp.concatenate([b * e_end, k * e_end], axis=0))
        yield y, state * jnp.exp(cum_last) + jnp.where(same_head, _dot(uv_t, bk), 0.0)

    def finish(sl, lanes, y):
        lnx_w, lnx_b, r_k = (vec_ref[j:j + 1, lanes] for j in range(3))
        r, k, v = (ref[0, sl, lanes].astype(F32) for ref in (r_ref, k_ref, v_ref))
        dy = y - head_sum(y) * (1.0 / HEAD)
        yn = dy * lax.rsqrt(head_sum(dy * dy) * (1.0 / HEAD) + GN_EPS) * lnx_w + lnx_b
        y_ref[0, sl, lanes] = _bf(yn + head_sum(r * k * r_k) * v)

    pair_lanes = [slice(p * LANES, (p + 1) * LANES) for p in range(n_pairs)]
    chunk_rows = lambda c: pl.ds(pl.multiple_of(c * c_len, c_len), c_len)

    def body(c, carry):
        before = chunk_rows(jnp.maximum(c - 1, 0))
        for p in range(n_pairs):
            finish(before, pair_lanes[p], raw_ref[p])
        sl = chunk_rows(c)
        results = _round_robin([chunk_pair(sl, pair_lanes[p], st_ref[p]) for p in range(n_pairs)])
        for p, (y, new_state) in enumerate(results):
            raw_ref[p] = y
            st_ref[p] = new_state
        return carry

    lax.fori_loop(0, n_chunks, body, 0, unroll=2 if n_chunks % 2 == 0 else 1)
    for p in range(n_pairs):
        finish(chunk_rows(n_chunks - 1), pair_lanes[p], raw_ref[p])


def _wkv(r, lw, k, v, kk, a, vec, chunk, seq_tile, n_pairs):
    b, s, d = r.shape
    width = n_pairs * LANES
    spec = pl.BlockSpec((1, seq_tile, width), lambda bi, hi, si: (bi, si, hi))
    return pl.pallas_call(
        functools.partial(_wkv_kernel, chunk=chunk),
        grid=(b, d // width, s // seq_tile),
        in_specs=[spec] * 6 + [pl.BlockSpec((vec.shape[0], width), lambda bi, hi, si: (0, hi))],
        out_specs=spec,
        out_shape=jax.ShapeDtypeStruct((b, s, d), BF16),
        scratch_shapes=[pltpu.VMEM((n_pairs, LANES, LANES), F32),
                        pltpu.VMEM((n_pairs, chunk, LANES), F32)],
        compiler_params=_params("parallel", "parallel", "arbitrary"),
        name="wkv7",
    )(r, lw, k, v, kk, a, vec)


def _rwkv_post_kernel(z_ref, g_ref, x_ref, wo_ref, o_ref):
    o_ref[...] = x_ref[...] + _dot(z_ref[...] * g_ref[...], wo_ref[...])


def _rwkv_post(z, g, x, wo, tm):
    t, d = x.shape
    tok = pl.BlockSpec((tm, d), lambda i: (i, 0))
    return pl.pallas_call(
        _rwkv_post_kernel,
        grid=(t // tm,),
        in_specs=[tok] * 3 + [_const_spec(wo.shape)],
        out_specs=tok,
        out_shape=jax.ShapeDtypeStruct((t, d), F32),
        compiler_params=_params("parallel"),
        name="rwkv_post",
    )(z, g, x, wo)


FFN_SUB_ROWS = 512


def _ffn_kernel(x_ref, gain_ref, w1_ref, w3_ref, w2_ref, o_ref, h_ref):
    @pl.when(pl.program_id(1) == 0)
    def _():
        h_ref[...] = _bf(_rms(x_ref[...], gain_ref[...]))
        o_ref[...] = x_ref[...]

    for start in range(0, h_ref.shape[0], FFN_SUB_ROWS):
        rows = slice(start, min(start + FFN_SUB_ROWS, h_ref.shape[0]))
        h = h_ref[rows, :]
        u = _dot(h, w1_ref[...])
        act = _bf(u * _sigmoid(u) * _dot(h, w3_ref[...]))
        o_ref[rows, :] += _dot(act, w2_ref[...])


def _ffn(x, gain, w1, w3, w2, tm, tf):
    t, d = x.shape
    ff = w1.shape[1]
    tok = pl.BlockSpec((tm, d), lambda i, f: (i, 0))
    return pl.pallas_call(
        _ffn_kernel,
        grid=(t // tm, ff // tf),
        in_specs=[tok, _const_spec(gain.shape),
                  pl.BlockSpec((d, tf), lambda i, f: (0, f)),
                  pl.BlockSpec((d, tf), lambda i, f: (0, f)),
                  pl.BlockSpec((tf, d), lambda i, f: (f, 0))],
        out_specs=tok,
        out_shape=jax.ShapeDtypeStruct((t, d), F32),
        scratch_shapes=[pltpu.VMEM((tm, d), BF16)],
        compiler_params=_params("parallel", "arbitrary"),
        name="ffn_dense",
    )(x, gain, w1, w3, w2)


BRANCH_WIDTH = HEADS_PER_BRANCH * HEAD


def _attn_proj_kernel(x_ref, gains_ref, wq_ref, wkv_ref, *refs):
    outs, (q_scr, kv_scr) = refs[:-2], refs[-2:]
    x = x_ref[0]
    tm = x.shape[0]
    n = x * lax.rsqrt(jnp.mean(x * x, axis=-1, keepdims=True) + RMS_EPS)
    gains = gains_ref[...]
    q_tiles = q_scr.shape[0]
    qw = q_tiles * LANES
    tiles_br = BRANCH_WIDTH // LANES
    hq = _bf(n * gains[0:1])
    hkv = _bf(n * gains[1:2])

    def project(h, w_ref, col, scr, first_tile, scale=None):
        val = _dot(h, w_ref[:, col:col + BRANCH_WIDTH])
        for ti in range(tiles_br):
            tile = val[:, ti * LANES:(ti + 1) * LANES]
            scr[first_tile + ti] = tile if scale is None else tile * scale

    def permute(out, dil, scr, first_tile):
        for res in range(dil):
            rows = pl.ds(res, tm // dil, stride=dil) if dil > 1 else slice(None)
            for ti in range(tiles_br):
                out[0, res, :, ti * LANES:(ti + 1) * LANES] = _bf(scr[first_tile + ti, rows, :])

    pending = None
    for which, (h, w_ref, base, scr, tile0, scale) in enumerate(
            ((hq, wq_ref, 0, q_scr, 0, 1.0 / HEAD ** 0.5), (hkv, wkv_ref, 0, kv_scr, 0, None),
             (hkv, wkv_ref, qw, kv_scr, q_tiles, None))):
        for br, (_, dil) in enumerate(BRANCHES):
            project(h, w_ref, base + br * BRANCH_WIDTH, scr, tile0 + br * tiles_br, scale)
            if pending is not None:
                permute(*pending)
            pending = (outs[3 * br + which], dil, scr, tile0 + br * tiles_br)
    permute(*pending)


def _attn_proj(x, gains, wq, wkv, tm):
    b, s, d = x.shape
    qw = wq.shape[1]
    out_specs, out_shape = [], []
    for _, dil in BRANCHES:
        assert tm % (2 * SUBLANES * dil) == 0
        out_specs += [pl.BlockSpec((1, dil, tm // dil, BRANCH_WIDTH), lambda bi, i: (bi, 0, i, 0))] * 3
        out_shape += [jax.ShapeDtypeStruct((b, dil, s // dil, BRANCH_WIDTH), BF16)] * 3
    return pl.pallas_call(
        _attn_proj_kernel,
        grid=(b, s // tm),
        in_specs=[pl.BlockSpec((1, tm, d), lambda bi, i: (bi, i, 0)),
                  _const_spec(gains.shape), _const_spec(wq.shape), _const_spec(wkv.shape)],
        out_specs=out_specs,
        out_shape=out_shape,
        scratch_shapes=[pltpu.VMEM((qw // LANES, tm, LANES), F32),
                        pltpu.VMEM((2 * qw // LANES, tm, LANES), F32)],
        compiler_params=_params("parallel", "parallel"),
        name="attn_proj",
    )(x, gains, wq, wkv)


def _attn_kernel(slopes_ref, q_ref, k_ref, v_ref, o_ref, lse_ref, *, branch, dilation, qb):
    sub_len = q_ref.shape[2]
    n_blk = sub_len // qb
    n_pairs = q_ref.shape[3] // LANES
    col2 = lax.broadcasted_iota(jnp.int32, (qb, 2 * qb), 1)
    rel = (lax.broadcasted_iota(jnp.int32, (qb, 2 * qb), 0) - jnp.where(col2 < qb, col2, col2 - qb)).astype(F32)
    dist_d = jnp.where(rel >= 0, rel, -NEG)
    dist_p = jnp.where(rel <= 0, rel + float(qb), -NEG)
    first_head = lax.broadcasted_iota(jnp.int32, (1, 2 * qb), 1) < qb
    lane = lax.broadcasted_iota(jnp.int32, (qb, LANES), 1)
    head0 = lane < HEAD
    lane_row = lax.broadcasted_iota(jnp.int32, (1, LANES), 1)
    head_mask = [jnp.where(lane_row < HEAD, 1.0, 0.0).astype(BF16),
                 jnp.where(lane_row < HEAD, 0.0, 1.0).astype(BF16)]
    head_ones = [jnp.where(head0, 1.0, 0.0).astype(BF16), jnp.where(head0, 0.0, 1.0).astype(BF16)]

    block_rows = lambda i: pl.ds(pl.multiple_of(i * qb, qb), qb)

    def blocks(items):
        def one_pair(r, i, p):
            cur = block_rows(i)
            if n_blk > 1:
                prv = block_rows(jnp.maximum(i - 1, 0))
                has_prev = jnp.where(i > 0, 1.0, -NEG)
            lanes = slice(p * LANES, (p + 1) * LANES)
            slope = [slopes_ref[branch * HEADS_PER_BRANCH + 2 * p + j] * float(dilation) for j in range(2)]
            step = jnp.where(first_head, slope[0], slope[1])

            def keys(rows):
                k = k_ref[0, r, rows, lanes]
                return jnp.concatenate([k * head_mask[0], k * head_mask[1]], axis=0)

            def values(rows):
                v = v_ref[0, r, rows, lanes]
                return jnp.concatenate(
                    [jnp.concatenate([v * head_mask[0], head_ones[0]], axis=1),
                     jnp.concatenate([v * head_mask[1], head_ones[1]], axis=1)], axis=0)

            q = q_ref[0, r, cur, lanes]
            sd = _dot_nt(q, keys(cur)) - step * dist_d
            if n_blk > 1:
                sp = _dot_nt(q, keys(prv)) - (step * has_prev) * dist_p
            yield
            both = jnp.maximum(sd, sp) if n_blk > 1 else sd
            m = [jnp.max(both[:, :qb], axis=-1, keepdims=True), jnp.max(both[:, qb:], axis=-1, keepdims=True)]
            probs = lambda s: _bf(jnp.concatenate([jnp.exp(s[:, :qb] - m[0]), jnp.exp(s[:, qb:] - m[1])], axis=1))
            acc = _dot(probs(sd), values(cur))
            if n_blk > 1:
                acc = acc + _dot(probs(sp), values(prv))
            yield
            den = acc[:, LANES:]
            lse = jnp.where(head0, m[0], m[1]) + jnp.log(den)
            yield acc[:, :LANES] / den, lse[:, 0:1], lse[:, HEAD:HEAD + 1]

        res = _round_robin([one_pair(r, i, p) for r, i in items for p in range(n_pairs)])
        for n, (r, i) in enumerate(items):
            lse = jnp.zeros((qb, LANES), F32)
            for p in range(n_pairs):
                out, l_a, l_b = res[n * n_pairs + p]
                o_ref[0, r, block_rows(i), p * LANES:(p + 1) * LANES] = _bf(out)
                lse = jnp.where(lane == 2 * p, l_a, jnp.where(lane == 2 * p + 1, l_b, lse))
            lse_ref[0, r, block_rows(i), :] = lse

    n_res = q_ref.shape[1]
    if n_blk == 1:
        blocks([(r, 0) for r in range(n_res)])
    else:
        group = next(g for g in (4, 2, 1) if n_blk % g == 0)
        for r in range(n_res):
            lax.fori_loop(0, n_blk // group,
                          lambda j, c, r=r: (blocks([(r, group * j + g) for g in range(group)]), c)[1], 0)


def _attn_branch(slopes, q, k, v, branch):
    window, dilation = BRANCHES[branch]
    assert window % dilation == 0 and window // dilation == LANES
    b, _, sub_len, width = q.shape
    qb = min(LANES, sub_len)
    assert sub_len % qb == 0
    n_res = max(1, min(dilation, TOKEN_TILE // sub_len))
    assert dilation % n_res == 0
    spec = pl.BlockSpec((1, n_res, sub_len, width), lambda bi, ri: (bi, ri, 0, 0))
    return pl.pallas_call(
        functools.partial(_attn_kernel, branch=branch, dilation=dilation, qb=qb),
        grid=(b, dilation // n_res),
        in_specs=[pl.BlockSpec(memory_space=pltpu.SMEM)] + [spec] * 3,
        out_specs=[spec, pl.BlockSpec((1, n_res, sub_len, LANES), lambda bi, ri: (bi, ri, 0, 0))],
        out_shape=[jax.ShapeDtypeStruct(q.shape, BF16),
                   jax.ShapeDtypeStruct(q.shape[:3] + (LANES,), F32)],
        compiler_params=_params("parallel", "parallel"),
        name=f"dilated_attn_{branch}",
    )(slopes, q, k, v)


def _attn_out_kernel(o0, o1, o2, l0, l1, l2, x_ref, gain_ref, wo_ref, router_ref, rb_ref, expand_ref,
                     x_out, h_out, route_out, cnt_out, cnt_ref, o_scr, l_scr):
    @pl.when((pl.program_id(0) == 0) & (pl.program_id(1) == 0))
    def _():
        cnt_ref[...] = jnp.zeros_like(cnt_ref)

    tm = x_ref.shape[0]
    tiles_br = BRANCH_WIDTH // LANES
    for br, (o_ref, l_ref) in enumerate(((o0, l0), (o1, l1), (o2, l2))):
        dil = BRANCHES[br][1]
        for res in range(dil):
            rows = pl.ds(res, tm // dil, stride=dil) if dil > 1 else slice(None)
            l_scr[br, rows, :] = l_ref[0, res]
            for ti in range(tiles_br):
                lanes = slice(ti * LANES, (ti + 1) * LANES)
                o_scr[br, ti, rows, :] = o_ref[0, res, :, lanes].astype(F32)
    ls = [l_scr[br] for br in range(3)]
    m = jnp.maximum(jnp.maximum(ls[0], ls[1]), ls[2])
    ws = [jnp.exp(l - m) for l in ls]
    total = ws[0] + ws[1] + ws[2]
    spread = [_dot(_bf(ws[br] / total), expand_ref[...]) for br in range(3)]
    merged = []
    for ti in range(tiles_br):
        lanes = slice(ti * LANES, (ti + 1) * LANES)
        merged.append(_bf(sum(spread[br][:, lanes] * o_scr[br, ti] for br in range(3))))
    x = x_ref[...] + _dot(jnp.concatenate(merged, axis=1), wo_ref[...])
    x_out[...] = x
    h = _rms(x, gain_ref[...])
    h_out[...] = h
    hhi, hlo = _split2(h)
    both = _dot(hhi, router_ref[...])
    logits = both[:, :LANES] + both[:, LANES:] + _dot(hlo, router_ref[:, :LANES]) + rb_ref[...]
    lane = lax.broadcasted_iota(jnp.int32, logits.shape, 1)
    m1 = jnp.max(logits, axis=-1, keepdims=True)
    i1 = jnp.min(jnp.where(logits == m1, lane, LANES), axis=-1, keepdims=True)
    rest = jnp.where(lane == i1, 2 * NEG, logits)
    m2 = jnp.max(rest, axis=-1, keepdims=True)
    i2 = jnp.min(jnp.where(rest == m2, lane, LANES), axis=-1, keepdims=True)
    ex = jnp.exp(m2 - m1)
    sel = jnp.where((lane == i1) | (lane == i2), 1.0, 0.0)
    rr = lax.broadcasted_iota(jnp.int32, (tm, tm), 0)
    cc = lax.broadcasted_iota(jnp.int32, (tm, tm), 1)
    before = jnp.where(rr > cc, 1.0, 0.0).astype(BF16)
    rank = cnt_ref[...] + _dot(before, _bf(sel))
    cnt_ref[...] += jnp.sum(sel, axis=0, keepdims=True)
    cnt_out[...] = cnt_ref[...]
    columns = [i1.astype(F32), i2.astype(F32),
               jnp.sum(jnp.where(lane == i1, rank, 0.0), axis=-1, keepdims=True),
               jnp.sum(jnp.where(lane == i2, rank, 0.0), axis=-1, keepdims=True),
               1.0 / (1.0 + ex), ex / (1.0 + ex)]
    route = jnp.zeros(logits.shape, F32)
    for j, col in enumerate(columns):
        route = jnp.where(lane == j, col, route)
    route_out[...] = route


def _attn_out(os_, ls_, x, gain, wo, router, rb, b, tm):
    t, d = x.shape
    tiles = t // (b * tm)
    tok = pl.BlockSpec((tm, d), lambda bi, i: (bi * tiles + i, 0))
    small = pl.BlockSpec((tm, LANES), lambda bi, i: (bi * tiles + i, 0))
    att = [pl.BlockSpec((1, dil, tm // dil, BRANCH_WIDTH), lambda bi, i: (bi, 0, i, 0))
           for _, dil in BRANCHES]
    lse = [pl.BlockSpec((1, dil, tm // dil, LANES), lambda bi, i: (bi, 0, i, 0)) for _, dil in BRANCHES]
    expand = (jnp.arange(LANES)[:, None] == jnp.arange(BRANCH_WIDTH)[None, :] // HEAD).astype(BF16)
    consts = [gain, wo, router, rb, expand]
    return pl.pallas_call(
        _attn_out_kernel,
        grid=(b, tiles),
        in_specs=att + lse + [tok] + [_const_spec(c.shape) for c in consts],
        out_specs=[tok, tok, small, _const_spec((1, LANES))],
        out_shape=[jax.ShapeDtypeStruct((t, d), F32), jax.ShapeDtypeStruct((t, d), F32),
                   jax.ShapeDtypeStruct((t, LANES), F32), jax.ShapeDtypeStruct((1, LANES), F32)],
        scratch_shapes=[pltpu.VMEM((1, LANES), F32),
                        pltpu.VMEM((len(BRANCHES), BRANCH_WIDTH // LANES, tm, LANES), F32),
                        pltpu.VMEM((len(BRANCHES), tm, LANES), F32)],
        compiler_params=_params("arbitrary", "arbitrary"),
        name="attn_out_route",
    )(*os_, *ls_, x, *consts)


DMA_UNROLL = 8
DMA_TOKEN_TILE = 1024


def _row_copy(src, src_row, dst, dst_row, sem):
    return pltpu.make_async_copy(src.at[pl.ds(src_row, 1)], dst.at[pl.ds(dst_row, 1)], sem)


def _moe_scatter_kernel(meta_ref, dest_ref, h_ref, out_ref, zeros, sem, zero_sem):
    n_tok = h_ref.shape[0]
    tile = zeros.shape[0]
    ends = lambda e: meta_ref[N_EXPERTS + e]

    def zero_fill(e):
        if e < N_EXPERTS:
            start, used = jnp.maximum(ends(e) - tile, 0), ends(e) > meta_ref[e]
        else:
            start = ends(N_EXPERTS - 1) + (e - N_EXPERTS) * tile
            used = start < out_ref.shape[0]
            start = jnp.minimum(start, out_ref.shape[0] - tile)
        dst = out_ref.at[pl.ds(pl.multiple_of(start, tile), tile)]
        return used, pltpu.make_async_copy(zeros, dst, zero_sem)

    @pl.when(pl.program_id(0) == 0)
    def _():
        zeros[...] = jnp.zeros_like(zeros)
        for e in range(2 * N_EXPERTS):
            used, copy = zero_fill(e)
            pl.when(used)(copy.start)
        for e in range(2 * N_EXPERTS):
            used, copy = zero_fill(e)
            pl.when(used)(copy.wait)

    def issue(j, carry):
        for k in range(2):
            _row_copy(h_ref, j, out_ref, dest_ref[2 * j + k], sem).start(priority=k)
        return carry

    def drain(j, carry):
        for _ in range(2):
            _row_copy(h_ref, 0, out_ref, 0, sem).wait()
        return carry

    lax.fori_loop(0, n_tok, issue, 0, unroll=DMA_UNROLL)
    lax.fori_loop(0, n_tok, drain, 0, unroll=DMA_UNROLL)


def _moe_scatter(meta, dest_flat, h, rows, ts, tile):
    t, d = h.shape
    return pl.pallas_call(
        _moe_scatter_kernel,
        grid=(t // ts,),
        in_specs=[pl.BlockSpec(memory_space=pltpu.SMEM),
                  pl.BlockSpec((2 * ts,), lambda i: (i,), memory_space=pltpu.SMEM),
                  pl.BlockSpec((ts, d), lambda i: (i, 0))],
        out_specs=pl.BlockSpec(memory_space=pl.ANY),
        out_shape=jax.ShapeDtypeStruct((rows, d), F32),
        scratch_shapes=[pltpu.VMEM((tile, d), F32), pltpu.SemaphoreType.DMA(()),
                        pltpu.SemaphoreType.DMA(())],
        compiler_params=_params("arbitrary"),
        name="moe_scatter",
    )(meta, dest_flat, h)


def _moe_ffn_kernel(src_ref, te_ref, nv_ref, x_ref, w1_ref, w3_ref, w2_ref, o_ref):
    del src_ref, te_ref
    @pl.when(pl.program_id(1) == 0)
    def _():
        o_ref[...] = jnp.zeros_like(o_ref)

    @pl.when(pl.program_id(0) < nv_ref[0])
    def _():
        h = _bf(x_ref[...])
        u = _dot(h, w1_ref[0])
        act = _bf(u * _sigmoid(u) * _dot(h, w3_ref[0]))
        o_ref[...] += _dot(act, w2_ref[0])


def _moe_ffn(tile_src, tile_expert, n_valid, xs, w1, w3, w2, tm, tf):
    rows, d = xs.shape
    ff = w1.shape[2]
    tok = pl.BlockSpec((tm, d), lambda i, f, src, te, nv: (src[i], 0))
    return pl.pallas_call(
        _moe_ffn_kernel,
        grid_spec=pltpu.PrefetchScalarGridSpec(
            num_scalar_prefetch=3,
            grid=(rows // tm, ff // tf),
            in_specs=[tok,
                      pl.BlockSpec((1, d, tf), lambda i, f, src, te, nv: (te[i], 0, f)),
                      pl.BlockSpec((1, d, tf), lambda i, f, src, te, nv: (te[i], 0, f)),
                      pl.BlockSpec((1, tf, d), lambda i, f, src, te, nv: (te[i], f, 0))],
            out_specs=pl.BlockSpec((tm, d), lambda i, f, src, te, nv: (i, 0)),
        ),
        out_shape=jax.ShapeDtypeStruct((rows, d), F32),
        compiler_params=_params("arbitrary", "arbitrary"),
        name="moe_experts",
    )(tile_src, tile_expert, n_valid, xs, w1, w3, w2)


def _moe_combine_kernel(dest_ref, x_ref, gates_ref, gain_ref, y_hbm, o_ref, buf, sem):
    n_tok = x_ref.shape[0]

    def issue(j, carry):
        for k in range(2):
            _row_copy(y_hbm, dest_ref[2 * j + k], buf.at[k], j, sem).start(priority=k)
        return carry

    def drain(j, carry):
        for k in range(2):
            _row_copy(y_hbm, 0, buf.at[k], 0, sem).wait()
        return carry

    lax.fori_loop(0, n_tok, issue, 0, unroll=DMA_UNROLL)
    lax.fori_loop(0, n_tok, drain, 0, unroll=DMA_UNROLL)
    gates = gates_ref[...]
    x = x_ref[...] + gates[:, 4:5] * buf[0] + gates[:, 5:6] * buf[1]
    o_ref[...] = _rms(x, gain_ref[...])


def _moe_combine(dest_flat, x, route, gain, ys, ts):
    t, d = x.shape
    tok = pl.BlockSpec((ts, d), lambda i: (i, 0))
    return pl.pallas_call(
        _moe_combine_kernel,
        grid=(t // ts,),
        in_specs=[pl.BlockSpec((2 * ts,), lambda i: (i,), memory_space=pltpu.SMEM),
                  tok, pl.BlockSpec((ts, LANES), lambda i: (i, 0)), _const_spec(gain.shape),
                  pl.BlockSpec(memory_space=pl.ANY)],
        out_specs=tok,
        out_shape=jax.ShapeDtypeStruct((t, d), F32),
        scratch_shapes=[pltpu.VMEM((2, ts, d), F32), pltpu.SemaphoreType.DMA(())],
        compiler_params=_params("arbitrary"),
        name="moe_combine",
    )(dest_flat, x, route, gain, ys)


def _tile(n, want):
    t = min(n, want)
    assert n % t == 0
    return t


def _pad_cols(w, n):
    return jnp.pad(w, ((0, 0), (0, n - w.shape[1])))


def _pad_rows(w, n):
    return jnp.pad(w, ((0, n - w.shape[0]), (0, 0)))


def kernel(x, norm_gain, rwkv_mu, rwkv_wr, rwkv_wk, rwkv_wv, rwkv_w0, rwkv_w1, rwkv_w2, rwkv_a0, rwkv_a1, rwkv_a2, rwkv_g1, rwkv_g2, rwkv_k_k, rwkv_k_a, rwkv_r_k, rwkv_lnx_w, rwkv_lnx_b, rwkv_wo, kv_norm_gain, w_kv, attn_wq, attn_wo, ffn_w1, ffn_w3, ffn_w2, moe_router, moe_router_bias, moe_w1, moe_w3, moe_w2, final_norm_gain):
    b, s, d = x.shape
    t = b * s
    assert norm_gain.shape[0] == 2 and d % LANES == 0 and s % SUBLANES == 0

    zeros = jnp.zeros((d,), F32)
    vec_pre = jnp.stack([norm_gain[0, 0], rwkv_w0[0], rwkv_a0[0], rwkv_k_k[0], rwkv_k_a[0],
                         zeros, zeros, zeros])
    tm_pre = _tile(s, RWKV_PRE_TILE)
    chunk = _tile(s, LANES)
    r, lw, k, v, kk, a, g = _rwkv_pre(
        x, vec_pre, rwkv_mu[0], _bf(rwkv_wr[0]), _bf(rwkv_wk[0]), _bf(rwkv_wv[0]),
        _bf(_pad_cols(rwkv_w1[0], LANES)), _bf(_pad_rows(rwkv_w2[0], LANES)),
        _bf(_pad_cols(rwkv_a1[0], LANES)), _bf(_pad_rows(rwkv_a2[0], LANES)),
        _bf(rwkv_g1[0]), _bf(rwkv_g2[0]), tm_pre)
    vec_post = jnp.stack([rwkv_lnx_w[0], rwkv_lnx_b[0], rwkv_r_k[0].reshape(d),
                          zeros, zeros, zeros, zeros, zeros])
    z = _wkv(r, lw, k, v, kk, a, vec_post, chunk=chunk, seq_tile=_tile(s, WKV_SEQ_TILE),
             n_pairs=WKV_PAIRS)

    flat = lambda arr: arr.reshape(t, d)
    tm = _tile(t, TOKEN_TILE)
    x1 = _rwkv_post(flat(z), flat(g), flat(x), _bf(rwkv_wo[0]), tm)

    ff = ffn_w1.shape[2]
    tf = ff // 2 if (ff // 2) % LANES == 0 else ff
    x2 = _ffn(x1, norm_gain[0, 1][None, :], _bf(ffn_w1[0]), _bf(ffn_w3[0]), _bf(ffn_w2[0]),
              _tile(t, 2 * FFN_SUB_ROWS), tf)

    n_slopes = len(BRANCHES) * HEADS_PER_BRANCH
    slopes = jnp.exp2(-ALIBI_MAX * (jnp.arange(n_slopes, dtype=F32) + 1.0) / n_slopes)
    gains = jnp.stack([norm_gain[1, 0], kv_norm_gain] + [zeros] * 6)
    tm_a = _tile(s, TOKEN_TILE)
    qkv = _attn_proj(x2.reshape(b, s, d), gains, _bf(attn_wq[0]), _bf(w_kv), tm_a)
    os_, ls_ = [], []
    for br in range(len(BRANCHES)):
        o_br, l_br = _attn_branch(slopes, *qkv[3 * br:3 * br + 3], br)
        os_.append(o_br)
        ls_.append(l_br)

    router = jnp.concatenate(_split2(_pad_cols(moe_router[0], LANES)), axis=1)
    rbias = jnp.full((1, LANES), NEG, F32).at[0, :N_EXPERTS].set(moe_router_bias[0])
    x3, h4, route, cnt = _attn_out(os_, ls_, x2, norm_gain[1, 1][None, :], _bf(attn_wo[0]),
                                   router, rbias, b, tm_a)

    tm_e = TOKEN_TILE if t >= N_EXPERTS * TOKEN_TILE else LANES
    ts = _tile(t, DMA_TOKEN_TILE)
    counts = cnt[0, :N_EXPERTS].astype(jnp.int32)
    padded = ((counts + tm_e - 1) // tm_e) * tm_e
    ends = jnp.cumsum(padded)
    meta = jnp.concatenate([ends - padded, ends]).astype(jnp.int32)
    route_i = route[:, :4].astype(jnp.int32)
    group_start = jnp.sum(jnp.where(route_i[:, :2, None] == jnp.arange(N_EXPERTS), ends - padded, 0), axis=-1)
    dest = (group_start + route_i[:, 2:4]).astype(jnp.int32).reshape(2 * t)
    rows = 2 * t + N_EXPERTS * tm_e
    n_tiles = rows // tm_e
    n_valid = (ends[-1] // tm_e).astype(jnp.int32)
    tile_src = jnp.minimum(jnp.arange(n_tiles, dtype=jnp.int32), n_valid - 1)
    tile_expert = jnp.minimum(
        jnp.sum(tile_src[:, None] * tm_e >= ends[None, :], axis=1), N_EXPERTS - 1).astype(jnp.int32)

    xs = _moe_scatter(meta, dest, h4, rows, ts, tm_e)
    ffe = moe_w1.shape[3]
    tfe = ffe // 2 if (ffe // 2) % LANES == 0 else ffe
    ys = _moe_ffn(tile_src, tile_expert, n_valid.reshape(1), xs,
                  _bf(moe_w1[0]), _bf(moe_w3[0]), _bf(moe_w2[0]), tm_e, tfe)
    out = _moe_combine(dest, x3, route, final_norm_gain[None, :], ys, ts)
    return out.reshape(b, s, d)
```

```python
import functools
import math

import jax
import jax.numpy as jnp
from jax import lax
from jax.experimental import pallas as pl
from jax.experimental.pallas import tpu as pltpu

F32 = jnp.float32
BF16 = jnp.bfloat16

RMS_EPS = 1e-5
GN_EPS = 64e-5
HEAD = 64
LANES = 128
SUBLANES = 8
BRANCHES = ((128, 1), (512, 4), (2048, 16))
HEADS_PER_BRANCH = 8
ALIBI_MAX = 8.0
N_EXPERTS = 8
NEG = -1e30
VMEM_LIMIT_BYTES = 56 * 1024 * 1024
TOKEN_TILE = 512
RWKV_PRE_TILE = 512
WKV_SEQ_TILE = 1024
WKV_PAIRS = 8


def _params(*sem):
    return pltpu.CompilerParams(dimension_semantics=sem, vmem_limit_bytes=VMEM_LIMIT_BYTES)


def _dot(a, b):
    return jnp.dot(a, b, preferred_element_type=F32)


def _dot_nt(a, b):
    return lax.dot_general(a, b, (((1,), (1,)), ((), ())), preferred_element_type=F32)


def _bf(x):
    return x.astype(BF16)


def _split2(x):
    hi = x.astype(BF16)
    lo = (x - hi.astype(F32)).astype(BF16)
    return hi, lo


def _sigmoid(z):
    return 1.0 / (1.0 + jnp.exp(-z))


def _rms(x, gain):
    return x * lax.rsqrt(jnp.mean(x * x, axis=-1, keepdims=True) + RMS_EPS) * gain


def _const_spec(shape):
    nd = len(shape)
    return pl.BlockSpec(shape, lambda *_: (0,) * nd)


def _rwkv_pre_kernel(x_ref, xp_ref, vec_ref, mu_ref, wr_ref, wk_ref, wv_ref, w1_ref, w2_ref,
                     a1_ref, a2_ref, g1_ref, g2_ref,
                     r_out, lw_out, k_out, v_out, kk_out, a_out, g_out):
    i = pl.program_id(1)
    vec = vec_ref[...]
    gain, w0, a0, k_k, k_a = (vec[j:j + 1] for j in range(5))
    x = x_ref[0]
    h = _rms(x, gain)
    hp = _rms(xp_ref[0][SUBLANES - 1:SUBLANES, :], gain)
    hp = jnp.where(i > 0, hp, 0.0)
    rows = lax.broadcasted_iota(jnp.int32, h.shape, 0)
    hprev = jnp.where(rows == 0, hp, pltpu.roll(h, 1, 0))
    xx = hprev - h
    mu = mu_ref[...]
    xr, xw, xk, xv, xa, xg = (_bf(h + xx * mu[j:j + 1]) for j in range(6))
    w_low = _dot(xw, w1_ref[...])
    a_low = _dot(xa, a1_ref[...])
    g_low = _dot(xg, g1_ref[...])
    r = _dot(xr, wr_ref[...])
    k = _dot(xk, wk_ref[...])
    v = _dot(xv, wv_ref[...])
    wl = w0 + _dot(_bf(jnp.tanh(w_low)), w2_ref[...])
    a = _sigmoid(a0 + _dot(_bf(a_low), a2_ref[...]))
    g = _dot(_bf(_sigmoid(g_low)), g2_ref[...])
    lw_out[0] = _sigmoid(wl) * (-math.exp(-0.5))
    r_out[0] = _bf(r)
    k_out[0] = _bf(k * (1.0 + (a - 1.0) * k_a))
    v_out[0] = _bf(v)
    kk_out[0] = _bf(k * k_k)
    a_out[0] = _bf(a)
    g_out[0] = _bf(g)


def _rwkv_pre(x, vec, mu, wr, wk, wv, w1, w2, a1, a2, g1, g2, tm):
    b, s, d = x.shape
    tok = pl.BlockSpec((1, tm, d), lambda bi, i: (bi, i, 0))
    prev = pl.BlockSpec((1, SUBLANES, d), lambda bi, i: (bi, jnp.maximum(i * (tm // SUBLANES) - 1, 0), 0))
    consts = [vec, mu, wr, wk, wv, w1, w2, a1, a2, g1, g2]
    out = lambda dt: jax.ShapeDtypeStruct((b, s, d), dt)
    return pl.pallas_call(
        _rwkv_pre_kernel,
        grid=(b, s // tm),
        in_specs=[tok, prev] + [_const_spec(c.shape) for c in consts],
        out_specs=[tok] * 7,
        out_shape=[out(BF16), out(F32)] + [out(BF16)] * 5,
        compiler_params=_params("parallel", "arbitrary"),
        name="rwkv_pre",
    )(x, x, *consts)


def _round_robin(generators):
    results = [None] * len(generators)
    live = list(enumerate(generators))
    while live:
        still = []
        for idx, gen in live:
            try:
                results[idx] = next(gen)
                still.append((idx, gen))
            except StopIteration:
                pass
        live = still
    return results


def _wkv_kernel(r_ref, lw_ref, k_ref, v_ref, kk_ref, a_ref, vec_ref, y_ref, st_ref, raw_ref, *, chunk):
    c_len = chunk
    n_chunks = r_ref.shape[1] // c_len
    n_pairs = r_ref.shape[2] // LANES
    inv_steps = max(c_len.bit_length() - 2, 0)
    lane = lax.broadcasted_iota(jnp.int32, (c_len, LANES), 1)
    head0 = lane < HEAD
    rr = lax.broadcasted_iota(jnp.int32, (c_len, c_len), 0)
    cc = lax.broadcasted_iota(jnp.int32, (c_len, c_len), 1)
    lower = rr >= cc
    strict = rr > cc
    eye = jnp.where(rr == cc, 1.0, 0.0).astype(F32)
    r2 = lax.broadcasted_iota(jnp.int32, (LANES, LANES), 0)
    c2 = lax.broadcasted_iota(jnp.int32, (LANES, LANES), 1)
    same_head = (r2 < HEAD) == (c2 < HEAD)

    @pl.when(pl.program_id(2) == 0)
    def _():
        st_ref[...] = jnp.zeros_like(st_ref)
        raw_ref[...] = jnp.zeros_like(raw_ref)

    row_index = lax.broadcasted_iota(jnp.int32, (c_len, LANES), 0)
    scan_shifts = [1 << j for j in range(c_len.bit_length() - 1)]
    rows = lambda mat, j: mat[j * c_len:(j + 1) * c_len]
    own_head = lambda mat: jnp.where(head0, rows(mat, 0), rows(mat, 1))

    def head_sum(x):
        s0 = jnp.sum(jnp.where(head0, x, 0.0), axis=-1, keepdims=True)
        s1 = jnp.sum(jnp.where(head0, 0.0, x), axis=-1, keepdims=True)
        return jnp.where(head0, s0, s1)

    def chunk_pair(sl, lanes, state):
        cum = lw_ref[0, sl, lanes]
        for shift in scan_shifts:
            cum = cum + jnp.where(row_index >= shift, pltpu.roll(cum, shift, 0), 0.0)
        cum_prev = jnp.where(row_index == 0, 0.0, pltpu.roll(cum, 1, 0))
        cum_last = cum[c_len - 1:c_len, :]
        kk = kk_ref[0, sl, lanes].astype(F32)
        kk = kk / jnp.maximum(jnp.sqrt(head_sum(kk * kk)), 1e-12)
        b = kk * a_ref[0, sl, lanes].astype(F32)
        k = k_ref[0, sl, lanes].astype(F32)
        vb = v_ref[0, sl, lanes]
        e_neg = jnp.exp(-cum)
        e_end = jnp.exp(cum_last - cum)
        a_t = -kk * jnp.exp(cum_prev)
        r_t = r_ref[0, sl, lanes].astype(F32) * jnp.exp(cum)
        lhs = _bf(jnp.concatenate([jnp.where(head0, a_t, 0.0), jnp.where(head0, 0.0, a_t),
                                   jnp.where(head0, r_t, 0.0), jnp.where(head0, 0.0, r_t)], axis=0))
        gram = _dot_nt(lhs, _bf(jnp.concatenate([b * e_neg, k * e_neg], axis=0)))
        from_state = _dot_nt(_bf(jnp.concatenate([a_t, r_t], axis=0)), _bf(state))
        yield
        gram_b, gram_k = gram[:, :c_len], gram[:, c_len:]
        from_v = _dot(_bf(jnp.concatenate(
            [jnp.where(strict, rows(gram_k, 0), 0.0), jnp.where(strict, rows(gram_k, 1), 0.0),
             jnp.where(lower, rows(gram_k, 2), 0.0), jnp.where(lower, rows(gram_k, 3), 0.0)], axis=0)), vb)
        m_rb = _bf(jnp.concatenate([jnp.where(lower, rows(gram_b, 2), 0.0),
                                    jnp.where(lower, rows(gram_b, 3), 0.0)], axis=0))
        w = rows(from_state, 0) + own_head(from_v[:2 * c_len])
        y1 = rows(from_state, 1) + own_head(from_v[2 * c_len:])
        pw = [jnp.where(strict, rows(gram_b, hd), 0.0) for hd in range(2)]
        inv = [eye + pw[hd] for hd in range(2)]
        if inv_steps:
            pw = [_dot(_bf(pw[hd]), _bf(pw[hd])) for hd in range(2)]
        for step in range(inv_steps):
            yield
            if step < inv_steps - 1:
                both = [_dot(_bf(jnp.concatenate([inv[hd], pw[hd]], axis=0)), _bf(pw[hd])) for hd in range(2)]
                inv = [inv[hd] + rows(both[hd], 0) for hd in range(2)]
                pw = [rows(both[hd], 1) for hd in range(2)]
            else:
                inv = [inv[hd] + _dot(_bf(inv[hd]), _bf(pw[hd])) for hd in range(2)]
        yield
        u = own_head(_dot(_bf(jnp.concatenate(inv, axis=0)), _bf(w)))
        yield
        y = y1 + own_head(_dot(m_rb, _bf(u)))
        uv_t = _bf(jnp.transpose(jnp.concatenate([u, vb.astype(F32)], axis=0)))
        bk = _bf(jnp.concatenate([b * e_end, k * e_end], axis=0))
        yield y, state * jnp.exp(cum_last) + jnp.where(same_head, _dot(uv_t, bk), 0.0)

    def finish(sl, lanes, y):
        lnx_w, lnx_b, r_k = (vec_ref[j:j + 1, lanes] for j in range(3))
        r, k, v = (ref[0, sl, lanes].astype(F32) for ref in (r_ref, k_ref, v_ref))
        dy = y - head_sum(y) * (1.0 / HEAD)
        yn = dy * lax.rsqrt(head_sum(dy * dy) * (1.0 / HEAD) + GN_EPS) * lnx_w + lnx_b
        y_ref[0, sl, lanes] = _bf(yn + head_sum(r * k * r_k) * v)

    pair_lanes = [slice(p * LANES, (p + 1) * LANES) for p in range(n_pairs)]
    chunk_rows = lambda c: pl.ds(pl.multiple_of(c * c_len, c_len), c_len)

    def body(c, carry):
        before = chunk_rows(jnp.maximum(c - 1, 0))
        for p in range(n_pairs):
            finish(before, pair_lanes[p], raw_ref[p])
        sl = chunk_rows(c)
        results = _round_robin([chunk_pair(sl, pair_lanes[p], st_ref[p]) for p in range(n_pairs)])
        for p, (y, new_state) in enumerate(results):
            raw_ref[p] = y
            st_ref[p] = new_state
        return carry

    lax.fori_loop(0, n_chunks, body, 0, unroll=2 if n_chunks % 2 == 0 else 1)
    for p in range(n_pairs):
        finish(chunk_rows(n_chunks - 1), pair_lanes[p], raw_ref[p])


def _wkv(r, lw, k, v, kk, a, vec, chunk, seq_tile, n_pairs):
    b, s, d = r.shape
    width = n_pairs * LANES
    spec = pl.BlockSpec((1, seq_tile, width), lambda bi, hi, si: (bi, si, hi))
    return pl.pallas_call(
        functools.partial(_wkv_kernel, chunk=chunk),
        grid=(b, d // width, s // seq_tile),
        in_specs=[spec] * 6 + [pl.BlockSpec((vec.shape[0], width), lambda bi, hi, si: (0, hi))],
        out_specs=spec,
        out_shape=jax.ShapeDtypeStruct((b, s, d), BF16),
        scratch_shapes=[pltpu.VMEM((n_pairs, LANES, LANES), F32),
                        pltpu.VMEM((n_pairs, chunk, LANES), F32)],
        compiler_params=_params("parallel", "parallel", "arbitrary"),
        name="wkv7",
    )(r, lw, k, v, kk, a, vec)


def _rwkv_post_kernel(z_ref, g_ref, x_ref, wo_ref, o_ref):
    o_ref[...] = x_ref[...] + _dot(z_ref[...] * g_ref[...], wo_ref[...])


def _rwkv_post(z, g, x, wo, tm):
    t, d = x.shape
    tok = pl.BlockSpec((tm, d), lambda i: (i, 0))
    return pl.pallas_call(
        _rwkv_post_kernel,
        grid=(t // tm,),
        in_specs=[tok] * 3 + [_const_spec(wo.shape)],
        out_specs=tok,
        out_shape=jax.ShapeDtypeStruct((t, d), F32),
        compiler_params=_params("parallel"),
        name="rwkv_post",
    )(z, g, x, wo)


FFN_SUB_ROWS = 512


def _ffn_kernel(x_ref, gain_ref, w1_ref, w3_ref, w2_ref, o_ref, h_ref):
    @pl.when(pl.program_id(1) == 0)
    def _():
        h_ref[...] = _bf(_rms(x_ref[...], gain_ref[...]))
        o_ref[...] = x_ref[...]

    for start in range(0, h_ref.shape[0], FFN_SUB_ROWS):
        rows = slice(start, min(start + FFN_SUB_ROWS, h_ref.shape[0]))
        h = h_ref[rows, :]
        u = _dot(h, w1_ref[...])
        act = _bf(u * _sigmoid(u) * _dot(h, w3_ref[...]))
        o_ref[rows, :] += _dot(act, w2_ref[...])


def _ffn(x, gain, w1, w3, w2, tm, tf):
    t, d = x.shape
    ff = w1.shape[1]
    tok = pl.BlockSpec((tm, d), lambda i, f: (i, 0))
    return pl.pallas_call(
        _ffn_kernel,
        grid=(t // tm, ff // tf),
        in_specs=[tok, _const_spec(gain.shape),
                  pl.BlockSpec((d, tf), lambda i, f: (0, f)),
                  pl.BlockSpec((d, tf), lambda i, f: (0, f)),
                  pl.BlockSpec((tf, d), lambda i, f: (f, 0))],
        out_specs=tok,
        out_shape=jax.ShapeDtypeStruct((t, d), F32),
        scratch_shapes=[pltpu.VMEM((tm, d), BF16)],
        compiler_params=_params("parallel", "arbitrary"),
        name="ffn_dense",
    )(x, gain, w1, w3, w2)


BRANCH_WIDTH = HEADS_PER_BRANCH * HEAD


def _attn_proj_kernel(x_ref, gains_ref, wq_ref, wkv_ref, *refs):
    outs, (q_scr, kv_scr) = refs[:-2], refs[-2:]
    x = x_ref[0]
    tm = x.shape[0]
    n = x * lax.rsqrt(jnp.mean(x * x, axis=-1, keepdims=True) + RMS_EPS)
    gains = gains_ref[...]
    q_tiles = q_scr.shape[0]
    qw = q_tiles * LANES
    tiles_br = BRANCH_WIDTH // LANES
    hq = _bf(n * gains[0:1])
    hkv = _bf(n * gains[1:2])

    def project(h, w_ref, col, scr, first_tile, scale=None):
        val = _dot(h, w_ref[:, col:col + BRANCH_WIDTH])
        for ti in range(tiles_br):
            tile = val[:, ti * LANES:(ti + 1) * LANES]
            scr[first_tile + ti] = tile if scale is None else tile * scale

    def permute(out, dil, scr, first_tile):
        for res in range(dil):
            rows = pl.ds(res, tm // dil, stride=dil) if dil > 1 else slice(None)
            for ti in range(tiles_br):
                out[0, res, :, ti * LANES:(ti + 1) * LANES] = _bf(scr[first_tile + ti, rows, :])

    pending = None
    for which, (h, w_ref, base, scr, tile0, scale) in enumerate(
            ((hq, wq_ref, 0, q_scr, 0, 1.0 / HEAD ** 0.5), (hkv, wkv_ref, 0, kv_scr, 0, None),
             (hkv, wkv_ref, qw, kv_scr, q_tiles, None))):
        for br, (_, dil) in enumerate(BRANCHES):
            project(h, w_ref, base + br * BRANCH_WIDTH, scr, tile0 + br * tiles_br, scale)
            if pending is not None:
                permute(*pending)
            pending = (outs[3 * br + which], dil, scr, tile0 + br * tiles_br)
    permute(*pending)


def _attn_proj(x, gains, wq, wkv, tm):
    b, s, d = x.shape
    qw = wq.shape[1]
    out_specs, out_shape = [], []
    for _, dil in BRANCHES:
        assert tm % (2 * SUBLANES * dil) == 0
        out_specs += [pl.BlockSpec((1, dil, tm // dil, BRANCH_WIDTH), lambda bi, i: (bi, 0, i, 0))] * 3
        out_shape += [jax.ShapeDtypeStruct((b, dil, s // dil, BRANCH_WIDTH), BF16)] * 3
    return pl.pallas_call(
        _attn_proj_kernel,
        grid=(b, s // tm),
        in_specs=[pl.BlockSpec((1, tm, d), lambda bi, i: (bi, i, 0)),
                  _const_spec(gains.shape), _const_spec(wq.shape), _const_spec(wkv.shape)],
        out_specs=out_specs,
        out_shape=out_shape,
        scratch_shapes=[pltpu.VMEM((qw // LANES, tm, LANES), F32),
                        pltpu.VMEM((2 * qw // LANES, tm, LANES), F32)],
        compiler_params=_params("parallel", "parallel"),
        name="attn_proj",
    )(x, gains, wq, wkv)


def _attn_kernel(slopes_ref, q_ref, k_ref, v_ref, o_ref, lse_ref, *, branch, dilation, qb):
    sub_len = q_ref.shape[2]
    n_blk = sub_len // qb
    n_pairs = q_ref.shape[3] // LANES
    col2 = lax.broadcasted_iota(jnp.int32, (qb, 2 * qb), 1)
    rel = (lax.broadcasted_iota(jnp.int32, (qb, 2 * qb), 0) - jnp.where(col2 < qb, col2, col2 - qb)).astype(F32)
    dist_d = jnp.where(rel >= 0, rel, -NEG)
    dist_p = jnp.where(rel <= 0, rel + float(qb), -NEG)
    first_head = lax.broadcasted_iota(jnp.int32, (1, 2 * qb), 1) < qb
    lane = lax.broadcasted_iota(jnp.int32, (qb, LANES), 1)
    head0 = lane < HEAD
    lane_row = lax.broadcasted_iota(jnp.int32, (1, LANES), 1)
    head_mask = [jnp.where(lane_row < HEAD, 1.0, 0.0).astype(BF16),
                 jnp.where(lane_row < HEAD, 0.0, 1.0).astype(BF16)]
    head_ones = [jnp.where(head0, 1.0, 0.0).astype(BF16), jnp.where(head0, 0.0, 1.0).astype(BF16)]

    block_rows = lambda i: pl.ds(pl.multiple_of(i * qb, qb), qb)

    def blocks(items):
        def one_pair(r, i, p):
            cur = block_rows(i)
            if n_blk > 1:
                prv = block_rows(jnp.maximum(i - 1, 0))
                has_prev = jnp.where(i > 0, 1.0, -NEG)
            lanes = slice(p * LANES, (p + 1) * LANES)
            slope = [slopes_ref[branch * HEADS_PER_BRANCH + 2 * p + j] * float(dilation) for j in range(2)]
            step = jnp.where(first_head, slope[0], slope[1])

            def keys(rows):
                k = k_ref[0, r, rows, lanes]
                return jnp.concatenate([k * head_mask[0], k * head_mask[1]], axis=0)

            def values(rows):
                v = v_ref[0, r, rows, lanes]
                return jnp.concatenate(
                    [jnp.concatenate([v * head_mask[0], head_ones[0]], axis=1),
                     jnp.concatenate([v * head_mask[1], head_ones[1]], axis=1)], axis=0)

            q = q_ref[0, r, cur, lanes]
            sd = _dot_nt(q, keys(cur)) - step * dist_d
            if n_blk > 1:
                sp = _dot_nt(q, keys(prv)) - (step * has_prev) * dist_p
            yield
            both = jnp.maximum(sd, sp) if n_blk > 1 else sd
            m = [jnp.max(both[:, :qb], axis=-1, keepdims=True), jnp.max(both[:, qb:], axis=-1, keepdims=True)]
            probs = lambda s: _bf(jnp.concatenate([jnp.exp(s[:, :qb] - m[0]), jnp.exp(s[:, qb:] - m[1])], axis=1))
            acc = _dot(probs(sd), values(cur))
            if n_blk > 1:
                acc = acc + _dot(probs(sp), values(prv))
            yield
            den = acc[:, LANES:]
            lse = jnp.where(head0, m[0], m[1]) + jnp.log(den)
            yield acc[:, :LANES] / den, lse[:, 0:1], lse[:, HEAD:HEAD + 1]

        res = _round_robin([one_pair(r, i, p) for r, i in items for p in range(n_pairs)])
        for n, (r, i) in enumerate(items):
            lse = jnp.zeros((qb, LANES), F32)
            for p in range(n_pairs):
                out, l_a, l_b = res[n * n_pairs + p]
                o_ref[0, r, block_rows(i), p * LANES:(p + 1) * LANES] = _bf(out)
                lse = jnp.where(lane == 2 * p, l_a, jnp.where(lane == 2 * p + 1, l_b, lse))
            lse_ref[0, r, block_rows(i), :] = lse

    n_res = q_ref.shape[1]
    if n_blk == 1:
        blocks([(r, 0) for r in range(n_res)])
    else:
        group = next(g for g in (4, 2, 1) if n_blk % g == 0)
        for r in range(n_res):
            lax.fori_loop(0, n_blk // group,
                          lambda j, c, r=r: (blocks([(r, group * j + g) for g in range(group)]), c)[1], 0)


def _attn_branch(slopes, q, k, v, branch):
    window, dilation = BRANCHES[branch]
    assert window % dilation == 0 and window // dilation == LANES
    b, _, sub_len, width = q.shape
    qb = min(LANES, sub_len)
    assert sub_len % qb == 0
    n_res = max(1, min(dilation, TOKEN_TILE // sub_len))
    assert dilation % n_res == 0
    spec = pl.BlockSpec((1, n_res, sub_len, width), lambda bi, ri: (bi, ri, 0, 0))
    return pl.pallas_call(
        functools.partial(_attn_kernel, branch=branch, dilation=dilation, qb=qb),
        grid=(b, dilation // n_res),
        in_specs=[pl.BlockSpec(memory_space=pltpu.SMEM)] + [spec] * 3,
        out_specs=[spec, pl.BlockSpec((1, n_res, sub_len, LANES), lambda bi, ri: (bi, ri, 0, 0))],
        out_shape=[jax.ShapeDtypeStruct(q.shape, BF16),
                   jax.ShapeDtypeStruct(q.shape[:3] + (LANES,), F32)],
        compiler_params=_params("parallel", "parallel"),
        name=f"dilated_attn_{branch}",
    )(slopes, q, k, v)


def _attn_out_kernel(o0, o1, o2, l0, l1, l2, x_ref, gain_ref, wo_ref, router_ref, rb_ref, expand_ref,
                     x_out, h_out, route_out, cnt_out, cnt_ref, o_scr, l_scr):
    @pl.when((pl.program_id(0) == 0) & (pl.program_id(1) == 0))
    def _():
        cnt_ref[...] = jnp.zeros_like(cnt_ref)

    tm = x_ref.shape[0]
    tiles_br = BRANCH_WIDTH // LANES
    for br, (o_ref, l_ref) in enumerate(((o0, l0), (o1, l1), (o2, l2))):
        dil = BRANCHES[br][1]
        for res in range(dil):
            rows = pl.ds(res, tm // dil, stride=dil) if dil > 1 else slice(None)
            l_scr[br, rows, :] = l_ref[0, res]
            for ti in range(tiles_br):
                lanes = slice(ti * LANES, (ti + 1) * LANES)
                o_scr[br, ti, rows, :] = o_ref[0, res, :, lanes].astype(F32)
    ls = [l_scr[br] for br in range(3)]
    m = jnp.maximum(jnp.maximum(ls[0], ls[1]), ls[2])
    ws = [jnp.exp(l - m) for l in ls]
    total = ws[0] + ws[1] + ws[2]
    spread = [_dot(_bf(ws[br] / total), expand_ref[...]) for br in range(3)]
    merged = []
    for ti in range(tiles_br):
        lanes = slice(ti * LANES, (ti + 1) * LANES)
        merged.append(_bf(sum(spread[br][:, lanes] * o_scr[br, ti] for br in range(3))))
    x = x_ref[...] + _dot(jnp.concatenate(merged, axis=1), wo_ref[...])
    x_out[...] = x
    h = _rms(x, gain_ref[...])
    h_out[...] = h
    hhi, hlo = _split2(h)
    both = _dot(hhi, router_ref[...])
    logits = both[:, :LANES] + both[:, LANES:] + _dot(hlo, router_ref[:, :LANES]) + rb_ref[...]
    lane = lax.broadcasted_iota(jnp.int32, logits.shape, 1)
    m1 = jnp.max(logits, axis=-1, keepdims=True)
    i1 = jnp.min(jnp.where(logits == m1, lane, LANES), axis=-1, keepdims=True)
    rest = jnp.where(lane == i1, 2 * NEG, logits)
    m2 = jnp.max(rest, axis=-1, keepdims=True)
    i2 = jnp.min(jnp.where(rest == m2, lane, LANES), axis=-1, keepdims=True)
    ex = jnp.exp(m2 - m1)
    sel = jnp.where((lane == i1) | (lane == i2), 1.0, 0.0)
    rr = lax.broadcasted_iota(jnp.int32, (tm, tm), 0)
    cc = lax.broadcasted_iota(jnp.int32, (tm, tm), 1)
    before = jnp.where(rr > cc, 1.0, 0.0).astype(BF16)
    rank = cnt_ref[...] + _dot(before, _bf(sel))
    cnt_ref[...] += jnp.sum(sel, axis=0, keepdims=True)
    cnt_out[...] = cnt_ref[...]
    columns = [i1.astype(F32), i2.astype(F32),
               jnp.sum(jnp.where(lane == i1, rank, 0.0), axis=-1, keepdims=True),
               jnp.sum(jnp.where(lane == i2, rank, 0.0), axis=-1, keepdims=True),
               1.0 / (1.0 + ex), ex / (1.0 + ex)]
    route = jnp.zeros(logits.shape, F32)
    for j, col in enumerate(columns):
        route = jnp.where(lane == j, col, route)
    route_out[...] = route


def _attn_out(os_, ls_, x, gain, wo, router, rb, b, tm):
    t, d = x.shape
    tiles = t // (b * tm)
    tok = pl.BlockSpec((tm, d), lambda bi, i: (bi * tiles + i, 0))
    small = pl.BlockSpec((tm, LANES), lambda bi, i: (bi * tiles + i, 0))
    att = [pl.BlockSpec((1, dil, tm // dil, BRANCH_WIDTH), lambda bi, i: (bi, 0, i, 0))
           for _, dil in BRANCHES]
    lse = [pl.BlockSpec((1, dil, tm // dil, LANES), lambda bi, i: (bi, 0, i, 0)) for _, dil in BRANCHES]
    expand = (jnp.arange(LANES)[:, None] == jnp.arange(BRANCH_WIDTH)[None, :] // HEAD).astype(BF16)
    consts = [gain, wo, router, rb, expand]
    return pl.pallas_call(
        _attn_out_kernel,
        grid=(b, tiles),
        in_specs=att + lse + [tok] + [_const_spec(c.shape) for c in consts],
        out_specs=[tok, tok, small, _const_spec((1, LANES))],
        out_shape=[jax.ShapeDtypeStruct((t, d), F32), jax.ShapeDtypeStruct((t, d), F32),
                   jax.ShapeDtypeStruct((t, LANES), F32), jax.ShapeDtypeStruct((1, LANES), F32)],
        scratch_shapes=[pltpu.VMEM((1, LANES), F32),
                        pltpu.VMEM((len(BRANCHES), BRANCH_WIDTH // LANES, tm, LANES), F32),
                        pltpu.VMEM((len(BRANCHES), tm, LANES), F32)],
        compiler_params=_params("arbitrary", "arbitrary"),
        name="attn_out_route",
    )(*os_, *ls_, x, *consts)


DMA_UNROLL = 8
DMA_TOKEN_TILE = 1024


def _row_copy(src, src_row, dst, dst_row, sem):
    return pltpu.make_async_copy(src.at[pl.ds(src_row, 1)], dst.at[pl.ds(dst_row, 1)], sem)


def _moe_scatter_kernel(meta_ref, dest_ref, h_ref, out_ref, zeros, sem, zero_sem):
    n_tok = h_ref.shape[0]
    tile = zeros.shape[0]
    ends = lambda e: meta_ref[N_EXPERTS + e]

    def zero_fill(e):
        if e < N_EXPERTS:
            start, used = jnp.maximum(ends(e) - tile, 0), ends(e) > meta_ref[e]
        else:
            start = ends(N_EXPERTS - 1) + (e - N_EXPERTS) * tile
            used = start < out_ref.shape[0]
            start = jnp.minimum(start, out_ref.shape[0] - tile)
        dst = out_ref.at[pl.ds(pl.multiple_of(start, tile), tile)]
        return used, pltpu.make_async_copy(zeros, dst, zero_sem)

    @pl.when(pl.program_id(0) == 0)
    def _():
        zeros[...] = jnp.zeros_like(zeros)
        for e in range(2 * N_EXPERTS):
            used, copy = zero_fill(e)
            pl.when(used)(copy.start)
        for e in range(2 * N_EXPERTS):
            used, copy = zero_fill(e)
            pl.when(used)(copy.wait)

    def issue(j, carry):
        for k in range(2):
            _row_copy(h_ref, j, out_ref, dest_ref[2 * j + k], sem).start(priority=k)
        return carry

    def drain(j, carry):
        for _ in range(2):
            _row_copy(h_ref, 0, out_ref, 0, sem).wait()
        return carry

    lax.fori_loop(0, n_tok, issue, 0, unroll=DMA_UNROLL)
    lax.fori_loop(0, n_tok, drain, 0, unroll=DMA_UNROLL)


def _moe_scatter(meta, dest_flat, h, rows, ts, tile):
    t, d = h.shape
    return pl.pallas_call(
        _moe_scatter_kernel,
        grid=(t // ts,),
        in_specs=[pl.BlockSpec(memory_space=pltpu.SMEM),
                  pl.BlockSpec((2 * ts,), lambda i: (i,), memory_space=pltpu.SMEM),
                  pl.BlockSpec((ts, d), lambda i: (i, 0))],
        out_specs=pl.BlockSpec(memory_space=pl.ANY),
        out_shape=jax.ShapeDtypeStruct((rows, d), F32),
        scratch_shapes=[pltpu.VMEM((tile, d), F32), pltpu.SemaphoreType.DMA(()),
                        pltpu.SemaphoreType.DMA(())],
        compiler_params=_params("arbitrary"),
        name="moe_scatter",
    )(meta, dest_flat, h)


def _moe_ffn_kernel(src_ref, te_ref, nv_ref, x_ref, w1_ref, w3_ref, w2_ref, o_ref):
    del src_ref, te_ref
    @pl.when(pl.program_id(1) == 0)
    def _():
        o_ref[...] = jnp.zeros_like(o_ref)

    @pl.when(pl.program_id(0) < nv_ref[0])
    def _():
        h = _bf(x_ref[...])
        u = _dot(h, w1_ref[0])
        act = _bf(u * _sigmoid(u) * _dot(h, w3_ref[0]))
        o_ref[...] += _dot(act, w2_ref[0])


def _moe_ffn(tile_src, tile_expert, n_valid, xs, w1, w3, w2, tm, tf):
    rows, d = xs.shape
    ff = w1.shape[2]
    tok = pl.BlockSpec((tm, d), lambda i, f, src, te, nv: (src[i], 0))
    n_f = ff // tf
    snake = lambda i, f: jnp.where(i % 2 == 0, f, n_f - 1 - f)
    return pl.pallas_call(
        _moe_ffn_kernel,
        grid_spec=pltpu.PrefetchScalarGridSpec(
            num_scalar_prefetch=3,
            grid=(rows // tm, n_f),
            in_specs=[tok,
                      pl.BlockSpec((1, d, tf), lambda i, f, src, te, nv: (te[i], 0, snake(i, f))),
                      pl.BlockSpec((1, d, tf), lambda i, f, src, te, nv: (te[i], 0, snake(i, f))),
                      pl.BlockSpec((1, tf, d), lambda i, f, src, te, nv: (te[i], snake(i, f), 0))],
            out_specs=pl.BlockSpec((tm, d), lambda i, f, src, te, nv: (i, 0)),
        ),
        out_shape=jax.ShapeDtypeStruct((rows, d), F32),
        compiler_params=_params("arbitrary", "arbitrary"),
        name="moe_experts",
    )(tile_src, tile_expert, n_valid, xs, w1, w3, w2)


def _moe_combine_kernel(dest_ref, x_ref, gates_ref, gain_ref, y_hbm, o_ref, buf, sem):
    n_tok = x_ref.shape[0]

    def issue(j, carry):
        for k in range(2):
            _row_copy(y_hbm, dest_ref[2 * j + k], buf.at[k], j, sem).start(priority=k)
        return carry

    def drain(j, carry):
        for k in range(2):
            _row_copy(y_hbm, 0, buf.at[k], 0, sem).wait()
        return carry

    lax.fori_loop(0, n_tok, issue, 0, unroll=DMA_UNROLL)
    lax.fori_loop(0, n_tok, drain, 0, unroll=DMA_UNROLL)
    gates = gates_ref[...]
    x = x_ref[...] + gates[:, 4:5] * buf[0] + gates[:, 5:6] * buf[1]
    o_ref[...] = _rms(x, gain_ref[...])


def _moe_combine(dest_flat, x, route, gain, ys, ts):
    t, d = x.shape
    tok = pl.BlockSpec((ts, d), lambda i: (i, 0))
    return pl.pallas_call(
        _moe_combine_kernel,
        grid=(t // ts,),
        in_specs=[pl.BlockSpec((2 * ts,), lambda i: (i,), memory_space=pltpu.SMEM),
                  tok, pl.BlockSpec((ts, LANES), lambda i: (i, 0)), _const_spec(gain.shape),
                  pl.BlockSpec(memory_space=pl.ANY)],
        out_specs=tok,
        out_shape=jax.ShapeDtypeStruct((t, d), F32),
        scratch_shapes=[pltpu.VMEM((2, ts, d), F32), pltpu.SemaphoreType.DMA(())],
        compiler_params=_params("arbitrary"),
        name="moe_combine",
    )(dest_flat, x, route, gain, ys)


def _tile(n, want):
    t = min(n, want)
    assert n % t == 0
    return t


def _pad_cols(w, n):
    return jnp.pad(w, ((0, 0), (0, n - w.shape[1])))


def _pad_rows(w, n):
    return jnp.pad(w, ((0, n - w.shape[0]), (0, 0)))


def kernel(x, norm_gain, rwkv_mu, rwkv_wr, rwkv_wk, rwkv_wv, rwkv_w0, rwkv_w1, rwkv_w2, rwkv_a0, rwkv_a1, rwkv_a2, rwkv_g1, rwkv_g2, rwkv_k_k, rwkv_k_a, rwkv_r_k, rwkv_lnx_w, rwkv_lnx_b, rwkv_wo, kv_norm_gain, w_kv, attn_wq, attn_wo, ffn_w1, ffn_w3, ffn_w2, moe_router, moe_router_bias, moe_w1, moe_w3, moe_w2, final_norm_gain):
    b, s, d = x.shape
    t = b * s
    assert norm_gain.shape[0] == 2 and d % LANES == 0 and s % SUBLANES == 0

    zeros = jnp.zeros((d,), F32)
    vec_pre = jnp.stack([norm_gain[0, 0], rwkv_w0[0], rwkv_a0[0], rwkv_k_k[0], rwkv_k_a[0],
                         zeros, zeros, zeros])
    tm_pre = _tile(s, RWKV_PRE_TILE)
    chunk = _tile(s, LANES)
    r, lw, k, v, kk, a, g = _rwkv_pre(
        x, vec_pre, rwkv_mu[0], _bf(rwkv_wr[0]), _bf(rwkv_wk[0]), _bf(rwkv_wv[0]),
        _bf(_pad_cols(rwkv_w1[0], LANES)), _bf(_pad_rows(rwkv_w2[0], LANES)),
        _bf(_pad_cols(rwkv_a1[0], LANES)), _bf(_pad_rows(rwkv_a2[0], LANES)),
        _bf(rwkv_g1[0]), _bf(rwkv_g2[0]), tm_pre)
    vec_post = jnp.stack([rwkv_lnx_w[0], rwkv_lnx_b[0], rwkv_r_k[0].reshape(d),
                          zeros, zeros, zeros, zeros, zeros])
    z = _wkv(r, lw, k, v, kk, a, vec_post, chunk=chunk, seq_tile=_tile(s, WKV_SEQ_TILE),
             n_pairs=WKV_PAIRS)

    flat = lambda arr: arr.reshape(t, d)
    tm = _tile(t, TOKEN_TILE)
    x1 = _rwkv_post(flat(z), flat(g), flat(x), _bf(rwkv_wo[0]), tm)

    ff = ffn_w1.shape[2]
    tf = ff // 2 if (ff // 2) % LANES == 0 else ff
    x2 = _ffn(x1, norm_gain[0, 1][None, :], _bf(ffn_w1[0]), _bf(ffn_w3[0]), _bf(ffn_w2[0]),
              _tile(t, 2 * FFN_SUB_ROWS), tf)

    n_slopes = len(BRANCHES) * HEADS_PER_BRANCH
    slopes = jnp.exp2(-ALIBI_MAX * (jnp.arange(n_slopes, dtype=F32) + 1.0) / n_slopes)
    gains = jnp.stack([norm_gain[1, 0], kv_norm_gain] + [zeros] * 6)
    tm_a = _tile(s, TOKEN_TILE)
    qkv = _attn_proj(x2.reshape(b, s, d), gains, _bf(attn_wq[0]), _bf(w_kv), tm_a)
    os_, ls_ = [], []
    for br in range(len(BRANCHES)):
        o_br, l_br = _attn_branch(slopes, *qkv[3 * br:3 * br + 3], br)
        os_.append(o_br)
        ls_.append(l_br)

    router = jnp.concatenate(_split2(_pad_cols(moe_router[0], LANES)), axis=1)
    rbias = jnp.full((1, LANES), NEG, F32).at[0, :N_EXPERTS].set(moe_router_bias[0])
    x3, h4, route, cnt = _attn_out(os_, ls_, x2, norm_gain[1, 1][None, :], _bf(attn_wo[0]),
                                   router, rbias, b, tm_a)

    tm_e = TOKEN_TILE if t >= N_EXPERTS * TOKEN_TILE else LANES
    ts = _tile(t, DMA_TOKEN_TILE)
    counts = cnt[0, :N_EXPERTS].astype(jnp.int32)
    padded = ((counts + tm_e - 1) // tm_e) * tm_e
    ends = jnp.cumsum(padded)
    meta = jnp.concatenate([ends - padded, ends]).astype(jnp.int32)
    route_i = route[:, :4].astype(jnp.int32)
    group_start = jnp.sum(jnp.where(route_i[:, :2, None] == jnp.arange(N_EXPERTS), ends - padded, 0), axis=-1)
    dest = (group_start + route_i[:, 2:4]).astype(jnp.int32).reshape(2 * t)
    rows = 2 * t + N_EXPERTS * tm_e
    n_tiles = rows // tm_e
    n_valid = (ends[-1] // tm_e).astype(jnp.int32)
    tile_src = jnp.minimum(jnp.arange(n_tiles, dtype=jnp.int32), n_valid - 1)
    tile_expert = jnp.minimum(
        jnp.sum(tile_src[:, None] * tm_e >= ends[None, :], axis=1), N_EXPERTS - 1).astype(jnp.int32)

    xs = _moe_scatter(meta, dest, h4, rows, ts, tm_e)
    ffe = moe_w1.shape[3]
    tfe = ffe // 2 if (ffe // 2) % LANES == 0 else ffe
    ys = _moe_ffn(tile_src, tile_expert, n_valid.reshape(1), xs,
                  _bf(moe_w1[0]), _bf(moe_w3[0]), _bf(moe_w2[0]), tm_e, tfe)
    out = _moe_combine(dest, x3, route, final_norm_gain[None, :], ys, ts)
    return out.reshape(b, s, d)
```
